```python
import math
import jax
import jax.numpy as jnp
from jax import lax
import numpy as np

D_MODEL = 1024
BATCH = 1
SEQ = 16384
DEPTH = 2
DEC_BATCH = 32
DEC_SEQ = 4
PAST_LEN = 16384
PAGE_SIZE = 128

N_EVEN = (DEPTH + 1) // 2
N_ODD = DEPTH // 2
MLP_HIDDEN = 4 * D_MODEL
C_CONV = D_MODEL // 2
CONV_WIDTH = 31
NSA_HEAD_DIM = 64
NSA_HEADS = (D_MODEL // 2) // NSA_HEAD_DIM
NSA_KV_HEADS = NSA_HEADS // 4
NSA_GROUP = NSA_HEADS // NSA_KV_HEADS
CMP_STRIDE = 16
CMP_LEN = 2 * CMP_STRIDE
CMP_HIDDEN = 2 * NSA_HEAD_DIM
SEL_BLOCK = 64
TOP_N = 16
WINDOW = 512
Q_BLOCK = 128
REL_BUCKETS = 32
REL_EXACT = REL_BUCKETS // 2
REL_MAX_DIST = 128
HG_DK = 128
HG_HEADS = D_MODEL // HG_DK
HG_DV = D_MODEL // HG_HEADS
HG_CHUNK = 64
RMS_EPS = 1e-6
NEG_INF = -1e30
BIG = 1e9
O_Q = 2 * C_CONV
O_KV = O_Q + NSA_HEADS * NSA_HEAD_DIM
O_GATE = O_KV + 3 * 2 * NSA_KV_HEADS * NSA_HEAD_DIM
ATT_IN_COLS = O_GATE + 3 * NSA_HEADS
ATT_OUT_ROWS = C_CONV + NSA_HEADS * NSA_HEAD_DIM

kernel_name = 'hybrid_conformer_nsa_hgrn2_step'


def rms_norm(x, g):
    xf = x.astype(jnp.float32)
    y = xf * lax.rsqrt(jnp.mean(xf * xf, axis=-1, keepdims=True) + RMS_EPS) * g.astype(jnp.float32)
    return y.astype(x.dtype)


def layer_norm(x, g, b):
    xf = x.astype(jnp.float32)
    mu = jnp.mean(xf, axis=-1, keepdims=True)
    var = jnp.mean(jnp.square(xf - mu), axis=-1, keepdims=True)
    return ((xf - mu) * lax.rsqrt(var + RMS_EPS) * g.astype(jnp.float32) + b.astype(jnp.float32)).astype(x.dtype)


def masked_softmax(logits, mask):
    logits = jnp.where(mask, logits.astype(jnp.float32), NEG_INF)
    m = jnp.max(logits, axis=-1, keepdims=True)
    e = jnp.where(mask, jnp.exp(logits - m), 0.0)
    return e / jnp.maximum(jnp.sum(e, axis=-1, keepdims=True), 1e-30)


def t5_bucket(dist):
    n = jnp.maximum(dist, 0)
    large = REL_EXACT + (jnp.log(jnp.maximum(n, 1).astype(jnp.float32) / REL_EXACT)
                         / math.log(REL_MAX_DIST / REL_EXACT) * (REL_BUCKETS - REL_EXACT)).astype(jnp.int32)
    return jnp.where(n < REL_EXACT, n, jnp.minimum(large, REL_BUCKETS - 1))


def sq_relu_mlp(h, w_up, w_down):
    return jnp.square(jax.nn.relu(h @ w_up)) @ w_down


def conformer_conv(u, state, w, b, ln_g, ln_b):
    a, gt = jnp.split(u, 2, axis=-1)
    glu = a * jax.nn.sigmoid(gt)
    xin = jnp.concatenate([state.astype(glu.dtype), glu], axis=1)
    y = lax.conv_general_dilated(xin, w[:, None, :].astype(xin.dtype), (1,), 'VALID',
                                 dimension_numbers=('NWC', 'WIO', 'NWC'),
                                 feature_group_count=C_CONV) + b
    y = jax.nn.silu(layer_norm(y, ln_g, ln_b))
    return y, xin[:, xin.shape[1] - (CONV_WIDTH - 1):]


def compress_rows(rows, pe, w1, b1, w2):
    B, T = rows.shape[:2]
    nch = T // CMP_STRIDE
    ch = rows[:, :nch * CMP_STRIDE].reshape(B, nch, CMP_STRIDE, NSA_KV_HEADS, NSA_HEAD_DIM)
    h_first = jnp.einsum('bcshd,sdf->bchf', ch + pe[:CMP_STRIDE, None, :], w1[:CMP_STRIDE])
    h_second = jnp.einsum('bcshd,sdf->bchf', ch + pe[CMP_STRIDE:, None, :], w1[CMP_STRIDE:])
    hid = jax.nn.gelu(h_first[:, :-1] + h_second[:, 1:] + b1)
    return jnp.einsum('bchf,fd->bchd', hid, w2)


def selection_scores(imp, ns):
    nc = imp.shape[-1]
    ratio = SEL_BLOCK // CMP_STRIDE
    span = CMP_LEN // CMP_STRIDE
    offs = np.array([m - n for m in range(ratio) for n in range(span)], dtype=np.int32)
    idx = ratio * np.arange(ns, dtype=np.int32)[:, None] + offs[None, :]
    valid = (idx >= 0) & (idx < nc)
    got = jnp.take(imp, np.clip(idx, 0, nc - 1).reshape(-1), axis=-1).reshape(imp.shape[:-1] + idx.shape)
    return jnp.sum(jnp.where(valid, got, 0.0), axis=-1)


def gather_blocks(blocks, idx):
    per_head = jax.vmap(lambda bh, ih: bh[ih], in_axes=(0, 1), out_axes=1)
    return jax.vmap(per_head)(blocks, idx)


def nsa_block(q, gates, q_pos, kc, vc, c_end, ks_blocks, vs_blocks, kw, vw, kw_pos, rel_table):
    B, Q = q.shape[:2]
    scale = NSA_HEAD_DIM ** -0.5
    table_hg = rel_table.astype(jnp.float32).reshape(REL_BUCKETS, NSA_KV_HEADS, NSA_GROUP)
    dist_c = q_pos[:, None] - c_end[None, :]
    bias_c = table_hg[t5_bucket(dist_c)].transpose(0, 2, 3, 1)
    logit_c = jnp.einsum('bqhgd,bnhd->bqhgn', q, kc).astype(jnp.float32) * scale + bias_c
    p_c = masked_softmax(logit_c, (dist_c >= 0)[:, None, None, :])
    o_c = jnp.einsum('bqhgn,bnhd->bqhgd', p_c, vc.astype(jnp.float32))
    ns = ks_blocks.shape[2]
    s = selection_scores(jnp.sum(p_c, axis=3), ns)
    j = jnp.arange(ns, dtype=jnp.int32)[None, :]
    qp = q_pos[:, None]
    cur = qp // SEL_BLOCK
    forced = (j == 0) | (j == cur) | (j == cur - 1)
    valid = j * SEL_BLOCK <= qp
    s = jnp.where(forced[None, :, None, :], BIG, s)
    s = jnp.where(valid[None, :, None, :], s, -BIG)
    _, idx = lax.top_k(s, min(TOP_N, ns))
    kg = gather_blocks(ks_blocks, idx)
    vg = gather_blocks(vs_blocks, idx)
    n_sel = idx.shape[-1] * SEL_BLOCK
    key_pos = idx[..., None] * SEL_BLOCK + jnp.arange(SEL_BLOCK, dtype=jnp.int32)
    dist_s = q_pos[None, :, None, None, None] - key_pos
    bias_s = table_hg[t5_bucket(dist_s), jnp.arange(NSA_KV_HEADS)[None, None, :, None, None]]
    bias_s = bias_s.transpose(0, 1, 2, 5, 3, 4).reshape(B, Q, NSA_KV_HEADS, NSA_GROUP, n_sel)
    logit_s = jnp.einsum('bqhgd,bqhnsd->bqhgns', q, kg).astype(jnp.float32).reshape(
        B, Q, NSA_KV_HEADS, NSA_GROUP, n_sel) * scale + bias_s
    p_s = masked_softmax(logit_s, (dist_s >= 0).reshape(B, Q, NSA_KV_HEADS, 1, n_sel))
    o_s = jnp.einsum('bqhgn,bqhnd->bqhgd', p_s,
                     vg.reshape(B, Q, NSA_KV_HEADS, n_sel, NSA_HEAD_DIM).astype(jnp.float32))
    dist_w = q_pos[:, None] - kw_pos[None, :]
    mask_w = (dist_w >= 0) & (dist_w <= WINDOW) & (kw_pos[None, :] >= 0)
    bias_w = table_hg[t5_bucket(dist_w)].transpose(0, 2, 3, 1)
    logit_w = jnp.einsum('bqhgd,bkhd->bqhgk', q, kw).astype(jnp.float32) * scale + bias_w
    p_w = masked_softmax(logit_w, mask_w[:, None, None, :])
    o_w = jnp.einsum('bqhgk,bkhd->bqhgd', p_w, vw.astype(jnp.float32))
    g = gates.astype(jnp.float32)
    return (g[..., 0:1] * o_c + g[..., 1:2] * o_s + g[..., 2:3] * o_w).astype(q.dtype)


def nsa_attend(q, gates, pos0, kc, vc, c_end, ks_blocks, vs_blocks, kw_all, rel_table):
    B, T = q.shape[:2]
    qb = min(Q_BLOCK, T)
    nb = T // qb
    q_blocks = q.reshape((B, nb, qb) + q.shape[2:]).swapaxes(0, 1)
    g_blocks = gates.reshape((B, nb, qb) + gates.shape[2:]).swapaxes(0, 1)

    def one_block(args):
        i, qi, gi = args
        start = i * qb
        q_pos = pos0 + start + jnp.arange(qb, dtype=jnp.int32)
        kw = lax.dynamic_slice_in_dim(kw_all, start, WINDOW + qb, axis=1)
        kw_pos = pos0 - WINDOW + start + jnp.arange(WINDOW + qb, dtype=jnp.int32)
        return nsa_block(qi, gi, q_pos, kc, vc, c_end, ks_blocks, vs_blocks,
                         kw[:, :, 0], kw[:, :, 1], kw_pos, rel_table)

    o = lax.map(one_block, (jnp.arange(nb, dtype=jnp.int32), q_blocks, g_blocks))
    return o.swapaxes(0, 1).reshape(q.shape)


def even_mixer(h, past_kv, win_buf, conv_state, rel_table, w_in, w_out, q_g, k_g,
               pe, w1, b1, w2, conv_w, conv_b, ln_g, ln_b):
    B, T, _ = h.shape
    pos0 = past_kv.shape[1]
    proj = h @ w_in
    conv_y, conv_new = conformer_conv(proj[..., :O_Q], conv_state, conv_w, conv_b, ln_g, ln_b)
    q = rms_norm(proj[..., O_Q:O_KV].reshape(B, T, NSA_KV_HEADS, NSA_GROUP, NSA_HEAD_DIM), q_g)
    kv = proj[..., O_KV:O_GATE].reshape(B, T, 3, 2, NSA_KV_HEADS, NSA_HEAD_DIM)
    gates = jax.nn.sigmoid(proj[..., O_GATE:].astype(jnp.float32)).reshape(B, T, NSA_KV_HEADS, NSA_GROUP, 3)
    kv_new = jnp.stack([kv[:, :, 0, 0], kv[:, :, 0, 1], rms_norm(kv[:, :, 1, 0], k_g[1]), kv[:, :, 1, 1]], axis=2)
    kv_all = jnp.concatenate([past_kv.astype(kv_new.dtype), kv_new], axis=1)
    t_all = pos0 + T
    kc = rms_norm(compress_rows(kv_all[:, :, 0], pe[0], w1[0], b1[0], w2[0]), k_g[0])
    vc = compress_rows(kv_all[:, :, 1], pe[1], w1[1], b1[1], w2[1])
    c_end = jnp.arange(kc.shape[1], dtype=jnp.int32) * CMP_STRIDE + (CMP_LEN - 1)
    ns = -(-t_all // SEL_BLOCK)
    sel = jnp.pad(kv_all[:, :, 2:], ((0, 0), (0, ns * SEL_BLOCK - t_all), (0, 0), (0, 0), (0, 0)))
    sel = sel.reshape(B, ns, SEL_BLOCK, 2, NSA_KV_HEADS, NSA_HEAD_DIM).transpose(3, 0, 4, 1, 2, 5)
    win_new = jnp.stack([rms_norm(kv[:, :, 2, 0], k_g[2]), kv[:, :, 2, 1]], axis=2)
    win_all = jnp.concatenate([win_buf.astype(win_new.dtype), win_new], axis=1)
    win_keep = win_all[:, win_all.shape[1] - min(WINDOW, win_all.shape[1]):]
    kw_all = jnp.pad(win_all, ((0, 0), (WINDOW - win_buf.shape[1], 0), (0, 0), (0, 0), (0, 0)))
    o = nsa_attend(q, gates, pos0, kc, vc, c_end, sel[0], sel[1], kw_all, rel_table)
    mix = jnp.concatenate([conv_y, o.reshape(B, T, NSA_HEADS * NSA_HEAD_DIM)], axis=-1) @ w_out
    return mix, kv_new, win_keep, conv_new


def hgrn2_scan(q, k, v, log_f, s0):
    B, T, H, DK = q.shape
    DV = v.shape[-1]
    c = min(HG_CHUNK, T)
    n = -(-T // c)
    pad = n * c - T

    def blocks(a):
        a = jnp.pad(a.astype(jnp.float32), ((0, 0), (0, pad), (0, 0), (0, 0)))
        return a.reshape(B, n, c, H, a.shape[-1]).transpose(1, 0, 3, 2, 4)

    tri = jnp.tril(jnp.ones((c, c), dtype=bool))

    def step(s, inp):
        qc, kc, vc, lf = inp
        b = jnp.cumsum(lf, axis=2)
        decay = jnp.exp(jnp.where(tri[:, :, None], b[:, :, :, None, :] - b[:, :, None, :, :], -jnp.inf))
        attn = jnp.einsum('bhtd,bhsd,bhtsd->bhts', qc, kc, decay)
        o = jnp.einsum('bhts,bhsv->bhtv', attn, vc) + jnp.einsum('bhtd,bhdv->bhtv', qc * jnp.exp(b), s)
        b_last = b[:, :, -1:, :]
        s = jnp.exp(b_last[:, :, 0, :, None]) * s + jnp.einsum('bhsd,bhsv->bhdv', kc * jnp.exp(b_last - b), vc)
        return s, o

    s, o = lax.scan(step, s0.astype(jnp.float32), (blocks(q), blocks(k), blocks(v), blocks(log_f)))
    o = o.transpose(1, 0, 3, 2, 4).reshape(B, n * c, H, DV)[:, :T]
    return o.astype(v.dtype), s.astype(s0.dtype)


def hgrn2_mixer(h, s0, layer, w_in, w_out, lb_logits, norm_g):
    B, T, _ = h.shape
    q, fz, i, g = jnp.split(h @ w_in, 4, axis=-1)
    cum = jnp.cumsum(jax.nn.softmax(lb_logits.astype(jnp.float32), axis=0), axis=0)
    lb = cum[layer] - cum[0]
    f = lb + (1.0 - lb) * jax.nn.sigmoid(fz.astype(jnp.float32))

    def heads(a):
        return a.reshape(B, T, HG_HEADS, -1)

    o, s_new = hgrn2_scan(heads(q), heads(1.0 - f), heads(i), heads(jnp.log(f)), s0)
    o = rms_norm(o, norm_g) * jax.nn.silu(heads(g))
    return o.reshape(B, T, D_MODEL) @ w_out, s_new


def trunk(x, past_kv, win_buf, conv_state, hg_state, rel_bias_table, norm_mix_g, norm_mlp_g,
          w_mlp_up, w_mlp_down, att_w_in, att_w_out, q_norm_g, k_norm_g, cmp_pe, cmp_w1, cmp_b1,
          cmp_w2, conv_w, conv_b, conv_ln_g, conv_ln_b, hg_w_in, hg_w_out, hg_lb_logits, hg_norm_g):
    kv_rows, win_rows, conv_rows, hg_rows = [], [], [], []
    for layer in range(DEPTH):
        h = rms_norm(x, norm_mix_g[layer])
        if layer % 2 == 0:
            e = layer // 2
            mix, kv_new, win_new, conv_new = even_mixer(
                h, past_kv[e], win_buf[e], conv_state[e], rel_bias_table, att_w_in[e], att_w_out[e],
                q_norm_g[e], k_norm_g[e], cmp_pe[e], cmp_w1[e], cmp_b1[e], cmp_w2[e],
                conv_w[e], conv_b[e], conv_ln_g[e], conv_ln_b[e])
            kv_rows.append(kv_new)
            win_rows.append(win_new)
            conv_rows.append(conv_new)
        else:
            o = layer // 2
            mix, s_new = hgrn2_mixer(h, hg_state[o], layer, hg_w_in[o], hg_w_out[o], hg_lb_logits, hg_norm_g[o])
            hg_rows.append(s_new)
        x = x + mix
        x = x + sq_relu_mlp(rms_norm(x, norm_mlp_g[layer]), w_mlp_up[layer], w_mlp_down[layer])
    return (x, jnp.stack(kv_rows), jnp.stack(win_rows), jnp.stack(conv_rows), jnp.stack(hg_rows))


def setup_inputs(seed: int = 0) -> dict:
    key = jax.random.key(seed)
    ks = jax.random.split(key, 28)
    n_pages = PAST_LEN // PAGE_SIZE
    n_pool = (DEC_BATCH * n_pages * 5) // 4
    win_rows = min(WINDOW, PAST_LEN)

    def nrm(k, shape, scale=1.0):
        return jax.random.normal(k, shape, jnp.float32) * scale

    def gain(k, shape):
        return 1.0 + nrm(k, shape, 0.05)

    page_table = jax.random.permutation(ks[3], n_pool)[: DEC_BATCH * n_pages].reshape(
        DEC_BATCH, n_pages).astype(jnp.int32)
    return {
        'x_prompt': nrm(ks[0], (BATCH, SEQ, D_MODEL)),
        'x_sample': nrm(ks[1], (DEC_BATCH, DEC_SEQ, D_MODEL)),
        'cache_nsa_kv': nrm(ks[2], (N_EVEN, n_pool, PAGE_SIZE, 4, NSA_KV_HEADS, NSA_HEAD_DIM)),
        'page_table': page_table,
        'state_nsa_win': nrm(ks[4], (N_EVEN, DEC_BATCH, win_rows, 2, NSA_KV_HEADS, NSA_HEAD_DIM)),
        'state_conv': nrm(ks[5], (N_EVEN, DEC_BATCH, CONV_WIDTH - 1, C_CONV)),
        'state_hgrn': nrm(ks[6], (N_ODD, DEC_BATCH, HG_HEADS, HG_DK, HG_DV), 0.5),
        'rel_bias_table': nrm(ks[7], (REL_BUCKETS, NSA_HEADS), 0.5),
        'norm_mix_g': gain(ks[8], (DEPTH, D_MODEL)),
        'norm_mlp_g': gain(ks[9], (DEPTH, D_MODEL)),
        'w_mlp_up': nrm(ks[10], (DEPTH, D_MODEL, MLP_HIDDEN), D_MODEL ** -0.5),
        'w_mlp_down': nrm(ks[11], (DEPTH, MLP_HIDDEN, D_MODEL), MLP_HIDDEN ** -0.5),
        'att_w_in': nrm(ks[12], (N_EVEN, D_MODEL, ATT_IN_COLS), D_MODEL ** -0.5),
        'att_w_out': nrm(ks[13], (N_EVEN, ATT_OUT_ROWS, D_MODEL), ATT_OUT_ROWS ** -0.5),
        'q_norm_g': gain(ks[14], (N_EVEN, NSA_HEAD_DIM)),
        'k_norm_g': gain(ks[15], (N_EVEN, 3, NSA_HEAD_DIM)),
        'cmp_pe': nrm(ks[16], (N_EVEN, 2, CMP_LEN, NSA_HEAD_DIM), 0.1),
        'cmp_w1': nrm(ks[17], (N_EVEN, 2, CMP_LEN, NSA_HEAD_DIM, CMP_HIDDEN), (CMP_LEN * NSA_HEAD_DIM) ** -0.5),
        'cmp_b1': nrm(ks[18], (N_EVEN, 2, CMP_HIDDEN), 0.01),
        'cmp_w2': nrm(ks[19], (N_EVEN, 2, CMP_HIDDEN, NSA_HEAD_DIM), CMP_HIDDEN ** -0.5),
        'conv_w': nrm(ks[20], (N_EVEN, CONV_WIDTH, C_CONV), CONV_WIDTH ** -0.5),
        'conv_b': nrm(ks[21], (N_EVEN, C_CONV), 0.01),
        'conv_ln_g': gain(ks[22], (N_EVEN, C_CONV)),
        'conv_ln_b': nrm(ks[23], (N_EVEN, C_CONV), 0.01),
        'hg_w_in': nrm(ks[24], (N_ODD, D_MODEL, 4 * D_MODEL), D_MODEL ** -0.5),
        'hg_w_out': nrm(ks[25], (N_ODD, D_MODEL, D_MODEL), D_MODEL ** -0.5),
        'hg_lb_logits': nrm(ks[26], (DEPTH, D_MODEL), 0.5),
        'hg_norm_g': gain(ks[27], (N_ODD, HG_DV)),
    }


def reference(x_prompt, x_sample, cache_nsa_kv, page_table, state_nsa_win, state_conv, state_hgrn,
              rel_bias_table, norm_mix_g, norm_mlp_g, w_mlp_up, w_mlp_down, att_w_in, att_w_out,
              q_norm_g, k_norm_g, cmp_pe, cmp_w1, cmp_b1, cmp_w2, conv_w, conv_b, conv_ln_g, conv_ln_b,
              hg_w_in, hg_w_out, hg_lb_logits, hg_norm_g):
    past_len = page_table.shape[1] * cache_nsa_kv.shape[2]
    db = x_sample.shape[0]
    b = x_prompt.shape[0]
    dt = x_prompt.dtype
    past_s = cache_nsa_kv[:, page_table].reshape(N_EVEN, db, past_len, 4, NSA_KV_HEADS, NSA_HEAD_DIM)
    y_prompt, kv_p, win_p, conv_p, hg_p = trunk(
        x_prompt,
        jnp.zeros((N_EVEN, b, 0, 4, NSA_KV_HEADS, NSA_HEAD_DIM), dt),
        jnp.zeros((N_EVEN, b, 0, 2, NSA_KV_HEADS, NSA_HEAD_DIM), dt),
        jnp.zeros((N_EVEN, b, CONV_WIDTH - 1, C_CONV), dt),
        jnp.zeros((N_ODD, b, HG_HEADS, HG_DK, HG_DV), dt),
        rel_bias_table, norm_mix_g, norm_mlp_g, w_mlp_up, w_mlp_down, att_w_in, att_w_out,
        q_norm_g, k_norm_g, cmp_pe, cmp_w1, cmp_b1, cmp_w2, conv_w, conv_b, conv_ln_g, conv_ln_b,
        hg_w_in, hg_w_out, hg_lb_logits, hg_norm_g)
    y_sample, kv_s, win_s, conv_s, hg_s = trunk(
        x_sample, past_s, state_nsa_win, state_conv, state_hgrn,
        rel_bias_table, norm_mix_g, norm_mlp_g, w_mlp_up, w_mlp_down, att_w_in, att_w_out,
        q_norm_g, k_norm_g, cmp_pe, cmp_w1, cmp_b1, cmp_w2, conv_w, conv_b, conv_ln_g, conv_ln_b,
        hg_w_in, hg_w_out, hg_lb_logits, hg_norm_g)
    return (y_prompt, y_sample, kv_p, kv_s, win_p, win_s, conv_p, conv_s, hg_p, hg_s)
```

```python
import functools
import math

import jax
import jax.numpy as jnp
import numpy as np
from jax import lax
from jax.experimental import pallas as pl
from jax.experimental.pallas import tpu as pltpu

D_MODEL = 1024
C_CONV = 512
CONV_WIDTH = 31
DH = 64
HKV = 2
GRP = 4
NH = HKV * GRP
CMP_STRIDE = 16
CMP_LEN = 32
CMP_HIDDEN = 128
SEL_BLOCK = 64
TOP_N = 16
WINDOW = 512
Q_BLOCK = 128
REL_BUCKETS = 32
REL_EXACT = 16
REL_MAX_DIST = 128
HG_DK = 128
HG_HEADS = 8
HG_DV = 128
HG_CHUNK = 64
RMS_EPS = 1e-6
NEG = -1e30
BIG = 1e9
O_Q = 2 * C_CONV
O_KV = O_Q + NH * DH
O_GATE = O_KV + 3 * 2 * HKV * DH
ATT_IN_COLS = O_GATE + 3 * NH
ATT_IN_PAD = 2432
KVW = HKV * DH

VMEM_LIMIT = 56 * 1024 * 1024
BF = jnp.bfloat16
F32 = jnp.float32


def _cp(*sem):
    return pltpu.CompilerParams(dimension_semantics=sem, vmem_limit_bytes=VMEM_LIMIT)


def _dot(a, b):
    return jnp.dot(a, b, preferred_element_type=F32)


def _dot_nt(a, b):
    return lax.dot_general(a, b, (((1,), (1,)), ((), ())), preferred_element_type=F32)


def _dot_tn(a, b):
    return lax.dot_general(a, b, (((0,), (0,)), ((), ())), preferred_element_type=F32)


def _rms(x, g):
    return x * lax.rsqrt(jnp.mean(x * x, axis=-1, keepdims=True) + RMS_EPS) * g


def _norm_matmul_kernel(x_ref, g_ref, w_ref, o_ref):
    hn = _rms(x_ref[...], g_ref[...]).astype(BF)
    o_ref[...] = _dot(hn, w_ref[...])


def norm_matmul(x, g, w_bf, tn=None):
    m, d = x.shape
    n = w_bf.shape[1]
    tm = min(m, 512)
    tn = n if tn is None else tn
    return pl.pallas_call(
        _norm_matmul_kernel,
        grid=(m // tm, n // tn),
        in_specs=[pl.BlockSpec((tm, d), lambda i, j: (i, 0)),
                  pl.BlockSpec((1, d), lambda i, j: (0, 0)),
                  pl.BlockSpec((d, tn), lambda i, j: (0, j))],
        out_specs=pl.BlockSpec((tm, tn), lambda i, j: (i, j)),
        out_shape=jax.ShapeDtypeStruct((m, n), F32),
        compiler_params=_cp("parallel", "arbitrary"),
        name="norm_matmul",
    )(x, g.reshape(1, d), w_bf)


def _out_proj_kernel(r_ref, a1_ref, a2_ref, w1_ref, w2_ref, o_ref):
    o_ref[...] = (r_ref[...] + _dot(a1_ref[...].astype(BF), w1_ref[...])
                  + _dot(a2_ref[...].astype(BF), w2_ref[...]))


def out_proj2(res, a1, a2, w1_bf, w2_bf):
    m, d = res.shape
    k1, k2 = a1.shape[1], a2.shape[1]
    tm = min(m, 512)
    return pl.pallas_call(
        _out_proj_kernel,
        grid=(m // tm,),
        in_specs=[pl.BlockSpec((tm, d), lambda i: (i, 0)),
                  pl.BlockSpec((tm, k1), lambda i: (i, 0)),
                  pl.BlockSpec((tm, k2), lambda i: (i, 0)),
                  pl.BlockSpec((k1, d), lambda i: (0, 0)),
                  pl.BlockSpec((k2, d), lambda i: (0, 0))],
        out_specs=pl.BlockSpec((tm, d), lambda i: (i, 0)),
        out_shape=jax.ShapeDtypeStruct((m, d), F32),
        compiler_params=_cp("parallel"),
        name="out_proj2",
    )(res, a1, a2, w1_bf, w2_bf)


def _out_proj1_kernel(r_ref, a_ref, w_ref, o_ref):
    o_ref[...] = r_ref[...] + _dot(a_ref[...].astype(BF), w_ref[...])


def out_proj1(res, a, w_bf):
    m, d = res.shape
    k = a.shape[1]
    tm = min(m, 512)
    return pl.pallas_call(
        _out_proj1_kernel,
        grid=(m // tm,),
        in_specs=[pl.BlockSpec((tm, d), lambda i: (i, 0)),
                  pl.BlockSpec((tm, k), lambda i: (i, 0)),
                  pl.BlockSpec((k, d), lambda i: (0, 0))],
        out_specs=pl.BlockSpec((tm, d), lambda i: (i, 0)),
        out_shape=jax.ShapeDtypeStruct((m, d), F32),
        compiler_params=_cp("parallel"),
        name="out_proj1",
    )(res, a, w_bf)


def _mlp_kernel(x_ref, g_ref, wu_ref, wd_ref, o_ref, hn_ref, acc_ref):
    j = pl.program_id(1)

    @pl.when(j == 0)
    def _():
        hn_ref[...] = _rms(x_ref[...], g_ref[...]).astype(BF)
        acc_ref[...] = x_ref[...]

    hid = jnp.maximum(_dot(hn_ref[...], wu_ref[...]), 0.0)
    acc_ref[...] += _dot((hid * hid).astype(BF), wd_ref[...])

    @pl.when(j == pl.num_programs(1) - 1)
    def _():
        o_ref[...] = acc_ref[...]


def mlp(x, g, wu_bf, wd_bf):
    m, d = x.shape
    hdim = wu_bf.shape[1]
    tm = min(m, 1024)
    th = 1024
    return pl.pallas_call(
        _mlp_kernel,
        grid=(m // tm, hdim // th),
        in_specs=[pl.BlockSpec((tm, d), lambda i, j: (i, 0)),
                  pl.BlockSpec((1, d), lambda i, j: (0, 0)),
                  pl.BlockSpec((d, th), lambda i, j: (0, j)),
                  pl.BlockSpec((th, d), lambda i, j: (j, 0))],
        out_specs=pl.BlockSpec((tm, d), lambda i, j: (i, 0)),
        out_shape=jax.ShapeDtypeStruct((m, d), F32),
        scratch_shapes=[pltpu.VMEM((tm, d), BF), pltpu.VMEM((tm, d), F32)],
        compiler_params=_cp("parallel", "arbitrary"),
        name="mlp",
    )(x, g.reshape(1, d), wu_bf, wd_bf)


CONV_HALO = 32
CONV_PAD = CONV_HALO - (CONV_WIDTH - 1)


def _conv_kernel(u_ref, st_ref, w_ref, b_ref, lg_ref, lb_ref, y_ref, new_ref, xin_ref, *, tt):
    t = pl.program_id(1)

    @pl.when(t == 0)
    def _():
        xin_ref[0:CONV_HALO, :] = st_ref[0]

    a = u_ref[0, :, 0:C_CONV]
    gt = u_ref[0, :, C_CONV:2 * C_CONV]
    xin_ref[CONV_HALO:CONV_HALO + tt, :] = a * jax.nn.sigmoid(gt)

    cols = []
    for c in range(C_CONV // 128):
        cs = slice(c * 128, (c + 1) * 128)
        acc = jnp.zeros((tt, 128), F32) + b_ref[:, cs]
        for k in range(CONV_WIDTH):
            acc = acc + w_ref[k:k + 1, cs] * xin_ref[CONV_PAD + k:CONV_PAD + k + tt, cs]
        cols.append(acc)
    y = jnp.concatenate(cols, axis=1)
    mu = jnp.mean(y, axis=-1, keepdims=True)
    yc = y - mu
    var = jnp.mean(yc * yc, axis=-1, keepdims=True)
    z = yc * lax.rsqrt(var + RMS_EPS) * lg_ref[...] + lb_ref[...]
    y_ref[0] = z * jax.nn.sigmoid(z)

    @pl.when(t == pl.num_programs(1) - 1)
    def _():
        new_ref[0] = xin_ref[tt + CONV_PAD:tt + CONV_HALO, :]

    if tt >= CONV_HALO:
        @pl.when(t < pl.num_programs(1) - 1)
        def _():
            xin_ref[0:CONV_HALO, :] = xin_ref[tt:tt + CONV_HALO, :]


def conformer_conv(proj, state, w, b, ln_g, ln_b):
    bsz, t, _ = proj.shape
    tt = min(t, 256)
    assert t % tt == 0 and (t == tt or tt >= CONV_HALO)
    st = jnp.pad(state, ((0, 0), (CONV_PAD, 0), (0, 0)))
    row = lambda v: v.reshape(1, C_CONV)
    return pl.pallas_call(
        functools.partial(_conv_kernel, tt=tt),
        grid=(bsz, t // tt),
        in_specs=[pl.BlockSpec((1, tt, 2 * C_CONV), lambda i, j: (i, j, 0)),
                  pl.BlockSpec((1, CONV_HALO, C_CONV), lambda i, j: (i, 0, 0)),
                  pl.BlockSpec((CONV_WIDTH, C_CONV), lambda i, j: (0, 0)),
                  pl.BlockSpec((1, C_CONV), lambda i, j: (0, 0)),
                  pl.BlockSpec((1, C_CONV), lambda i, j: (0, 0)),
                  pl.BlockSpec((1, C_CONV), lambda i, j: (0, 0))],
        out_specs=[pl.BlockSpec((1, tt, C_CONV), lambda i, j: (i, j, 0)),
                   pl.BlockSpec((1, CONV_WIDTH - 1, C_CONV), lambda i, j: (i, 0, 0))],
        out_shape=[jax.ShapeDtypeStruct((bsz, t, C_CONV), F32),
                   jax.ShapeDtypeStruct((bsz, CONV_WIDTH - 1, C_CONV), F32)],
        scratch_shapes=[pltpu.VMEM((CONV_HALO + tt, C_CONV), F32)],
        compiler_params=_cp("parallel", "arbitrary"),
        name="conformer_conv",
    )(proj, st, w, row(b), row(ln_g), row(ln_b))


HG_LEVELS = (32, 16, 8, 4, 2, 1)


def _hgrn_tables():
    c = HG_CHUNK
    idx = np.arange(c)
    mats = [(idx[None, :] <= idx[:, None]), (idx[None, :] > idx[:, None])]
    masks = []
    for h in HG_LEVELS:
        blk = idx // (2 * h)
        upper = (idx % (2 * h)) >= h
        piv = blk * 2 * h + h - 1
        mats.append(upper[:, None] & (idx[None, :] > piv[:, None]) & (idx[None, :] <= idx[:, None]))
        mats.append((~upper)[:, None] & (idx[None, :] > idx[:, None]) & (idx[None, :] <= piv[:, None]))
        masks.append((blk[:, None] == blk[None, :]) & upper[:, None] & (~upper)[None, :])
    masks.append(np.eye(c, dtype=bool))
    return (np.concatenate(mats, axis=0).astype(np.float32),
            np.stack(masks).astype(np.float32))


def _hgrn_kernel(q_ref, fz_ref, v_ref, g_ref, lb_ref, ng_ref, s0_ref, tab_ref, msk_ref,
                 o_ref, sn_ref, st_ref, *, t_valid):
    ci = pl.program_id(2)
    c = HG_CHUNK

    @pl.when(ci == 0)
    def _():
        st_ref[...] = s0_ref[0, 0].T

    row = ci * c + lax.broadcasted_iota(jnp.int32, (c, 1), 0)
    live = row < t_valid
    lb = lb_ref[...]
    f = lb + (1.0 - lb) * jax.nn.sigmoid(fz_ref[0])
    lf = jnp.where(live, jnp.log(f), 0.0)
    k = jnp.where(live, 1.0 - f, 0.0)
    q = q_ref[0]
    v = v_ref[0]

    hi = lf.astype(BF)
    r1 = lf - hi.astype(F32)
    mid = r1.astype(BF)
    lo = (r1 - mid.astype(F32)).astype(BF)
    tab = tab_ref[...]
    ex = _dot(tab, hi) + _dot(tab, mid) + _dot(tab, lo)

    attn = jnp.zeros((c, c), F32)
    for li in range(len(HG_LEVELS)):
        qd = (q * jnp.exp(ex[(2 + 2 * li) * c:(3 + 2 * li) * c])).astype(BF)
        ke = (k * jnp.exp(ex[(3 + 2 * li) * c:(4 + 2 * li) * c])).astype(BF)
        attn = attn + msk_ref[li] * _dot_nt(qd, ke)
    attn = attn + msk_ref[len(HG_LEVELS)] * _dot_nt(q.astype(BF), k.astype(BF))

    st = st_ref[...]
    qb = (q * jnp.exp(ex[0:c])).astype(BF)
    o = _dot(attn.astype(BF), v.astype(BF)) + _dot_nt(qb, st.astype(BF))
    ke = (k * jnp.exp(ex[c:2 * c])).astype(BF)
    decay = jnp.exp(ex[c - 1:c])
    st_new = st * decay + _dot(v.T.astype(BF), ke)
    st_ref[...] = st_new

    on = _rms(o, ng_ref[...])
    gate = g_ref[0]
    o_ref[0] = on * (gate * jax.nn.sigmoid(gate))

    @pl.when(ci == pl.num_programs(2) - 1)
    def _():
        sn_ref[0, 0] = st_new.T


def hgrn2(z, s0, lb, norm_g, t_valid):
    bsz, tp, _ = z.shape
    c = HG_CHUNK
    nc = tp // c
    tab, msk = _hgrn_tables()
    hh = HG_HEADS
    blk = lambda off: pl.BlockSpec((1, c, HG_DK), lambda b, h, i: (b, i, off + h))
    return pl.pallas_call(
        functools.partial(_hgrn_kernel, t_valid=t_valid),
        grid=(bsz, hh, nc),
        in_specs=[blk(0), blk(hh), blk(2 * hh), blk(3 * hh),
                  pl.BlockSpec((1, HG_DK), lambda b, h, i: (0, h)),
                  pl.BlockSpec((1, HG_DV), lambda b, h, i: (0, 0)),
                  pl.BlockSpec((1, 1, HG_DK, HG_DV), lambda b, h, i: (b, h, 0, 0)),
                  pl.BlockSpec(tab.shape, lambda b, h, i: (0, 0)),
                  pl.BlockSpec(msk.shape, lambda b, h, i: (0, 0, 0))],
        out_specs=[pl.BlockSpec((1, c, HG_DV), lambda b, h, i: (b, i, h)),
                   pl.BlockSpec((1, 1, HG_DK, HG_DV), lambda b, h, i: (b, h, 0, 0))],
        out_shape=[jax.ShapeDtypeStruct((bsz, tp, D_MODEL), F32),
                   jax.ShapeDtypeStruct((bsz, hh, HG_DK, HG_DV), F32)],
        scratch_shapes=[pltpu.VMEM((HG_DV, HG_DK), F32)],
        compiler_params=_cp("parallel", "parallel", "arbitrary"),
        name="hgrn2",
    )(z, z, z, z, lb.reshape(1, D_MODEL), norm_g.reshape(1, HG_DV), s0,
      jnp.asarray(tab, BF), jnp.asarray(msk))


def _pair_norm(seg, g):
    lo = lax.broadcasted_iota(jnp.int32, seg.shape, 1) < DH
    sq = seg * seg
    s0 = jnp.sum(jnp.where(lo, sq, 0.0), axis=-1, keepdims=True)
    s1 = jnp.sum(jnp.where(lo, 0.0, sq), axis=-1, keepdims=True)
    inv = jnp.where(lo, lax.rsqrt(s0 * (1.0 / DH) + RMS_EPS), lax.rsqrt(s1 * (1.0 / DH) + RMS_EPS))
    return seg * inv * g


def _kv_post_kernel(p_ref, kg_ref, kv_ref, win_ref):
    x = p_ref[...]
    kv_ref[:, 0:2 * KVW] = x[:, 0:2 * KVW]
    kv_ref[:, 2 * KVW:3 * KVW] = _pair_norm(x[:, 2 * KVW:3 * KVW], kg_ref[1:2, :])
    kv_ref[:, 3 * KVW:4 * KVW] = x[:, 3 * KVW:4 * KVW]
    win_ref[:, 0:KVW] = _pair_norm(x[:, 4 * KVW:5 * KVW], kg_ref[2:3, :])
    win_ref[:, KVW:2 * KVW] = x[:, 5 * KVW:6 * KVW]


def kv_post(proj, k_g):
    m = proj.shape[0]
    tm = min(m, 1024)
    kvcols = 6 * KVW
    assert O_KV % kvcols == 0
    return pl.pallas_call(
        _kv_post_kernel,
        grid=(m // tm,),
        in_specs=[pl.BlockSpec((tm, kvcols), lambda i: (i, O_KV // kvcols)),
                  pl.BlockSpec((3, KVW), lambda i: (0, 0))],
        out_specs=[pl.BlockSpec((tm, 4 * KVW), lambda i: (i, 0)),
                   pl.BlockSpec((tm, 2 * KVW), lambda i: (i, 0))],
        out_shape=[jax.ShapeDtypeStruct((m, 4 * KVW), F32), jax.ShapeDtypeStruct((m, 2 * KVW), F32)],
        compiler_params=_cp("parallel"),
        name="kv_post",
    )(proj, jnp.concatenate([k_g, k_g], axis=1))


def _q_norm_kernel(q_ref, g_ref, o_ref):
    for c in range(NH // 2):
        cs = slice(c * 2 * DH, (c + 1) * 2 * DH)
        o_ref[:, cs] = _pair_norm(q_ref[:, cs], g_ref[...]) * (DH ** -0.5)


def q_norm(proj, q_g):
    m = proj.shape[0]
    tm = min(m, 1024)
    return pl.pallas_call(
        _q_norm_kernel,
        grid=(m // tm,),
        in_specs=[pl.BlockSpec((tm, NH * DH), lambda i: (i, O_Q // (NH * DH))),
                  pl.BlockSpec((1, 2 * DH), lambda i: (0, 0))],
        out_specs=pl.BlockSpec((tm, NH * DH), lambda i: (i, 0)),
        out_shape=jax.ShapeDtypeStruct((m, NH * DH), F32),
        compiler_params=_cp("parallel"),
        name="q_norm",
    )(proj, jnp.concatenate([q_g, q_g]).reshape(1, 2 * DH))


def _compress_kernel(*refs, n_prefetch, n_src, rows, nch):
    refs = refs[n_prefetch:]
    srcs = (refs[:n_src], refs[n_src:2 * n_src])
    pef_ref, pes_ref, w1f_ref, w1s_ref, b1_ref, w2_ref, kg_ref, kc_ref, vc_ref, hf_ref, hs_ref = refs[2 * n_src:]
    j = pl.program_id(1)
    cpr = rows // CMP_STRIDE
    m = cpr * n_src

    @pl.when(j == 0)
    def _():
        hs_ref[:, nch:nch + 8, :] = jnp.zeros((2, 8, 2 * CMP_HIDDEN), F32)

    for kind in range(2):
        accf = jnp.zeros((m, 2 * CMP_HIDDEN), F32)
        accs = jnp.zeros((m, 2 * CMP_HIDDEN), F32)
        for s in range(CMP_STRIDE):
            xs = jnp.concatenate(
                [r[0, pl.ds(s, cpr, stride=CMP_STRIDE), :] for r in srcs[kind]], axis=0)
            accf = accf + _dot((xs + pef_ref[kind, s:s + 1, :]).astype(BF), w1f_ref[kind, s])
            accs = accs + _dot((xs + pes_ref[kind, s:s + 1, :]).astype(BF), w1s_ref[kind, s])
        row0 = pl.multiple_of(j * m, 8)
        hf_ref[kind, pl.ds(row0, m), :] = accf
        hs_ref[kind, pl.ds(row0, m), :] = accs

    @pl.when(j == pl.num_programs(1) - 1)
    def _():
        for kind in range(2):
            hid = jax.nn.gelu(hf_ref[kind, 0:nch, :] + hs_ref[kind, 1:nch + 1, :] + b1_ref[kind])
            out = _dot(hid.astype(BF), w2_ref[kind])
            if kind == 0:
                kc_ref[0] = _pair_norm(out, kg_ref[...]).astype(BF)
            else:
                vc_ref[0] = out.astype(BF)


def _compress_weights(pe, w1, b1, w2, kg0):
    eye = jnp.eye(HKV, dtype=F32)
    bd = lambda w: jnp.einsum('ab,ksdf->ksadbf', eye, w).reshape(2, CMP_STRIDE, KVW, 2 * CMP_HIDDEN).astype(BF)
    tile2 = lambda a: jnp.concatenate([a, a], axis=-1)
    w2bd = jnp.einsum('ab,kfd->kafbd', eye, w2).reshape(2, 2 * CMP_HIDDEN, KVW).astype(BF)
    return (tile2(pe[:, :CMP_STRIDE]), tile2(pe[:, CMP_STRIDE:]), bd(w1[:, :CMP_STRIDE]), bd(w1[:, CMP_STRIDE:]),
            tile2(b1).reshape(2, 1, 2 * CMP_HIDDEN), w2bd, tile2(kg0).reshape(1, KVW))


def _compress_common(n_prefetch, n_src, rows, nch, bsz, weights):
    wspecs = [pl.BlockSpec(w.shape, functools.partial(lambda nd, *a: (0,) * nd, w.ndim)) for w in weights]
    out_specs = [pl.BlockSpec((1, nch, KVW), lambda b, j, *a: (b, 0, 0))] * 2
    out_shape = [jax.ShapeDtypeStruct((bsz, nch, KVW), BF)] * 2
    scratch = [pltpu.VMEM((2, nch + 8, 2 * CMP_HIDDEN), F32)] * 2
    kern = functools.partial(_compress_kernel, n_prefetch=n_prefetch, n_src=n_src, rows=rows, nch=nch)
    return kern, wspecs, out_specs, out_shape, scratch


def compress_dense(proj3, weights):
    bsz, t, _ = proj3.shape
    rows = min(t, 2048)
    nch = t // CMP_STRIDE
    assert O_KV % KVW == 0
    kern, wspecs, out_specs, out_shape, scratch = _compress_common(0, 1, rows, nch, bsz, weights)
    src = lambda kind: pl.BlockSpec((1, rows, KVW), lambda b, j: (b, j, O_KV // KVW + kind))
    return pl.pallas_call(
        kern, grid=(bsz, t // rows),
        in_specs=[src(0), src(1)] + wspecs,
        out_specs=out_specs, out_shape=out_shape, scratch_shapes=scratch,
        compiler_params=_cp("parallel", "arbitrary"), name="compress_dense",
    )(proj3, proj3, *weights)


CMP_PAGES = 16


def compress_paged(cache2, page_table, weights):
    bsz, n_pages = page_table.shape
    page = cache2.shape[1]
    npg = min(CMP_PAGES, n_pages)
    nch = n_pages * page // CMP_STRIDE
    src_specs = [pl.BlockSpec((1, page, KVW),
                              functools.partial(lambda r, kind, b, j, pt: (pt[b, j * npg + r], 0, kind), r, kind))
                 for kind in range(2) for r in range(npg)]
    kern, wspecs, out_specs, out_shape, scratch = _compress_common(1, npg, page, nch, bsz, weights)
    return pl.pallas_call(
        kern,
        grid_spec=pltpu.PrefetchScalarGridSpec(
            num_scalar_prefetch=1, grid=(bsz, n_pages // npg), in_specs=src_specs + wspecs,
            out_specs=out_specs, scratch_shapes=scratch),
        out_shape=out_shape, compiler_params=_cp("parallel", "arbitrary"), name="compress_paged",
    )(page_table, *([cache2] * (2 * npg)), *weights)


NS_PAD = 256
SEL_HALF = 128
PEN = 30000.0
NEAR = 3 * Q_BLOCK
CBAND = 24


def _t5_bucket(dist):
    n = jnp.maximum(dist, 0)
    large = REL_EXACT + (jnp.log(jnp.maximum(n, 1).astype(F32) / REL_EXACT)
                         / math.log(REL_MAX_DIST / REL_EXACT) * (REL_BUCKETS - REL_EXACT)).astype(jnp.int32)
    return jnp.where(n < REL_EXACT, n, jnp.minimum(large, REL_BUCKETS - 1))


def _selection_matrix(nc, ns_pad, nc_pad):
    ratio, span = SEL_BLOCK // CMP_STRIDE, CMP_LEN // CMP_STRIDE
    a = np.zeros((ns_pad, nc_pad), np.float32)
    for j in range(ns_pad):
        for mm in range(ratio):
            for nn in range(span):
                n = ratio * j + mm - nn
                if 0 <= n < nc:
                    a[j, n] += 1.0
    return a


def _split3(x):
    hi = x.astype(BF)
    r1 = x - hi.astype(F32)
    mid = r1.astype(BF)
    lo = (r1 - mid.astype(F32)).astype(BF)
    return hi, mid, lo


def _top_blocks(s, qp, n_rounds):
    j = lax.broadcasted_iota(jnp.int32, s.shape, 0)
    cur = qp // SEL_BLOCK
    forced = (j == 0) | (j == cur) | (j == cur - 1)
    valid = j * SEL_BLOCK <= qp
    s = jnp.where(forced, BIG, s)
    s = jnp.where(valid, s, -BIG)
    sel = jnp.zeros(s.shape, F32)
    for _ in range(n_rounds):
        mx = jnp.max(s, axis=0, keepdims=True)
        jm = jnp.min(jnp.where(s == mx, j, 2 * NS_PAD), axis=0, keepdims=True)
        hit = j == jm
        sel = jnp.where(hit, 1.0, sel)
        s = jnp.where(hit, -3e38, s)
    return sel


def _masked_softmax_rows(s, axis):
    valid = s > 0.5 * NEG
    m = jnp.max(s, axis=axis, keepdims=True)
    e = jnp.where(valid, jnp.exp(s - m), 0.0)
    return e / jnp.maximum(jnp.sum(e, axis=axis, keepdims=True), 1e-30)


def _nsa_prompt_kernel(q_ref, gate_ref, qg_ref, kc_ref, vc_ref, amat_ref, kaug_ref, vsel_ref, kwin_ref,
                       vwin_ref, bc_ref, bs_ref, bw_ref, o_ref, lg_ref, *, ncp, n_top):
    i = pl.program_id(0)
    qb = Q_BLOCK
    rows = GRP * qb
    start = i * qb
    qall = q_ref[...]
    gates = jax.nn.sigmoid(gate_ref[...])
    lane = lax.broadcasted_iota(jnp.int32, (qb, 2 * DH), 1)
    qpos_row = start + lax.broadcasted_iota(jnp.int32, (1, qb), 1)
    n_far = jnp.maximum((i - 1) // 2, 0)

    @pl.when(i == 0)
    def _():
        lg_ref[0:16, :] = jnp.zeros((16, rows), F32)

    for h in range(HKV):
        parts = []
        for g in range(GRP):
            hd = h * GRP + g
            slab = qall[:, (hd // 2) * 2 * DH:(hd // 2 + 1) * 2 * DH]
            mine = (lane >= DH) if hd % 2 else (lane < DH)
            ss = jnp.sum(jnp.where(mine, slab * slab, 0.0), axis=-1, keepdims=True)
            xn = jnp.where(mine, slab * lax.rsqrt(ss * (1.0 / DH) + RMS_EPS) * qg_ref[...] * (DH ** -0.5), 0.0)
            if hd % 2 != h:
                xn = pltpu.roll(xn, DH, 1)
            parts.append(xn)
        qh = jnp.concatenate(parts, axis=0).astype(BF)

        lg_ref[16:16 + ncp, :] = _dot_nt(kc_ref[...], qh)
        band0 = pl.multiple_of(i * 8, 8)
        lg_ref[pl.ds(band0, CBAND), :] = lg_ref[pl.ds(band0, CBAND), :] + bc_ref[h]
        tok = lax.broadcasted_iota(jnp.int32, (ncp, 1), 0)
        p_c = _masked_softmax_rows(jnp.where(tok < 8 * i + 8, lg_ref[16:16 + ncp, :], NEG), 0)
        o_c = _dot_tn(p_c.astype(BF), vc_ref[...])[:, h * DH:(h + 1) * DH]

        imp = p_c[:, 0:qb]
        for g in range(1, GRP):
            imp = imp + p_c[:, g * qb:(g + 1) * qb]
        hi, mid, lo = _split3(imp)
        amat = amat_ref[...]
        score = _dot(amat, hi) + _dot(amat, mid) + _dot(amat, lo)
        sel = _top_blocks(score, qpos_row, n_top)
        pen = ((sel.T - 1.0) * PEN).astype(BF)
        q_lo = jnp.concatenate([jnp.concatenate([pen[:, 0:SEL_HALF]] * GRP, axis=0), qh], axis=1)
        q_hi = jnp.concatenate([jnp.concatenate([pen[:, SEL_HALF:]] * GRP, axis=0), qh], axis=1)

        def online(carry, s, v):
            m, l, acc = carry
            m2 = jnp.maximum(m, jnp.max(s, axis=-1, keepdims=True))
            a = jnp.exp(m - m2)
            p = jnp.exp(s - m2)
            return (m2, a * l + jnp.sum(p, axis=-1, keepdims=True), a * acc + _dot(p.astype(BF), v))

        def far_tile(t, carry):
            r = pl.multiple_of(2 * qb + t * 2 * qb, 2 * qb)
            qa = jnp.where(t * 2 * qb < SEL_HALF * SEL_BLOCK, q_lo, q_hi)
            s = _dot_nt(qa, kaug_ref[pl.ds(r, 2 * qb), :])
            return online(carry, s, vsel_ref[pl.ds(r, 2 * qb), :])

        carry = (jnp.full((rows, 1), NEG, F32), jnp.zeros((rows, 1), F32), jnp.zeros((rows, 2 * DH), F32))
        carry = lax.fori_loop(0, n_far, far_tile, carry)
        for u in range(NEAR // qb):
            kp0 = start - 2 * qb + u * qb
            r = pl.multiple_of(start + u * qb, qb)
            qa = jnp.where(kp0 < SEL_HALF * SEL_BLOCK, q_lo, q_hi)
            s = _dot_nt(qa, kaug_ref[pl.ds(r, qb), :]) + bs_ref[h, :, u * qb:(u + 1) * qb]
            kpos = kp0 + lax.broadcasted_iota(jnp.int32, (1, qb), 1)
            s = jnp.where(kpos >= n_far * 2 * qb, s, NEG)
            carry = online(carry, s, vsel_ref[pl.ds(r, qb), :])
        _, l_s, acc_s = carry
        o_s = acc_s[:, h * DH:(h + 1) * DH] / jnp.maximum(l_s, 1e-30)

        r = pl.multiple_of(start, qb)
        s = _dot_nt(qh, kwin_ref[pl.ds(r, WINDOW + qb), :]) + bw_ref[h]
        kpos = start - WINDOW + lax.broadcasted_iota(jnp.int32, (1, WINDOW + qb), 1)
        p_w = _masked_softmax_rows(jnp.where(kpos >= 0, s, NEG), 1)
        o_w = _dot(p_w.astype(BF), vwin_ref[pl.ds(r, WINDOW + qb), :])[:, h * DH:(h + 1) * DH]

        for g in range(GRP):
            c = (h * GRP + g) * 3
            rs = slice(g * qb, (g + 1) * qb)
            o_ref[:, (h * GRP + g) * DH:(h * GRP + g + 1) * DH] = (
                gates[:, c:c + 1] * o_c[rs] + gates[:, c + 1:c + 2] * o_s[rs] + gates[:, c + 2:c + 3] * o_w[rs])


def _bias_tables(rel_table):
    tab = rel_table.astype(F32)
    far = tab[REL_BUCKETS - 1]
    ql = jnp.arange(Q_BLOCK, dtype=jnp.int32)

    def rows_gq(b):
        return b.transpose(2, 0, 1).reshape(HKV, GRP * Q_BLOCK, b.shape[1])

    dist = ql[:, None] - CMP_STRIDE * (jnp.arange(CBAND, dtype=jnp.int32)[None, :] - 16) - (CMP_LEN - 1)
    bc = jnp.where((dist >= 0)[..., None], tab[_t5_bucket(dist)] - far, NEG)
    bc = rows_gq(bc).transpose(0, 2, 1)
    dist = ql[:, None] + 2 * Q_BLOCK - jnp.arange(NEAR, dtype=jnp.int32)[None, :]
    bs = rows_gq(jnp.where((dist >= 0)[..., None], tab[_t5_bucket(dist)] - far, NEG))
    dist = ql[:, None] + WINDOW - jnp.arange(WINDOW + Q_BLOCK, dtype=jnp.int32)[None, :]
    bw = rows_gq(jnp.where(((dist >= 0) & (dist <= WINDOW))[..., None], tab[_t5_bucket(dist)], NEG))
    return bc, bs, bw


def nsa_prompt(proj, kc, vc, ksel, vsel, kwin, vwin, q_g, rel_table):
    t = proj.shape[0]
    nb = t // Q_BLOCK
    nch = kc.shape[0]
    ncp = -(-nch // 128) * 128
    assert t // SEL_BLOCK <= NS_PAD and t % Q_BLOCK == 0
    n_top = min(TOP_N, t // SEL_BLOCK)
    kcp = jnp.pad(kc, ((0, ncp - nch), (0, 0)))
    vcp = jnp.pad(vc, ((0, ncp - nch), (0, 0)))
    amat = jnp.asarray(_selection_matrix(nch - 1, NS_PAD, ncp), BF)
    onehot = jax.nn.one_hot((jnp.arange(t) // SEL_BLOCK) % SEL_HALF, SEL_HALF, dtype=BF)
    kaug = jnp.pad(jnp.concatenate([onehot, ksel.astype(BF)], axis=1), ((2 * Q_BLOCK, 0), (0, 0)))
    vselp = jnp.pad(vsel.astype(BF), ((2 * Q_BLOCK, 0), (0, 0)))
    kwinp = jnp.pad(kwin.astype(BF), ((WINDOW, 0), (0, 0)))
    vwinp = jnp.pad(vwin.astype(BF), ((WINDOW, 0), (0, 0)))
    bc, bs, bw = _bias_tables(rel_table)
    whole = pl.BlockSpec(memory_space=pltpu.VMEM)
    return pl.pallas_call(
        functools.partial(_nsa_prompt_kernel, ncp=ncp, n_top=n_top),
        grid=(nb,),
        in_specs=[pl.BlockSpec((Q_BLOCK, NH * DH), lambda i: (i, O_Q // (NH * DH))),
                  pl.BlockSpec((Q_BLOCK, 128), lambda i: (i, O_GATE // 128)),
                  whole, whole, whole, whole, whole, whole, whole, whole, whole, whole, whole],
        out_specs=pl.BlockSpec((Q_BLOCK, NH * DH), lambda i: (i, 0)),
        out_shape=jax.ShapeDtypeStruct((t, NH * DH), F32),
        scratch_shapes=[pltpu.VMEM((16 + ncp, GRP * Q_BLOCK), F32)],
        compiler_params=_cp("arbitrary"),
        name="nsa_prompt",
    )(proj, proj, jnp.concatenate([q_g, q_g]).reshape(1, 2 * DH), kcp, vcp, amat, kaug, vselp, kwinp, vwinp,
      bc, bs, bw)


def _pick_head(x, lane_h):
    return jnp.where(lane_h == 0, x[0:DH], x[DH:2 * DH])


def _nsa_dec_pre_kernel(qbd_ref, kc_ref, vc_ref, amat_ref, bc_ref, win_ref, wnew_ref, bwa_ref, bwb_ref,
                        pen_ref, oc_ref, ow_ref, *, ts, pos0, n_top):
    r_all = GRP * HKV * ts
    qbd = qbd_ref[0]
    lane_h = (lax.broadcasted_iota(jnp.int32, (1, r_all), 1) // ts) % HKV

    p_c = _masked_softmax_rows(_dot(kc_ref[0], qbd) + bc_ref[...], 0)
    oc_ref[0] = _pick_head(_dot_tn(vc_ref[0], p_c.astype(BF)), lane_h)

    hi, mid, lo = _split3(p_c)
    amat = amat_ref[...]
    sc = _dot(amat, hi) + _dot(amat, mid) + _dot(amat, lo)
    w8 = HKV * ts
    score = sc[:, 0:w8]
    for g in range(1, GRP):
        score = score + sc[:, g * w8:(g + 1) * w8]
    qp = pos0 + lax.broadcasted_iota(jnp.int32, (1, w8), 1) % ts
    sel = _top_blocks(score, qp, n_top)
    pen_ref[0] = (jnp.concatenate([sel] * GRP, axis=1) - 1.0) * PEN

    win = win_ref[0]
    wnew = wnew_ref[0]
    s_a = _dot(win[:, 0:KVW].astype(BF), qbd) + bwa_ref[...]
    s_b = _dot(wnew[:, 0:KVW].astype(BF), qbd) + bwb_ref[...]
    m = jnp.maximum(jnp.max(s_a, axis=0, keepdims=True), jnp.max(s_b, axis=0, keepdims=True))
    e_a = jnp.where(s_a > 0.5 * NEG, jnp.exp(s_a - m), 0.0)
    e_b = jnp.where(s_b > 0.5 * NEG, jnp.exp(s_b - m), 0.0)
    den = jnp.maximum(jnp.sum(e_a, axis=0, keepdims=True) + jnp.sum(e_b, axis=0, keepdims=True), 1e-30)
    o_w = (_dot_tn(win[:, KVW:].astype(BF), e_a.astype(BF)) + _dot_tn(wnew[:, KVW:].astype(BF), e_b.astype(BF)))
    ow_ref[0] = _pick_head(o_w, lane_h) / den


SEL_PAGES = 16


def _nsa_dec_sel_kernel(*refs, npg, page, ts):
    pt_ref = refs[0]
    pages = refs[1:1 + npg]
    (qbd_ref, pen_ref, exp_ref, blast_ref, knew_ref, bnew_ref, gate_ref, oc_ref, ow_ref,
     o_ref, m_ref, l_ref, acc_ref) = refs[1 + npg:]
    del pt_ref
    j = pl.program_id(1)
    last = pl.num_programs(1) - 1
    r_all = GRP * HKV * ts
    bps = npg * page // SEL_BLOCK

    @pl.when(j == 0)
    def _():
        m_ref[...] = jnp.full((1, r_all), NEG, F32)
        l_ref[...] = jnp.zeros((1, r_all), F32)
        acc_ref[...] = jnp.zeros((2 * DH, r_all), F32)

    qbd = qbd_ref[0]

    def online(s, v):
        m = m_ref[...]
        m2 = jnp.maximum(m, jnp.max(s, axis=0, keepdims=True))
        a = jnp.exp(m - m2)
        p = jnp.exp(s - m2)
        m_ref[...] = m2
        l_ref[...] = a * l_ref[...] + jnp.sum(p, axis=0, keepdims=True)
        acc_ref[...] = a * acc_ref[...] + _dot_tn(v, p.astype(BF))

    kv = jnp.concatenate([r[0] for r in pages], axis=0)
    pen = pen_ref[0, pl.ds(pl.multiple_of(j * bps, 8), bps), :].astype(BF)
    s = _dot(kv[:, 0:KVW].astype(BF), qbd) + _dot(exp_ref[...], pen)
    s = s + jnp.where(j == last, blast_ref[...], 0.0)
    online(s, kv[:, KVW:].astype(BF))

    @pl.when(j == last)
    def _():
        knew = knew_ref[0]
        online(_dot(knew[:, 0:KVW].astype(BF), qbd) + bnew_ref[...], knew[:, KVW:].astype(BF))
        lane_h = (lax.broadcasted_iota(jnp.int32, (1, r_all), 1) // ts) % HKV
        o_s = _pick_head(acc_ref[...], lane_h) / jnp.maximum(l_ref[...], 1e-30)
        g = jax.nn.sigmoid(gate_ref[0])
        o_ref[0] = g[0:1] * oc_ref[0] + g[1:2] * o_s + g[2:3] * ow_ref[0]


def _dec_bias_tables(rel_table, pos0, ts, nch, wb, page, npg):
    tab = rel_table.astype(F32)
    far = tab[REL_BUCKETS - 1]
    r = jnp.arange(GRP * HKV * ts, dtype=jnp.int32)
    head = ((r // ts) % HKV) * GRP + r // (HKV * ts)
    qpos = pos0 + r % ts

    def bias(kpos, ok, shift):
        dist = qpos[None, :] - kpos[:, None]
        b = tab[_t5_bucket(dist), head[None, :]] - (far[head][None, :] if shift else 0.0)
        return jnp.where((dist >= 0) & ok(dist), b, NEG)

    always = lambda d: d >= 0
    bc = bias(jnp.arange(nch, dtype=jnp.int32) * CMP_STRIDE + CMP_LEN - 1, always, False)
    bwa = bias(pos0 - wb + jnp.arange(wb, dtype=jnp.int32), lambda d: d <= WINDOW, False)
    tnew = jnp.arange(8, dtype=jnp.int32)
    newpos = jnp.where(tnew < ts, pos0 + tnew, pos0 + 2 * WINDOW + SEL_BLOCK)
    bwb = bias(newpos, lambda d: d <= WINDOW, False)
    bnew = bias(newpos, always, True)
    step = npg * page
    blast = jnp.zeros((step, r.shape[0]), F32).at[step - page:].set(
        bias(pos0 - page + jnp.arange(page, dtype=jnp.int32), always, True))
    return bc, bwa, bwb, bnew, blast


def nsa_decode(qn, gate_raw, kc, vc, cache2, page_table, win_state, kv_new, win_new, rel_table):
    bsz, ts, _ = qn.shape
    n_pages = page_table.shape[1]
    page = cache2.shape[1]
    pos0 = n_pages * page
    nch = kc.shape[1]
    wb = win_state.shape[1]
    r_all = GRP * HKV * ts
    assert pos0 % SEL_BLOCK == 0 and ts <= 8 and ts <= SEL_BLOCK and nch % 8 == 0
    ns = -(-(pos0 + ts) // SEL_BLOCK)
    npg = min(SEL_PAGES, n_pages)
    assert n_pages % npg == 0
    bps = npg * page // SEL_BLOCK
    ns_pad = max(-(-ns // 8) * 8, (n_pages // npg) * bps)
    n_top = min(TOP_N, ns)

    q5 = qn.reshape(bsz, ts, HKV, GRP, DH)
    qbd = jnp.einsum('bqhgd,hk->bkdghq', q5, jnp.eye(HKV, dtype=F32)).reshape(bsz, 2 * DH, r_all).astype(BF)
    gate_t = gate_raw.reshape(bsz, ts, HKV, GRP, 3).transpose(0, 4, 3, 2, 1).reshape(bsz, 3, r_all)
    pad8 = lambda a: jnp.pad(a, ((0, 0), (0, 8 - ts), (0, 0)))
    amat = jnp.asarray(_selection_matrix(nch - 1, ns_pad, nch), BF)
    expand = jnp.asarray(np.repeat(np.eye(bps, dtype=np.float32), SEL_BLOCK, axis=0), BF)
    bc, bwa, bwb, bnew, blast = _dec_bias_tables(rel_table, pos0, ts, nch, wb, page, npg)

    full = lambda a: pl.BlockSpec(a.shape, functools.partial(lambda nd, *_: (0,) * nd, a.ndim))
    per_b = lambda a: pl.BlockSpec((1,) + a.shape[1:], functools.partial(lambda nd, b, *_: (b,) + (0,) * nd, a.ndim - 1))
    wnew = pad8(win_new)
    pre_in = [qbd, kc, vc, amat, bc, win_state, wnew, bwa, bwb]
    pre_specs = [per_b(qbd), per_b(kc), per_b(vc), full(amat), full(bc), per_b(win_state), per_b(wnew),
                 full(bwa), full(bwb)]
    small = jax.ShapeDtypeStruct((bsz, DH, r_all), F32)
    pen, o_c, o_w = pl.pallas_call(
        functools.partial(_nsa_dec_pre_kernel, ts=ts, pos0=pos0, n_top=n_top),
        grid=(bsz,), in_specs=pre_specs,
        out_specs=[pl.BlockSpec((1, ns_pad, r_all), lambda b: (b, 0, 0)),
                   pl.BlockSpec((1, DH, r_all), lambda b: (b, 0, 0)),
                   pl.BlockSpec((1, DH, r_all), lambda b: (b, 0, 0))],
        out_shape=[jax.ShapeDtypeStruct((bsz, ns_pad, r_all), F32), small, small],
        compiler_params=_cp("parallel"), name="nsa_dec_pre",
    )(*pre_in)

    knew = pad8(kv_new[:, :, 2 * KVW:])
    page_specs = [pl.BlockSpec((1, page, 2 * KVW),
                               functools.partial(lambda r, b, j, pt: (pt[b, j * npg + r], 0, 1), r))
                  for r in range(npg)]
    sel_in = [qbd, pen, expand, blast, knew, bnew, gate_t, o_c, o_w]
    sel_specs = [per_b(qbd), per_b(pen), full(expand), full(blast), per_b(knew), full(bnew), per_b(gate_t),
                 per_b(o_c), per_b(o_w)]
    o_t = pl.pallas_call(
        functools.partial(_nsa_dec_sel_kernel, npg=npg, page=page, ts=ts),
        grid_spec=pltpu.PrefetchScalarGridSpec(
            num_scalar_prefetch=1, grid=(bsz, n_pages // npg), in_specs=page_specs + sel_specs,
            out_specs=pl.BlockSpec((1, DH, r_all), lambda b, j, pt: (b, 0, 0)),
            scratch_shapes=[pltpu.VMEM((1, r_all), F32), pltpu.VMEM((1, r_all), F32),
                            pltpu.VMEM((2 * DH, r_all), F32)]),
        out_shape=small, compiler_params=_cp("parallel", "arbitrary"), name="nsa_dec_sel",
    )(page_table, *([cache2] * npg), *sel_in)
    return o_t.reshape(bsz, DH, GRP, HKV, ts).transpose(0, 4, 3, 2, 1).reshape(bsz, ts, NH * DH)


def _even_weights(p, e):
    w_in = jnp.pad(p['att_w_in'][e], ((0, 0), (0, ATT_IN_PAD - ATT_IN_COLS))).astype(BF)
    w_out = p['att_w_out'][e].astype(BF)
    return dict(
        w_in=w_in, w_out_conv=w_out[:C_CONV], w_out_att=w_out[C_CONV:],
        cmp=_compress_weights(p['cmp_pe'][e], p['cmp_w1'][e], p['cmp_b1'][e], p['cmp_w2'][e], p['k_norm_g'][e][0]),
        k_g=p['k_norm_g'][e], q_g=p['q_norm_g'][e], conv_w=p['conv_w'][e], conv_b=p['conv_b'][e],
        ln_g=p['conv_ln_g'][e], ln_b=p['conv_ln_b'][e])


def _even_prompt(x2, norm_g, w, rel_table):
    t = x2.shape[0]
    proj = norm_matmul(x2, norm_g, w['w_in'])
    conv_y, conv_new = conformer_conv(proj[None], jnp.zeros((1, CONV_WIDTH - 1, C_CONV), F32),
                                      w['conv_w'], w['conv_b'], w['ln_g'], w['ln_b'])
    kv_new, win_new = kv_post(proj, w['k_g'])
    kc, vc = compress_dense(proj[None], w['cmp'])
    o = nsa_prompt(proj, kc[0], vc[0], kv_new[:, 2 * KVW:3 * KVW], kv_new[:, 3 * KVW:], win_new[:, :KVW],
                   win_new[:, KVW:], w['q_g'], rel_table)
    x2 = out_proj2(x2, conv_y[0], o, w['w_out_conv'], w['w_out_att'])
    keep = min(WINDOW, t)
    return (x2, kv_new.reshape(1, t, 4, HKV, DH), win_new[t - keep:].reshape(1, keep, 2, HKV, DH), conv_new)


def _even_decode(x3, norm_g, w, rel_table, cache2, page_table, win_state, conv_state):
    bsz, t, d = x3.shape
    x2 = x3.reshape(bsz * t, d)
    proj = norm_matmul(x2, norm_g, w['w_in'])
    proj3 = proj.reshape(bsz, t, ATT_IN_PAD)
    conv_y, conv_new = conformer_conv(proj3, conv_state, w['conv_w'], w['conv_b'], w['ln_g'], w['ln_b'])
    kv_new, win_new = kv_post(proj, w['k_g'])
    qn = q_norm(proj, w['q_g'])
    kc, vc = compress_paged(cache2, page_table, w['cmp'])
    wb = win_state.shape[1]
    o = nsa_decode(qn.reshape(bsz, t, NH * DH), proj3[:, :, O_GATE:ATT_IN_COLS], kc, vc, cache2, page_table,
                   win_state.reshape(bsz, wb, 2 * KVW), kv_new.reshape(bsz, t, 4 * KVW),
                   win_new.reshape(bsz, t, 2 * KVW), rel_table)
    x2 = out_proj2(x2, conv_y.reshape(bsz * t, C_CONV), o.reshape(bsz * t, NH * DH), w['w_out_conv'],
                   w['w_out_att'])
    win_all = jnp.concatenate([win_state, win_new.reshape(bsz, t, 2, HKV, DH)], axis=1)
    keep = min(WINDOW, wb + t)
    return (x2.reshape(bsz, t, d), kv_new.reshape(bsz, t, 4, HKV, DH), win_all[:, wb + t - keep:], conv_new)


def _odd_layer(x3, s0, norm_g, w_in_bf, w_out_bf, lb, hg_norm_g):
    bsz, t, d = x3.shape
    x2 = x3.reshape(bsz * t, d)
    z = norm_matmul(x2, norm_g, w_in_bf, tn=2048).reshape(bsz, t, 4 * d)
    tp = -(-t // HG_CHUNK) * HG_CHUNK
    if tp != t:
        z = jnp.pad(z, ((0, 0), (0, tp - t), (0, 0)))
    o, s_new = hgrn2(z, s0, lb, hg_norm_g, t)
    x2 = out_proj1(x2, o[:, :t].reshape(bsz * t, d), w_out_bf)
    return x2.reshape(bsz, t, d), s_new


def kernel(x_prompt, x_sample, cache_nsa_kv, page_table, state_nsa_win, state_conv, state_hgrn, rel_bias_table,
           norm_mix_g, norm_mlp_g, w_mlp_up, w_mlp_down, att_w_in, att_w_out, q_norm_g, k_norm_g, cmp_pe, cmp_w1,
           cmp_b1, cmp_w2, conv_w, conv_b, conv_ln_g, conv_ln_b, hg_w_in, hg_w_out, hg_lb_logits, hg_norm_g):
    p = dict(att_w_in=att_w_in, att_w_out=att_w_out, q_norm_g=q_norm_g, k_norm_g=k_norm_g, cmp_pe=cmp_pe,
             cmp_w1=cmp_w1, cmp_b1=cmp_b1, cmp_w2=cmp_w2, conv_w=conv_w, conv_b=conv_b, conv_ln_g=conv_ln_g,
             conv_ln_b=conv_ln_b)
    bp, tp_, d = x_prompt.shape
    assert bp == 1
    db, ts, _ = x_sample.shape
    w_up = w_mlp_up.astype(BF)
    w_down = w_mlp_down.astype(BF)
    cum = jnp.cumsum(jax.nn.softmax(hg_lb_logits.astype(F32), axis=0), axis=0)

    xp = x_prompt[0]
    kv_p, win_p, conv_p, hg_p = [], [], [], []
    for layer in range(norm_mix_g.shape[0]):
        if layer % 2 == 0:
            e = layer // 2
            w = _even_weights(p, e)
            xp, kv_new, win_new, conv_new = _even_prompt(xp, norm_mix_g[layer], w, rel_bias_table)
            kv_p.append(kv_new)
            win_p.append(win_new)
            conv_p.append(conv_new)
        else:
            o = layer // 2
            x3, s_new = _odd_layer(xp[None], jnp.zeros((1, HG_HEADS, HG_DK, HG_DV), F32), norm_mix_g[layer],
                                   hg_w_in[o].astype(BF), hg_w_out[o].astype(BF), cum[layer] - cum[0],
                                   hg_norm_g[o])
            xp = x3[0]
            hg_p.append(s_new)
        xp = mlp(xp, norm_mlp_g[layer], w_up[layer], w_down[layer])

    cache2 = cache_nsa_kv.reshape(cache_nsa_kv.shape[0], cache_nsa_kv.shape[1], cache_nsa_kv.shape[2], 4 * KVW)
    xs = x_sample
    kv_s, win_s, conv_s, hg_s = [], [], [], []
    for layer in range(norm_mix_g.shape[0]):
        if layer % 2 == 0:
            e = layer // 2
            w = _even_weights(p, e)
            xs, kv_new, win_new, conv_new = _even_decode(xs, norm_mix_g[layer], w, rel_bias_table, cache2[e],
                                                         page_table, state_nsa_win[e], state_conv[e])
            kv_s.append(kv_new)
            win_s.append(win_new)
            conv_s.append(conv_new)
        else:
            o = layer // 2
            xs, s_new = _odd_layer(xs, state_hgrn[o], norm_mix_g[layer], hg_w_in[o].astype(BF),
                                   hg_w_out[o].astype(BF), cum[layer] - cum[0], hg_norm_g[o])
            hg_s.append(s_new)
        xs = mlp(xs.reshape(db * ts, d), norm_mlp_g[layer], w_up[layer], w_down[layer]).reshape(db, ts, d)
    return (xp[None], xs, jnp.stack(kv_p), jnp.stack(kv_s), jnp.stack(win_p), jnp.stack(win_s),
            jnp.stack(conv_p), jnp.stack(conv_s), jnp.stack(hg_p), jnp.stack(hg_s))
```

```python
import functools
import math

import jax
import jax.numpy as jnp
import numpy as np
from jax import lax
from jax.experimental import pallas as pl
from jax.experimental.pallas import tpu as pltpu

D_MODEL = 1024
C_CONV = 512
CONV_WIDTH = 31
DH = 64
HKV = 2
GRP = 4
NH = HKV * GRP
CMP_STRIDE = 16
CMP_LEN = 32
CMP_HIDDEN = 128
SEL_BLOCK = 64
TOP_N = 16
WINDOW = 512
Q_BLOCK = 128
REL_BUCKETS = 32
REL_EXACT = 16
REL_MAX_DIST = 128
HG_DK = 128
HG_HEADS = 8
HG_DV = 128
HG_CHUNK = 64
RMS_EPS = 1e-6
NEG = -1e30
BIG = 1e9
O_Q = 2 * C_CONV
O_KV = O_Q + NH * DH
O_GATE = O_KV + 3 * 2 * HKV * DH
ATT_IN_COLS = O_GATE + 3 * NH
ATT_IN_PAD = 2432
KVW = HKV * DH

VMEM_LIMIT = 56 * 1024 * 1024
BF = jnp.bfloat16
F32 = jnp.float32


def _cp(*sem):
    return pltpu.CompilerParams(dimension_semantics=sem, vmem_limit_bytes=VMEM_LIMIT)


def _dot(a, b):
    return jnp.dot(a, b, preferred_element_type=F32)


def _dot_nt(a, b):
    return lax.dot_general(a, b, (((1,), (1,)), ((), ())), preferred_element_type=F32)


def _dot_tn(a, b):
    return lax.dot_general(a, b, (((0,), (0,)), ((), ())), preferred_element_type=F32)


def _rms(x, g):
    return x * lax.rsqrt(jnp.mean(x * x, axis=-1, keepdims=True) + RMS_EPS) * g


def _norm_matmul_kernel(x_ref, g_ref, w_ref, o_ref):
    hn = _rms(x_ref[...], g_ref[...]).astype(BF)
    o_ref[...] = _dot(hn, w_ref[...])


def norm_matmul(x, g, w_bf, tn=None):
    m, d = x.shape
    n = w_bf.shape[1]
    tm = min(m, 512)
    tn = n if tn is None else tn
    return pl.pallas_call(
        _norm_matmul_kernel,
        grid=(m // tm, n // tn),
        in_specs=[pl.BlockSpec((tm, d), lambda i, j: (i, 0)),
                  pl.BlockSpec((1, d), lambda i, j: (0, 0)),
                  pl.BlockSpec((d, tn), lambda i, j: (0, j))],
        out_specs=pl.BlockSpec((tm, tn), lambda i, j: (i, j)),
        out_shape=jax.ShapeDtypeStruct((m, n), F32),
        compiler_params=_cp("parallel", "arbitrary"),
        name="norm_matmul",
    )(x, g.reshape(1, d), w_bf)


def _out_proj_kernel(r_ref, a1_ref, a2_ref, w1_ref, w2_ref, o_ref):
    o_ref[...] = (r_ref[...] + _dot(a1_ref[...].astype(BF), w1_ref[...])
                  + _dot(a2_ref[...].astype(BF), w2_ref[...]))


def out_proj2(res, a1, a2, w1_bf, w2_bf):
    m, d = res.shape
    k1, k2 = a1.shape[1], a2.shape[1]
    tm = min(m, 512)
    return pl.pallas_call(
        _out_proj_kernel,
        grid=(m // tm,),
        in_specs=[pl.BlockSpec((tm, d), lambda i: (i, 0)),
                  pl.BlockSpec((tm, k1), lambda i: (i, 0)),
                  pl.BlockSpec((tm, k2), lambda i: (i, 0)),
                  pl.BlockSpec((k1, d), lambda i: (0, 0)),
                  pl.BlockSpec((k2, d), lambda i: (0, 0))],
        out_specs=pl.BlockSpec((tm, d), lambda i: (i, 0)),
        out_shape=jax.ShapeDtypeStruct((m, d), F32),
        compiler_params=_cp("parallel"),
        name="out_proj2",
    )(res, a1, a2, w1_bf, w2_bf)


def _out_proj1_kernel(r_ref, a_ref, w_ref, o_ref):
    o_ref[...] = r_ref[...] + _dot(a_ref[...].astype(BF), w_ref[...])


def out_proj1(res, a, w_bf):
    m, d = res.shape
    k = a.shape[1]
    tm = min(m, 512)
    return pl.pallas_call(
        _out_proj1_kernel,
        grid=(m // tm,),
        in_specs=[pl.BlockSpec((tm, d), lambda i: (i, 0)),
                  pl.BlockSpec((tm, k), lambda i: (i, 0)),
                  pl.BlockSpec((k, d), lambda i: (0, 0))],
        out_specs=pl.BlockSpec((tm, d), lambda i: (i, 0)),
        out_shape=jax.ShapeDtypeStruct((m, d), F32),
        compiler_params=_cp("parallel"),
        name="out_proj1",
    )(res, a, w_bf)


def _mlp_kernel(x_ref, g_ref, wu_ref, wd_ref, o_ref, hn_ref, acc_ref):
    j = pl.program_id(1)

    @pl.when(j == 0)
    def _():
        hn_ref[...] = _rms(x_ref[...], g_ref[...]).astype(BF)
        acc_ref[...] = x_ref[...]

    hid = jnp.maximum(_dot(hn_ref[...], wu_ref[...]), 0.0)
    acc_ref[...] += _dot((hid * hid).astype(BF), wd_ref[...])

    @pl.when(j == pl.num_programs(1) - 1)
    def _():
        o_ref[...] = acc_ref[...]


def mlp(x, g, wu_bf, wd_bf):
    m, d = x.shape
    hdim = wu_bf.shape[1]
    tm = min(m, 1024)
    th = 1024
    return pl.pallas_call(
        _mlp_kernel,
        grid=(m // tm, hdim // th),
        in_specs=[pl.BlockSpec((tm, d), lambda i, j: (i, 0)),
                  pl.BlockSpec((1, d), lambda i, j: (0, 0)),
                  pl.BlockSpec((d, th), lambda i, j: (0, j)),
                  pl.BlockSpec((th, d), lambda i, j: (j, 0))],
        out_specs=pl.BlockSpec((tm, d), lambda i, j: (i, 0)),
        out_shape=jax.ShapeDtypeStruct((m, d), F32),
        scratch_shapes=[pltpu.VMEM((tm, d), BF), pltpu.VMEM((tm, d), F32)],
        compiler_params=_cp("parallel", "arbitrary"),
        name="mlp",
    )(x, g.reshape(1, d), wu_bf, wd_bf)


CONV_HALO = 32
CONV_PAD = CONV_HALO - (CONV_WIDTH - 1)


def _conv_kernel(u_ref, st_ref, w_ref, b_ref, lg_ref, lb_ref, y_ref, new_ref, xin_ref, *, tt):
    t = pl.program_id(1)

    @pl.when(t == 0)
    def _():
        xin_ref[0:CONV_HALO, :] = st_ref[0]

    a = u_ref[0, :, 0:C_CONV]
    gt = u_ref[0, :, C_CONV:2 * C_CONV]
    xin_ref[CONV_HALO:CONV_HALO + tt, :] = a * jax.nn.sigmoid(gt)

    cols = []
    for c in range(C_CONV // 128):
        cs = slice(c * 128, (c + 1) * 128)
        acc = jnp.zeros((tt, 128), F32) + b_ref[:, cs]
        for k in range(CONV_WIDTH):
            acc = acc + w_ref[k:k + 1, cs] * xin_ref[CONV_PAD + k:CONV_PAD + k + tt, cs]
        cols.append(acc)
    y = jnp.concatenate(cols, axis=1)
    mu = jnp.mean(y, axis=-1, keepdims=True)
    yc = y - mu
    var = jnp.mean(yc * yc, axis=-1, keepdims=True)
    z = yc * lax.rsqrt(var + RMS_EPS) * lg_ref[...] + lb_ref[...]
    y_ref[0] = z * jax.nn.sigmoid(z)

    @pl.when(t == pl.num_programs(1) - 1)
    def _():
        new_ref[0] = xin_ref[tt + CONV_PAD:tt + CONV_HALO, :]

    if tt >= CONV_HALO:
        @pl.when(t < pl.num_programs(1) - 1)
        def _():
            xin_ref[0:CONV_HALO, :] = xin_ref[tt:tt + CONV_HALO, :]


def conformer_conv(proj, state, w, b, ln_g, ln_b):
    bsz, t, _ = proj.shape
    tt = min(t, 256)
    assert t % tt == 0 and (t == tt or tt >= CONV_HALO)
    st = jnp.pad(state, ((0, 0), (CONV_PAD, 0), (0, 0)))
    row = lambda v: v.reshape(1, C_CONV)
    return pl.pallas_call(
        functools.partial(_conv_kernel, tt=tt),
        grid=(bsz, t // tt),
        in_specs=[pl.BlockSpec((1, tt, 2 * C_CONV), lambda i, j: (i, j, 0)),
                  pl.BlockSpec((1, CONV_HALO, C_CONV), lambda i, j: (i, 0, 0)),
                  pl.BlockSpec((CONV_WIDTH, C_CONV), lambda i, j: (0, 0)),
                  pl.BlockSpec((1, C_CONV), lambda i, j: (0, 0)),
                  pl.BlockSpec((1, C_CONV), lambda i, j: (0, 0)),
                  pl.BlockSpec((1, C_CONV), lambda i, j: (0, 0))],
        out_specs=[pl.BlockSpec((1, tt, C_CONV), lambda i, j: (i, j, 0)),
                   pl.BlockSpec((1, CONV_WIDTH - 1, C_CONV), lambda i, j: (i, 0, 0))],
        out_shape=[jax.ShapeDtypeStruct((bsz, t, C_CONV), F32),
                   jax.ShapeDtypeStruct((bsz, CONV_WIDTH - 1, C_CONV), F32)],
        scratch_shapes=[pltpu.VMEM((CONV_HALO + tt, C_CONV), F32)],
        compiler_params=_cp("parallel", "arbitrary"),
        name="conformer_conv",
    )(proj, st, w, row(b), row(ln_g), row(ln_b))


HG_LEVELS = (32, 16, 8, 4, 2, 1)


def _hgrn_tables():
    c = HG_CHUNK
    idx = np.arange(c)
    mats = [(idx[None, :] <= idx[:, None]), (idx[None, :] > idx[:, None])]
    masks = []
    for h in HG_LEVELS:
        blk = idx // (2 * h)
        upper = (idx % (2 * h)) >= h
        piv = blk * 2 * h + h - 1
        mats.append(upper[:, None] & (idx[None, :] > piv[:, None]) & (idx[None, :] <= idx[:, None]))
        mats.append((~upper)[:, None] & (idx[None, :] > idx[:, None]) & (idx[None, :] <= piv[:, None]))
        masks.append((blk[:, None] == blk[None, :]) & upper[:, None] & (~upper)[None, :])
    masks.append(np.eye(c, dtype=bool))
    return (np.concatenate(mats, axis=0).astype(np.float32),
            np.stack(masks).astype(np.float32))


HG_HEADS_PER_STEP = 4


def _hgrn_kernel(q_ref, fz_ref, v_ref, g_ref, lb_ref, ng_ref, s0_ref, tab_ref, msk_ref,
                 o_ref, sn_ref, st_ref, *, t_valid, nh):
    ci = pl.program_id(2)
    c = HG_CHUNK

    @pl.when(ci == 0)
    def _():
        for hh in range(nh):
            st_ref[hh] = s0_ref[0, hh].T

    row = ci * c + lax.broadcasted_iota(jnp.int32, (c, 1), 0)
    live = row < t_valid
    tab = tab_ref[...]
    for hh in range(nh):
        cs = slice(hh * HG_DK, (hh + 1) * HG_DK)
        lb = lb_ref[:, cs]
        f = lb + (1.0 - lb) * jax.nn.sigmoid(fz_ref[0, :, cs])
        lf = jnp.where(live, jnp.log(f), 0.0)
        k = jnp.where(live, 1.0 - f, 0.0)
        q = q_ref[0, :, cs]
        v = v_ref[0, :, cs]

        hi = lf.astype(BF)
        ex2 = _dot(tab, jnp.concatenate([hi, (lf - hi.astype(F32)).astype(BF)], axis=1))
        ex = ex2[:, 0:HG_DK] + ex2[:, HG_DK:]

        attn = jnp.zeros((c, c), F32)
        for li in range(len(HG_LEVELS)):
            qd = (q * jnp.exp(ex[(2 + 2 * li) * c:(3 + 2 * li) * c])).astype(BF)
            ke = (k * jnp.exp(ex[(3 + 2 * li) * c:(4 + 2 * li) * c])).astype(BF)
            attn = attn + msk_ref[li] * _dot_nt(qd, ke)
        attn = attn + msk_ref[len(HG_LEVELS)] * _dot_nt(q.astype(BF), k.astype(BF))

        st = st_ref[hh]
        qb = (q * jnp.exp(ex[0:c])).astype(BF)
        o = _dot(attn.astype(BF), v.astype(BF)) + _dot_nt(qb, st.astype(BF))
        ke = (k * jnp.exp(ex[c:2 * c])).astype(BF)
        decay = jnp.exp(ex[c - 1:c])
        st_ref[hh] = st * decay + _dot(v.T.astype(BF), ke)

        gate = g_ref[0, :, cs]
        o_ref[0, :, cs] = _rms(o, ng_ref[...]) * (gate * jax.nn.sigmoid(gate))

    @pl.when(ci == pl.num_programs(2) - 1)
    def _():
        for hh in range(nh):
            sn_ref[0, hh] = st_ref[hh].T


def hgrn2(z, s0, lb, norm_g, t_valid):
    bsz, tp, _ = z.shape
    c = HG_CHUNK
    nc = tp // c
    tab, msk = _hgrn_tables()
    nh = HG_HEADS_PER_STEP
    ng = HG_HEADS // nh
    w = nh * HG_DK
    blk = lambda off: pl.BlockSpec((1, c, w), lambda b, h, i: (b, i, off + h))
    return pl.pallas_call(
        functools.partial(_hgrn_kernel, t_valid=t_valid, nh=nh),
        grid=(bsz, ng, nc),
        in_specs=[blk(0), blk(ng), blk(2 * ng), blk(3 * ng),
                  pl.BlockSpec((1, w), lambda b, h, i: (0, h)),
                  pl.BlockSpec((1, HG_DV), lambda b, h, i: (0, 0)),
                  pl.BlockSpec((1, nh, HG_DK, HG_DV), lambda b, h, i: (b, h, 0, 0)),
                  pl.BlockSpec(tab.shape, lambda b, h, i: (0, 0)),
                  pl.BlockSpec(msk.shape, lambda b, h, i: (0, 0, 0))],
        out_specs=[pl.BlockSpec((1, c, w), lambda b, h, i: (b, i, h)),
                   pl.BlockSpec((1, nh, HG_DK, HG_DV), lambda b, h, i: (b, h, 0, 0))],
        out_shape=[jax.ShapeDtypeStruct((bsz, tp, D_MODEL), F32),
                   jax.ShapeDtypeStruct((bsz, HG_HEADS, HG_DK, HG_DV), F32)],
        scratch_shapes=[pltpu.VMEM((nh, HG_DV, HG_DK), F32)],
        compiler_params=_cp("parallel", "parallel", "arbitrary"),
        name="hgrn2",
    )(z, z, z, z, lb.reshape(1, D_MODEL), norm_g.reshape(1, HG_DV), s0,
      jnp.asarray(tab, BF), jnp.asarray(msk))


def _pair_norm(seg, g):
    lo = lax.broadcasted_iota(jnp.int32, seg.shape, 1) < DH
    sq = seg * seg
    s0 = jnp.sum(jnp.where(lo, sq, 0.0), axis=-1, keepdims=True)
    s1 = jnp.sum(jnp.where(lo, 0.0, sq), axis=-1, keepdims=True)
    inv = jnp.where(lo, lax.rsqrt(s0 * (1.0 / DH) + RMS_EPS), lax.rsqrt(s1 * (1.0 / DH) + RMS_EPS))
    return seg * inv * g


def _kv_post_kernel(p_ref, kg_ref, kv_ref, win_ref):
    x = p_ref[...]
    kv_ref[:, 0:2 * KVW] = x[:, 0:2 * KVW]
    kv_ref[:, 2 * KVW:3 * KVW] = _pair_norm(x[:, 2 * KVW:3 * KVW], kg_ref[1:2, :])
    kv_ref[:, 3 * KVW:4 * KVW] = x[:, 3 * KVW:4 * KVW]
    win_ref[:, 0:KVW] = _pair_norm(x[:, 4 * KVW:5 * KVW], kg_ref[2:3, :])
    win_ref[:, KVW:2 * KVW] = x[:, 5 * KVW:6 * KVW]


def kv_post(proj, k_g):
    m = proj.shape[0]
    tm = min(m, 1024)
    kvcols = 6 * KVW
    assert O_KV % kvcols == 0
    return pl.pallas_call(
        _kv_post_kernel,
        grid=(m // tm,),
        in_specs=[pl.BlockSpec((tm, kvcols), lambda i: (i, O_KV // kvcols)),
                  pl.BlockSpec((3, KVW), lambda i: (0, 0))],
        out_specs=[pl.BlockSpec((tm, 4 * KVW), lambda i: (i, 0)),
                   pl.BlockSpec((tm, 2 * KVW), lambda i: (i, 0))],
        out_shape=[jax.ShapeDtypeStruct((m, 4 * KVW), F32), jax.ShapeDtypeStruct((m, 2 * KVW), F32)],
        compiler_params=_cp("parallel"),
        name="kv_post",
    )(proj, jnp.concatenate([k_g, k_g], axis=1))


def _q_norm_kernel(q_ref, g_ref, o_ref):
    for c in range(NH // 2):
        cs = slice(c * 2 * DH, (c + 1) * 2 * DH)
        o_ref[:, cs] = _pair_norm(q_ref[:, cs], g_ref[...]) * (DH ** -0.5)


def q_norm(proj, q_g):
    m = proj.shape[0]
    tm = min(m, 1024)
    return pl.pallas_call(
        _q_norm_kernel,
        grid=(m // tm,),
        in_specs=[pl.BlockSpec((tm, NH * DH), lambda i: (i, O_Q // (NH * DH))),
                  pl.BlockSpec((1, 2 * DH), lambda i: (0, 0))],
        out_specs=pl.BlockSpec((tm, NH * DH), lambda i: (i, 0)),
        out_shape=jax.ShapeDtypeStruct((m, NH * DH), F32),
        compiler_params=_cp("parallel"),
        name="q_norm",
    )(proj, jnp.concatenate([q_g, q_g]).reshape(1, 2 * DH))


def _compress_kernel(*refs, n_prefetch, n_src, rows, nch):
    refs = refs[n_prefetch:]
    srcs = (refs[:n_src], refs[n_src:2 * n_src])
    pef_ref, pes_ref, w1f_ref, w1s_ref, b1_ref, w2_ref, kg_ref, kc_ref, vc_ref, hf_ref, hs_ref = refs[2 * n_src:]
    j = pl.program_id(1)
    cpr = rows // CMP_STRIDE
    m = cpr * n_src

    @pl.when(j == 0)
    def _():
        hs_ref[:, nch:nch + 8, :] = jnp.zeros((2, 8, 2 * CMP_HIDDEN), F32)

    for kind in range(2):
        accf = jnp.zeros((m, 2 * CMP_HIDDEN), F32)
        accs = jnp.zeros((m, 2 * CMP_HIDDEN), F32)
        for s in range(CMP_STRIDE):
            xs = jnp.concatenate(
                [r[0, pl.ds(s, cpr, stride=CMP_STRIDE), :] for r in srcs[kind]], axis=0)
            accf = accf + _dot((xs + pef_ref[kind, s:s + 1, :]).astype(BF), w1f_ref[kind, s])
            accs = accs + _dot((xs + pes_ref[kind, s:s + 1, :]).astype(BF), w1s_ref[kind, s])
        row0 = pl.multiple_of(j * m, 8)
        hf_ref[kind, pl.ds(row0, m), :] = accf
        hs_ref[kind, pl.ds(row0, m), :] = accs

    @pl.when(j == pl.num_programs(1) - 1)
    def _():
        for kind in range(2):
            hid = jax.nn.gelu(hf_ref[kind, 0:nch, :] + hs_ref[kind, 1:nch + 1, :] + b1_ref[kind])
            out = _dot(hid.astype(BF), w2_ref[kind])
            if kind == 0:
                kc_ref[0] = _pair_norm(out, kg_ref[...]).astype(BF)
            else:
                vc_ref[0] = out.astype(BF)


def _compress_weights(pe, w1, b1, w2, kg0):
    eye = jnp.eye(HKV, dtype=F32)
    bd = lambda w: jnp.einsum('ab,ksdf->ksadbf', eye, w).reshape(2, CMP_STRIDE, KVW, 2 * CMP_HIDDEN).astype(BF)
    tile2 = lambda a: jnp.concatenate([a, a], axis=-1)
    w2bd = jnp.einsum('ab,kfd->kafbd', eye, w2).reshape(2, 2 * CMP_HIDDEN, KVW).astype(BF)
    return (tile2(pe[:, :CMP_STRIDE]), tile2(pe[:, CMP_STRIDE:]), bd(w1[:, :CMP_STRIDE]), bd(w1[:, CMP_STRIDE:]),
            tile2(b1).reshape(2, 1, 2 * CMP_HIDDEN), w2bd, tile2(kg0).reshape(1, KVW))


def _compress_common(n_prefetch, n_src, rows, nch, bsz, weights):
    wspecs = [pl.BlockSpec(w.shape, functools.partial(lambda nd, *a: (0,) * nd, w.ndim)) for w in weights]
    out_specs = [pl.BlockSpec((1, nch, KVW), lambda b, j, *a: (b, 0, 0))] * 2
    out_shape = [jax.ShapeDtypeStruct((bsz, nch, KVW), BF)] * 2
    scratch = [pltpu.VMEM((2, nch + 8, 2 * CMP_HIDDEN), F32)] * 2
    kern = functools.partial(_compress_kernel, n_prefetch=n_prefetch, n_src=n_src, rows=rows, nch=nch)
    return kern, wspecs, out_specs, out_shape, scratch


def compress_dense(proj3, weights):
    bsz, t, _ = proj3.shape
    rows = min(t, 2048)
    nch = t // CMP_STRIDE
    assert O_KV % KVW == 0
    kern, wspecs, out_specs, out_shape, scratch = _compress_common(0, 1, rows, nch, bsz, weights)
    src = lambda kind: pl.BlockSpec((1, rows, KVW), lambda b, j: (b, j, O_KV // KVW + kind))
    return pl.pallas_call(
        kern, grid=(bsz, t // rows),
        in_specs=[src(0), src(1)] + wspecs,
        out_specs=out_specs, out_shape=out_shape, scratch_shapes=scratch,
        compiler_params=_cp("parallel", "arbitrary"), name="compress_dense",
    )(proj3, proj3, *weights)


CMP_PAGES = 16


def compress_paged(cache2, page_table, weights):
    bsz, n_pages = page_table.shape
    page = cache2.shape[1]
    npg = min(CMP_PAGES, n_pages)
    nch = n_pages * page // CMP_STRIDE
    src_specs = [pl.BlockSpec((1, page, KVW),
                              functools.partial(lambda r, kind, b, j, pt: (pt[b, j * npg + r], 0, kind), r, kind))
                 for kind in range(2) for r in range(npg)]
    kern, wspecs, out_specs, out_shape, scratch = _compress_common(1, npg, page, nch, bsz, weights)
    return pl.pallas_call(
        kern,
        grid_spec=pltpu.PrefetchScalarGridSpec(
            num_scalar_prefetch=1, grid=(bsz, n_pages // npg), in_specs=src_specs + wspecs,
            out_specs=out_specs, scratch_shapes=scratch),
        out_shape=out_shape, compiler_params=_cp("parallel", "arbitrary"), name="compress_paged",
    )(page_table, *([cache2] * (2 * npg)), *weights)


NS_PAD = 256
SEL_HALF = 128
LOG2E = math.log2(math.e)
PEN = 30000.0
KT = 256
HALF_TILES = SEL_HALF * SEL_BLOCK // KT
CBAND = 24


def _t5_bucket(dist):
    n = jnp.maximum(dist, 0)
    large = REL_EXACT + (jnp.log(jnp.maximum(n, 1).astype(F32) / REL_EXACT)
                         / math.log(REL_MAX_DIST / REL_EXACT) * (REL_BUCKETS - REL_EXACT)).astype(jnp.int32)
    return jnp.where(n < REL_EXACT, n, jnp.minimum(large, REL_BUCKETS - 1))


def _selection_matrix(nc, ns_pad, nc_pad):
    ratio, span = SEL_BLOCK // CMP_STRIDE, CMP_LEN // CMP_STRIDE
    a = np.zeros((ns_pad, nc_pad), np.float32)
    for j in range(ns_pad):
        for mm in range(ratio):
            for nn in range(span):
                n = ratio * j + mm - nn
                if 0 <= n < nc:
                    a[j, n] += 1.0
    return a


def _split3(x):
    hi = x.astype(BF)
    r1 = x - hi.astype(F32)
    mid = r1.astype(BF)
    lo = (r1 - mid.astype(F32)).astype(BF)
    return hi, mid, lo


def _top_blocks(s, qp, n_rounds):
    j = lax.broadcasted_iota(jnp.int32, s.shape, 0)
    cur = qp // SEL_BLOCK
    forced = (j == 0) | (j == cur) | (j == cur - 1)
    valid = j * SEL_BLOCK <= qp
    s = jnp.where(forced, BIG, s)
    s = jnp.where(valid, s, -BIG)
    sel = jnp.zeros(s.shape, F32)
    for _ in range(n_rounds):
        mx = jnp.max(s, axis=0, keepdims=True)
        jm = jnp.min(jnp.where(s == mx, j, 2 * NS_PAD), axis=0, keepdims=True)
        hit = j == jm
        sel = jnp.where(hit, 1.0, sel)
        s = jnp.where(hit, -3e38, s)
    return sel


def _masked_softmax_rows(s, axis):
    m = jnp.max(s, axis=axis, keepdims=True)
    e = jnp.exp(s - m)
    den = jnp.maximum(jnp.sum(e, axis=axis, keepdims=True), 1e-30)
    return e * jnp.where(m > 0.5 * NEG, 1.0 / den, 0.0)


def _nsa_prompt_kernel(q_ref, gate_ref, qg_ref, kc_ref, vc_ref, amat_ref, kaug_ref, vselt_ref, kwin_ref,
                       vwin_ref, bc_ref, bs_ref, bw_ref, o_ref, lg_ref, acc_ref, qa_ref, qh_ref, sa_ref, sb_ref,
                       m_ref, mx_ref, *, ncp, n_top):
    i = pl.program_id(0)
    qb = Q_BLOCK
    rows = GRP * qb
    start = i * qb
    qall = q_ref[...]
    gates = jax.nn.sigmoid(gate_ref[...])
    lane = lax.broadcasted_iota(jnp.int32, (qb, 2 * DH), 1)
    qpos_row = start + lax.broadcasted_iota(jnp.int32, (1, qb), 1)
    t_last = start // KT
    n_far = jnp.maximum(t_last - 1, 0)

    @pl.when(i == 0)
    def _():
        lg_ref[0:16, :] = jnp.zeros((16, rows), F32)

    o_cs = []
    for h in range(HKV):
        parts = []
        for g in range(GRP):
            hd = h * GRP + g
            slab = qall[:, (hd // 2) * 2 * DH:(hd // 2 + 1) * 2 * DH]
            mine = (lane >= DH) if hd % 2 else (lane < DH)
            ss = jnp.sum(jnp.where(mine, slab * slab, 0.0), axis=-1, keepdims=True)
            xn = jnp.where(mine, slab * lax.rsqrt(ss * (1.0 / DH) + RMS_EPS) * qg_ref[...] * (DH ** -0.5), 0.0)
            if hd % 2 != h:
                xn = pltpu.roll(xn, DH, 1)
            parts.append(xn)
        xf = jnp.concatenate(parts, axis=0)
        qh = xf.astype(BF)

        lg_ref[16:16 + ncp, :] = _dot_nt(kc_ref[...], qh)
        band0 = pl.multiple_of(i * 8, 8)
        lg_ref[pl.ds(band0, CBAND), :] = lg_ref[pl.ds(band0, CBAND), :] + bc_ref[h]
        tok = lax.broadcasted_iota(jnp.int32, (ncp, 1), 0)
        p_c = _masked_softmax_rows(jnp.where(tok < 8 * i + 8, lg_ref[16:16 + ncp, :], NEG), 0)
        o_c = _dot_tn(p_c.astype(BF), vc_ref[...])[:, h * DH:(h + 1) * DH]

        imp = p_c[:, 0:qb]
        for g in range(1, GRP):
            imp = imp + p_c[:, g * qb:(g + 1) * qb]
        hi, mid, lo = _split3(imp)
        amat = amat_ref[...]
        score = _dot(amat, hi) + _dot(amat, mid) + _dot(amat, lo)
        sel = _top_blocks(score, qpos_row, n_top)
        pen = ((sel.T - 1.0) * PEN).astype(BF)
        qh2 = (xf * LOG2E).astype(BF)
        qa_ref[h, 0] = jnp.concatenate([jnp.concatenate([pen[:, 0:SEL_HALF]] * GRP, axis=0), qh2], axis=1)
        qa_ref[h, 1] = jnp.concatenate([jnp.concatenate([pen[:, SEL_HALF:]] * GRP, axis=0), qh2], axis=1)
        qh_ref[h] = qh
        o_cs.append(o_c)

    def scores(h, t):
        qa = jnp.where(t < HALF_TILES, qa_ref[h, 0], qa_ref[h, 1])
        return _dot_nt(kaug_ref[t + 1], qa)

    def absorb(h, s, mx, t, m):
        m2 = jnp.maximum(m, mx)
        p = jnp.exp2(s - m2)
        acc_ref[h] = jnp.exp2(m - m2) * acc_ref[h] + _dot(vselt_ref[h, t + 1], p.astype(BF))
        return m2

    col_max = lambda s: jnp.max(s, axis=0, keepdims=True)

    def pair_body(k, carry):
        ms, mxa = list(carry[:HKV]), list(carry[HKV:])
        mxb = []
        for h in range(HKV):
            s1 = scores(h, 2 * k + 1)
            sb_ref[h] = s1
            mxb.append(col_max(s1))
            ms[h] = absorb(h, sa_ref[h], mxa[h], 2 * k, ms[h])
        for h in range(HKV):
            s2 = scores(h, 2 * k + 2)
            ms[h] = absorb(h, sb_ref[h], mxb[h], 2 * k + 1, ms[h])
            sa_ref[h] = s2
            mxa[h] = col_max(s2)
        return tuple(ms) + tuple(mxa)

    acc_ref[...] = jnp.zeros((HKV, 2 * DH, rows), F32)
    mxa = []
    for h in range(HKV):
        s0 = scores(h, 0)
        sa_ref[h] = s0
        mxa.append(col_max(s0))
    carry = tuple(jnp.full((1, rows), NEG, F32) for _ in range(HKV)) + tuple(mxa)
    carry = lax.fori_loop(0, n_far // 2, pair_body, carry)
    for h in range(HKV):
        m_ref[h] = carry[h]
        mx_ref[h] = carry[HKV + h]

    @pl.when(n_far % 2 == 1)
    def _():
        for h in range(HKV):
            m_ref[h] = absorb(h, sa_ref[h], mx_ref[h], n_far - 1, m_ref[h])

    kiota = lax.broadcasted_iota(jnp.int32, (KT, 1), 0)
    ms = [m_ref[h] for h in range(HKV)]
    for u in range(2):
        t = t_last - 1 + u
        for h in range(HKV):
            s = jnp.where(t * KT + kiota >= 0, scores(h, t) + bs_ref[i % 2, h, u], NEG)
            ms[h] = absorb(h, s, col_max(s), t, ms[h])

    for h in range(HKV):
        o_st = acc_ref[h, 0:DH, :] / jnp.maximum(acc_ref[h, DH:DH + 1, :], 1e-30)
        o_s = jnp.concatenate([o_st[:, g * qb:(g + 1) * qb].T for g in range(GRP)], axis=0)
        o_c = o_cs[h]
        qh = qh_ref[h]

        r = pl.multiple_of(start, qb)
        s = _dot_nt(qh, kwin_ref[pl.ds(r, WINDOW + qb), :]) + bw_ref[h]
        kpos = start - WINDOW + lax.broadcasted_iota(jnp.int32, (1, WINDOW + qb), 1)
        p_w = _masked_softmax_rows(jnp.where(kpos >= 0, s, NEG), 1)
        o_w = _dot(p_w.astype(BF), vwin_ref[pl.ds(r, WINDOW + qb), :])[:, h * DH:(h + 1) * DH]

        for g in range(GRP):
            c = (h * GRP + g) * 3
            rs = slice(g * qb, (g + 1) * qb)
            o_ref[:, (h * GRP + g) * DH:(h * GRP + g + 1) * DH] = (
                gates[:, c:c + 1] * o_c[rs] + gates[:, c + 1:c + 2] * o_s[rs] + gates[:, c + 2:c + 3] * o_w[rs])


def _bias_tables(rel_table):
    tab = rel_table.astype(F32)
    far = tab[REL_BUCKETS - 1]
    qb = Q_BLOCK

    def toeplitz(off, nk, dmax, shift):
        d = off - (nk - 1) + jnp.arange(nk + qb - 1, dtype=jnp.int32)
        w = jnp.where(((d >= 0) & (d <= dmax))[None, :], (tab[_t5_bucket(d)] - (far if shift else 0.0)).T, NEG)
        p = w.shape[1]
        hank = jnp.tile(w, (1, nk + 1))[:, :nk * (p + 1)].reshape(NH, nk, p + 1)[:, :, :qb]
        return hank[:, ::-1, :]

    def lanes_gq(b):
        return b.reshape(HKV, GRP, b.shape[1], qb).transpose(0, 2, 1, 3).reshape(HKV, b.shape[1], GRP * qb)

    big = 1 << 30
    bc = lanes_gq(toeplitz(16 * CMP_STRIDE - (CMP_LEN - 1), CBAND * CMP_STRIDE, big, True)[:, ::CMP_STRIDE, :])
    bs = jnp.stack([lanes_gq(toeplitz(off, 2 * KT, big, True)).reshape(HKV, 2, KT, GRP * qb)
                    for off in (KT, KT + qb)]) * LOG2E
    bw = lanes_gq(toeplitz(WINDOW, WINDOW + qb, WINDOW, False)).transpose(0, 2, 1)
    return bc, bs, bw


def nsa_prompt(proj, kc, vc, ksel, vsel, kwin, vwin, q_g, rel_table):
    t = proj.shape[0]
    nb = t // Q_BLOCK
    nch = kc.shape[0]
    ncp = -(-nch // 128) * 128
    assert t // SEL_BLOCK <= NS_PAD and t % KT == 0
    n_top = min(TOP_N, t // SEL_BLOCK)
    kcp = jnp.pad(kc, ((0, ncp - nch), (0, 0)))
    vcp = jnp.pad(vc, ((0, ncp - nch), (0, 0)))
    amat = jnp.asarray(_selection_matrix(nch - 1, NS_PAD, ncp), BF)
    onehot = ((jnp.arange(t, dtype=jnp.int32)[:, None] // SEL_BLOCK) % SEL_HALF
              == jnp.arange(SEL_HALF, dtype=jnp.int32)[None, :]).astype(BF)
    kaug = jnp.pad(jnp.concatenate([onehot, ksel.astype(BF)], axis=1), ((KT, 0), (0, 0))).reshape(-1, KT, 2 * KVW)
    vt = jnp.pad(vsel.astype(BF), ((KT, 0), (0, 0))).reshape(-1, KT, HKV, DH).transpose(2, 0, 3, 1)
    vselt = jnp.concatenate([vt, jnp.ones(vt.shape[:2] + (1, KT), BF), jnp.zeros(vt.shape[:2] + (DH - 1, KT), BF)],
                            axis=2)
    rows = GRP * Q_BLOCK
    kwinp = jnp.pad(kwin.astype(BF), ((WINDOW, 0), (0, 0)))
    vwinp = jnp.pad(vwin.astype(BF), ((WINDOW, 0), (0, 0)))
    bc, bs, bw = _bias_tables(rel_table)
    whole = pl.BlockSpec(memory_space=pltpu.VMEM)
    return pl.pallas_call(
        functools.partial(_nsa_prompt_kernel, ncp=ncp, n_top=n_top),
        grid=(nb,),
        in_specs=[pl.BlockSpec((Q_BLOCK, NH * DH), lambda i: (i, O_Q // (NH * DH))),
                  pl.BlockSpec((Q_BLOCK, 128), lambda i: (i, O_GATE // 128)),
                  whole, whole, whole, whole, whole, whole, whole, whole, whole, whole, whole],
        out_specs=pl.BlockSpec((Q_BLOCK, NH * DH), lambda i: (i, 0)),
        out_shape=jax.ShapeDtypeStruct((t, NH * DH), F32),
        scratch_shapes=[pltpu.VMEM((16 + ncp, GRP * Q_BLOCK), F32), pltpu.VMEM((HKV, 2 * DH, rows), F32),
                        pltpu.VMEM((HKV, 2, rows, SEL_HALF + KVW), BF), pltpu.VMEM((HKV, rows, KVW), BF),
                        pltpu.VMEM((HKV, KT, rows), F32), pltpu.VMEM((HKV, KT, rows), F32),
                        pltpu.VMEM((HKV, 1, rows), F32), pltpu.VMEM((HKV, 1, rows), F32)],
        compiler_params=_cp("arbitrary"),
        name="nsa_prompt",
    )(proj, proj, jnp.concatenate([q_g, q_g]).reshape(1, 2 * DH), kcp, vcp, amat, kaug, vselt, kwinp, vwinp,
      bc, bs, bw)


def _pick_head(x, lane_h):
    return jnp.where(lane_h == 0, x[0:DH], x[DH:2 * DH])


def _nsa_dec_pre_kernel(qbd_ref, kc_ref, vc_ref, amat_ref, bc_ref, win_ref, wnew_ref, bwa_ref, bwb_ref,
                        pen_ref, oc_ref, ow_ref, *, ts, pos0, n_top):
    r_all = GRP * HKV * ts
    qbd = qbd_ref[0]
    lane_h = (lax.broadcasted_iota(jnp.int32, (1, r_all), 1) // ts) % HKV

    p_c = _masked_softmax_rows(_dot(kc_ref[0], qbd) + bc_ref[...], 0)
    oc_ref[0] = _pick_head(_dot_tn(vc_ref[0], p_c.astype(BF)), lane_h)

    hi, mid, lo = _split3(p_c)
    amat = amat_ref[...]
    sc = _dot(amat, hi) + _dot(amat, mid) + _dot(amat, lo)
    w8 = HKV * ts
    score = sc[:, 0:w8]
    for g in range(1, GRP):
        score = score + sc[:, g * w8:(g + 1) * w8]
    qp = pos0 + lax.broadcasted_iota(jnp.int32, (1, w8), 1) % ts
    sel = _top_blocks(score, qp, n_top)
    pen_ref[0] = (jnp.concatenate([sel] * GRP, axis=1) - 1.0) * PEN

    win = win_ref[0]
    wnew = wnew_ref[0]
    s_a = _dot(win[:, 0:KVW].astype(BF), qbd) + bwa_ref[...]
    s_b = _dot(wnew[:, 0:KVW].astype(BF), qbd) + bwb_ref[...]
    m = jnp.maximum(jnp.max(s_a, axis=0, keepdims=True), jnp.max(s_b, axis=0, keepdims=True))
    e_a = jnp.where(s_a > 0.5 * NEG, jnp.exp(s_a - m), 0.0)
    e_b = jnp.where(s_b > 0.5 * NEG, jnp.exp(s_b - m), 0.0)
    den = jnp.maximum(jnp.sum(e_a, axis=0, keepdims=True) + jnp.sum(e_b, axis=0, keepdims=True), 1e-30)
    o_w = (_dot_tn(win[:, KVW:].astype(BF), e_a.astype(BF)) + _dot_tn(wnew[:, KVW:].astype(BF), e_b.astype(BF)))
    ow_ref[0] = _pick_head(o_w, lane_h) / den


SEL_PAGES = 16


def _nsa_dec_sel_kernel(*refs, npg, page, ts):
    pt_ref = refs[0]
    pages = refs[1:1 + npg]
    (qbd_ref, pen_ref, exp_ref, blast_ref, knew_ref, bnew_ref, gate_ref, oc_ref, ow_ref,
     o_ref, m_ref, l_ref, acc_ref) = refs[1 + npg:]
    del pt_ref
    j = pl.program_id(1)
    last = pl.num_programs(1) - 1
    r_all = GRP * HKV * ts
    bps = npg * page // SEL_BLOCK

    @pl.when(j == 0)
    def _():
        m_ref[...] = jnp.full((1, r_all), NEG, F32)
        l_ref[...] = jnp.zeros((1, r_all), F32)
        acc_ref[...] = jnp.zeros((2 * DH, r_all), F32)

    qbd = qbd_ref[0]

    def online(s, v):
        m = m_ref[...]
        m2 = jnp.maximum(m, jnp.max(s, axis=0, keepdims=True))
        a = jnp.exp(m - m2)
        p = jnp.exp(s - m2)
        m_ref[...] = m2
        l_ref[...] = a * l_ref[...] + jnp.sum(p, axis=0, keepdims=True)
        acc_ref[...] = a * acc_ref[...] + _dot_tn(v, p.astype(BF))

    kv = jnp.concatenate([r[0] for r in pages], axis=0)
    pen = pen_ref[0, pl.ds(pl.multiple_of(j * bps, 8), bps), :].astype(BF)
    s = _dot(kv[:, 0:KVW].astype(BF), qbd) + _dot(exp_ref[...], pen)
    s = s + jnp.where(j == last, blast_ref[...], 0.0)
    online(s, kv[:, KVW:].astype(BF))

    @pl.when(j == last)
    def _():
        knew = knew_ref[0]
        online(_dot(knew[:, 0:KVW].astype(BF), qbd) + bnew_ref[...], knew[:, KVW:].astype(BF))
        lane_h = (lax.broadcasted_iota(jnp.int32, (1, r_all), 1) // ts) % HKV
        o_s = _pick_head(acc_ref[...], lane_h) / jnp.maximum(l_ref[...], 1e-30)
        g = jax.nn.sigmoid(gate_ref[0])
        o_ref[0] = g[0:1] * oc_ref[0] + g[1:2] * o_s + g[2:3] * ow_ref[0]


def _dec_bias_tables(rel_table, pos0, ts, nch, wb, page, npg):
    tab = rel_table.astype(F32)
    far = tab[REL_BUCKETS - 1]
    r = jnp.arange(GRP * HKV * ts, dtype=jnp.int32)
    head = ((r // ts) % HKV) * GRP + r // (HKV * ts)
    qpos = pos0 + r % ts

    def bias(kpos, ok, shift):
        dist = qpos[None, :] - kpos[:, None]
        b = tab[_t5_bucket(dist), head[None, :]] - (far[head][None, :] if shift else 0.0)
        return jnp.where((dist >= 0) & ok(dist), b, NEG)

    always = lambda d: d >= 0
    bc = bias(jnp.arange(nch, dtype=jnp.int32) * CMP_STRIDE + CMP_LEN - 1, always, False)
    bwa = bias(pos0 - wb + jnp.arange(wb, dtype=jnp.int32), lambda d: d <= WINDOW, False)
    tnew = jnp.arange(8, dtype=jnp.int32)
    newpos = jnp.where(tnew < ts, pos0 + tnew, pos0 + 2 * WINDOW + SEL_BLOCK)
    bwb = bias(newpos, lambda d: d <= WINDOW, False)
    bnew = bias(newpos, always, True)
    step = npg * page
    blast = jnp.zeros((step, r.shape[0]), F32).at[step - page:].set(
        bias(pos0 - page + jnp.arange(page, dtype=jnp.int32), always, True))
    return bc, bwa, bwb, bnew, blast


def nsa_decode(qn, gate_raw, kc, vc, cache2, page_table, win_state, kv_new, win_new, rel_table):
    bsz, ts, _ = qn.shape
    n_pages = page_table.shape[1]
    page = cache2.shape[1]
    pos0 = n_pages * page
    nch = kc.shape[1]
    wb = win_state.shape[1]
    r_all = GRP * HKV * ts
    assert pos0 % SEL_BLOCK == 0 and ts <= 8 and ts <= SEL_BLOCK and nch % 8 == 0
    ns = -(-(pos0 + ts) // SEL_BLOCK)
    npg = min(SEL_PAGES, n_pages)
    assert n_pages % npg == 0
    bps = npg * page // SEL_BLOCK
    ns_pad = max(-(-ns // 8) * 8, (n_pages // npg) * bps)
    n_top = min(TOP_N, ns)

    q5 = qn.reshape(bsz, ts, HKV, GRP, DH)
    qbd = jnp.einsum('bqhgd,hk->bkdghq', q5, jnp.eye(HKV, dtype=F32)).reshape(bsz, 2 * DH, r_all).astype(BF)
    gate_t = gate_raw.reshape(bsz, ts, HKV, GRP, 3).transpose(0, 4, 3, 2, 1).reshape(bsz, 3, r_all)
    pad8 = lambda a: jnp.pad(a, ((0, 0), (0, 8 - ts), (0, 0)))
    amat = jnp.asarray(_selection_matrix(nch - 1, ns_pad, nch), BF)
    expand = jnp.asarray(np.repeat(np.eye(bps, dtype=np.float32), SEL_BLOCK, axis=0), BF)
    bc, bwa, bwb, bnew, blast = _dec_bias_tables(rel_table, pos0, ts, nch, wb, page, npg)

    full = lambda a: pl.BlockSpec(a.shape, functools.partial(lambda nd, *_: (0,) * nd, a.ndim))
    per_b = lambda a: pl.BlockSpec((1,) + a.shape[1:], functools.partial(lambda nd, b, *_: (b,) + (0,) * nd, a.ndim - 1))
    wnew = pad8(win_new)
    pre_in = [qbd, kc, vc, amat, bc, win_state, wnew, bwa, bwb]
    pre_specs = [per_b(qbd), per_b(kc), per_b(vc), full(amat), full(bc), per_b(win_state), per_b(wnew),
                 full(bwa), full(bwb)]
    small = jax.ShapeDtypeStruct((bsz, DH, r_all), F32)
    pen, o_c, o_w = pl.pallas_call(
        functools.partial(_nsa_dec_pre_kernel, ts=ts, pos0=pos0, n_top=n_top),
        grid=(bsz,), in_specs=pre_specs,
        out_specs=[pl.BlockSpec((1, ns_pad, r_all), lambda b: (b, 0, 0)),
                   pl.BlockSpec((1, DH, r_all), lambda b: (b, 0, 0)),
                   pl.BlockSpec((1, DH, r_all), lambda b: (b, 0, 0))],
        out_shape=[jax.ShapeDtypeStruct((bsz, ns_pad, r_all), F32), small, small],
        compiler_params=_cp("parallel"), name="nsa_dec_pre",
    )(*pre_in)

    knew = pad8(kv_new[:, :, 2 * KVW:])
    page_specs = [pl.BlockSpec((1, page, 2 * KVW),
                               functools.partial(lambda r, b, j, pt: (pt[b, j * npg + r], 0, 1), r))
                  for r in range(npg)]
    sel_in = [qbd, pen, expand, blast, knew, bnew, gate_t, o_c, o_w]
    sel_specs = [per_b(qbd), per_b(pen), full(expand), full(blast), per_b(knew), full(bnew), per_b(gate_t),
                 per_b(o_c), per_b(o_w)]
    o_t = pl.pallas_call(
        functools.partial(_nsa_dec_sel_kernel, npg=npg, page=page, ts=ts),
        grid_spec=pltpu.PrefetchScalarGridSpec(
            num_scalar_prefetch=1, grid=(bsz, n_pages // npg), in_specs=page_specs + sel_specs,
            out_specs=pl.BlockSpec((1, DH, r_all), lambda b, j, pt: (b, 0, 0)),
            scratch_shapes=[pltpu.VMEM((1, r_all), F32), pltpu.VMEM((1, r_all), F32),
                            pltpu.VMEM((2 * DH, r_all), F32)]),
        out_shape=small, compiler_params=_cp("parallel", "arbitrary"), name="nsa_dec_sel",
    )(page_table, *([cache2] * npg), *sel_in)
    return o_t.reshape(bsz, DH, GRP, HKV, ts).transpose(0, 4, 3, 2, 1).reshape(bsz, ts, NH * DH)


def _even_weights(p, e):
    w_in = jnp.pad(p['att_w_in'][e], ((0, 0), (0, ATT_IN_PAD - ATT_IN_COLS))).astype(BF)
    w_out = p['att_w_out'][e].astype(BF)
    return dict(
        w_in=w_in, w_out_conv=w_out[:C_CONV], w_out_att=w_out[C_CONV:],
        cmp=_compress_weights(p['cmp_pe'][e], p['cmp_w1'][e], p['cmp_b1'][e], p['cmp_w2'][e], p['k_norm_g'][e][0]),
        k_g=p['k_norm_g'][e], q_g=p['q_norm_g'][e], conv_w=p['conv_w'][e], conv_b=p['conv_b'][e],
        ln_g=p['conv_ln_g'][e], ln_b=p['conv_ln_b'][e])


def _even_prompt(x2, norm_g, w, rel_table):
    t = x2.shape[0]
    proj = norm_matmul(x2, norm_g, w['w_in'])
    conv_y, conv_new = conformer_conv(proj[None], jnp.zeros((1, CONV_WIDTH - 1, C_CONV), F32),
                                      w['conv_w'], w['conv_b'], w['ln_g'], w['ln_b'])
    kv_new, win_new = kv_post(proj, w['k_g'])
    kc, vc = compress_dense(proj[None], w['cmp'])
    o = nsa_prompt(proj, kc[0], vc[0], kv_new[:, 2 * KVW:3 * KVW], kv_new[:, 3 * KVW:], win_new[:, :KVW],
                   win_new[:, KVW:], w['q_g'], rel_table)
    x2 = out_proj2(x2, conv_y[0], o, w['w_out_conv'], w['w_out_att'])
    keep = min(WINDOW, t)
    return (x2, kv_new.reshape(1, t, 4, HKV, DH), win_new[t - keep:].reshape(1, keep, 2, HKV, DH), conv_new)


def _even_decode(x3, norm_g, w, rel_table, cache2, page_table, win_state, conv_state):
    bsz, t, d = x3.shape
    x2 = x3.reshape(bsz * t, d)
    proj = norm_matmul(x2, norm_g, w['w_in'])
    proj3 = proj.reshape(bsz, t, ATT_IN_PAD)
    conv_y, conv_new = conformer_conv(proj3, conv_state, w['conv_w'], w['conv_b'], w['ln_g'], w['ln_b'])
    kv_new, win_new = kv_post(proj, w['k_g'])
    qn = q_norm(proj, w['q_g'])
    kc, vc = compress_paged(cache2, page_table, w['cmp'])
    wb = win_state.shape[1]
    o = nsa_decode(qn.reshape(bsz, t, NH * DH), proj3[:, :, O_GATE:ATT_IN_COLS], kc, vc, cache2, page_table,
                   win_state.reshape(bsz, wb, 2 * KVW), kv_new.reshape(bsz, t, 4 * KVW),
                   win_new.reshape(bsz, t, 2 * KVW), rel_table)
    x2 = out_proj2(x2, conv_y.reshape(bsz * t, C_CONV), o.reshape(bsz * t, NH * DH), w['w_out_conv'],
                   w['w_out_att'])
    win_all = jnp.concatenate([win_state, win_new.reshape(bsz, t, 2, HKV, DH)], axis=1)
    keep = min(WINDOW, wb + t)
    return (x2.reshape(bsz, t, d), kv_new.reshape(bsz, t, 4, HKV, DH), win_all[:, wb + t - keep:], conv_new)


def _odd_layer(x3, s0, norm_g, w_in_bf, w_out_bf, lb, hg_norm_g):
    bsz, t, d = x3.shape
    x2 = x3.reshape(bsz * t, d)
    z = norm_matmul(x2, norm_g, w_in_bf, tn=2048).reshape(bsz, t, 4 * d)
    tp = -(-t // HG_CHUNK) * HG_CHUNK
    if tp != t:
        z = jnp.pad(z, ((0, 0), (0, tp - t), (0, 0)))
    o, s_new = hgrn2(z, s0, lb, hg_norm_g, t)
    x2 = out_proj1(x2, o[:, :t].reshape(bsz * t, d), w_out_bf)
    return x2.reshape(bsz, t, d), s_new


def kernel(x_prompt, x_sample, cache_nsa_kv, page_table, state_nsa_win, state_conv, state_hgrn, rel_bias_table,
           norm_mix_g, norm_mlp_g, w_mlp_up, w_mlp_down, att_w_in, att_w_out, q_norm_g, k_norm_g, cmp_pe, cmp_w1,
           cmp_b1, cmp_w2, conv_w, conv_b, conv_ln_g, conv_ln_b, hg_w_in, hg_w_out, hg_lb_logits, hg_norm_g):
    p = dict(att_w_in=att_w_in, att_w_out=att_w_out, q_norm_g=q_norm_g, k_norm_g=k_norm_g, cmp_pe=cmp_pe,
             cmp_w1=cmp_w1, cmp_b1=cmp_b1, cmp_w2=cmp_w2, conv_w=conv_w, conv_b=conv_b, conv_ln_g=conv_ln_g,
             conv_ln_b=conv_ln_b)
    bp, tp_, d = x_prompt.shape
    assert bp == 1
    db, ts, _ = x_sample.shape
    w_up = w_mlp_up.astype(BF)
    w_down = w_mlp_down.astype(BF)
    cum = jnp.cumsum(jax.nn.softmax(hg_lb_logits.astype(F32), axis=0), axis=0)

    xp = x_prompt[0]
    kv_p, win_p, conv_p, hg_p = [], [], [], []
    for layer in range(norm_mix_g.shape[0]):
        if layer % 2 == 0:
            e = layer // 2
            w = _even_weights(p, e)
            xp, kv_new, win_new, conv_new = _even_prompt(xp, norm_mix_g[layer], w, rel_bias_table)
            kv_p.append(kv_new)
            win_p.append(win_new)
            conv_p.append(conv_new)
        else:
            o = layer // 2
            x3, s_new = _odd_layer(xp[None], jnp.zeros((1, HG_HEADS, HG_DK, HG_DV), F32), norm_mix_g[layer],
                                   hg_w_in[o].astype(BF), hg_w_out[o].astype(BF), cum[layer] - cum[0],
                                   hg_norm_g[o])
            xp = x3[0]
            hg_p.append(s_new)
        xp = mlp(xp, norm_mlp_g[layer], w_up[layer], w_down[layer])

    cache2 = cache_nsa_kv.reshape(cache_nsa_kv.shape[0], cache_nsa_kv.shape[1], cache_nsa_kv.shape[2], 4 * KVW)
    xs = x_sample
    kv_s, win_s, conv_s, hg_s = [], [], [], []
    for layer in range(norm_mix_g.shape[0]):
        if layer % 2 == 0:
            e = layer // 2
            w = _even_weights(p, e)
            xs, kv_new, win_new, conv_new = _even_decode(xs, norm_mix_g[layer], w, rel_bias_table, cache2[e],
                                                         page_table, state_nsa_win[e], state_conv[e])
            kv_s.append(kv_new)
            win_s.append(win_new)
            conv_s.append(conv_new)
        else:
            o = layer // 2
            xs, s_new = _odd_layer(xs, state_hgrn[o], norm_mix_g[layer], hg_w_in[o].astype(BF),
                                   hg_w_out[o].astype(BF), cum[layer] - cum[0], hg_norm_g[o])
            hg_s.append(s_new)
        xs = mlp(xs.reshape(db * ts, d), norm_mlp_g[layer], w_up[layer], w_down[layer]).reshape(db, ts, d)
    return (xp[None], xs, jnp.stack(kv_p), jnp.stack(kv_s), jnp.stack(win_p), jnp.stack(win_s),
            jnp.stack(conv_p), jnp.stack(conv_s), jnp.stack(hg_p), jnp.stack(hg_s))
```

```python
import functools
import math

import jax
import jax.numpy as jnp
import numpy as np
from jax import lax
from jax.experimental import pallas as pl
from jax.experimental.pallas import tpu as pltpu

D_MODEL = 1024
C_CONV = 512
CONV_WIDTH = 31
DH = 64
HKV = 2
GRP = 4
NH = HKV * GRP
CMP_STRIDE = 16
CMP_LEN = 32
CMP_HIDDEN = 128
SEL_BLOCK = 64
TOP_N = 16
WINDOW = 512
Q_BLOCK = 128
REL_BUCKETS = 32
REL_EXACT = 16
REL_MAX_DIST = 128
HG_DK = 128
HG_HEADS = 8
HG_DV = 128
HG_CHUNK = 64
RMS_EPS = 1e-6
NEG = -1e30
BIG = 1e9
O_Q = 2 * C_CONV
O_KV = O_Q + NH * DH
O_GATE = O_KV + 3 * 2 * HKV * DH
ATT_IN_COLS = O_GATE + 3 * NH
ATT_IN_PAD = 2432
KVW = HKV * DH

VMEM_LIMIT = 56 * 1024 * 1024
BF = jnp.bfloat16
F32 = jnp.float32


def _cp(*sem):
    return pltpu.CompilerParams(dimension_semantics=sem, vmem_limit_bytes=VMEM_LIMIT)


def _dot(a, b):
    return jnp.dot(a, b, preferred_element_type=F32)


def _dot_nt(a, b):
    return lax.dot_general(a, b, (((1,), (1,)), ((), ())), preferred_element_type=F32)


def _dot_tn(a, b):
    return lax.dot_general(a, b, (((0,), (0,)), ((), ())), preferred_element_type=F32)


def _rms(x, g):
    return x * lax.rsqrt(jnp.mean(x * x, axis=-1, keepdims=True) + RMS_EPS) * g


def _norm_matmul_kernel(x_ref, g_ref, w_ref, o_ref):
    hn = _rms(x_ref[...], g_ref[...]).astype(BF)
    o_ref[...] = _dot(hn, w_ref[...])


def norm_matmul(x, g, w_bf, tn=None):
    m, d = x.shape
    n = w_bf.shape[1]
    tm = min(m, 512)
    tn = n if tn is None else tn
    return pl.pallas_call(
        _norm_matmul_kernel,
        grid=(m // tm, n // tn),
        in_specs=[pl.BlockSpec((tm, d), lambda i, j: (i, 0)),
                  pl.BlockSpec((1, d), lambda i, j: (0, 0)),
                  pl.BlockSpec((d, tn), lambda i, j: (0, j))],
        out_specs=pl.BlockSpec((tm, tn), lambda i, j: (i, j)),
        out_shape=jax.ShapeDtypeStruct((m, n), F32),
        compiler_params=_cp("parallel", "arbitrary"),
        name="norm_matmul",
    )(x, g.reshape(1, d), w_bf)


def _out_proj_kernel(r_ref, a1_ref, a2_ref, w1_ref, w2_ref, o_ref):
    o_ref[...] = (r_ref[...] + _dot(a1_ref[...].astype(BF), w1_ref[...])
                  + _dot(a2_ref[...].astype(BF), w2_ref[...]))


def out_proj2(res, a1, a2, w1_bf, w2_bf):
    m, d = res.shape
    k1, k2 = a1.shape[1], a2.shape[1]
    tm = min(m, 512)
    return pl.pallas_call(
        _out_proj_kernel,
        grid=(m // tm,),
        in_specs=[pl.BlockSpec((tm, d), lambda i: (i, 0)),
                  pl.BlockSpec((tm, k1), lambda i: (i, 0)),
                  pl.BlockSpec((tm, k2), lambda i: (i, 0)),
                  pl.BlockSpec((k1, d), lambda i: (0, 0)),
                  pl.BlockSpec((k2, d), lambda i: (0, 0))],
        out_specs=pl.BlockSpec((tm, d), lambda i: (i, 0)),
        out_shape=jax.ShapeDtypeStruct((m, d), F32),
        compiler_params=_cp("parallel"),
        name="out_proj2",
    )(res, a1, a2, w1_bf, w2_bf)


def _out_proj1_kernel(r_ref, a_ref, w_ref, o_ref):
    o_ref[...] = r_ref[...] + _dot(a_ref[...].astype(BF), w_ref[...])


def out_proj1(res, a, w_bf):
    m, d = res.shape
    k = a.shape[1]
    tm = min(m, 512)
    return pl.pallas_call(
        _out_proj1_kernel,
        grid=(m // tm,),
        in_specs=[pl.BlockSpec((tm, d), lambda i: (i, 0)),
                  pl.BlockSpec((tm, k), lambda i: (i, 0)),
                  pl.BlockSpec((k, d), lambda i: (0, 0))],
        out_specs=pl.BlockSpec((tm, d), lambda i: (i, 0)),
        out_shape=jax.ShapeDtypeStruct((m, d), F32),
        compiler_params=_cp("parallel"),
        name="out_proj1",
    )(res, a, w_bf)


def _mlp_kernel(x_ref, g_ref, wu_ref, wd_ref, o_ref, hn_ref, acc_ref):
    j = pl.program_id(1)

    @pl.when(j == 0)
    def _():
        hn_ref[...] = _rms(x_ref[...], g_ref[...]).astype(BF)
        acc_ref[...] = x_ref[...]

    hid = jnp.maximum(_dot(hn_ref[...], wu_ref[...]), 0.0)
    acc_ref[...] += _dot((hid * hid).astype(BF), wd_ref[...])

    @pl.when(j == pl.num_programs(1) - 1)
    def _():
        o_ref[...] = acc_ref[...]


def mlp(x, g, wu_bf, wd_bf):
    m, d = x.shape
    hdim = wu_bf.shape[1]
    tm = min(m, 1024)
    th = 1024
    return pl.pallas_call(
        _mlp_kernel,
        grid=(m // tm, hdim // th),
        in_specs=[pl.BlockSpec((tm, d), lambda i, j: (i, 0)),
                  pl.BlockSpec((1, d), lambda i, j: (0, 0)),
                  pl.BlockSpec((d, th), lambda i, j: (0, j)),
                  pl.BlockSpec((th, d), lambda i, j: (j, 0))],
        out_specs=pl.BlockSpec((tm, d), lambda i, j: (i, 0)),
        out_shape=jax.ShapeDtypeStruct((m, d), F32),
        scratch_shapes=[pltpu.VMEM((tm, d), BF), pltpu.VMEM((tm, d), F32)],
        compiler_params=_cp("parallel", "arbitrary"),
        name="mlp",
    )(x, g.reshape(1, d), wu_bf, wd_bf)


CONV_HALO = 32
CONV_PAD = CONV_HALO - (CONV_WIDTH - 1)


def _conv_kernel(u_ref, st_ref, w_ref, b_ref, lg_ref, lb_ref, y_ref, new_ref, xin_ref, *, tt):
    t = pl.program_id(1)

    @pl.when(t == 0)
    def _():
        xin_ref[0:CONV_HALO, :] = st_ref[0]

    a = u_ref[0, :, 0:C_CONV]
    gt = u_ref[0, :, C_CONV:2 * C_CONV]
    xin_ref[CONV_HALO:CONV_HALO + tt, :] = a * jax.nn.sigmoid(gt)

    cols = []
    for c in range(C_CONV // 128):
        cs = slice(c * 128, (c + 1) * 128)
        acc = jnp.zeros((tt, 128), F32) + b_ref[:, cs]
        for k in range(CONV_WIDTH):
            acc = acc + w_ref[k:k + 1, cs] * xin_ref[CONV_PAD + k:CONV_PAD + k + tt, cs]
        cols.append(acc)
    y = jnp.concatenate(cols, axis=1)
    mu = jnp.mean(y, axis=-1, keepdims=True)
    yc = y - mu
    var = jnp.mean(yc * yc, axis=-1, keepdims=True)
    z = yc * lax.rsqrt(var + RMS_EPS) * lg_ref[...] + lb_ref[...]
    y_ref[0] = z * jax.nn.sigmoid(z)

    @pl.when(t == pl.num_programs(1) - 1)
    def _():
        new_ref[0] = xin_ref[tt + CONV_PAD:tt + CONV_HALO, :]

    if tt >= CONV_HALO:
        @pl.when(t < pl.num_programs(1) - 1)
        def _():
            xin_ref[0:CONV_HALO, :] = xin_ref[tt:tt + CONV_HALO, :]


def conformer_conv(proj, state, w, b, ln_g, ln_b):
    bsz, t, _ = proj.shape
    tt = min(t, 256)
    assert t % tt == 0 and (t == tt or tt >= CONV_HALO)
    st = jnp.pad(state, ((0, 0), (CONV_PAD, 0), (0, 0)))
    row = lambda v: v.reshape(1, C_CONV)
    return pl.pallas_call(
        functools.partial(_conv_kernel, tt=tt),
        grid=(bsz, t // tt),
        in_specs=[pl.BlockSpec((1, tt, 2 * C_CONV), lambda i, j: (i, j, 0)),
                  pl.BlockSpec((1, CONV_HALO, C_CONV), lambda i, j: (i, 0, 0)),
                  pl.BlockSpec((CONV_WIDTH, C_CONV), lambda i, j: (0, 0)),
                  pl.BlockSpec((1, C_CONV), lambda i, j: (0, 0)),
                  pl.BlockSpec((1, C_CONV), lambda i, j: (0, 0)),
                  pl.BlockSpec((1, C_CONV), lambda i, j: (0, 0))],
        out_specs=[pl.BlockSpec((1, tt, C_CONV), lambda i, j: (i, j, 0)),
                   pl.BlockSpec((1, CONV_WIDTH - 1, C_CONV), lambda i, j: (i, 0, 0))],
        out_shape=[jax.ShapeDtypeStruct((bsz, t, C_CONV), F32),
                   jax.ShapeDtypeStruct((bsz, CONV_WIDTH - 1, C_CONV), F32)],
        scratch_shapes=[pltpu.VMEM((CONV_HALO + tt, C_CONV), F32)],
        compiler_params=_cp("parallel", "arbitrary"),
        name="conformer_conv",
    )(proj, st, w, row(b), row(ln_g), row(ln_b))


HG_LEVELS = (32, 16, 8, 4, 2, 1)


def _hgrn_tables():
    c = HG_CHUNK
    idx = np.arange(c)
    mats = [(idx[None, :] <= idx[:, None]), (idx[None, :] > idx[:, None])]
    masks = []
    for h in HG_LEVELS:
        blk = idx // (2 * h)
        upper = (idx % (2 * h)) >= h
        piv = blk * 2 * h + h - 1
        mats.append((upper[:, None] & (idx[None, :] > piv[:, None]) & (idx[None, :] <= idx[:, None]))
                    | ((~upper)[:, None] & (idx[None, :] > idx[:, None]) & (idx[None, :] <= piv[:, None])))
        masks.append((blk[:, None] == blk[None, :]) & upper[:, None] & (~upper)[None, :])
    masks.append(np.eye(c, dtype=bool))
    return (np.concatenate(mats, axis=0).astype(np.float32),
            np.stack(masks).astype(np.float32))


HG_HEADS_PER_STEP = 8


def _hgrn_kernel(q_ref, fz_ref, v_ref, g_ref, lb_ref, ng_ref, s0_ref, tab_ref, msk_ref,
                 o_ref, sn_ref, st_ref, *, t_valid, nh):
    ci = pl.program_id(2)
    c = HG_CHUNK

    @pl.when(ci == 0)
    def _():
        for hh in range(nh):
            st_ref[hh] = s0_ref[0, hh].T

    row = ci * c + lax.broadcasted_iota(jnp.int32, (c, 1), 0)
    live = row < t_valid
    tab = tab_ref[...]
    for hh in range(nh):
        cs = slice(hh * HG_DK, (hh + 1) * HG_DK)
        lb = lb_ref[:, cs]
        f = lb + (1.0 - lb) * jax.nn.sigmoid(fz_ref[0, :, cs])
        lf = jnp.where(live, jnp.log(f), 0.0)
        k = jnp.where(live, 1.0 - f, 0.0)
        q = q_ref[0, :, cs]
        v = v_ref[0, :, cs]

        hi = lf.astype(BF)
        ex2 = _dot(tab, jnp.concatenate([hi, (lf - hi.astype(F32)).astype(BF)], axis=1))
        ex = ex2[:, 0:HG_DK] + ex2[:, HG_DK:]

        attn = jnp.zeros((c, c), F32)
        for li in range(len(HG_LEVELS)):
            fac = jnp.exp(ex[(2 + li) * c:(3 + li) * c])
            attn = attn + msk_ref[li] * _dot_nt((q * fac).astype(BF), (k * fac).astype(BF))
        attn = attn + msk_ref[len(HG_LEVELS)] * _dot_nt(q.astype(BF), k.astype(BF))

        st = st_ref[hh]
        qb = (q * jnp.exp(ex[0:c])).astype(BF)
        o = _dot(attn.astype(BF), v.astype(BF)) + _dot_nt(qb, st.astype(BF))
        ke = (k * jnp.exp(ex[c:2 * c])).astype(BF)
        decay = jnp.exp(ex[c - 1:c])
        st_ref[hh] = st * decay + _dot(v.T.astype(BF), ke)

        gate = g_ref[0, :, cs]
        o_ref[0, :, cs] = _rms(o, ng_ref[...]) * (gate * jax.nn.sigmoid(gate))

    @pl.when(ci == pl.num_programs(2) - 1)
    def _():
        for hh in range(nh):
            sn_ref[0, hh] = st_ref[hh].T


def hgrn2(z, s0, lb, norm_g, t_valid):
    bsz, tp, _ = z.shape
    c = HG_CHUNK
    nc = tp // c
    tab, msk = _hgrn_tables()
    nh = HG_HEADS_PER_STEP
    ng = HG_HEADS // nh
    w = nh * HG_DK
    blk = lambda off: pl.BlockSpec((1, c, w), lambda b, h, i: (b, i, off + h))
    return pl.pallas_call(
        functools.partial(_hgrn_kernel, t_valid=t_valid, nh=nh),
        grid=(bsz, ng, nc),
        in_specs=[blk(0), blk(ng), blk(2 * ng), blk(3 * ng),
                  pl.BlockSpec((1, w), lambda b, h, i: (0, h)),
                  pl.BlockSpec((1, HG_DV), lambda b, h, i: (0, 0)),
                  pl.BlockSpec((1, nh, HG_DK, HG_DV), lambda b, h, i: (b, h, 0, 0)),
                  pl.BlockSpec(tab.shape, lambda b, h, i: (0, 0)),
                  pl.BlockSpec(msk.shape, lambda b, h, i: (0, 0, 0))],
        out_specs=[pl.BlockSpec((1, c, w), lambda b, h, i: (b, i, h)),
                   pl.BlockSpec((1, nh, HG_DK, HG_DV), lambda b, h, i: (b, h, 0, 0))],
        out_shape=[jax.ShapeDtypeStruct((bsz, tp, D_MODEL), F32),
                   jax.ShapeDtypeStruct((bsz, HG_HEADS, HG_DK, HG_DV), F32)],
        scratch_shapes=[pltpu.VMEM((nh, HG_DV, HG_DK), F32)],
        compiler_params=_cp("parallel", "parallel", "arbitrary"),
        name="hgrn2",
    )(z, z, z, z, lb.reshape(1, D_MODEL), norm_g.reshape(1, HG_DV), s0,
      jnp.asarray(tab, BF), jnp.asarray(msk))


def _pair_norm(seg, g):
    lo = lax.broadcasted_iota(jnp.int32, seg.shape, 1) < DH
    sq = seg * seg
    s0 = jnp.sum(jnp.where(lo, sq, 0.0), axis=-1, keepdims=True)
    s1 = jnp.sum(jnp.where(lo, 0.0, sq), axis=-1, keepdims=True)
    inv = jnp.where(lo, lax.rsqrt(s0 * (1.0 / DH) + RMS_EPS), lax.rsqrt(s1 * (1.0 / DH) + RMS_EPS))
    return seg * inv * g


def _kv_post_kernel(p_ref, kg_ref, kv_ref, win_ref):
    x = p_ref[...]
    kv_ref[:, 0:2 * KVW] = x[:, 0:2 * KVW]
    kv_ref[:, 2 * KVW:3 * KVW] = _pair_norm(x[:, 2 * KVW:3 * KVW], kg_ref[1:2, :])
    kv_ref[:, 3 * KVW:4 * KVW] = x[:, 3 * KVW:4 * KVW]
    win_ref[:, 0:KVW] = _pair_norm(x[:, 4 * KVW:5 * KVW], kg_ref[2:3, :])
    win_ref[:, KVW:2 * KVW] = x[:, 5 * KVW:6 * KVW]


def kv_post(proj, k_g):
    m = proj.shape[0]
    tm = min(m, 1024)
    kvcols = 6 * KVW
    assert O_KV % kvcols == 0
    return pl.pallas_call(
        _kv_post_kernel,
        grid=(m // tm,),
        in_specs=[pl.BlockSpec((tm, kvcols), lambda i: (i, O_KV // kvcols)),
                  pl.BlockSpec((3, KVW), lambda i: (0, 0))],
        out_specs=[pl.BlockSpec((tm, 4 * KVW), lambda i: (i, 0)),
                   pl.BlockSpec((tm, 2 * KVW), lambda i: (i, 0))],
        out_shape=[jax.ShapeDtypeStruct((m, 4 * KVW), F32), jax.ShapeDtypeStruct((m, 2 * KVW), F32)],
        compiler_params=_cp("parallel"),
        name="kv_post",
    )(proj, jnp.concatenate([k_g, k_g], axis=1))


def _q_norm_kernel(q_ref, g_ref, o_ref):
    for c in range(NH // 2):
        cs = slice(c * 2 * DH, (c + 1) * 2 * DH)
        o_ref[:, cs] = _pair_norm(q_ref[:, cs], g_ref[...]) * (DH ** -0.5)


def q_norm(proj, q_g):
    m = proj.shape[0]
    tm = min(m, 1024)
    return pl.pallas_call(
        _q_norm_kernel,
        grid=(m // tm,),
        in_specs=[pl.BlockSpec((tm, NH * DH), lambda i: (i, O_Q // (NH * DH))),
                  pl.BlockSpec((1, 2 * DH), lambda i: (0, 0))],
        out_specs=pl.BlockSpec((tm, NH * DH), lambda i: (i, 0)),
        out_shape=jax.ShapeDtypeStruct((m, NH * DH), F32),
        compiler_params=_cp("parallel"),
        name="q_norm",
    )(proj, jnp.concatenate([q_g, q_g]).reshape(1, 2 * DH))


def _compress_kernel(*refs, n_prefetch, n_src, rows, nch):
    refs = refs[n_prefetch:]
    srcs = (refs[:n_src], refs[n_src:2 * n_src])
    pef_ref, pes_ref, w1f_ref, w1s_ref, b1_ref, w2_ref, kg_ref, kc_ref, vc_ref, hf_ref, hs_ref = refs[2 * n_src:]
    j = pl.program_id(1)
    cpr = rows // CMP_STRIDE
    m = cpr * n_src

    @pl.when(j == 0)
    def _():
        hs_ref[:, nch:nch + 8, :] = jnp.zeros((2, 8, 2 * CMP_HIDDEN), F32)

    for kind in range(2):
        accf = jnp.zeros((m, 2 * CMP_HIDDEN), F32)
        accs = jnp.zeros((m, 2 * CMP_HIDDEN), F32)
        for s in range(CMP_STRIDE):
            xs = jnp.concatenate(
                [r[0, pl.ds(s, cpr, stride=CMP_STRIDE), :] for r in srcs[kind]], axis=0)
            accf = accf + _dot((xs + pef_ref[kind, s:s + 1, :]).astype(BF), w1f_ref[kind, s])
            accs = accs + _dot((xs + pes_ref[kind, s:s + 1, :]).astype(BF), w1s_ref[kind, s])
        row0 = pl.multiple_of(j * m, 8)
        hf_ref[kind, pl.ds(row0, m), :] = accf
        hs_ref[kind, pl.ds(row0, m), :] = accs

    @pl.when(j == pl.num_programs(1) - 1)
    def _():
        for kind in range(2):
            hid = jax.nn.gelu(hf_ref[kind, 0:nch, :] + hs_ref[kind, 1:nch + 1, :] + b1_ref[kind])
            out = _dot(hid.astype(BF), w2_ref[kind])
            if kind == 0:
                kc_ref[0] = _pair_norm(out, kg_ref[...]).astype(BF)
            else:
                vc_ref[0] = out.astype(BF)


def _compress_weights(pe, w1, b1, w2, kg0):
    eye = jnp.eye(HKV, dtype=F32)
    bd = lambda w: jnp.einsum('ab,ksdf->ksadbf', eye, w).reshape(2, CMP_STRIDE, KVW, 2 * CMP_HIDDEN).astype(BF)
    tile2 = lambda a: jnp.concatenate([a, a], axis=-1)
    w2bd = jnp.einsum('ab,kfd->kafbd', eye, w2).reshape(2, 2 * CMP_HIDDEN, KVW).astype(BF)
    return (tile2(pe[:, :CMP_STRIDE]), tile2(pe[:, CMP_STRIDE:]), bd(w1[:, :CMP_STRIDE]), bd(w1[:, CMP_STRIDE:]),
            tile2(b1).reshape(2, 1, 2 * CMP_HIDDEN), w2bd, tile2(kg0).reshape(1, KVW))


def _compress_common(n_prefetch, n_src, rows, nch, bsz, weights):
    wspecs = [pl.BlockSpec(w.shape, functools.partial(lambda nd, *a: (0,) * nd, w.ndim)) for w in weights]
    out_specs = [pl.BlockSpec((1, nch, KVW), lambda b, j, *a: (b, 0, 0))] * 2
    out_shape = [jax.ShapeDtypeStruct((bsz, nch, KVW), BF)] * 2
    scratch = [pltpu.VMEM((2, nch + 8, 2 * CMP_HIDDEN), F32)] * 2
    kern = functools.partial(_compress_kernel, n_prefetch=n_prefetch, n_src=n_src, rows=rows, nch=nch)
    return kern, wspecs, out_specs, out_shape, scratch


def compress_dense(proj3, weights):
    bsz, t, _ = proj3.shape
    rows = min(t, 2048)
    nch = t // CMP_STRIDE
    assert O_KV % KVW == 0
    kern, wspecs, out_specs, out_shape, scratch = _compress_common(0, 1, rows, nch, bsz, weights)
    src = lambda kind: pl.BlockSpec((1, rows, KVW), lambda b, j: (b, j, O_KV // KVW + kind))
    return pl.pallas_call(
        kern, grid=(bsz, t // rows),
        in_specs=[src(0), src(1)] + wspecs,
        out_specs=out_specs, out_shape=out_shape, scratch_shapes=scratch,
        compiler_params=_cp("parallel", "arbitrary"), name="compress_dense",
    )(proj3, proj3, *weights)


CMP_PAGES = 16


def _compress_paged_kernel(*refs, npg, page, nch):
    pages = refs[1:1 + npg]
    (pef_ref, pes_ref, w1f_ref, w1s_ref, b1_ref, w2_ref, kg_ref, kc_ref, vc_ref,
     hf_ref, hs_ref, xs_ref) = refs[1 + npg:]
    j = pl.program_id(1)
    m = npg * page // CMP_STRIDE

    @pl.when(j == 0)
    def _():
        hs_ref[:, nch:nch + 8, :] = jnp.zeros((2, 8, 2 * CMP_HIDDEN), F32)

    for kind in range(2):
        for r in range(npg):
            xt = jnp.concatenate([pages[r][0, kind, h] for h in range(HKV)], axis=0)
            xs_ref[kind, r * page:(r + 1) * page, :] = xt.T
        accf = jnp.zeros((m, 2 * CMP_HIDDEN), F32)
        accs = jnp.zeros((m, 2 * CMP_HIDDEN), F32)
        for s2 in range(CMP_STRIDE // 2):
            xa = xs_ref[kind, pl.ds(2 * s2, m, stride=CMP_STRIDE), :]
            xb = xs_ref[kind, pl.ds(2 * s2 + 1, m, stride=CMP_STRIDE), :]
            pair = lambda pe: jnp.concatenate([(xa + pe[kind, 2 * s2:2 * s2 + 1, :]).astype(BF),
                                               (xb + pe[kind, 2 * s2 + 1:2 * s2 + 2, :]).astype(BF)], axis=1)
            accf = accf + _dot(pair(pef_ref), w1f_ref[kind, 2 * s2:2 * s2 + 2].reshape(2 * KVW, 2 * CMP_HIDDEN))
            accs = accs + _dot(pair(pes_ref), w1s_ref[kind, 2 * s2:2 * s2 + 2].reshape(2 * KVW, 2 * CMP_HIDDEN))
        row0 = pl.multiple_of(j * m, 8)
        hf_ref[kind, pl.ds(row0, m), :] = accf
        hs_ref[kind, pl.ds(row0, m), :] = accs

    @pl.when(j == pl.num_programs(1) - 1)
    def _():
        for kind in range(2):
            hid = jax.nn.gelu(hf_ref[kind, 0:nch, :] + hs_ref[kind, 1:nch + 1, :] + b1_ref[kind])
            out = _dot(hid.astype(BF), w2_ref[kind])
            if kind == 0:
                kc_ref[0] = _pair_norm(out, kg_ref[...]).astype(BF)
            else:
                vc_ref[0] = out.astype(BF)


def compress_paged(cachet, page_table, weights):
    bsz, n_pages = page_table.shape
    page = cachet.shape[-1]
    npg = min(CMP_PAGES, n_pages)
    nch = n_pages * page // CMP_STRIDE
    src_specs = [pl.BlockSpec((1, 2, HKV, DH, page),
                              functools.partial(lambda r, b, j, pt: (pt[b, j * npg + r], 0, 0, 0, 0), r))
                 for r in range(npg)]
    _, wspecs, out_specs, out_shape, scratch = _compress_common(1, npg, page, nch, bsz, weights)
    return pl.pallas_call(
        functools.partial(_compress_paged_kernel, npg=npg, page=page, nch=nch),
        grid_spec=pltpu.PrefetchScalarGridSpec(
            num_scalar_prefetch=1, grid=(bsz, n_pages // npg), in_specs=src_specs + wspecs,
            out_specs=out_specs, scratch_shapes=scratch + [pltpu.VMEM((2, npg * page, KVW), F32)]),
        out_shape=out_shape, compiler_params=_cp("parallel", "arbitrary"), name="compress_paged",
    )(page_table, *([cachet] * npg), *weights)


NS_PAD = 256
SEL_HALF = 128
LOG2E = math.log2(math.e)
PEN = 30000.0
KT = 256
HALF_TILES = SEL_HALF * SEL_BLOCK // KT
CBAND = 24


def _t5_bucket(dist):
    n = jnp.maximum(dist, 0)
    large = REL_EXACT + (jnp.log(jnp.maximum(n, 1).astype(F32) / REL_EXACT)
                         / math.log(REL_MAX_DIST / REL_EXACT) * (REL_BUCKETS - REL_EXACT)).astype(jnp.int32)
    return jnp.where(n < REL_EXACT, n, jnp.minimum(large, REL_BUCKETS - 1))


def _selection_matrix(nc, ns_pad, nc_pad):
    ratio, span = SEL_BLOCK // CMP_STRIDE, CMP_LEN // CMP_STRIDE
    a = np.zeros((ns_pad, nc_pad), np.float32)
    for j in range(ns_pad):
        for mm in range(ratio):
            for nn in range(span):
                n = ratio * j + mm - nn
                if 0 <= n < nc:
                    a[j, n] += 1.0
    return a


def _split3(x):
    hi = x.astype(BF)
    r1 = x - hi.astype(F32)
    mid = r1.astype(BF)
    lo = (r1 - mid.astype(F32)).astype(BF)
    return hi, mid, lo


def _top_blocks(s, qp, n_rounds):
    j = lax.broadcasted_iota(jnp.int32, s.shape, 0)
    cur = qp // SEL_BLOCK
    forced = (j == 0) | (j == cur) | (j == cur - 1)
    valid = j * SEL_BLOCK <= qp
    s = jnp.where(forced, BIG, s)
    s = jnp.where(valid, s, -BIG)
    sel = jnp.zeros(s.shape, F32)
    for _ in range(n_rounds):
        mx = jnp.max(s, axis=0, keepdims=True)
        jm = jnp.min(jnp.where(s == mx, j, 2 * NS_PAD), axis=0, keepdims=True)
        hit = j == jm
        sel = jnp.where(hit, 1.0, sel)
        s = jnp.where(hit, -3e38, s)
    return sel


def _masked_softmax_rows(s, axis):
    m = jnp.max(s, axis=axis, keepdims=True)
    e = jnp.exp(s - m)
    den = jnp.maximum(jnp.sum(e, axis=axis, keepdims=True), 1e-30)
    return e * jnp.where(m > 0.5 * NEG, 1.0 / den, 0.0)


def _nsa_prompt_kernel(q_ref, gate_ref, qg_ref, kc_ref, vc_ref, amat_ref, kaug_ref, vselt_ref, kwin_ref,
                       vwin_ref, bc_ref, bs_ref, bw_ref, o_ref, lg_ref, acc_ref, qa_ref, qh_ref, sa_ref, sb_ref,
                       m_ref, mx_ref, *, ncp, n_top):
    i = pl.program_id(0)
    qb = Q_BLOCK
    rows = GRP * qb
    start = i * qb
    qall = q_ref[...]
    gates = jax.nn.sigmoid(gate_ref[...])
    lane = lax.broadcasted_iota(jnp.int32, (qb, 2 * DH), 1)
    qpos_row = start + lax.broadcasted_iota(jnp.int32, (1, qb), 1)
    t_last = start // KT
    n_far = jnp.maximum(t_last - 1, 0)

    @pl.when(i == 0)
    def _():
        lg_ref[0:16, :] = jnp.zeros((16, rows), F32)

    o_cs = []
    for h in range(HKV):
        parts = []
        for g in range(GRP):
            hd = h * GRP + g
            slab = qall[:, (hd // 2) * 2 * DH:(hd // 2 + 1) * 2 * DH]
            mine = (lane >= DH) if hd % 2 else (lane < DH)
            ss = jnp.sum(jnp.where(mine, slab * slab, 0.0), axis=-1, keepdims=True)
            xn = jnp.where(mine, slab * lax.rsqrt(ss * (1.0 / DH) + RMS_EPS) * qg_ref[...] * (DH ** -0.5), 0.0)
            if hd % 2 != h:
                xn = pltpu.roll(xn, DH, 1)
            parts.append(xn)
        xf = jnp.concatenate(parts, axis=0)
        qh = xf.astype(BF)

        lg_ref[16:16 + ncp, :] = _dot_nt(kc_ref[...], qh)
        band0 = pl.multiple_of(i * 8, 8)
        lg_ref[pl.ds(band0, CBAND), :] = lg_ref[pl.ds(band0, CBAND), :] + bc_ref[h]
        tok = lax.broadcasted_iota(jnp.int32, (ncp, 1), 0)
        p_c = _masked_softmax_rows(jnp.where(tok < 8 * i + 8, lg_ref[16:16 + ncp, :], NEG), 0)
        o_c = _dot_tn(p_c.astype(BF), vc_ref[...])[:, h * DH:(h + 1) * DH]

        imp = p_c[:, 0:qb]
        for g in range(1, GRP):
            imp = imp + p_c[:, g * qb:(g + 1) * qb]
        hi, mid, lo = _split3(imp)
        amat = amat_ref[...]
        score = _dot(amat, hi) + _dot(amat, mid) + _dot(amat, lo)
        sel = _top_blocks(score, qpos_row, n_top)
        pen = ((sel.T - 1.0) * PEN).astype(BF)
        qh2 = (xf * LOG2E).astype(BF)
        qa_ref[h, 0] = jnp.concatenate([jnp.concatenate([pen[:, 0:SEL_HALF]] * GRP, axis=0), qh2], axis=1)
        qa_ref[h, 1] = jnp.concatenate([jnp.concatenate([pen[:, SEL_HALF:]] * GRP, axis=0), qh2], axis=1)
        qh_ref[h] = qh
        o_cs.append(o_c)

    def scores(h, t):
        qa = jnp.where(t < HALF_TILES, qa_ref[h, 0], qa_ref[h, 1])
        return _dot_nt(kaug_ref[t + 1], qa)

    def absorb(h, s, mx, t, m):
        m2 = jnp.maximum(m, mx)
        p = jnp.exp2(s - m2)
        acc_ref[h] = jnp.exp2(m - m2) * acc_ref[h] + _dot(vselt_ref[h, t + 1], p.astype(BF))
        return m2

    col_max = lambda s: jnp.max(s, axis=0, keepdims=True)

    def pair_body(k, carry):
        ms, mxa = list(carry[:HKV]), list(carry[HKV:])
        mxb = []
        for h in range(HKV):
            s1 = scores(h, 2 * k + 1)
            sb_ref[h] = s1
            mxb.append(col_max(s1))
            ms[h] = absorb(h, sa_ref[h], mxa[h], 2 * k, ms[h])
        for h in range(HKV):
            s2 = scores(h, 2 * k + 2)
            ms[h] = absorb(h, sb_ref[h], mxb[h], 2 * k + 1, ms[h])
            sa_ref[h] = s2
            mxa[h] = col_max(s2)
        return tuple(ms) + tuple(mxa)

    acc_ref[...] = jnp.zeros((HKV, 2 * DH, rows), F32)
    mxa = []
    for h in range(HKV):
        s0 = scores(h, 0)
        sa_ref[h] = s0
        mxa.append(col_max(s0))
    carry = tuple(jnp.full((1, rows), NEG, F32) for _ in range(HKV)) + tuple(mxa)
    carry = lax.fori_loop(0, n_far // 2, pair_body, carry)
    for h in range(HKV):
        m_ref[h] = carry[h]
        mx_ref[h] = carry[HKV + h]

    @pl.when(n_far % 2 == 1)
    def _():
        for h in range(HKV):
            m_ref[h] = absorb(h, sa_ref[h], mx_ref[h], n_far - 1, m_ref[h])

    kiota = lax.broadcasted_iota(jnp.int32, (KT, 1), 0)
    ms = [m_ref[h] for h in range(HKV)]
    for u in range(2):
        t = t_last - 1 + u
        for h in range(HKV):
            s = jnp.where(t * KT + kiota >= 0, scores(h, t) + bs_ref[i % 2, h, u], NEG)
            ms[h] = absorb(h, s, col_max(s), t, ms[h])

    for h in range(HKV):
        o_st = acc_ref[h, 0:DH, :] / jnp.maximum(acc_ref[h, DH:DH + 1, :], 1e-30)
        o_s = jnp.concatenate([o_st[:, g * qb:(g + 1) * qb].T for g in range(GRP)], axis=0)
        o_c = o_cs[h]
        qh = qh_ref[h]

        r = pl.multiple_of(start, qb)
        s = _dot_nt(qh, kwin_ref[pl.ds(r, WINDOW + qb), :]) + bw_ref[h]
        kpos = start - WINDOW + lax.broadcasted_iota(jnp.int32, (1, WINDOW + qb), 1)
        p_w = _masked_softmax_rows(jnp.where(kpos >= 0, s, NEG), 1)
        o_w = _dot(p_w.astype(BF), vwin_ref[pl.ds(r, WINDOW + qb), :])[:, h * DH:(h + 1) * DH]

        for g in range(GRP):
            c = (h * GRP + g) * 3
            rs = slice(g * qb, (g + 1) * qb)
            o_ref[:, (h * GRP + g) * DH:(h * GRP + g + 1) * DH] = (
                gates[:, c:c + 1] * o_c[rs] + gates[:, c + 1:c + 2] * o_s[rs] + gates[:, c + 2:c + 3] * o_w[rs])


def _bias_tables(rel_table):
    tab = rel_table.astype(F32)
    far = tab[REL_BUCKETS - 1]
    qb = Q_BLOCK

    def toeplitz(off, nk, dmax, shift):
        d = off - (nk - 1) + jnp.arange(nk + qb - 1, dtype=jnp.int32)
        w = jnp.where(((d >= 0) & (d <= dmax))[None, :], (tab[_t5_bucket(d)] - (far if shift else 0.0)).T, NEG)
        p = w.shape[1]
        hank = jnp.tile(w, (1, nk + 1))[:, :nk * (p + 1)].reshape(NH, nk, p + 1)[:, :, :qb]
        return hank[:, ::-1, :]

    def lanes_gq(b):
        return b.reshape(HKV, GRP, b.shape[1], qb).transpose(0, 2, 1, 3).reshape(HKV, b.shape[1], GRP * qb)

    big = 1 << 30
    bc = lanes_gq(toeplitz(16 * CMP_STRIDE - (CMP_LEN - 1), CBAND * CMP_STRIDE, big, True)[:, ::CMP_STRIDE, :])
    bs = jnp.stack([lanes_gq(toeplitz(off, 2 * KT, big, True)).reshape(HKV, 2, KT, GRP * qb)
                    for off in (KT, KT + qb)]) * LOG2E
    bw = lanes_gq(toeplitz(WINDOW, WINDOW + qb, WINDOW, False)).transpose(0, 2, 1)
    return bc, bs, bw


def nsa_prompt(proj, kc, vc, ksel, vsel, kwin, vwin, q_g, rel_table):
    t = proj.shape[0]
    nb = t // Q_BLOCK
    nch = kc.shape[0]
    ncp = -(-nch // 128) * 128
    assert t // SEL_BLOCK <= NS_PAD and t % KT == 0
    n_top = min(TOP_N, t // SEL_BLOCK)
    kcp = jnp.pad(kc, ((0, ncp - nch), (0, 0)))
    vcp = jnp.pad(vc, ((0, ncp - nch), (0, 0)))
    amat = jnp.asarray(_selection_matrix(nch - 1, NS_PAD, ncp), BF)
    onehot = ((jnp.arange(t, dtype=jnp.int32)[:, None] // SEL_BLOCK) % SEL_HALF
              == jnp.arange(SEL_HALF, dtype=jnp.int32)[None, :]).astype(BF)
    kaug = jnp.pad(jnp.concatenate([onehot, ksel.astype(BF)], axis=1), ((KT, 0), (0, 0))).reshape(-1, KT, 2 * KVW)
    vt = jnp.pad(vsel.astype(BF), ((KT, 0), (0, 0))).reshape(-1, KT, HKV, DH).transpose(2, 0, 3, 1)
    vselt = jnp.concatenate([vt, jnp.ones(vt.shape[:2] + (1, KT), BF), jnp.zeros(vt.shape[:2] + (DH - 1, KT), BF)],
                            axis=2)
    rows = GRP * Q_BLOCK
    kwinp = jnp.pad(kwin.astype(BF), ((WINDOW, 0), (0, 0)))
    vwinp = jnp.pad(vwin.astype(BF), ((WINDOW, 0), (0, 0)))
    bc, bs, bw = _bias_tables(rel_table)
    whole = pl.BlockSpec(memory_space=pltpu.VMEM)
    return pl.pallas_call(
        functools.partial(_nsa_prompt_kernel, ncp=ncp, n_top=n_top),
        grid=(nb,),
        in_specs=[pl.BlockSpec((Q_BLOCK, NH * DH), lambda i: (i, O_Q // (NH * DH))),
                  pl.BlockSpec((Q_BLOCK, 128), lambda i: (i, O_GATE // 128)),
                  whole, whole, whole, whole, whole, whole, whole, whole, whole, whole, whole],
        out_specs=pl.BlockSpec((Q_BLOCK, NH * DH), lambda i: (i, 0)),
        out_shape=jax.ShapeDtypeStruct((t, NH * DH), F32),
        scratch_shapes=[pltpu.VMEM((16 + ncp, GRP * Q_BLOCK), F32), pltpu.VMEM((HKV, 2 * DH, rows), F32),
                        pltpu.VMEM((HKV, 2, rows, SEL_HALF + KVW), BF), pltpu.VMEM((HKV, rows, KVW), BF),
                        pltpu.VMEM((HKV, KT, rows), F32), pltpu.VMEM((HKV, KT, rows), F32),
                        pltpu.VMEM((HKV, 1, rows), F32), pltpu.VMEM((HKV, 1, rows), F32)],
        compiler_params=_cp("arbitrary"),
        name="nsa_prompt",
    )(proj, proj, jnp.concatenate([q_g, q_g]).reshape(1, 2 * DH), kcp, vcp, amat, kaug, vselt, kwinp, vwinp,
      bc, bs, bw)


def _pick_head(x, lane_h):
    return jnp.where(lane_h == 0, x[0:DH], x[DH:2 * DH])


def _nsa_dec_pre_kernel(qbd_ref, kc_ref, vc_ref, amat_ref, bc_ref, win_ref, wnew_ref, bwa_ref, bwb_ref,
                        pen_ref, oc_ref, ow_ref, *, ts, pos0, n_top):
    r_all = GRP * HKV * ts
    qbd = qbd_ref[0]
    lane_h = (lax.broadcasted_iota(jnp.int32, (1, r_all), 1) // ts) % HKV

    p_c = _masked_softmax_rows(_dot(kc_ref[0], qbd) + bc_ref[...], 0)
    oc_ref[0] = _pick_head(_dot_tn(vc_ref[0], p_c.astype(BF)), lane_h)

    hi, mid, lo = _split3(p_c)
    amat = amat_ref[...]
    sc = _dot(amat, hi) + _dot(amat, mid) + _dot(amat, lo)
    w8 = HKV * ts
    score = sc[:, 0:w8]
    for g in range(1, GRP):
        score = score + sc[:, g * w8:(g + 1) * w8]
    qp = pos0 + lax.broadcasted_iota(jnp.int32, (1, w8), 1) % ts
    sel = _top_blocks(score, qp, n_top)
    pen_ref[0] = (jnp.concatenate([sel] * GRP, axis=1) - 1.0) * PEN

    win = win_ref[0]
    wnew = wnew_ref[0]
    s_a = _dot(win[:, 0:KVW].astype(BF), qbd) + bwa_ref[...]
    s_b = _dot(wnew[:, 0:KVW].astype(BF), qbd) + bwb_ref[...]
    m = jnp.maximum(jnp.max(s_a, axis=0, keepdims=True), jnp.max(s_b, axis=0, keepdims=True))
    e_a = jnp.where(s_a > 0.5 * NEG, jnp.exp(s_a - m), 0.0)
    e_b = jnp.where(s_b > 0.5 * NEG, jnp.exp(s_b - m), 0.0)
    den = jnp.maximum(jnp.sum(e_a, axis=0, keepdims=True) + jnp.sum(e_b, axis=0, keepdims=True), 1e-30)
    o_w = (_dot_tn(win[:, KVW:].astype(BF), e_a.astype(BF)) + _dot_tn(wnew[:, KVW:].astype(BF), e_b.astype(BF)))
    ow_ref[0] = _pick_head(o_w, lane_h) / den


SEL_PAGES = 32


def _nsa_dec_sel_kernel(*refs, npg, page, ts):
    pages = refs[1:1 + npg]
    (q_ref, pen_ref, expt_ref, blast_ref, knew_ref, vnew_ref, bnew_ref, gate_ref, oc_ref, ow_ref,
     o_ref, m_ref, l_ref, acc_ref) = refs[1 + npg:]
    j = pl.program_id(1)
    last = pl.num_programs(1) - 1
    rows = GRP * ts

    @pl.when(j == 0)
    def _():
        m_ref[...] = jnp.full((HKV * rows, 1), NEG, F32)
        l_ref[...] = jnp.zeros((HKV * rows, 1), F32)
        acc_ref[...] = jnp.zeros((HKV * rows, DH), F32)

    def online(s, pvs):
        m = m_ref[...]
        m2 = jnp.maximum(m, jnp.max(s, axis=-1, keepdims=True))
        a = jnp.exp(m - m2)
        p = jnp.exp(s - m2)
        m_ref[...] = m2
        l_ref[...] = a * l_ref[...] + jnp.sum(p, axis=-1, keepdims=True)
        pb = p.astype(BF)
        pv = jnp.concatenate([pvs[h](pb[h * rows:(h + 1) * rows]) for h in range(HKV)], axis=0)
        acc_ref[...] = a * acc_ref[...] + pv

    both = lambda f: jnp.concatenate([f(h) for h in range(HKV)], axis=0)
    vts = [jnp.concatenate([r[0, 1, h] for r in pages], axis=1).astype(BF) for h in range(HKV)]
    s = both(lambda h: _dot(q_ref[0, h], jnp.concatenate([r[0, 0, h] for r in pages], axis=1).astype(BF))
             + _dot(pen_ref[0, h, j], expt_ref[...]))
    s = s + jnp.where(j == last, blast_ref[...], 0.0)
    online(s, [functools.partial(lambda vt, p: _dot_nt(p, vt), vts[h]) for h in range(HKV)])

    @pl.when(j == last)
    def _():
        s_new = both(lambda h: _dot_nt(q_ref[0, h], knew_ref[0, h].astype(BF))) + bnew_ref[...]
        online(s_new, [functools.partial(lambda v, p: _dot(p, v), vnew_ref[0, h].astype(BF)) for h in range(HKV)])
        o_s = acc_ref[...] / jnp.maximum(l_ref[...], 1e-30)
        g = jax.nn.sigmoid(gate_ref[0])
        o_ref[0] = g[:, 0:1] * oc_ref[0] + g[:, 1:2] * o_s + g[:, 2:3] * ow_ref[0]


def _dec_bias_tables(rel_table, pos0, ts, nch, wb, page, npg):
    tab = rel_table.astype(F32)
    far = tab[REL_BUCKETS - 1]
    r = jnp.arange(GRP * HKV * ts, dtype=jnp.int32)
    head = ((r // ts) % HKV) * GRP + r // (HKV * ts)
    qpos = pos0 + r % ts

    def bias(kpos, ok, shift):
        dist = qpos[None, :] - kpos[:, None]
        b = tab[_t5_bucket(dist), head[None, :]] - (far[head][None, :] if shift else 0.0)
        return jnp.where((dist >= 0) & ok(dist), b, NEG)

    always = lambda d: d >= 0
    bc = bias(jnp.arange(nch, dtype=jnp.int32) * CMP_STRIDE + CMP_LEN - 1, always, False)
    bwa = bias(pos0 - wb + jnp.arange(wb, dtype=jnp.int32), lambda d: d <= WINDOW, False)
    tnew = jnp.arange(8, dtype=jnp.int32)
    newpos = jnp.where(tnew < ts, pos0 + tnew, pos0 + 2 * WINDOW + SEL_BLOCK)
    bwb = bias(newpos, lambda d: d <= WINDOW, False)
    bnew = bias(newpos, always, True)
    step = npg * page
    blast = jnp.zeros((step, r.shape[0]), F32).at[step - page:].set(
        bias(pos0 - page + jnp.arange(page, dtype=jnp.int32), always, True))
    return bc, bwa, bwb, bnew, blast


def nsa_decode(qn, gate_raw, kc, vc, cachet, page_table, win_state, kv_new, win_new, rel_table):
    bsz, ts, _ = qn.shape
    n_pages = page_table.shape[1]
    page = cachet.shape[-1]
    pos0 = n_pages * page
    nch = kc.shape[1]
    wb = win_state.shape[1]
    r_all = GRP * HKV * ts
    assert pos0 % SEL_BLOCK == 0 and ts <= 8 and ts <= SEL_BLOCK and nch % 8 == 0
    ns = -(-(pos0 + ts) // SEL_BLOCK)
    npg = min(SEL_PAGES, n_pages)
    assert n_pages % npg == 0
    bps = npg * page // SEL_BLOCK
    ns_pad = max(-(-ns // 8) * 8, (n_pages // npg) * bps)
    n_top = min(TOP_N, ns)

    q5 = qn.reshape(bsz, ts, HKV, GRP, DH)
    qbd = jnp.einsum('bqhgd,hk->bkdghq', q5, jnp.eye(HKV, dtype=F32)).reshape(bsz, 2 * DH, r_all).astype(BF)
    gate_t = gate_raw.reshape(bsz, ts, HKV, GRP, 3).transpose(0, 4, 3, 2, 1).reshape(bsz, 3, r_all)
    pad8 = lambda a: jnp.pad(a, ((0, 0), (0, 8 - ts), (0, 0)))
    amat = jnp.asarray(_selection_matrix(nch - 1, ns_pad, nch), BF)
    expand = jnp.asarray(np.repeat(np.eye(bps, dtype=np.float32), SEL_BLOCK, axis=0), BF)
    bc, bwa, bwb, bnew, blast = _dec_bias_tables(rel_table, pos0, ts, nch, wb, page, npg)

    full = lambda a: pl.BlockSpec(a.shape, functools.partial(lambda nd, *_: (0,) * nd, a.ndim))
    per_b = lambda a: pl.BlockSpec((1,) + a.shape[1:], functools.partial(lambda nd, b, *_: (b,) + (0,) * nd, a.ndim - 1))
    wnew = pad8(win_new)
    pre_in = [qbd, kc, vc, amat, bc, win_state, wnew, bwa, bwb]
    pre_specs = [per_b(qbd), per_b(kc), per_b(vc), full(amat), full(bc), per_b(win_state), per_b(wnew),
                 full(bwa), full(bwb)]
    small = jax.ShapeDtypeStruct((bsz, DH, r_all), F32)
    pen, o_c, o_w = pl.pallas_call(
        functools.partial(_nsa_dec_pre_kernel, ts=ts, pos0=pos0, n_top=n_top),
        grid=(bsz,), in_specs=pre_specs,
        out_specs=[pl.BlockSpec((1, ns_pad, r_all), lambda b: (b, 0, 0)),
                   pl.BlockSpec((1, DH, r_all), lambda b: (b, 0, 0)),
                   pl.BlockSpec((1, DH, r_all), lambda b: (b, 0, 0))],
        out_shape=[jax.ShapeDtypeStruct((bsz, ns_pad, r_all), F32), small, small],
        compiler_params=_cp("parallel"), name="nsa_dec_pre",
    )(*pre_in)

    rows = GRP * ts
    nsteps = n_pages // npg
    by_head = lambda a: a.reshape(a.shape[:-1] + (GRP, HKV, ts))
    q_h = q5.transpose(0, 2, 3, 1, 4).reshape(bsz, HKV, rows, DH).astype(BF)
    gate_h = gate_raw.reshape(bsz, ts, HKV, GRP, 3).transpose(0, 2, 3, 1, 4).reshape(bsz, HKV, rows, 3)
    pen_h = by_head(pen[:, :nsteps * bps].reshape(bsz, nsteps, bps, r_all)).transpose(0, 4, 1, 3, 5, 2).reshape(
        bsz, HKV, nsteps, rows, bps).astype(BF)
    oc_h, ow_h = (by_head(a).transpose(0, 3, 2, 4, 1).reshape(bsz, HKV, rows, DH) for a in (o_c, o_w))
    blast_h, bnew_h = (by_head(a).transpose(2, 1, 3, 0).reshape(HKV, rows, a.shape[0]) for a in (blast, bnew))
    new5 = pad8(kv_new).reshape(bsz, 8, 4, HKV, DH)
    knew = new5[:, :, 2].transpose(0, 2, 1, 3)
    vnew = new5[:, :, 3].transpose(0, 2, 1, 3)
    page_specs = [pl.BlockSpec((1, 2, HKV, DH, page),
                               functools.partial(lambda r, b, j, pt: (pt[b, j * npg + r], 1, 0, 0, 0), r))
                  for r in range(npg)]
    heads_rows = lambda a: a.reshape(a.shape[:-3] + (HKV * rows, a.shape[-1]))
    blast_h, bnew_h, gate_h, oc_h, ow_h = map(heads_rows, (blast_h, bnew_h, gate_h, oc_h, ow_h))
    sel_in = [q_h, pen_h, expand.T, blast_h, knew, vnew, bnew_h, gate_h, oc_h, ow_h]
    sel_specs = [per_b(q_h), per_b(pen_h), full(expand.T), full(blast_h), per_b(knew), per_b(vnew), full(bnew_h),
                 per_b(gate_h), per_b(oc_h), per_b(ow_h)]
    o_h = pl.pallas_call(
        functools.partial(_nsa_dec_sel_kernel, npg=npg, page=page, ts=ts),
        grid_spec=pltpu.PrefetchScalarGridSpec(
            num_scalar_prefetch=1, grid=(bsz, nsteps), in_specs=page_specs + sel_specs,
            out_specs=pl.BlockSpec((1, HKV * rows, DH), lambda b, j, pt: (b, 0, 0)),
            scratch_shapes=[pltpu.VMEM((HKV * rows, 1), F32), pltpu.VMEM((HKV * rows, 1), F32),
                            pltpu.VMEM((HKV * rows, DH), F32)]),
        out_shape=jax.ShapeDtypeStruct((bsz, HKV * rows, DH), F32),
        compiler_params=_cp("parallel", "arbitrary"), name="nsa_dec_sel",
    )(page_table, *([cachet] * npg), *sel_in)
    return o_h.reshape(bsz, HKV, GRP, ts, DH).transpose(0, 3, 1, 2, 4).reshape(bsz, ts, NH * DH)


def _even_weights(p, e):
    w_in = jnp.pad(p['att_w_in'][e], ((0, 0), (0, ATT_IN_PAD - ATT_IN_COLS))).astype(BF)
    w_out = p['att_w_out'][e].astype(BF)
    return dict(
        w_in=w_in, w_out_conv=w_out[:C_CONV], w_out_att=w_out[C_CONV:],
        cmp=_compress_weights(p['cmp_pe'][e], p['cmp_w1'][e], p['cmp_b1'][e], p['cmp_w2'][e], p['k_norm_g'][e][0]),
        k_g=p['k_norm_g'][e], q_g=p['q_norm_g'][e], conv_w=p['conv_w'][e], conv_b=p['conv_b'][e],
        ln_g=p['conv_ln_g'][e], ln_b=p['conv_ln_b'][e])


def _even_prompt(x2, norm_g, w, rel_table):
    t = x2.shape[0]
    proj = norm_matmul(x2, norm_g, w['w_in'])
    conv_y, conv_new = conformer_conv(proj[None], jnp.zeros((1, CONV_WIDTH - 1, C_CONV), F32),
                                      w['conv_w'], w['conv_b'], w['ln_g'], w['ln_b'])
    kv_new, win_new = kv_post(proj, w['k_g'])
    kc, vc = compress_dense(proj[None], w['cmp'])
    o = nsa_prompt(proj, kc[0], vc[0], kv_new[:, 2 * KVW:3 * KVW], kv_new[:, 3 * KVW:], win_new[:, :KVW],
                   win_new[:, KVW:], w['q_g'], rel_table)
    x2 = out_proj2(x2, conv_y[0], o, w['w_out_conv'], w['w_out_att'])
    keep = min(WINDOW, t)
    return (x2, kv_new.reshape(1, t, 4, HKV, DH), win_new[t - keep:].reshape(1, keep, 2, HKV, DH), conv_new)


def _even_decode(x3, norm_g, w, rel_table, cachet, page_table, win_state, conv_state):
    bsz, t, d = x3.shape
    x2 = x3.reshape(bsz * t, d)
    proj = norm_matmul(x2, norm_g, w['w_in'])
    proj3 = proj.reshape(bsz, t, ATT_IN_PAD)
    conv_y, conv_new = conformer_conv(proj3, conv_state, w['conv_w'], w['conv_b'], w['ln_g'], w['ln_b'])
    kv_new, win_new = kv_post(proj, w['k_g'])
    qn = q_norm(proj, w['q_g'])
    kc, vc = compress_paged(cachet, page_table, w['cmp'])
    wb = win_state.shape[1]
    o = nsa_decode(qn.reshape(bsz, t, NH * DH), proj3[:, :, O_GATE:ATT_IN_COLS], kc, vc, cachet, page_table,
                   win_state.reshape(bsz, wb, 2 * KVW), kv_new.reshape(bsz, t, 4 * KVW),
                   win_new.reshape(bsz, t, 2 * KVW), rel_table)
    x2 = out_proj2(x2, conv_y.reshape(bsz * t, C_CONV), o.reshape(bsz * t, NH * DH), w['w_out_conv'],
                   w['w_out_att'])
    win_all = jnp.concatenate([win_state, win_new.reshape(bsz, t, 2, HKV, DH)], axis=1)
    keep = min(WINDOW, wb + t)
    return (x2.reshape(bsz, t, d), kv_new.reshape(bsz, t, 4, HKV, DH), win_all[:, wb + t - keep:], conv_new)


def _odd_layer(x3, s0, norm_g, w_in_bf, w_out_bf, lb, hg_norm_g):
    bsz, t, d = x3.shape
    x2 = x3.reshape(bsz * t, d)
    z = norm_matmul(x2, norm_g, w_in_bf, tn=2048).reshape(bsz, t, 4 * d)
    tp = -(-t // HG_CHUNK) * HG_CHUNK
    if tp != t:
        z = jnp.pad(z, ((0, 0), (0, tp - t), (0, 0)))
    o, s_new = hgrn2(z, s0, lb, hg_norm_g, t)
    x2 = out_proj1(x2, o[:, :t].reshape(bsz * t, d), w_out_bf)
    return x2.reshape(bsz, t, d), s_new


def kernel(x_prompt, x_sample, cache_nsa_kv, page_table, state_nsa_win, state_conv, state_hgrn, rel_bias_table,
           norm_mix_g, norm_mlp_g, w_mlp_up, w_mlp_down, att_w_in, att_w_out, q_norm_g, k_norm_g, cmp_pe, cmp_w1,
           cmp_b1, cmp_w2, conv_w, conv_b, conv_ln_g, conv_ln_b, hg_w_in, hg_w_out, hg_lb_logits, hg_norm_g):
    p = dict(att_w_in=att_w_in, att_w_out=att_w_out, q_norm_g=q_norm_g, k_norm_g=k_norm_g, cmp_pe=cmp_pe,
             cmp_w1=cmp_w1, cmp_b1=cmp_b1, cmp_w2=cmp_w2, conv_w=conv_w, conv_b=conv_b, conv_ln_g=conv_ln_g,
             conv_ln_b=conv_ln_b)
    bp, tp_, d = x_prompt.shape
    assert bp == 1
    db, ts, _ = x_sample.shape
    w_up = w_mlp_up.astype(BF)
    w_down = w_mlp_down.astype(BF)
    cum = jnp.cumsum(jax.nn.softmax(hg_lb_logits.astype(F32), axis=0), axis=0)

    xp = x_prompt[0]
    kv_p, win_p, conv_p, hg_p = [], [], [], []
    for layer in range(norm_mix_g.shape[0]):
        if layer % 2 == 0:
            e = layer // 2
            w = _even_weights(p, e)
            xp, kv_new, win_new, conv_new = _even_prompt(xp, norm_mix_g[layer], w, rel_bias_table)
            kv_p.append(kv_new)
            win_p.append(win_new)
            conv_p.append(conv_new)
        else:
            o = layer // 2
            x3, s_new = _odd_layer(xp[None], jnp.zeros((1, HG_HEADS, HG_DK, HG_DV), F32), norm_mix_g[layer],
                                   hg_w_in[o].astype(BF), hg_w_out[o].astype(BF), cum[layer] - cum[0],
                                   hg_norm_g[o])
            xp = x3[0]
            hg_p.append(s_new)
        xp = mlp(xp, norm_mlp_g[layer], w_up[layer], w_down[layer])

    cachet = cache_nsa_kv.transpose(0, 1, 3, 4, 5, 2)
    xs = x_sample
    kv_s, win_s, conv_s, hg_s = [], [], [], []
    for layer in range(norm_mix_g.shape[0]):
        if layer % 2 == 0:
            e = layer // 2
            w = _even_weights(p, e)
            xs, kv_new, win_new, conv_new = _even_decode(xs, norm_mix_g[layer], w, rel_bias_table, cachet[e],
                                                         page_table, state_nsa_win[e], state_conv[e])
            kv_s.append(kv_new)
            win_s.append(win_new)
            conv_s.append(conv_new)
        else:
            o = layer // 2
            xs, s_new = _odd_layer(xs, state_hgrn[o], norm_mix_g[layer], hg_w_in[o].astype(BF),
                                   hg_w_out[o].astype(BF), cum[layer] - cum[0], hg_norm_g[o])
            hg_s.append(s_new)
        xs = mlp(xs.reshape(db * ts, d), norm_mlp_g[layer], w_up[layer], w_down[layer]).reshape(db, ts, d)
    return (xp[None], xs, jnp.stack(kv_p), jnp.stack(kv_s), jnp.stack(win_p), jnp.stack(win_s),
            jnp.stack(conv_p), jnp.stack(conv_s), jnp.stack(hg_p), jnp.stack(hg_s))
```

```python
import functools
import math

import jax
import jax.numpy as jnp
import numpy as np
from jax import lax
from jax.experimental import pallas as pl
from jax.experimental.pallas import tpu as pltpu

D_MODEL = 1024
C_CONV = 512
CONV_WIDTH = 31
DH = 64
HKV = 2
GRP = 4
NH = HKV * GRP
CMP_STRIDE = 16
CMP_LEN = 32
CMP_HIDDEN = 128
SEL_BLOCK = 64
TOP_N = 16
WINDOW = 512
Q_BLOCK = 128
REL_BUCKETS = 32
REL_EXACT = 16
REL_MAX_DIST = 128
HG_DK = 128
HG_HEADS = 8
HG_DV = 128
HG_CHUNK = 64
RMS_EPS = 1e-6
NEG = -1e30
BIG = 1e9
O_Q = 2 * C_CONV
O_KV = O_Q + NH * DH
O_GATE = O_KV + 3 * 2 * HKV * DH
ATT_IN_COLS = O_GATE + 3 * NH
ATT_IN_PAD = 2432
KVW = HKV * DH

VMEM_LIMIT = 56 * 1024 * 1024
BF = jnp.bfloat16
F32 = jnp.float32


def _cp(*sem):
    return pltpu.CompilerParams(dimension_semantics=sem, vmem_limit_bytes=VMEM_LIMIT)


def _dot(a, b):
    return jnp.dot(a, b, preferred_element_type=F32)


def _dot_nt(a, b):
    return lax.dot_general(a, b, (((1,), (1,)), ((), ())), preferred_element_type=F32)


def _dot_tn(a, b):
    return lax.dot_general(a, b, (((0,), (0,)), ((), ())), preferred_element_type=F32)


def _rms(x, g):
    return x * lax.rsqrt(jnp.mean(x * x, axis=-1, keepdims=True) + RMS_EPS) * g


def _norm_matmul_kernel(x_ref, g_ref, w_ref, o_ref):
    hn = _rms(x_ref[...], g_ref[...]).astype(BF)
    o_ref[...] = _dot(hn, w_ref[...])


def norm_matmul(x, g, w_bf, tn=None):
    m, d = x.shape
    n = w_bf.shape[1]
    tm = min(m, 512)
    tn = n if tn is None else tn
    return pl.pallas_call(
        _norm_matmul_kernel,
        grid=(m // tm, n // tn),
        in_specs=[pl.BlockSpec((tm, d), lambda i, j: (i, 0)),
                  pl.BlockSpec((1, d), lambda i, j: (0, 0)),
                  pl.BlockSpec((d, tn), lambda i, j: (0, j))],
        out_specs=pl.BlockSpec((tm, tn), lambda i, j: (i, j)),
        out_shape=jax.ShapeDtypeStruct((m, n), F32),
        compiler_params=_cp("parallel", "arbitrary"),
        name="norm_matmul",
    )(x, g.reshape(1, d), w_bf)


def _out_proj_kernel(r_ref, a1_ref, a2_ref, w1_ref, w2_ref, o_ref):
    o_ref[...] = (r_ref[...] + _dot(a1_ref[...].astype(BF), w1_ref[...])
                  + _dot(a2_ref[...].astype(BF), w2_ref[...]))


def out_proj2(res, a1, a2, w1_bf, w2_bf):
    m, d = res.shape
    k1, k2 = a1.shape[1], a2.shape[1]
    tm = min(m, 512)
    return pl.pallas_call(
        _out_proj_kernel,
        grid=(m // tm,),
        in_specs=[pl.BlockSpec((tm, d), lambda i: (i, 0)),
                  pl.BlockSpec((tm, k1), lambda i: (i, 0)),
                  pl.BlockSpec((tm, k2), lambda i: (i, 0)),
                  pl.BlockSpec((k1, d), lambda i: (0, 0)),
                  pl.BlockSpec((k2, d), lambda i: (0, 0))],
        out_specs=pl.BlockSpec((tm, d), lambda i: (i, 0)),
        out_shape=jax.ShapeDtypeStruct((m, d), F32),
        compiler_params=_cp("parallel"),
        name="out_proj2",
    )(res, a1, a2, w1_bf, w2_bf)


def _out_proj1_kernel(r_ref, a_ref, w_ref, o_ref):
    o_ref[...] = r_ref[...] + _dot(a_ref[...].astype(BF), w_ref[...])


def out_proj1(res, a, w_bf):
    m, d = res.shape
    k = a.shape[1]
    tm = min(m, 512)
    return pl.pallas_call(
        _out_proj1_kernel,
        grid=(m // tm,),
        in_specs=[pl.BlockSpec((tm, d), lambda i: (i, 0)),
                  pl.BlockSpec((tm, k), lambda i: (i, 0)),
                  pl.BlockSpec((k, d), lambda i: (0, 0))],
        out_specs=pl.BlockSpec((tm, d), lambda i: (i, 0)),
        out_shape=jax.ShapeDtypeStruct((m, d), F32),
        compiler_params=_cp("parallel"),
        name="out_proj1",
    )(res, a, w_bf)


def _mlp_kernel(x_ref, g_ref, wu_ref, wd_ref, o_ref, hn_ref, acc_ref):
    j = pl.program_id(1)

    @pl.when(j == 0)
    def _():
        hn_ref[...] = _rms(x_ref[...], g_ref[...]).astype(BF)
        acc_ref[...] = x_ref[...]

    hid = jnp.maximum(_dot(hn_ref[...], wu_ref[...]), 0.0)
    acc_ref[...] += _dot((hid * hid).astype(BF), wd_ref[...])

    @pl.when(j == pl.num_programs(1) - 1)
    def _():
        o_ref[...] = acc_ref[...]


def mlp(x, g, wu_bf, wd_bf):
    m, d = x.shape
    hdim = wu_bf.shape[1]
    tm = min(m, 1024)
    th = 1024
    return pl.pallas_call(
        _mlp_kernel,
        grid=(m // tm, hdim // th),
        in_specs=[pl.BlockSpec((tm, d), lambda i, j: (i, 0)),
                  pl.BlockSpec((1, d), lambda i, j: (0, 0)),
                  pl.BlockSpec((d, th), lambda i, j: (0, j)),
                  pl.BlockSpec((th, d), lambda i, j: (j, 0))],
        out_specs=pl.BlockSpec((tm, d), lambda i, j: (i, 0)),
        out_shape=jax.ShapeDtypeStruct((m, d), F32),
        scratch_shapes=[pltpu.VMEM((tm, d), BF), pltpu.VMEM((tm, d), F32)],
        compiler_params=_cp("parallel", "arbitrary"),
        name="mlp",
    )(x, g.reshape(1, d), wu_bf, wd_bf)


CONV_HALO = 32
CONV_PAD = CONV_HALO - (CONV_WIDTH - 1)


def _conv_kernel(u_ref, st_ref, w_ref, b_ref, lg_ref, lb_ref, y_ref, new_ref, xin_ref, *, tt):
    t = pl.program_id(1)

    @pl.when(t == 0)
    def _():
        xin_ref[0:CONV_HALO, :] = st_ref[0]

    a = u_ref[0, :, 0:C_CONV]
    gt = u_ref[0, :, C_CONV:2 * C_CONV]
    xin_ref[CONV_HALO:CONV_HALO + tt, :] = a * jax.nn.sigmoid(gt)

    cols = []
    for c in range(C_CONV // 128):
        cs = slice(c * 128, (c + 1) * 128)
        acc = jnp.zeros((tt, 128), F32) + b_ref[:, cs]
        for k in range(CONV_WIDTH):
            acc = acc + w_ref[k:k + 1, cs] * xin_ref[CONV_PAD + k:CONV_PAD + k + tt, cs]
        cols.append(acc)
    y = jnp.concatenate(cols, axis=1)
    mu = jnp.mean(y, axis=-1, keepdims=True)
    yc = y - mu
    var = jnp.mean(yc * yc, axis=-1, keepdims=True)
    z = yc * lax.rsqrt(var + RMS_EPS) * lg_ref[...] + lb_ref[...]
    y_ref[0] = z * jax.nn.sigmoid(z)

    @pl.when(t == pl.num_programs(1) - 1)
    def _():
        new_ref[0] = xin_ref[tt + CONV_PAD:tt + CONV_HALO, :]

    if tt >= CONV_HALO:
        @pl.when(t < pl.num_programs(1) - 1)
        def _():
            xin_ref[0:CONV_HALO, :] = xin_ref[tt:tt + CONV_HALO, :]


def conformer_conv(proj, state, w, b, ln_g, ln_b):
    bsz, t, _ = proj.shape
    tt = min(t, 256)
    assert t % tt == 0 and (t == tt or tt >= CONV_HALO)
    st = jnp.pad(state, ((0, 0), (CONV_PAD, 0), (0, 0)))
    row = lambda v: v.reshape(1, C_CONV)
    return pl.pallas_call(
        functools.partial(_conv_kernel, tt=tt),
        grid=(bsz, t // tt),
        in_specs=[pl.BlockSpec((1, tt, 2 * C_CONV), lambda i, j: (i, j, 0)),
                  pl.BlockSpec((1, CONV_HALO, C_CONV), lambda i, j: (i, 0, 0)),
                  pl.BlockSpec((CONV_WIDTH, C_CONV), lambda i, j: (0, 0)),
                  pl.BlockSpec((1, C_CONV), lambda i, j: (0, 0)),
                  pl.BlockSpec((1, C_CONV), lambda i, j: (0, 0)),
                  pl.BlockSpec((1, C_CONV), lambda i, j: (0, 0))],
        out_specs=[pl.BlockSpec((1, tt, C_CONV), lambda i, j: (i, j, 0)),
                   pl.BlockSpec((1, CONV_WIDTH - 1, C_CONV), lambda i, j: (i, 0, 0))],
        out_shape=[jax.ShapeDtypeStruct((bsz, t, C_CONV), F32),
                   jax.ShapeDtypeStruct((bsz, CONV_WIDTH - 1, C_CONV), F32)],
        scratch_shapes=[pltpu.VMEM((CONV_HALO + tt, C_CONV), F32)],
        compiler_params=_cp("parallel", "arbitrary"),
        name="conformer_conv",
    )(proj, st, w, row(b), row(ln_g), row(ln_b))


HG_LEVELS = (32, 16, 8, 4, 2, 1)


def _hgrn_tables():
    c = HG_CHUNK
    idx = np.arange(c)
    mats = [(idx[None, :] <= idx[:, None]), (idx[None, :] > idx[:, None])]
    masks = []
    for h in HG_LEVELS:
        blk = idx // (2 * h)
        upper = (idx % (2 * h)) >= h
        piv = blk * 2 * h + h - 1
        mats.append((upper[:, None] & (idx[None, :] > piv[:, None]) & (idx[None, :] <= idx[:, None]))
                    | ((~upper)[:, None] & (idx[None, :] > idx[:, None]) & (idx[None, :] <= piv[:, None])))
        masks.append((blk[:, None] == blk[None, :]) & upper[:, None] & (~upper)[None, :])
    masks.append(np.eye(c, dtype=bool))
    return (np.concatenate(mats, axis=0).astype(np.float32),
            np.stack(masks).astype(np.float32))


HG_HEADS_PER_STEP = 8


def _hgrn_kernel(q_ref, fz_ref, v_ref, g_ref, lb_ref, ng_ref, s0_ref, tab_ref, msk_ref,
                 o_ref, sn_ref, st_ref, *, t_valid, nh):
    ci = pl.program_id(2)
    c = HG_CHUNK

    @pl.when(ci == 0)
    def _():
        for hh in range(nh):
            st_ref[hh] = s0_ref[0, hh].T

    row = ci * c + lax.broadcasted_iota(jnp.int32, (c, 1), 0)
    live = row < t_valid
    tab = tab_ref[...]
    for hh in range(nh):
        cs = slice(hh * HG_DK, (hh + 1) * HG_DK)
        lb = lb_ref[:, cs]
        f = lb + (1.0 - lb) * jax.nn.sigmoid(fz_ref[0, :, cs])
        lf = jnp.where(live, jnp.log(f), 0.0)
        k = jnp.where(live, 1.0 - f, 0.0)
        q = q_ref[0, :, cs]
        v = v_ref[0, :, cs]

        hi = lf.astype(BF)
        ex2 = _dot(tab, jnp.concatenate([hi, (lf - hi.astype(F32)).astype(BF)], axis=1))
        ex = ex2[:, 0:HG_DK] + ex2[:, HG_DK:]

        attn = jnp.zeros((c, c), F32)
        for li in range(len(HG_LEVELS)):
            fac = jnp.exp(ex[(2 + li) * c:(3 + li) * c])
            attn = attn + msk_ref[li] * _dot_nt((q * fac).astype(BF), (k * fac).astype(BF))
        attn = attn + msk_ref[len(HG_LEVELS)] * _dot_nt(q.astype(BF), k.astype(BF))

        st = st_ref[hh]
        qb = (q * jnp.exp(ex[0:c])).astype(BF)
        o = _dot(attn.astype(BF), v.astype(BF)) + _dot_nt(qb, st.astype(BF))
        ke = (k * jnp.exp(ex[c:2 * c])).astype(BF)
        decay = jnp.exp(ex[c - 1:c])
        st_ref[hh] = st * decay + _dot(v.T.astype(BF), ke)

        gate = g_ref[0, :, cs]
        o_ref[0, :, cs] = _rms(o, ng_ref[...]) * (gate * jax.nn.sigmoid(gate))

    @pl.when(ci == pl.num_programs(2) - 1)
    def _():
        for hh in range(nh):
            sn_ref[0, hh] = st_ref[hh].T


def hgrn2(z, s0, lb, norm_g, t_valid):
    bsz, tp, _ = z.shape
    c = HG_CHUNK
    nc = tp // c
    tab, msk = _hgrn_tables()
    nh = HG_HEADS_PER_STEP
    ng = HG_HEADS // nh
    w = nh * HG_DK
    blk = lambda off: pl.BlockSpec((1, c, w), lambda b, h, i: (b, i, off + h))
    return pl.pallas_call(
        functools.partial(_hgrn_kernel, t_valid=t_valid, nh=nh),
        grid=(bsz, ng, nc),
        in_specs=[blk(0), blk(ng), blk(2 * ng), blk(3 * ng),
                  pl.BlockSpec((1, w), lambda b, h, i: (0, h)),
                  pl.BlockSpec((1, HG_DV), lambda b, h, i: (0, 0)),
                  pl.BlockSpec((1, nh, HG_DK, HG_DV), lambda b, h, i: (b, h, 0, 0)),
                  pl.BlockSpec(tab.shape, lambda b, h, i: (0, 0)),
                  pl.BlockSpec(msk.shape, lambda b, h, i: (0, 0, 0))],
        out_specs=[pl.BlockSpec((1, c, w), lambda b, h, i: (b, i, h)),
                   pl.BlockSpec((1, nh, HG_DK, HG_DV), lambda b, h, i: (b, h, 0, 0))],
        out_shape=[jax.ShapeDtypeStruct((bsz, tp, D_MODEL), F32),
                   jax.ShapeDtypeStruct((bsz, HG_HEADS, HG_DK, HG_DV), F32)],
        scratch_shapes=[pltpu.VMEM((nh, HG_DV, HG_DK), F32)],
        compiler_params=_cp("parallel", "parallel", "arbitrary"),
        name="hgrn2",
    )(z, z, z, z, lb.reshape(1, D_MODEL), norm_g.reshape(1, HG_DV), s0,
      jnp.asarray(tab, BF), jnp.asarray(msk))


def _pair_norm(seg, g):
    lo = lax.broadcasted_iota(jnp.int32, seg.shape, 1) < DH
    sq = seg * seg
    s0 = jnp.sum(jnp.where(lo, sq, 0.0), axis=-1, keepdims=True)
    s1 = jnp.sum(jnp.where(lo, 0.0, sq), axis=-1, keepdims=True)
    inv = jnp.where(lo, lax.rsqrt(s0 * (1.0 / DH) + RMS_EPS), lax.rsqrt(s1 * (1.0 / DH) + RMS_EPS))
    return seg * inv * g


def _kv_post_kernel(p_ref, kg_ref, kv_ref, win_ref):
    x = p_ref[...]
    kv_ref[:, 0:2 * KVW] = x[:, 0:2 * KVW]
    kv_ref[:, 2 * KVW:3 * KVW] = _pair_norm(x[:, 2 * KVW:3 * KVW], kg_ref[1:2, :])
    kv_ref[:, 3 * KVW:4 * KVW] = x[:, 3 * KVW:4 * KVW]
    win_ref[:, 0:KVW] = _pair_norm(x[:, 4 * KVW:5 * KVW], kg_ref[2:3, :])
    win_ref[:, KVW:2 * KVW] = x[:, 5 * KVW:6 * KVW]


def kv_post(proj, k_g):
    m = proj.shape[0]
    tm = min(m, 1024)
    kvcols = 6 * KVW
    assert O_KV % kvcols == 0
    return pl.pallas_call(
        _kv_post_kernel,
        grid=(m // tm,),
        in_specs=[pl.BlockSpec((tm, kvcols), lambda i: (i, O_KV // kvcols)),
                  pl.BlockSpec((3, KVW), lambda i: (0, 0))],
        out_specs=[pl.BlockSpec((tm, 4 * KVW), lambda i: (i, 0)),
                   pl.BlockSpec((tm, 2 * KVW), lambda i: (i, 0))],
        out_shape=[jax.ShapeDtypeStruct((m, 4 * KVW), F32), jax.ShapeDtypeStruct((m, 2 * KVW), F32)],
        compiler_params=_cp("parallel"),
        name="kv_post",
    )(proj, jnp.concatenate([k_g, k_g], axis=1))


def _q_norm_kernel(q_ref, g_ref, o_ref):
    for c in range(NH // 2):
        cs = slice(c * 2 * DH, (c + 1) * 2 * DH)
        o_ref[:, cs] = _pair_norm(q_ref[:, cs], g_ref[...]) * (DH ** -0.5)


def q_norm(proj, q_g):
    m = proj.shape[0]
    tm = min(m, 1024)
    return pl.pallas_call(
        _q_norm_kernel,
        grid=(m // tm,),
        in_specs=[pl.BlockSpec((tm, NH * DH), lambda i: (i, O_Q // (NH * DH))),
                  pl.BlockSpec((1, 2 * DH), lambda i: (0, 0))],
        out_specs=pl.BlockSpec((tm, NH * DH), lambda i: (i, 0)),
        out_shape=jax.ShapeDtypeStruct((m, NH * DH), F32),
        compiler_params=_cp("parallel"),
        name="q_norm",
    )(proj, jnp.concatenate([q_g, q_g]).reshape(1, 2 * DH))


def _compress_kernel(*refs, n_prefetch, n_src, rows, nch):
    refs = refs[n_prefetch:]
    srcs = (refs[:n_src], refs[n_src:2 * n_src])
    pef_ref, pes_ref, w1f_ref, w1s_ref, b1_ref, w2_ref, kg_ref, kc_ref, vc_ref, hf_ref, hs_ref = refs[2 * n_src:]
    j = pl.program_id(1)
    cpr = rows // CMP_STRIDE
    m = cpr * n_src

    @pl.when(j == 0)
    def _():
        hs_ref[:, nch:nch + 8, :] = jnp.zeros((2, 8, 2 * CMP_HIDDEN), F32)

    for kind in range(2):
        accf = jnp.zeros((m, 2 * CMP_HIDDEN), F32)
        accs = jnp.zeros((m, 2 * CMP_HIDDEN), F32)
        for s in range(CMP_STRIDE):
            xs = jnp.concatenate(
                [r[0, pl.ds(s, cpr, stride=CMP_STRIDE), :] for r in srcs[kind]], axis=0)
            accf = accf + _dot((xs + pef_ref[kind, s:s + 1, :]).astype(BF), w1f_ref[kind, s])
            accs = accs + _dot((xs + pes_ref[kind, s:s + 1, :]).astype(BF), w1s_ref[kind, s])
        row0 = pl.multiple_of(j * m, 8)
        hf_ref[kind, pl.ds(row0, m), :] = accf
        hs_ref[kind, pl.ds(row0, m), :] = accs

    @pl.when(j == pl.num_programs(1) - 1)
    def _():
        for kind in range(2):
            hid = jax.nn.gelu(hf_ref[kind, 0:nch, :] + hs_ref[kind, 1:nch + 1, :] + b1_ref[kind])
            out = _dot(hid.astype(BF), w2_ref[kind])
            if kind == 0:
                kc_ref[0] = _pair_norm(out, kg_ref[...]).astype(BF)
            else:
                vc_ref[0] = out.astype(BF)


def _compress_weights(pe, w1, b1, w2, kg0):
    eye = jnp.eye(HKV, dtype=F32)
    bd = lambda w: jnp.einsum('ab,ksdf->ksadbf', eye, w).reshape(2, CMP_STRIDE, KVW, 2 * CMP_HIDDEN).astype(BF)
    tile2 = lambda a: jnp.concatenate([a, a], axis=-1)
    w2bd = jnp.einsum('ab,kfd->kafbd', eye, w2).reshape(2, 2 * CMP_HIDDEN, KVW).astype(BF)
    return (tile2(pe[:, :CMP_STRIDE]), tile2(pe[:, CMP_STRIDE:]), bd(w1[:, :CMP_STRIDE]), bd(w1[:, CMP_STRIDE:]),
            tile2(b1).reshape(2, 1, 2 * CMP_HIDDEN), w2bd, tile2(kg0).reshape(1, KVW))


def _compress_common(n_prefetch, n_src, rows, nch, bsz, weights):
    wspecs = [pl.BlockSpec(w.shape, functools.partial(lambda nd, *a: (0,) * nd, w.ndim)) for w in weights]
    out_specs = [pl.BlockSpec((1, nch, KVW), lambda b, j, *a: (b, 0, 0))] * 2
    out_shape = [jax.ShapeDtypeStruct((bsz, nch, KVW), BF)] * 2
    scratch = [pltpu.VMEM((2, nch + 8, 2 * CMP_HIDDEN), F32)] * 2
    kern = functools.partial(_compress_kernel, n_prefetch=n_prefetch, n_src=n_src, rows=rows, nch=nch)
    return kern, wspecs, out_specs, out_shape, scratch


def compress_dense(proj3, weights):
    bsz, t, _ = proj3.shape
    rows = min(t, 2048)
    nch = t // CMP_STRIDE
    assert O_KV % KVW == 0
    kern, wspecs, out_specs, out_shape, scratch = _compress_common(0, 1, rows, nch, bsz, weights)
    src = lambda kind: pl.BlockSpec((1, rows, KVW), lambda b, j: (b, j, O_KV // KVW + kind))
    return pl.pallas_call(
        kern, grid=(bsz, t // rows),
        in_specs=[src(0), src(1)] + wspecs,
        out_specs=out_specs, out_shape=out_shape, scratch_shapes=scratch,
        compiler_params=_cp("parallel", "arbitrary"), name="compress_dense",
    )(proj3, proj3, *weights)


CMP_PAGES = 16
CMP_PE_ROWS = 16


def _compress_paged_kernel(*refs, npg, page, nch):
    pages = refs[1:1 + npg]
    pe2_ref, w1p_ref, b1_ref, w2_ref, kg_ref, kc_ref, vc_ref, hf_ref, hs_ref, xs_ref = refs[1 + npg:]
    j = pl.program_id(1)
    m = npg * page // CMP_STRIDE
    hid2 = 2 * CMP_HIDDEN

    @pl.when(j == 0)
    def _():
        hs_ref[:, nch:nch + 8, :] = jnp.zeros((2, 8, hid2), F32)

    for kind in range(2):
        for r in range(npg):
            xt = jnp.concatenate([pages[r][0, kind, h] for h in range(HKV)], axis=0)
            xs_ref[kind, r * page:(r + 1) * page, :] = xt.T
        acc = jnp.zeros((m + CMP_PE_ROWS, 2 * hid2), F32)
        for s2 in range(CMP_STRIDE // 2):
            xa = xs_ref[kind, pl.ds(2 * s2, m, stride=CMP_STRIDE), :]
            xb = xs_ref[kind, pl.ds(2 * s2 + 1, m, stride=CMP_STRIDE), :]
            lhs = jnp.concatenate([jnp.concatenate([xa, xb], axis=1).astype(BF), pe2_ref[kind, s2]], axis=0)
            acc = acc + _dot(lhs, w1p_ref[kind, s2])
        row0 = pl.multiple_of(j * m, 8)
        hf_ref[kind, pl.ds(row0, m), :] = acc[0:m, 0:hid2] + acc[m:m + 1, 0:hid2]
        hs_ref[kind, pl.ds(row0, m), :] = acc[0:m, hid2:] + acc[m + 1:m + 2, hid2:]

    @pl.when(j == pl.num_programs(1) - 1)
    def _():
        for kind in range(2):
            hid = jax.nn.gelu(hf_ref[kind, 0:nch, :] + hs_ref[kind, 1:nch + 1, :] + b1_ref[kind])
            out = _dot(hid.astype(BF), w2_ref[kind])
            if kind == 0:
                kc_ref[0] = _pair_norm(out, kg_ref[...]).astype(BF)
            else:
                vc_ref[0] = out.astype(BF)


def compress_paged(cachet, page_table, weights):
    bsz, n_pages = page_table.shape
    page = cachet.shape[-1]
    npg = min(CMP_PAGES, n_pages)
    nch = n_pages * page // CMP_STRIDE
    src_specs = [pl.BlockSpec((1, 2, HKV, DH, page),
                              functools.partial(lambda r, b, j, pt: (pt[b, j * npg + r], 0, 0, 0, 0), r))
                 for r in range(npg)]
    pef, pes, w1f, w1s, b1, w2, kg = weights
    np2 = CMP_STRIDE // 2
    pair_rows = lambda a: a.reshape(2, np2, 2 * KVW)
    pe2 = jnp.zeros((2, np2, CMP_PE_ROWS, 2 * KVW), F32).at[:, :, 0].set(pair_rows(pef)).at[:, :, 1].set(
        pair_rows(pes)).astype(BF)
    pair_w = lambda w: w.reshape(2, np2, 2 * KVW, 2 * CMP_HIDDEN)
    w1p = jnp.concatenate([pair_w(w1f), pair_w(w1s)], axis=-1)
    weights = (pe2, w1p, b1, w2, kg)
    _, wspecs, out_specs, out_shape, scratch = _compress_common(1, npg, page, nch, bsz, weights)
    return pl.pallas_call(
        functools.partial(_compress_paged_kernel, npg=npg, page=page, nch=nch),
        grid_spec=pltpu.PrefetchScalarGridSpec(
            num_scalar_prefetch=1, grid=(bsz, n_pages // npg), in_specs=src_specs + wspecs,
            out_specs=out_specs, scratch_shapes=scratch + [pltpu.VMEM((2, npg * page, KVW), F32)]),
        out_shape=out_shape, compiler_params=_cp("parallel", "arbitrary"), name="compress_paged",
    )(page_table, *([cachet] * npg), *weights)


NS_PAD = 256
SEL_HALF = 128
LOG2E = math.log2(math.e)
PEN = 30000.0
KT = 256
SEL_SPLIT = 2
VROWS = 80
HALF_TILES = SEL_HALF * SEL_BLOCK // KT
CBAND = 24


def _t5_bucket(dist):
    n = jnp.maximum(dist, 0)
    large = REL_EXACT + (jnp.log(jnp.maximum(n, 1).astype(F32) / REL_EXACT)
                         / math.log(REL_MAX_DIST / REL_EXACT) * (REL_BUCKETS - REL_EXACT)).astype(jnp.int32)
    return jnp.where(n < REL_EXACT, n, jnp.minimum(large, REL_BUCKETS - 1))


def _selection_matrix(nc, ns_pad, nc_pad):
    ratio, span = SEL_BLOCK // CMP_STRIDE, CMP_LEN // CMP_STRIDE
    a = np.zeros((ns_pad, nc_pad), np.float32)
    for j in range(ns_pad):
        for mm in range(ratio):
            for nn in range(span):
                n = ratio * j + mm - nn
                if 0 <= n < nc:
                    a[j, n] += 1.0
    return a


def _split3(x):
    hi = x.astype(BF)
    r1 = x - hi.astype(F32)
    mid = r1.astype(BF)
    lo = (r1 - mid.astype(F32)).astype(BF)
    return hi, mid, lo


def _top_blocks(s, qp, n_rounds):
    j = lax.broadcasted_iota(jnp.int32, s.shape, 0)
    cur = qp // SEL_BLOCK
    forced = (j == 0) | (j == cur) | (j == cur - 1)
    valid = j * SEL_BLOCK <= qp
    s = jnp.where(forced, BIG, s)
    s = jnp.where(valid, s, -BIG)
    sel = jnp.zeros(s.shape, F32)
    for _ in range(n_rounds):
        mx = jnp.max(s, axis=0, keepdims=True)
        jm = jnp.min(jnp.where(s == mx, j, 2 * NS_PAD), axis=0, keepdims=True)
        hit = j == jm
        sel = jnp.where(hit, 1.0, sel)
        s = jnp.where(hit, -3e38, s)
    return sel


def _masked_softmax_rows(s, axis):
    m = jnp.max(s, axis=axis, keepdims=True)
    e = jnp.exp(s - m)
    den = jnp.maximum(jnp.sum(e, axis=axis, keepdims=True), 1e-30)
    return e * jnp.where(m > 0.5 * NEG, 1.0 / den, 0.0)


def _nsa_prompt_kernel(q_ref, gate_ref, qg_ref, kc_ref, vct_ref, amat_ref, kaug_ref, vselt_ref, kwin_ref,
                       vwint_ref, bc_ref, bs_ref, bw_ref, o_ref, lg_ref, acc_ref, qa_ref, qh_ref, sa_ref, sb_ref,
                       m_ref, mx_ref, *, ncp, n_top):
    i = pl.program_id(0)
    qb = Q_BLOCK
    rows = GRP * qb
    start = i * qb
    qall = q_ref[...]
    lane = lax.broadcasted_iota(jnp.int32, (qb, 2 * DH), 1)
    qpos_row = start + lax.broadcasted_iota(jnp.int32, (1, qb), 1)
    t_last = start // KT
    n_far = jnp.maximum(t_last - 1, 0)

    @pl.when(i == 0)
    def _():
        lg_ref[:, 0:16, :] = jnp.zeros((HKV, 16, rows), F32)

    o_cs = []
    for h in range(HKV):
        parts = []
        for g in range(GRP):
            hd = h * GRP + g
            slab = qall[:, (hd // 2) * 2 * DH:(hd // 2 + 1) * 2 * DH]
            mine = (lane >= DH) if hd % 2 else (lane < DH)
            ss = jnp.sum(jnp.where(mine, slab * slab, 0.0), axis=-1, keepdims=True)
            xn = jnp.where(mine, slab * lax.rsqrt(ss * (1.0 / DH) + RMS_EPS) * qg_ref[...] * (DH ** -0.5), 0.0)
            if hd % 2 != h:
                xn = pltpu.roll(xn, DH, 1)
            parts.append(xn)
        xf = jnp.concatenate(parts, axis=0)
        qh = xf.astype(BF)

        lg_ref[h, 16:16 + ncp, :] = _dot_nt(kc_ref[...], qh)
        band0 = pl.multiple_of(i * 8, 8)
        lg_ref[h, pl.ds(band0, CBAND), :] = lg_ref[h, pl.ds(band0, CBAND), :] + bc_ref[h]
        tok = lax.broadcasted_iota(jnp.int32, (ncp, 1), 0)
        p_c = _masked_softmax_rows(jnp.where(tok < 8 * i + 8, lg_ref[h, 16:16 + ncp, :], NEG), 0)
        o_c = _dot(vct_ref[h], p_c.astype(BF))

        imp = p_c[:, 0:qb]
        for g in range(1, GRP):
            imp = imp + p_c[:, g * qb:(g + 1) * qb]
        sc3 = _dot(amat_ref[...], jnp.concatenate(_split3(imp), axis=1))
        score = sc3[:, 0:qb] + sc3[:, qb:2 * qb] + sc3[:, 2 * qb:]
        sel = _top_blocks(score, qpos_row, n_top)
        pen = ((sel.T - 1.0) * PEN).astype(BF)
        qh2 = (xf * LOG2E).astype(BF)
        qa_ref[h, 0] = jnp.concatenate([jnp.concatenate([pen[:, 0:SEL_HALF]] * GRP, axis=0), qh2], axis=1)
        qa_ref[h, 1] = jnp.concatenate([jnp.concatenate([pen[:, SEL_HALF:]] * GRP, axis=0), qh2], axis=1)
        qh_ref[h] = qh
        o_cs.append(o_c)

    hr = rows // SEL_SPLIT
    chains = [(h, slice(sp * hr, (sp + 1) * hr)) for h in range(HKV) for sp in range(SEL_SPLIT)]
    nchain = len(chains)

    def scores(c, t):
        h, rs = chains[c]
        qa = jnp.where(t < HALF_TILES, qa_ref[h, 0, rs, :], qa_ref[h, 1, rs, :])
        return _dot_nt(kaug_ref[t + 1], qa)

    def absorb(c, s, mx, t, m):
        h, rs = chains[c]
        m2 = jnp.maximum(m, mx)
        p = jnp.exp2(s - m2)
        acc_ref[h, :, rs] = jnp.exp2(m - m2) * acc_ref[h, :, rs] + _dot(vselt_ref[h, t + 1], p.astype(BF))
        return m2

    col_max = lambda s: jnp.max(s, axis=0, keepdims=True)

    def pair_body(k, carry):
        ms, mxa = list(carry[:nchain]), list(carry[nchain:])
        mxb = []
        for c, (h, rs) in enumerate(chains):
            s1 = scores(c, 2 * k + 1)
            sb_ref[h, :, rs] = s1
            mxb.append(col_max(s1))
            ms[c] = absorb(c, sa_ref[h, :, rs], mxa[c], 2 * k, ms[c])
        for c, (h, rs) in enumerate(chains):
            s2 = scores(c, 2 * k + 2)
            ms[c] = absorb(c, sb_ref[h, :, rs], mxb[c], 2 * k + 1, ms[c])
            sa_ref[h, :, rs] = s2
            mxa[c] = col_max(s2)
        return tuple(ms) + tuple(mxa)

    acc_ref[...] = jnp.zeros((HKV, VROWS, rows), F32)
    mxa = []
    for c, (h, rs) in enumerate(chains):
        s0 = scores(c, 0)
        sa_ref[h, :, rs] = s0
        mxa.append(col_max(s0))
    carry = tuple(jnp.full((1, hr), NEG, F32) for _ in range(nchain)) + tuple(mxa)
    carry = lax.fori_loop(0, n_far // 2, pair_body, carry)
    for c, (h, rs) in enumerate(chains):
        m_ref[h, :, rs] = carry[c]
        mx_ref[h, :, rs] = carry[nchain + c]

    @pl.when(n_far % 2 == 1)
    def _():
        for c, (h, rs) in enumerate(chains):
            m_ref[h, :, rs] = absorb(c, sa_ref[h, :, rs], mx_ref[h, :, rs], n_far - 1, m_ref[h, :, rs])

    kiota = lax.broadcasted_iota(jnp.int32, (KT, 1), 0)
    ms = [m_ref[h, :, rs] for h, rs in chains]
    for u in range(2):
        t = t_last - 1 + u
        for c, (h, rs) in enumerate(chains):
            s = jnp.where(t * KT + kiota >= 0, scores(c, t) + bs_ref[i % 2, h, u, :, rs], NEG)
            ms[c] = absorb(c, s, col_max(s), t, ms[c])

    gates_t = jax.nn.sigmoid(gate_ref[...]).T
    wkeys = WINDOW + qb
    for h in range(HKV):
        o_s = acc_ref[h, 0:DH, :] / jnp.maximum(acc_ref[h, DH:DH + 1, :], 1e-30)

        s = _dot_nt(kwin_ref[pl.ds(pl.multiple_of(start, qb), wkeys), :], qh_ref[h]) + bw_ref[h]
        kpos = start - WINDOW + lax.broadcasted_iota(jnp.int32, (wkeys, 1), 0)
        s = jnp.where(kpos >= 0, s, NEG)
        e = jnp.exp(s - jnp.max(s, axis=0, keepdims=True))
        vw = jnp.concatenate([vwint_ref[h, i + u] for u in range(wkeys // qb)], axis=1)
        ow = _dot(vw, e.astype(BF))
        o_w = ow[0:DH] / jnp.maximum(ow[DH:DH + 1], 1e-30)

        gate = lambda br: jnp.concatenate(
            [gates_t[(h * GRP + g) * 3 + br:(h * GRP + g) * 3 + br + 1, :] for g in range(GRP)], axis=1)
        o_t = gate(0) * o_cs[h] + gate(1) * o_s + gate(2) * o_w
        for g in range(GRP):
            o_ref[:, (h * GRP + g) * DH:(h * GRP + g + 1) * DH] = o_t[:, g * qb:(g + 1) * qb].T


def _bias_tables(rel_table):
    tab = rel_table.astype(F32)
    far = tab[REL_BUCKETS - 1]
    qb = Q_BLOCK

    def toeplitz(off, nk, dmax, shift):
        d = off - (nk - 1) + jnp.arange(nk + qb - 1, dtype=jnp.int32)
        w = jnp.where(((d >= 0) & (d <= dmax))[None, :], (tab[_t5_bucket(d)] - (far if shift else 0.0)).T, NEG)
        p = w.shape[1]
        hank = jnp.tile(w, (1, nk + 1))[:, :nk * (p + 1)].reshape(NH, nk, p + 1)[:, :, :qb]
        return hank[:, ::-1, :]

    def lanes_gq(b):
        return b.reshape(HKV, GRP, b.shape[1], qb).transpose(0, 2, 1, 3).reshape(HKV, b.shape[1], GRP * qb)

    big = 1 << 30
    bc = lanes_gq(toeplitz(16 * CMP_STRIDE - (CMP_LEN - 1), CBAND * CMP_STRIDE, big, True)[:, ::CMP_STRIDE, :])
    bs = jnp.stack([lanes_gq(toeplitz(off, 2 * KT, big, True)).reshape(HKV, 2, KT, GRP * qb)
                    for off in (KT, KT + qb)]) * LOG2E
    bw = lanes_gq(toeplitz(WINDOW, WINDOW + qb, WINDOW, False))
    return bc, bs, bw


def nsa_prompt(proj, kc, vc, ksel, vsel, kwin, vwin, q_g, rel_table):
    t = proj.shape[0]
    nb = t // Q_BLOCK
    nch = kc.shape[0]
    ncp = -(-nch // 128) * 128
    assert t // SEL_BLOCK <= NS_PAD and t % KT == 0
    n_top = min(TOP_N, t // SEL_BLOCK)
    kcp = jnp.pad(kc, ((0, ncp - nch), (0, 0)))
    vct = jnp.pad(vc, ((0, ncp - nch), (0, 0))).reshape(ncp, HKV, DH).transpose(1, 2, 0)
    amat = jnp.asarray(_selection_matrix(nch - 1, NS_PAD, ncp), BF)

    def values_t(v, pad_rows, tile):
        vt = jnp.pad(v.astype(BF), ((pad_rows, 0), (0, 0))).reshape(-1, tile, HKV, DH).transpose(2, 0, 3, 1)
        return jnp.concatenate([vt, jnp.ones(vt.shape[:2] + (1, tile), BF),
                                jnp.zeros(vt.shape[:2] + (VROWS - DH - 1, tile), BF)], axis=2)

    onehot = ((jnp.arange(t, dtype=jnp.int32)[:, None] // SEL_BLOCK) % SEL_HALF
              == jnp.arange(SEL_HALF, dtype=jnp.int32)[None, :]).astype(BF)
    kaug = jnp.pad(jnp.concatenate([onehot, ksel.astype(BF)], axis=1), ((KT, 0), (0, 0))).reshape(-1, KT, 2 * KVW)
    vselt = values_t(vsel, KT, KT)
    vwint = values_t(vwin, WINDOW, Q_BLOCK)
    rows = GRP * Q_BLOCK
    kwinp = jnp.pad(kwin.astype(BF), ((WINDOW, 0), (0, 0)))
    bc, bs, bw = _bias_tables(rel_table)
    whole = pl.BlockSpec(memory_space=pltpu.VMEM)
    return pl.pallas_call(
        functools.partial(_nsa_prompt_kernel, ncp=ncp, n_top=n_top),
        grid=(nb,),
        in_specs=[pl.BlockSpec((Q_BLOCK, NH * DH), lambda i: (i, O_Q // (NH * DH))),
                  pl.BlockSpec((Q_BLOCK, 128), lambda i: (i, O_GATE // 128)),
                  whole, whole, whole, whole, whole, whole, whole, whole, whole, whole, whole],
        out_specs=pl.BlockSpec((Q_BLOCK, NH * DH), lambda i: (i, 0)),
        out_shape=jax.ShapeDtypeStruct((t, NH * DH), F32),
        scratch_shapes=[pltpu.VMEM((HKV, 16 + ncp, rows), F32), pltpu.VMEM((HKV, VROWS, rows), F32),
                        pltpu.VMEM((HKV, 2, rows, SEL_HALF + KVW), BF), pltpu.VMEM((HKV, rows, KVW), BF),
                        pltpu.VMEM((HKV, KT, rows), F32), pltpu.VMEM((HKV, KT, rows), F32),
                        pltpu.VMEM((HKV, 1, rows), F32), pltpu.VMEM((HKV, 1, rows), F32)],
        compiler_params=_cp("arbitrary"),
        name="nsa_prompt",
    )(proj, proj, jnp.concatenate([q_g, q_g]).reshape(1, 2 * DH), kcp, vct, amat, kaug, vselt, kwinp, vwint,
      bc, bs, bw)


def _pick_head(x, lane_h):
    return jnp.where(lane_h == 0, x[0:DH], x[DH:2 * DH])


def _nsa_dec_pre_kernel(qbd_ref, kc_ref, vc_ref, amat_ref, bc_ref, win_ref, wnew_ref, bwa_ref, bwb_ref,
                        pen_ref, oc_ref, ow_ref, *, ts, pos0, n_top):
    r_all = GRP * HKV * ts
    qbd = qbd_ref[0]
    lane_h = (lax.broadcasted_iota(jnp.int32, (1, r_all), 1) // ts) % HKV

    p_c = _masked_softmax_rows(_dot(kc_ref[0], qbd) + bc_ref[...], 0)
    oc_ref[0] = _pick_head(_dot_tn(vc_ref[0], p_c.astype(BF)), lane_h)

    hi, mid, lo = _split3(p_c)
    amat = amat_ref[...]
    sc = _dot(amat, hi) + _dot(amat, mid) + _dot(amat, lo)
    w8 = HKV * ts
    score = sc[:, 0:w8]
    for g in range(1, GRP):
        score = score + sc[:, g * w8:(g + 1) * w8]
    qp = pos0 + lax.broadcasted_iota(jnp.int32, (1, w8), 1) % ts
    sel = _top_blocks(score, qp, n_top)
    pen_ref[0] = (jnp.concatenate([sel] * GRP, axis=1) - 1.0) * PEN

    win = win_ref[0]
    wnew = wnew_ref[0]
    s_a = _dot(win[:, 0:KVW].astype(BF), qbd) + bwa_ref[...]
    s_b = _dot(wnew[:, 0:KVW].astype(BF), qbd) + bwb_ref[...]
    m = jnp.maximum(jnp.max(s_a, axis=0, keepdims=True), jnp.max(s_b, axis=0, keepdims=True))
    e_a = jnp.where(s_a > 0.5 * NEG, jnp.exp(s_a - m), 0.0)
    e_b = jnp.where(s_b > 0.5 * NEG, jnp.exp(s_b - m), 0.0)
    den = jnp.maximum(jnp.sum(e_a, axis=0, keepdims=True) + jnp.sum(e_b, axis=0, keepdims=True), 1e-30)
    o_w = (_dot_tn(win[:, KVW:].astype(BF), e_a.astype(BF)) + _dot_tn(wnew[:, KVW:].astype(BF), e_b.astype(BF)))
    ow_ref[0] = _pick_head(o_w, lane_h) / den


SEL_PAGES = 32


def _nsa_dec_sel_kernel(*refs, npg, page, ts):
    pages = refs[1:1 + npg]
    (q_ref, pen_ref, expt_ref, blast_ref, knew_ref, vnew_ref, bnew_ref, gate_ref, oc_ref, ow_ref,
     o_ref, m_ref, l_ref, acc_ref) = refs[1 + npg:]
    j = pl.program_id(1)
    last = pl.num_programs(1) - 1
    rows = GRP * ts

    @pl.when(j == 0)
    def _():
        m_ref[...] = jnp.full((HKV * rows, 1), NEG, F32)
        l_ref[...] = jnp.zeros((HKV * rows, 1), F32)
        acc_ref[...] = jnp.zeros((HKV * rows, DH), F32)

    def online(s, pvs):
        m = m_ref[...]
        m2 = jnp.maximum(m, jnp.max(s, axis=-1, keepdims=True))
        a = jnp.exp(m - m2)
        p = jnp.exp(s - m2)
        m_ref[...] = m2
        l_ref[...] = a * l_ref[...] + jnp.sum(p, axis=-1, keepdims=True)
        pb = p.astype(BF)
        pv = jnp.concatenate([pvs[h](pb[h * rows:(h + 1) * rows]) for h in range(HKV)], axis=0)
        acc_ref[...] = a * acc_ref[...] + pv

    both = lambda f: jnp.concatenate([f(h) for h in range(HKV)], axis=0)
    vts = [jnp.concatenate([r[0, 1, h] for r in pages], axis=1).astype(BF) for h in range(HKV)]
    s = both(lambda h: _dot(q_ref[0, h], jnp.concatenate([r[0, 0, h] for r in pages], axis=1).astype(BF))
             + _dot(pen_ref[0, h, j], expt_ref[...]))
    s = s + jnp.where(j == last, blast_ref[...], 0.0)
    online(s, [functools.partial(lambda vt, p: _dot_nt(p, vt), vts[h]) for h in range(HKV)])

    @pl.when(j == last)
    def _():
        s_new = both(lambda h: _dot_nt(q_ref[0, h], knew_ref[0, h].astype(BF))) + bnew_ref[...]
        online(s_new, [functools.partial(lambda v, p: _dot(p, v), vnew_ref[0, h].astype(BF)) for h in range(HKV)])
        o_s = acc_ref[...] / jnp.maximum(l_ref[...], 1e-30)
        g = jax.nn.sigmoid(gate_ref[0])
        o_ref[0] = g[:, 0:1] * oc_ref[0] + g[:, 1:2] * o_s + g[:, 2:3] * ow_ref[0]


def _dec_bias_tables(rel_table, pos0, ts, nch, wb, page, npg):
    tab = rel_table.astype(F32)
    far = tab[REL_BUCKETS - 1]
    r = np.arange(GRP * HKV * ts)
    head = ((r // ts) % HKV) * GRP + r // (HKV * ts)
    qpos = pos0 + r % ts
    far_r = far[head][None, :]

    def bias(kpos, dmax, shift):
        dist = qpos[None, :] - np.asarray(kpos)[:, None]
        ok = (dist >= 0) & (dist <= dmax)
        out = jnp.where(ok, 0.0 if shift else far_r, NEG)
        near = np.nonzero((ok & (dist < REL_MAX_DIST)).any(axis=1))[0]
        if near.size:
            lo, hi = int(near.min()), int(near.max()) + 1
            b = tab[_t5_bucket(jnp.asarray(dist[lo:hi], jnp.int32)), head[None, :]] - (far_r if shift else 0.0)
            out = jnp.concatenate([out[:lo], jnp.where(ok[lo:hi], b, NEG), out[hi:]], axis=0)
        return out

    big = 1 << 30
    bc = bias(np.arange(nch) * CMP_STRIDE + CMP_LEN - 1, big, False)
    bwa = bias(pos0 - wb + np.arange(wb), WINDOW, False)
    tnew = np.arange(8)
    newpos = np.where(tnew < ts, pos0 + tnew, pos0 + 2 * WINDOW + SEL_BLOCK)
    bwb = bias(newpos, WINDOW, False)
    bnew = bias(newpos, big, True)
    step = npg * page
    blast = jnp.concatenate([jnp.zeros((step - page, r.shape[0]), F32), bias(pos0 - page + np.arange(page), big, True)],
                            axis=0)
    return bc, bwa, bwb, bnew, blast


def nsa_decode(qn, gate_raw, kc, vc, cachet, page_table, win_state, kv_new, win_new, rel_table):
    bsz, ts, _ = qn.shape
    n_pages = page_table.shape[1]
    page = cachet.shape[-1]
    pos0 = n_pages * page
    nch = kc.shape[1]
    wb = win_state.shape[1]
    r_all = GRP * HKV * ts
    assert pos0 % SEL_BLOCK == 0 and ts <= 8 and ts <= SEL_BLOCK and nch % 8 == 0
    ns = -(-(pos0 + ts) // SEL_BLOCK)
    npg = min(SEL_PAGES, n_pages)
    assert n_pages % npg == 0
    bps = npg * page // SEL_BLOCK
    ns_pad = max(-(-ns // 8) * 8, (n_pages // npg) * bps)
    n_top = min(TOP_N, ns)

    q5 = qn.reshape(bsz, ts, HKV, GRP, DH)
    qbd = jnp.einsum('bqhgd,hk->bkdghq', q5, jnp.eye(HKV, dtype=F32)).reshape(bsz, 2 * DH, r_all).astype(BF)
    gate_t = gate_raw.reshape(bsz, ts, HKV, GRP, 3).transpose(0, 4, 3, 2, 1).reshape(bsz, 3, r_all)
    pad8 = lambda a: jnp.pad(a, ((0, 0), (0, 8 - ts), (0, 0)))
    amat = jnp.asarray(_selection_matrix(nch - 1, ns_pad, nch), BF)
    expand = jnp.asarray(np.repeat(np.eye(bps, dtype=np.float32), SEL_BLOCK, axis=0), BF)
    bc, bwa, bwb, bnew, blast = _dec_bias_tables(rel_table, pos0, ts, nch, wb, page, npg)

    full = lambda a: pl.BlockSpec(a.shape, functools.partial(lambda nd, *_: (0,) * nd, a.ndim))
    per_b = lambda a: pl.BlockSpec((1,) + a.shape[1:], functools.partial(lambda nd, b, *_: (b,) + (0,) * nd, a.ndim - 1))
    wnew = pad8(win_new)
    pre_in = [qbd, kc, vc, amat, bc, win_state, wnew, bwa, bwb]
    pre_specs = [per_b(qbd), per_b(kc), per_b(vc), full(amat), full(bc), per_b(win_state), per_b(wnew),
                 full(bwa), full(bwb)]
    small = jax.ShapeDtypeStruct((bsz, DH, r_all), F32)
    pen, o_c, o_w = pl.pallas_call(
        functools.partial(_nsa_dec_pre_kernel, ts=ts, pos0=pos0, n_top=n_top),
        grid=(bsz,), in_specs=pre_specs,
        out_specs=[pl.BlockSpec((1, ns_pad, r_all), lambda b: (b, 0, 0)),
                   pl.BlockSpec((1, DH, r_all), lambda b: (b, 0, 0)),
                   pl.BlockSpec((1, DH, r_all), lambda b: (b, 0, 0))],
        out_shape=[jax.ShapeDtypeStruct((bsz, ns_pad, r_all), F32), small, small],
        compiler_params=_cp("parallel"), name="nsa_dec_pre",
    )(*pre_in)

    rows = GRP * ts
    nsteps = n_pages // npg
    by_head = lambda a: a.reshape(a.shape[:-1] + (GRP, HKV, ts))
    q_h = q5.transpose(0, 2, 3, 1, 4).reshape(bsz, HKV, rows, DH).astype(BF)
    gate_h = gate_raw.reshape(bsz, ts, HKV, GRP, 3).transpose(0, 2, 3, 1, 4).reshape(bsz, HKV, rows, 3)
    pen_h = by_head(pen[:, :nsteps * bps].reshape(bsz, nsteps, bps, r_all)).transpose(0, 4, 1, 3, 5, 2).reshape(
        bsz, HKV, nsteps, rows, bps).astype(BF)
    oc_h, ow_h = (by_head(a).transpose(0, 3, 2, 4, 1).reshape(bsz, HKV, rows, DH) for a in (o_c, o_w))
    blast_h, bnew_h = (by_head(a).transpose(2, 1, 3, 0).reshape(HKV, rows, a.shape[0]) for a in (blast, bnew))
    new5 = pad8(kv_new).reshape(bsz, 8, 4, HKV, DH)
    knew = new5[:, :, 2].transpose(0, 2, 1, 3)
    vnew = new5[:, :, 3].transpose(0, 2, 1, 3)
    page_specs = [pl.BlockSpec((1, 2, HKV, DH, page),
                               functools.partial(lambda r, b, j, pt: (pt[b, j * npg + r], 1, 0, 0, 0), r))
                  for r in range(npg)]
    heads_rows = lambda a: a.reshape(a.shape[:-3] + (HKV * rows, a.shape[-1]))
    blast_h, bnew_h, gate_h, oc_h, ow_h = map(heads_rows, (blast_h, bnew_h, gate_h, oc_h, ow_h))
    sel_in = [q_h, pen_h, expand.T, blast_h, knew, vnew, bnew_h, gate_h, oc_h, ow_h]
    sel_specs = [per_b(q_h), per_b(pen_h), full(expand.T), full(blast_h), per_b(knew), per_b(vnew), full(bnew_h),
                 per_b(gate_h), per_b(oc_h), per_b(ow_h)]
    o_h = pl.pallas_call(
        functools.partial(_nsa_dec_sel_kernel, npg=npg, page=page, ts=ts),
        grid_spec=pltpu.PrefetchScalarGridSpec(
            num_scalar_prefetch=1, grid=(bsz, nsteps), in_specs=page_specs + sel_specs,
            out_specs=pl.BlockSpec((1, HKV * rows, DH), lambda b, j, pt: (b, 0, 0)),
            scratch_shapes=[pltpu.VMEM((HKV * rows, 1), F32), pltpu.VMEM((HKV * rows, 1), F32),
                            pltpu.VMEM((HKV * rows, DH), F32)]),
        out_shape=jax.ShapeDtypeStruct((bsz, HKV * rows, DH), F32),
        compiler_params=_cp("parallel", "arbitrary"), name="nsa_dec_sel",
    )(page_table, *([cachet] * npg), *sel_in)
    return o_h.reshape(bsz, HKV, GRP, ts, DH).transpose(0, 3, 1, 2, 4).reshape(bsz, ts, NH * DH)


def _even_weights(p, e):
    w_in = jnp.pad(p['att_w_in'][e], ((0, 0), (0, ATT_IN_PAD - ATT_IN_COLS))).astype(BF)
    w_out = p['att_w_out'][e].astype(BF)
    return dict(
        w_in=w_in, w_out_conv=w_out[:C_CONV], w_out_att=w_out[C_CONV:],
        cmp=_compress_weights(p['cmp_pe'][e], p['cmp_w1'][e], p['cmp_b1'][e], p['cmp_w2'][e], p['k_norm_g'][e][0]),
        k_g=p['k_norm_g'][e], q_g=p['q_norm_g'][e], conv_w=p['conv_w'][e], conv_b=p['conv_b'][e],
        ln_g=p['conv_ln_g'][e], ln_b=p['conv_ln_b'][e])


def _even_prompt(x2, norm_g, w, rel_table):
    t = x2.shape[0]
    proj = norm_matmul(x2, norm_g, w['w_in'])
    conv_y, conv_new = conformer_conv(proj[None], jnp.zeros((1, CONV_WIDTH - 1, C_CONV), F32),
                                      w['conv_w'], w['conv_b'], w['ln_g'], w['ln_b'])
    kv_new, win_new = kv_post(proj, w['k_g'])
    kc, vc = compress_dense(proj[None], w['cmp'])
    o = nsa_prompt(proj, kc[0], vc[0], kv_new[:, 2 * KVW:3 * KVW], kv_new[:, 3 * KVW:], win_new[:, :KVW],
                   win_new[:, KVW:], w['q_g'], rel_table)
    x2 = out_proj2(x2, conv_y[0], o, w['w_out_conv'], w['w_out_att'])
    keep = min(WINDOW, t)
    return (x2, kv_new.reshape(1, t, 4, HKV, DH), win_new[t - keep:].reshape(1, keep, 2, HKV, DH), conv_new)


def _even_decode(x3, norm_g, w, rel_table, cachet, page_table, win_state, conv_state):
    bsz, t, d = x3.shape
    x2 = x3.reshape(bsz * t, d)
    proj = norm_matmul(x2, norm_g, w['w_in'])
    proj3 = proj.reshape(bsz, t, ATT_IN_PAD)
    conv_y, conv_new = conformer_conv(proj3, conv_state, w['conv_w'], w['conv_b'], w['ln_g'], w['ln_b'])
    kv_new, win_new = kv_post(proj, w['k_g'])
    qn = q_norm(proj, w['q_g'])
    kc, vc = compress_paged(cachet, page_table, w['cmp'])
    wb = win_state.shape[1]
    o = nsa_decode(qn.reshape(bsz, t, NH * DH), proj3[:, :, O_GATE:ATT_IN_COLS], kc, vc, cachet, page_table,
                   win_state.reshape(bsz, wb, 2 * KVW), kv_new.reshape(bsz, t, 4 * KVW),
                   win_new.reshape(bsz, t, 2 * KVW), rel_table)
    x2 = out_proj2(x2, conv_y.reshape(bsz * t, C_CONV), o.reshape(bsz * t, NH * DH), w['w_out_conv'],
                   w['w_out_att'])
    win_all = jnp.concatenate([win_state, win_new.reshape(bsz, t, 2, HKV, DH)], axis=1)
    keep = min(WINDOW, wb + t)
    return (x2.reshape(bsz, t, d), kv_new.reshape(bsz, t, 4, HKV, DH), win_all[:, wb + t - keep:], conv_new)


def _odd_layer(x3, s0, norm_g, w_in_bf, w_out_bf, lb, hg_norm_g):
    bsz, t, d = x3.shape
    x2 = x3.reshape(bsz * t, d)
    z = norm_matmul(x2, norm_g, w_in_bf, tn=2048).reshape(bsz, t, 4 * d)
    tp = -(-t // HG_CHUNK) * HG_CHUNK
    if tp != t:
        z = jnp.pad(z, ((0, 0), (0, tp - t), (0, 0)))
    o, s_new = hgrn2(z, s0, lb, hg_norm_g, t)
    x2 = out_proj1(x2, o[:, :t].reshape(bsz * t, d), w_out_bf)
    return x2.reshape(bsz, t, d), s_new


def kernel(x_prompt, x_sample, cache_nsa_kv, page_table, state_nsa_win, state_conv, state_hgrn, rel_bias_table,
           norm_mix_g, norm_mlp_g, w_mlp_up, w_mlp_down, att_w_in, att_w_out, q_norm_g, k_norm_g, cmp_pe, cmp_w1,
           cmp_b1, cmp_w2, conv_w, conv_b, conv_ln_g, conv_ln_b, hg_w_in, hg_w_out, hg_lb_logits, hg_norm_g):
    p = dict(att_w_in=att_w_in, att_w_out=att_w_out, q_norm_g=q_norm_g, k_norm_g=k_norm_g, cmp_pe=cmp_pe,
             cmp_w1=cmp_w1, cmp_b1=cmp_b1, cmp_w2=cmp_w2, conv_w=conv_w, conv_b=conv_b, conv_ln_g=conv_ln_g,
             conv_ln_b=conv_ln_b)
    bp, tp_, d = x_prompt.shape
    assert bp == 1
    db, ts, _ = x_sample.shape
    w_up = w_mlp_up.astype(BF)
    w_down = w_mlp_down.astype(BF)
    cum = jnp.cumsum(jax.nn.softmax(hg_lb_logits.astype(F32), axis=0), axis=0)

    xp = x_prompt[0]
    kv_p, win_p, conv_p, hg_p = [], [], [], []
    for layer in range(norm_mix_g.shape[0]):
        if layer % 2 == 0:
            e = layer // 2
            w = _even_weights(p, e)
            xp, kv_new, win_new, conv_new = _even_prompt(xp, norm_mix_g[layer], w, rel_bias_table)
            kv_p.append(kv_new)
            win_p.append(win_new)
            conv_p.append(conv_new)
        else:
            o = layer // 2
            x3, s_new = _odd_layer(xp[None], jnp.zeros((1, HG_HEADS, HG_DK, HG_DV), F32), norm_mix_g[layer],
                                   hg_w_in[o].astype(BF), hg_w_out[o].astype(BF), cum[layer] - cum[0],
                                   hg_norm_g[o])
            xp = x3[0]
            hg_p.append(s_new)
        xp = mlp(xp, norm_mlp_g[layer], w_up[layer], w_down[layer])

    cachet = cache_nsa_kv.transpose(0, 1, 3, 4, 5, 2)
    xs = x_sample
    kv_s, win_s, conv_s, hg_s = [], [], [], []
    for layer in range(norm_mix_g.shape[0]):
        if layer % 2 == 0:
            e = layer // 2
            w = _even_weights(p, e)
            xs, kv_new, win_new, conv_new = _even_decode(xs, norm_mix_g[layer], w, rel_bias_table, cachet[e],
                                                         page_table, state_nsa_win[e], state_conv[e])
            kv_s.append(kv_new)
            win_s.append(win_new)
            conv_s.append(conv_new)
        else:
            o = layer // 2
            xs, s_new = _odd_layer(xs, state_hgrn[o], norm_mix_g[layer], hg_w_in[o].astype(BF),
                                   hg_w_out[o].astype(BF), cum[layer] - cum[0], hg_norm_g[o])
            hg_s.append(s_new)
        xs = mlp(xs.reshape(db * ts, d), norm_mlp_g[layer], w_up[layer], w_down[layer]).reshape(db, ts, d)
    return (xp[None], xs, jnp.stack(kv_p), jnp.stack(kv_s), jnp.stack(win_p), jnp.stack(win_s),
            jnp.stack(conv_p), jnp.stack(conv_s), jnp.stack(hg_p), jnp.stack(hg_s))
```

```python
import functools
import math

import jax
import jax.numpy as jnp
import numpy as np
from jax import lax
from jax.experimental import pallas as pl
from jax.experimental.pallas import tpu as pltpu

D_MODEL = 1024
C_CONV = 512
CONV_WIDTH = 31
DH = 64
HKV = 2
GRP = 4
NH = HKV * GRP
CMP_STRIDE = 16
CMP_LEN = 32
CMP_HIDDEN = 128
SEL_BLOCK = 64
TOP_N = 16
WINDOW = 512
Q_BLOCK = 128
REL_BUCKETS = 32
REL_EXACT = 16
REL_MAX_DIST = 128
HG_DK = 128
HG_HEADS = 8
HG_DV = 128
HG_CHUNK = 64
RMS_EPS = 1e-6
NEG = -1e30
BIG = 1e9
O_Q = 2 * C_CONV
O_KV = O_Q + NH * DH
O_GATE = O_KV + 3 * 2 * HKV * DH
ATT_IN_COLS = O_GATE + 3 * NH
ATT_IN_PAD = 2432
KVW = HKV * DH

VMEM_LIMIT = 56 * 1024 * 1024
BF = jnp.bfloat16
F32 = jnp.float32


def _cp(*sem):
    return pltpu.CompilerParams(dimension_semantics=sem, vmem_limit_bytes=VMEM_LIMIT)


def _dot(a, b):
    return jnp.dot(a, b, preferred_element_type=F32)


def _dot_nt(a, b):
    return lax.dot_general(a, b, (((1,), (1,)), ((), ())), preferred_element_type=F32)


def _dot_tn(a, b):
    return lax.dot_general(a, b, (((0,), (0,)), ((), ())), preferred_element_type=F32)


def _rms(x, g):
    return x * lax.rsqrt(jnp.mean(x * x, axis=-1, keepdims=True) + RMS_EPS) * g


def _norm_matmul_kernel(x_ref, g_ref, w_ref, o_ref):
    hn = _rms(x_ref[...], g_ref[...]).astype(BF)
    o_ref[...] = _dot(hn, w_ref[...])


def norm_matmul(x, g, w_bf, tn=None):
    m, d = x.shape
    n = w_bf.shape[1]
    tm = min(m, 512)
    tn = n if tn is None else tn
    return pl.pallas_call(
        _norm_matmul_kernel,
        grid=(m // tm, n // tn),
        in_specs=[pl.BlockSpec((tm, d), lambda i, j: (i, 0)),
                  pl.BlockSpec((1, d), lambda i, j: (0, 0)),
                  pl.BlockSpec((d, tn), lambda i, j: (0, j))],
        out_specs=pl.BlockSpec((tm, tn), lambda i, j: (i, j)),
        out_shape=jax.ShapeDtypeStruct((m, n), F32),
        compiler_params=_cp("parallel", "arbitrary"),
        name="norm_matmul",
    )(x, g.reshape(1, d), w_bf)


def _out_proj_kernel(r_ref, a1_ref, a2_ref, w1_ref, w2_ref, o_ref):
    o_ref[...] = (r_ref[...] + _dot(a1_ref[...].astype(BF), w1_ref[...])
                  + _dot(a2_ref[...].astype(BF), w2_ref[...]))


def out_proj2(res, a1, a2, w1_bf, w2_bf):
    m, d = res.shape
    k1, k2 = a1.shape[1], a2.shape[1]
    tm = min(m, 512)
    return pl.pallas_call(
        _out_proj_kernel,
        grid=(m // tm,),
        in_specs=[pl.BlockSpec((tm, d), lambda i: (i, 0)),
                  pl.BlockSpec((tm, k1), lambda i: (i, 0)),
                  pl.BlockSpec((tm, k2), lambda i: (i, 0)),
                  pl.BlockSpec((k1, d), lambda i: (0, 0)),
                  pl.BlockSpec((k2, d), lambda i: (0, 0))],
        out_specs=pl.BlockSpec((tm, d), lambda i: (i, 0)),
        out_shape=jax.ShapeDtypeStruct((m, d), F32),
        compiler_params=_cp("parallel"),
        name="out_proj2",
    )(res, a1, a2, w1_bf, w2_bf)


def _out_proj1_kernel(r_ref, a_ref, w_ref, o_ref):
    o_ref[...] = r_ref[...] + _dot(a_ref[...].astype(BF), w_ref[...])


def out_proj1(res, a, w_bf):
    m, d = res.shape
    k = a.shape[1]
    tm = min(m, 512)
    return pl.pallas_call(
        _out_proj1_kernel,
        grid=(m // tm,),
        in_specs=[pl.BlockSpec((tm, d), lambda i: (i, 0)),
                  pl.BlockSpec((tm, k), lambda i: (i, 0)),
                  pl.BlockSpec((k, d), lambda i: (0, 0))],
        out_specs=pl.BlockSpec((tm, d), lambda i: (i, 0)),
        out_shape=jax.ShapeDtypeStruct((m, d), F32),
        compiler_params=_cp("parallel"),
        name="out_proj1",
    )(res, a, w_bf)


def _mlp_kernel(x_ref, g_ref, wu_ref, wd_ref, o_ref, hn_ref, acc_ref):
    j = pl.program_id(1)

    @pl.when(j == 0)
    def _():
        hn_ref[...] = _rms(x_ref[...], g_ref[...]).astype(BF)
        acc_ref[...] = x_ref[...]

    hid = jnp.maximum(_dot(hn_ref[...], wu_ref[...]), 0.0)
    acc_ref[...] += _dot((hid * hid).astype(BF), wd_ref[...])

    @pl.when(j == pl.num_programs(1) - 1)
    def _():
        o_ref[...] = acc_ref[...]


def mlp(x, g, wu_bf, wd_bf):
    m, d = x.shape
    hdim = wu_bf.shape[1]
    tm = min(m, 1024)
    th = 1024
    return pl.pallas_call(
        _mlp_kernel,
        grid=(m // tm, hdim // th),
        in_specs=[pl.BlockSpec((tm, d), lambda i, j: (i, 0)),
                  pl.BlockSpec((1, d), lambda i, j: (0, 0)),
                  pl.BlockSpec((d, th), lambda i, j: (0, j)),
                  pl.BlockSpec((th, d), lambda i, j: (j, 0))],
        out_specs=pl.BlockSpec((tm, d), lambda i, j: (i, 0)),
        out_shape=jax.ShapeDtypeStruct((m, d), F32),
        scratch_shapes=[pltpu.VMEM((tm, d), BF), pltpu.VMEM((tm, d), F32)],
        compiler_params=_cp("parallel", "arbitrary"),
        name="mlp",
    )(x, g.reshape(1, d), wu_bf, wd_bf)


CONV_HALO = 32
CONV_PAD = CONV_HALO - (CONV_WIDTH - 1)


def _conv_kernel(u_ref, st_ref, w_ref, b_ref, lg_ref, lb_ref, y_ref, new_ref, xin_ref, *, tt):
    t = pl.program_id(1)

    @pl.when(t == 0)
    def _():
        xin_ref[0:CONV_HALO, :] = st_ref[0]

    a = u_ref[0, :, 0:C_CONV]
    gt = u_ref[0, :, C_CONV:2 * C_CONV]
    xin_ref[CONV_HALO:CONV_HALO + tt, :] = a * jax.nn.sigmoid(gt)

    cols = []
    for c in range(C_CONV // 128):
        cs = slice(c * 128, (c + 1) * 128)
        acc = jnp.zeros((tt, 128), F32) + b_ref[:, cs]
        for k in range(CONV_WIDTH):
            acc = acc + w_ref[k:k + 1, cs] * xin_ref[CONV_PAD + k:CONV_PAD + k + tt, cs]
        cols.append(acc)
    y = jnp.concatenate(cols, axis=1)
    mu = jnp.mean(y, axis=-1, keepdims=True)
    yc = y - mu
    var = jnp.mean(yc * yc, axis=-1, keepdims=True)
    z = yc * lax.rsqrt(var + RMS_EPS) * lg_ref[...] + lb_ref[...]
    y_ref[0] = z * jax.nn.sigmoid(z)

    @pl.when(t == pl.num_programs(1) - 1)
    def _():
        new_ref[0] = xin_ref[tt + CONV_PAD:tt + CONV_HALO, :]

    if tt >= CONV_HALO:
        @pl.when(t < pl.num_programs(1) - 1)
        def _():
            xin_ref[0:CONV_HALO, :] = xin_ref[tt:tt + CONV_HALO, :]


def conformer_conv(proj, state, w, b, ln_g, ln_b):
    bsz, t, _ = proj.shape
    tt = min(t, 256)
    assert t % tt == 0 and (t == tt or tt >= CONV_HALO)
    st = jnp.pad(state, ((0, 0), (CONV_PAD, 0), (0, 0)))
    row = lambda v: v.reshape(1, C_CONV)
    return pl.pallas_call(
        functools.partial(_conv_kernel, tt=tt),
        grid=(bsz, t // tt),
        in_specs=[pl.BlockSpec((1, tt, 2 * C_CONV), lambda i, j: (i, j, 0)),
                  pl.BlockSpec((1, CONV_HALO, C_CONV), lambda i, j: (i, 0, 0)),
                  pl.BlockSpec((CONV_WIDTH, C_CONV), lambda i, j: (0, 0)),
                  pl.BlockSpec((1, C_CONV), lambda i, j: (0, 0)),
                  pl.BlockSpec((1, C_CONV), lambda i, j: (0, 0)),
                  pl.BlockSpec((1, C_CONV), lambda i, j: (0, 0))],
        out_specs=[pl.BlockSpec((1, tt, C_CONV), lambda i, j: (i, j, 0)),
                   pl.BlockSpec((1, CONV_WIDTH - 1, C_CONV), lambda i, j: (i, 0, 0))],
        out_shape=[jax.ShapeDtypeStruct((bsz, t, C_CONV), F32),
                   jax.ShapeDtypeStruct((bsz, CONV_WIDTH - 1, C_CONV), F32)],
        scratch_shapes=[pltpu.VMEM((CONV_HALO + tt, C_CONV), F32)],
        compiler_params=_cp("parallel", "arbitrary"),
        name="conformer_conv",
    )(proj, st, w, row(b), row(ln_g), row(ln_b))


HG_LEVELS = (32, 16, 8, 4, 2, 1)


def _hgrn_tables():
    c = HG_CHUNK
    idx = np.arange(c)
    masks = []
    for h in HG_LEVELS:
        blk = idx // (2 * h)
        upper = (idx % (2 * h)) >= h
        masks.append((blk[:, None] == blk[None, :]) & upper[:, None] & (~upper)[None, :])
    masks.append(np.eye(c, dtype=bool))
    return (idx[None, :] <= idx[:, None]).astype(np.float32), np.stack(masks).astype(np.float32)


HG_HEADS_PER_STEP = 8


def _hgrn_kernel(q_ref, fz_ref, v_ref, g_ref, lb_ref, ng_ref, s0_ref, tab_ref, msk_ref,
                 o_ref, sn_ref, st_ref, *, t_valid, nh):
    ci = pl.program_id(2)
    c = HG_CHUNK

    @pl.when(ci == 0)
    def _():
        for hh in range(nh):
            st_ref[hh] = s0_ref[0, hh].T

    rowc = lax.broadcasted_iota(jnp.int32, (c, 1), 0)
    sub8 = lax.broadcasted_iota(jnp.int32, (8, 1), 0)
    live = ci * c + rowc < t_valid
    tab = tab_ref[...]
    for hh in range(nh):
        cs = slice(hh * HG_DK, (hh + 1) * HG_DK)
        lb = lb_ref[:, cs]
        f = lb + (1.0 - lb) * jax.nn.sigmoid(fz_ref[0, :, cs])
        lf = jnp.where(live, jnp.log(f), 0.0)
        k = jnp.where(live, 1.0 - f, 0.0)
        q = q_ref[0, :, cs]
        v = v_ref[0, :, cs]

        hi = lf.astype(BF)
        b2 = _dot(tab, jnp.concatenate([hi, (lf - hi.astype(F32)).astype(BF)], axis=1))
        b = b2[:, 0:HG_DK] + b2[:, HG_DK:]

        def pivot_rows(h):
            if 2 * h >= 8:
                return jnp.concatenate(
                    [jnp.broadcast_to(b[blk * 2 * h + h - 1:blk * 2 * h + h, :], (2 * h, HG_DK))
                     for blk in range(c // (2 * h))], axis=0)
            groups = []
            for g8 in range(c // 8):
                piece = None
                for kb in range(8 // (2 * h)):
                    r = 8 * g8 + kb * 2 * h + h - 1
                    cand = jnp.broadcast_to(b[r:r + 1, :], (8, HG_DK))
                    piece = cand if piece is None else jnp.where(sub8 >= kb * 2 * h, cand, piece)
                groups.append(piece)
            return jnp.concatenate(groups, axis=0)

        attn = jnp.zeros((c, c), F32)
        for li, h in enumerate(HG_LEVELS):
            piv = pivot_rows(h)
            fac = jnp.exp(jnp.where((rowc % (2 * h)) >= h, b - piv, piv - b))
            attn = attn + msk_ref[li] * _dot_nt((q * fac).astype(BF), (k * fac).astype(BF))
        attn = attn + msk_ref[len(HG_LEVELS)] * _dot_nt(q.astype(BF), k.astype(BF))

        st = st_ref[hh]
        qb = (q * jnp.exp(b)).astype(BF)
        o = _dot(attn.astype(BF), v.astype(BF)) + _dot_nt(qb, st.astype(BF))
        ke = (k * jnp.exp(b[c - 1:c] - b)).astype(BF)
        decay = jnp.exp(b[c - 1:c])
        st_ref[hh] = st * decay + _dot(v.T.astype(BF), ke)

        gate = g_ref[0, :, cs]
        o_ref[0, :, cs] = _rms(o, ng_ref[...]) * (gate * jax.nn.sigmoid(gate))

    @pl.when(ci == pl.num_programs(2) - 1)
    def _():
        for hh in range(nh):
            sn_ref[0, hh] = st_ref[hh].T


def hgrn2(z, s0, lb, norm_g, t_valid):
    bsz, tp, _ = z.shape
    c = HG_CHUNK
    nc = tp // c
    tab, msk = _hgrn_tables()
    nh = HG_HEADS_PER_STEP
    ng = HG_HEADS // nh
    w = nh * HG_DK
    blk = lambda off: pl.BlockSpec((1, c, w), lambda b, h, i: (b, i, off + h))
    return pl.pallas_call(
        functools.partial(_hgrn_kernel, t_valid=t_valid, nh=nh),
        grid=(bsz, ng, nc),
        in_specs=[blk(0), blk(ng), blk(2 * ng), blk(3 * ng),
                  pl.BlockSpec((1, w), lambda b, h, i: (0, h)),
                  pl.BlockSpec((1, HG_DV), lambda b, h, i: (0, 0)),
                  pl.BlockSpec((1, nh, HG_DK, HG_DV), lambda b, h, i: (b, h, 0, 0)),
                  pl.BlockSpec(tab.shape, lambda b, h, i: (0, 0)),
                  pl.BlockSpec(msk.shape, lambda b, h, i: (0, 0, 0))],
        out_specs=[pl.BlockSpec((1, c, w), lambda b, h, i: (b, i, h)),
                   pl.BlockSpec((1, nh, HG_DK, HG_DV), lambda b, h, i: (b, h, 0, 0))],
        out_shape=[jax.ShapeDtypeStruct((bsz, tp, D_MODEL), F32),
                   jax.ShapeDtypeStruct((bsz, HG_HEADS, HG_DK, HG_DV), F32)],
        scratch_shapes=[pltpu.VMEM((nh, HG_DV, HG_DK), F32)],
        compiler_params=_cp("parallel", "parallel", "arbitrary"),
        name="hgrn2",
    )(z, z, z, z, lb.reshape(1, D_MODEL), norm_g.reshape(1, HG_DV), s0,
      jnp.asarray(tab, BF), jnp.asarray(msk))


def _pair_norm(seg, g):
    lo = lax.broadcasted_iota(jnp.int32, seg.shape, 1) < DH
    sq = seg * seg
    s0 = jnp.sum(jnp.where(lo, sq, 0.0), axis=-1, keepdims=True)
    s1 = jnp.sum(jnp.where(lo, 0.0, sq), axis=-1, keepdims=True)
    inv = jnp.where(lo, lax.rsqrt(s0 * (1.0 / DH) + RMS_EPS), lax.rsqrt(s1 * (1.0 / DH) + RMS_EPS))
    return seg * inv * g


def _kv_post_kernel(p_ref, kg_ref, kv_ref, win_ref):
    x = p_ref[...]
    kv_ref[:, 0:2 * KVW] = x[:, 0:2 * KVW]
    kv_ref[:, 2 * KVW:3 * KVW] = _pair_norm(x[:, 2 * KVW:3 * KVW], kg_ref[1:2, :])
    kv_ref[:, 3 * KVW:4 * KVW] = x[:, 3 * KVW:4 * KVW]
    win_ref[:, 0:KVW] = _pair_norm(x[:, 4 * KVW:5 * KVW], kg_ref[2:3, :])
    win_ref[:, KVW:2 * KVW] = x[:, 5 * KVW:6 * KVW]


def kv_post(proj, k_g):
    m = proj.shape[0]
    tm = min(m, 1024)
    kvcols = 6 * KVW
    assert O_KV % kvcols == 0
    return pl.pallas_call(
        _kv_post_kernel,
        grid=(m // tm,),
        in_specs=[pl.BlockSpec((tm, kvcols), lambda i: (i, O_KV // kvcols)),
                  pl.BlockSpec((3, KVW), lambda i: (0, 0))],
        out_specs=[pl.BlockSpec((tm, 4 * KVW), lambda i: (i, 0)),
                   pl.BlockSpec((tm, 2 * KVW), lambda i: (i, 0))],
        out_shape=[jax.ShapeDtypeStruct((m, 4 * KVW), F32), jax.ShapeDtypeStruct((m, 2 * KVW), F32)],
        compiler_params=_cp("parallel"),
        name="kv_post",
    )(proj, jnp.concatenate([k_g, k_g], axis=1))


def _q_norm_kernel(q_ref, g_ref, o_ref):
    for c in range(NH // 2):
        cs = slice(c * 2 * DH, (c + 1) * 2 * DH)
        o_ref[:, cs] = _pair_norm(q_ref[:, cs], g_ref[...]) * (DH ** -0.5)


def q_norm(proj, q_g):
    m = proj.shape[0]
    tm = min(m, 1024)
    return pl.pallas_call(
        _q_norm_kernel,
        grid=(m // tm,),
        in_specs=[pl.BlockSpec((tm, NH * DH), lambda i: (i, O_Q // (NH * DH))),
                  pl.BlockSpec((1, 2 * DH), lambda i: (0, 0))],
        out_specs=pl.BlockSpec((tm, NH * DH), lambda i: (i, 0)),
        out_shape=jax.ShapeDtypeStruct((m, NH * DH), F32),
        compiler_params=_cp("parallel"),
        name="q_norm",
    )(proj, jnp.concatenate([q_g, q_g]).reshape(1, 2 * DH))


def _compress_kernel(*refs, n_prefetch, n_src, rows, nch):
    refs = refs[n_prefetch:]
    srcs = (refs[:n_src], refs[n_src:2 * n_src])
    pef_ref, pes_ref, w1f_ref, w1s_ref, b1_ref, w2_ref, kg_ref, kc_ref, vc_ref, hf_ref, hs_ref = refs[2 * n_src:]
    j = pl.program_id(1)
    cpr = rows // CMP_STRIDE
    m = cpr * n_src

    @pl.when(j == 0)
    def _():
        hs_ref[:, nch:nch + 8, :] = jnp.zeros((2, 8, 2 * CMP_HIDDEN), F32)

    for kind in range(2):
        accf = jnp.zeros((m, 2 * CMP_HIDDEN), F32)
        accs = jnp.zeros((m, 2 * CMP_HIDDEN), F32)
        for s in range(CMP_STRIDE):
            xs = jnp.concatenate(
                [r[0, pl.ds(s, cpr, stride=CMP_STRIDE), :] for r in srcs[kind]], axis=0)
            accf = accf + _dot((xs + pef_ref[kind, s:s + 1, :]).astype(BF), w1f_ref[kind, s])
            accs = accs + _dot((xs + pes_ref[kind, s:s + 1, :]).astype(BF), w1s_ref[kind, s])
        row0 = pl.multiple_of(j * m, 8)
        hf_ref[kind, pl.ds(row0, m), :] = accf
        hs_ref[kind, pl.ds(row0, m), :] = accs

    @pl.when(j == pl.num_programs(1) - 1)
    def _():
        for kind in range(2):
            hid = jax.nn.gelu(hf_ref[kind, 0:nch, :] + hs_ref[kind, 1:nch + 1, :] + b1_ref[kind])
            out = _dot(hid.astype(BF), w2_ref[kind])
            if kind == 0:
                kc_ref[0] = _pair_norm(out, kg_ref[...]).astype(BF)
            else:
                vc_ref[0] = out.astype(BF)


def _compress_weights(pe, w1, b1, w2, kg0):
    eye = jnp.eye(HKV, dtype=F32)
    bd = lambda w: jnp.einsum('ab,ksdf->ksadbf', eye, w).reshape(2, CMP_STRIDE, KVW, 2 * CMP_HIDDEN).astype(BF)
    tile2 = lambda a: jnp.concatenate([a, a], axis=-1)
    w2bd = jnp.einsum('ab,kfd->kafbd', eye, w2).reshape(2, 2 * CMP_HIDDEN, KVW).astype(BF)
    return (tile2(pe[:, :CMP_STRIDE]), tile2(pe[:, CMP_STRIDE:]), bd(w1[:, :CMP_STRIDE]), bd(w1[:, CMP_STRIDE:]),
            tile2(b1).reshape(2, 1, 2 * CMP_HIDDEN), w2bd, tile2(kg0).reshape(1, KVW))


def _compress_common(n_prefetch, n_src, rows, nch, bsz, weights):
    wspecs = [pl.BlockSpec(w.shape, functools.partial(lambda nd, *a: (0,) * nd, w.ndim)) for w in weights]
    out_specs = [pl.BlockSpec((1, nch, KVW), lambda b, j, *a: (b, 0, 0))] * 2
    out_shape = [jax.ShapeDtypeStruct((bsz, nch, KVW), BF)] * 2
    scratch = [pltpu.VMEM((2, nch + 8, 2 * CMP_HIDDEN), F32)] * 2
    kern = functools.partial(_compress_kernel, n_prefetch=n_prefetch, n_src=n_src, rows=rows, nch=nch)
    return kern, wspecs, out_specs, out_shape, scratch


def compress_dense(proj3, weights):
    bsz, t, _ = proj3.shape
    rows = min(t, 2048)
    nch = t // CMP_STRIDE
    assert O_KV % KVW == 0
    kern, wspecs, out_specs, out_shape, scratch = _compress_common(0, 1, rows, nch, bsz, weights)
    src = lambda kind: pl.BlockSpec((1, rows, KVW), lambda b, j: (b, j, O_KV // KVW + kind))
    return pl.pallas_call(
        kern, grid=(bsz, t // rows),
        in_specs=[src(0), src(1)] + wspecs,
        out_specs=out_specs, out_shape=out_shape, scratch_shapes=scratch,
        compiler_params=_cp("parallel", "arbitrary"), name="compress_dense",
    )(proj3, proj3, *weights)


CMP_PAGES = 16
CMP_PE_ROWS = 16


def _compress_paged_kernel(*refs, npg, page, nch):
    pages = refs[1:1 + npg]
    pe2_ref, w1p_ref, b1_ref, w2_ref, kg_ref, kc_ref, vc_ref, hf_ref, hs_ref, xs_ref = refs[1 + npg:]
    j = pl.program_id(1)
    m = npg * page // CMP_STRIDE
    hid2 = 2 * CMP_HIDDEN

    @pl.when(j == 0)
    def _():
        hs_ref[:, nch:nch + 8, :] = jnp.zeros((2, 8, hid2), F32)

    for kind in range(2):
        for r in range(npg):
            xt = jnp.concatenate([pages[r][0, kind, h] for h in range(HKV)], axis=0)
            xs_ref[kind, r * page:(r + 1) * page, :] = xt.T
        acc = jnp.zeros((m + CMP_PE_ROWS, 2 * hid2), F32)
        for s2 in range(CMP_STRIDE // 2):
            xa = xs_ref[kind, pl.ds(2 * s2, m, stride=CMP_STRIDE), :]
            xb = xs_ref[kind, pl.ds(2 * s2 + 1, m, stride=CMP_STRIDE), :]
            lhs = jnp.concatenate([jnp.concatenate([xa, xb], axis=1).astype(BF), pe2_ref[kind, s2]], axis=0)
            acc = acc + _dot(lhs, w1p_ref[kind, s2])
        row0 = pl.multiple_of(j * m, 8)
        hf_ref[kind, pl.ds(row0, m), :] = acc[0:m, 0:hid2] + acc[m:m + 1, 0:hid2]
        hs_ref[kind, pl.ds(row0, m), :] = acc[0:m, hid2:] + acc[m + 1:m + 2, hid2:]

    @pl.when(j == pl.num_programs(1) - 1)
    def _():
        for kind in range(2):
            hid = jax.nn.gelu(hf_ref[kind, 0:nch, :] + hs_ref[kind, 1:nch + 1, :] + b1_ref[kind])
            out = _dot(hid.astype(BF), w2_ref[kind])
            if kind == 0:
                kc_ref[0] = _pair_norm(out, kg_ref[...]).astype(BF)
            else:
                vc_ref[0] = out.astype(BF)


def compress_paged(cachet, page_table, weights):
    bsz, n_pages = page_table.shape
    page = cachet.shape[-1]
    npg = min(CMP_PAGES, n_pages)
    nch = n_pages * page // CMP_STRIDE
    src_specs = [pl.BlockSpec((1, 2, HKV, DH, page),
                              functools.partial(lambda r, b, j, pt: (pt[b, j * npg + r], 0, 0, 0, 0), r))
                 for r in range(npg)]
    pef, pes, w1f, w1s, b1, w2, kg = weights
    np2 = CMP_STRIDE // 2
    pair_rows = lambda a: a.reshape(2, np2, 2 * KVW)
    pe2 = jnp.zeros((2, np2, CMP_PE_ROWS, 2 * KVW), F32).at[:, :, 0].set(pair_rows(pef)).at[:, :, 1].set(
        pair_rows(pes)).astype(BF)
    pair_w = lambda w: w.reshape(2, np2, 2 * KVW, 2 * CMP_HIDDEN)
    w1p = jnp.concatenate([pair_w(w1f), pair_w(w1s)], axis=-1)
    weights = (pe2, w1p, b1, w2, kg)
    _, wspecs, out_specs, out_shape, scratch = _compress_common(1, npg, page, nch, bsz, weights)
    return pl.pallas_call(
        functools.partial(_compress_paged_kernel, npg=npg, page=page, nch=nch),
        grid_spec=pltpu.PrefetchScalarGridSpec(
            num_scalar_prefetch=1, grid=(bsz, n_pages // npg), in_specs=src_specs + wspecs,
            out_specs=out_specs, scratch_shapes=scratch + [pltpu.VMEM((2, npg * page, KVW), F32)]),
        out_shape=out_shape, compiler_params=_cp("parallel", "arbitrary"), name="compress_paged",
    )(page_table, *([cachet] * npg), *weights)


NS_PAD = 256
SEL_HALF = 128
LOG2E = math.log2(math.e)
PEN = 30000.0
KT = 256
SEL_SPLIT = 2
VROWS = 80
HALF_TILES = SEL_HALF * SEL_BLOCK // KT
CBAND = 24


def _t5_bucket(dist):
    n = jnp.maximum(dist, 0)
    large = REL_EXACT + (jnp.log(jnp.maximum(n, 1).astype(F32) / REL_EXACT)
                         / math.log(REL_MAX_DIST / REL_EXACT) * (REL_BUCKETS - REL_EXACT)).astype(jnp.int32)
    return jnp.where(n < REL_EXACT, n, jnp.minimum(large, REL_BUCKETS - 1))


def _selection_matrix(nc, ns_pad, nc_pad):
    ratio, span = SEL_BLOCK // CMP_STRIDE, CMP_LEN // CMP_STRIDE
    a = np.zeros((ns_pad, nc_pad), np.float32)
    for j in range(ns_pad):
        for mm in range(ratio):
            for nn in range(span):
                n = ratio * j + mm - nn
                if 0 <= n < nc:
                    a[j, n] += 1.0
    return a


def _split3(x):
    hi = x.astype(BF)
    r1 = x - hi.astype(F32)
    mid = r1.astype(BF)
    lo = (r1 - mid.astype(F32)).astype(BF)
    return hi, mid, lo


def _top_blocks(s, qp, n_rounds):
    j = lax.broadcasted_iota(jnp.int32, s.shape, 0)
    cur = qp // SEL_BLOCK
    forced = (j == 0) | (j == cur) | (j == cur - 1)
    valid = j * SEL_BLOCK <= qp
    s = jnp.where(forced, BIG, s)
    s = jnp.where(valid, s, -BIG)
    sel = jnp.zeros(s.shape, F32)
    for _ in range(n_rounds):
        mx = jnp.max(s, axis=0, keepdims=True)
        jm = jnp.min(jnp.where(s == mx, j, 2 * NS_PAD), axis=0, keepdims=True)
        hit = j == jm
        sel = jnp.where(hit, 1.0, sel)
        s = jnp.where(hit, -3e38, s)
    return sel


def _masked_softmax_rows(s, axis):
    m = jnp.max(s, axis=axis, keepdims=True)
    e = jnp.exp(s - m)
    den = jnp.maximum(jnp.sum(e, axis=axis, keepdims=True), 1e-30)
    return e * jnp.where(m > 0.5 * NEG, 1.0 / den, 0.0)


def _nsa_prompt_kernel(q_ref, gate_ref, qg_ref, kc_ref, vct_ref, amat_ref, kaug_ref, vselt_ref, kwin_ref,
                       vwint_ref, bc_ref, bs_ref, bw_ref, o_ref, lg_ref, acc_ref, qa_ref, qh_ref, sa_ref, sb_ref,
                       m_ref, mx_ref, *, ncp, n_top):
    i = pl.program_id(0)
    qb = Q_BLOCK
    rows = GRP * qb
    start = i * qb
    qall = q_ref[...]
    lane = lax.broadcasted_iota(jnp.int32, (qb, 2 * DH), 1)
    qpos_row = start + lax.broadcasted_iota(jnp.int32, (1, qb), 1)
    t_last = start // KT
    n_far = jnp.maximum(t_last - 1, 0)

    @pl.when(i == 0)
    def _():
        lg_ref[:, 0:16, :] = jnp.zeros((HKV, 16, rows), F32)

    o_cs = []
    for h in range(HKV):
        parts = []
        for g in range(GRP):
            hd = h * GRP + g
            slab = qall[:, (hd // 2) * 2 * DH:(hd // 2 + 1) * 2 * DH]
            mine = (lane >= DH) if hd % 2 else (lane < DH)
            ss = jnp.sum(jnp.where(mine, slab * slab, 0.0), axis=-1, keepdims=True)
            xn = jnp.where(mine, slab * lax.rsqrt(ss * (1.0 / DH) + RMS_EPS) * qg_ref[...] * (DH ** -0.5), 0.0)
            if hd % 2 != h:
                xn = pltpu.roll(xn, DH, 1)
            parts.append(xn)
        xf = jnp.concatenate(parts, axis=0)
        qh = xf.astype(BF)

        def compressed(n, h=h, qh=qh):
            lg_ref[h, 16:16 + n, :] = _dot_nt(kc_ref[0:n, :], qh)
            band0 = pl.multiple_of(i * 8, 8)
            lg_ref[h, pl.ds(band0, CBAND), :] = lg_ref[h, pl.ds(band0, CBAND), :] + bc_ref[h]
            tok = lax.broadcasted_iota(jnp.int32, (n, 1), 0)
            p_c = _masked_softmax_rows(jnp.where(tok < 8 * i + 8, lg_ref[h, 16:16 + n, :], NEG), 0)
            o_c = _dot(vct_ref[h, :, 0:n], p_c.astype(BF))
            imp = p_c[:, 0:qb]
            for g in range(1, GRP):
                imp = imp + p_c[:, g * qb:(g + 1) * qb]
            sc3 = _dot(amat_ref[:, 0:n], jnp.concatenate(_split3(imp), axis=1))
            return o_c, sc3[:, 0:qb] + sc3[:, qb:2 * qb] + sc3[:, 2 * qb:]

        cch = min(ncp, 2 * Q_BLOCK)
        nbr = ncp // cch
        o_c, score = lax.switch(jnp.minimum((8 * i + 7) // cch, nbr - 1),
                                [functools.partial(compressed, (k + 1) * cch) for k in range(nbr)])
        sel = _top_blocks(score, qpos_row, n_top)
        pen = ((sel.T - 1.0) * PEN).astype(BF)
        qh2 = (xf * LOG2E).astype(BF)
        qa_ref[h, 0] = jnp.concatenate([jnp.concatenate([pen[:, 0:SEL_HALF]] * GRP, axis=0), qh2], axis=1)
        qa_ref[h, 1] = jnp.concatenate([jnp.concatenate([pen[:, SEL_HALF:]] * GRP, axis=0), qh2], axis=1)
        qh_ref[h] = qh
        o_cs.append(o_c)

    hr = rows // SEL_SPLIT
    chains = [(h, slice(sp * hr, (sp + 1) * hr)) for h in range(HKV) for sp in range(SEL_SPLIT)]
    nchain = len(chains)

    def scores_f32(c, t):
        h, rs = chains[c]
        qa = jnp.where(t < HALF_TILES, qa_ref[h, 0, rs, :], qa_ref[h, 1, rs, :])
        return _dot_nt(kaug_ref[t + 1], qa)

    scores = lambda c, t: scores_f32(c, t).astype(BF)

    def absorb(c, s, mx, t, m):
        h, rs = chains[c]
        m2 = jnp.maximum(m, mx)
        p = jnp.exp2(s - m2.astype(BF))
        acc_ref[h, :, rs] = jnp.exp2(m - m2) * acc_ref[h, :, rs] + _dot(vselt_ref[h, t + 1], p)
        return m2

    col_max = lambda s: jnp.max(s, axis=0, keepdims=True).astype(F32)

    def pair_body(k, carry):
        ms, mxa = list(carry[:nchain]), list(carry[nchain:])
        mxb = []
        for c, (h, rs) in enumerate(chains):
            s1 = scores(c, 2 * k + 1)
            sb_ref[h, :, rs] = s1
            mxb.append(col_max(s1))
            ms[c] = absorb(c, sa_ref[h, :, rs], mxa[c], 2 * k, ms[c])
        for c, (h, rs) in enumerate(chains):
            s2 = scores(c, 2 * k + 2)
            ms[c] = absorb(c, sb_ref[h, :, rs], mxb[c], 2 * k + 1, ms[c])
            sa_ref[h, :, rs] = s2
            mxa[c] = col_max(s2)
        return tuple(ms) + tuple(mxa)

    acc_ref[...] = jnp.zeros((HKV, VROWS, rows), F32)
    mxa = []
    for c, (h, rs) in enumerate(chains):
        s0 = scores(c, 0)
        sa_ref[h, :, rs] = s0
        mxa.append(col_max(s0))
    carry = tuple(jnp.full((1, hr), NEG, F32) for _ in range(nchain)) + tuple(mxa)
    carry = lax.fori_loop(0, n_far // 2, pair_body, carry)
    for c, (h, rs) in enumerate(chains):
        m_ref[h, :, rs] = carry[c]
        mx_ref[h, :, rs] = carry[nchain + c]

    @pl.when(n_far % 2 == 1)
    def _():
        for c, (h, rs) in enumerate(chains):
            m_ref[h, :, rs] = absorb(c, sa_ref[h, :, rs], mx_ref[h, :, rs], n_far - 1, m_ref[h, :, rs])

    kiota = lax.broadcasted_iota(jnp.int32, (KT, 1), 0)
    ms = [m_ref[h, :, rs] for h, rs in chains]
    for u in range(2):
        t = t_last - 1 + u
        for c, (h, rs) in enumerate(chains):
            s = jnp.where(t * KT + kiota >= 0, scores_f32(c, t) + bs_ref[i % 2, h, u, :, rs], NEG).astype(BF)
            ms[c] = absorb(c, s, col_max(s), t, ms[c])

    gates_t = jax.nn.sigmoid(gate_ref[...]).T
    wkeys = WINDOW + qb
    for h in range(HKV):
        o_s = acc_ref[h, 0:DH, :] / jnp.maximum(acc_ref[h, DH:DH + 1, :], 1e-30)

        s = _dot_nt(kwin_ref[pl.ds(pl.multiple_of(start, qb), wkeys), :], qh_ref[h]) + bw_ref[h]
        kpos = start - WINDOW + lax.broadcasted_iota(jnp.int32, (wkeys, 1), 0)
        s = jnp.where(kpos >= 0, s, NEG)
        e = jnp.exp(s - jnp.max(s, axis=0, keepdims=True))
        vw = jnp.concatenate([vwint_ref[h, i + u] for u in range(wkeys // qb)], axis=1)
        ow = _dot(vw, e.astype(BF))
        o_w = ow[0:DH] / jnp.maximum(ow[DH:DH + 1], 1e-30)

        gate = lambda br: jnp.concatenate(
            [gates_t[(h * GRP + g) * 3 + br:(h * GRP + g) * 3 + br + 1, :] for g in range(GRP)], axis=1)
        o_t = gate(0) * o_cs[h] + gate(1) * o_s + gate(2) * o_w
        for g in range(GRP):
            o_ref[:, (h * GRP + g) * DH:(h * GRP + g + 1) * DH] = o_t[:, g * qb:(g + 1) * qb].T


def _bias_tables(rel_table):
    tab = rel_table.astype(F32)
    far = tab[REL_BUCKETS - 1]
    qb = Q_BLOCK

    def toeplitz(off, nk, dmax, shift):
        d = off - (nk - 1) + jnp.arange(nk + qb - 1, dtype=jnp.int32)
        w = jnp.where(((d >= 0) & (d <= dmax))[None, :], (tab[_t5_bucket(d)] - (far if shift else 0.0)).T, NEG)
        p = w.shape[1]
        hank = jnp.tile(w, (1, nk + 1))[:, :nk * (p + 1)].reshape(NH, nk, p + 1)[:, :, :qb]
        return hank[:, ::-1, :]

    def lanes_gq(b):
        return b.reshape(HKV, GRP, b.shape[1], qb).transpose(0, 2, 1, 3).reshape(HKV, b.shape[1], GRP * qb)

    big = 1 << 30
    bc = lanes_gq(toeplitz(16 * CMP_STRIDE - (CMP_LEN - 1), CBAND * CMP_STRIDE, big, True)[:, ::CMP_STRIDE, :])
    bs = jnp.stack([lanes_gq(toeplitz(off, 2 * KT, big, True)).reshape(HKV, 2, KT, GRP * qb)
                    for off in (KT, KT + qb)]) * LOG2E
    bw = lanes_gq(toeplitz(WINDOW, WINDOW + qb, WINDOW, False))
    return bc, bs, bw


def nsa_prompt(proj, kc, vc, ksel, vsel, kwin, vwin, q_g, rel_table):
    t = proj.shape[0]
    nb = t // Q_BLOCK
    nch = kc.shape[0]
    ncp = -(-nch // 128) * 128
    assert t // SEL_BLOCK <= NS_PAD and t % KT == 0
    n_top = min(TOP_N, t // SEL_BLOCK)
    kcp = jnp.pad(kc, ((0, ncp - nch), (0, 0)))
    vct = jnp.pad(vc, ((0, ncp - nch), (0, 0))).reshape(ncp, HKV, DH).transpose(1, 2, 0)
    amat = jnp.asarray(_selection_matrix(nch - 1, NS_PAD, ncp), BF)

    def values_t(v, pad_rows, tile):
        vt = jnp.pad(v.astype(BF), ((pad_rows, 0), (0, 0))).reshape(-1, tile, HKV, DH).transpose(2, 0, 3, 1)
        return jnp.concatenate([vt, jnp.ones(vt.shape[:2] + (1, tile), BF),
                                jnp.zeros(vt.shape[:2] + (VROWS - DH - 1, tile), BF)], axis=2)

    onehot = ((jnp.arange(t, dtype=jnp.int32)[:, None] // SEL_BLOCK) % SEL_HALF
              == jnp.arange(SEL_HALF, dtype=jnp.int32)[None, :]).astype(BF)
    kaug = jnp.pad(jnp.concatenate([onehot, ksel.astype(BF)], axis=1), ((KT, 0), (0, 0))).reshape(-1, KT, 2 * KVW)
    vselt = values_t(vsel, KT, KT)
    vwint = values_t(vwin, WINDOW, Q_BLOCK)
    rows = GRP * Q_BLOCK
    kwinp = jnp.pad(kwin.astype(BF), ((WINDOW, 0), (0, 0)))
    bc, bs, bw = _bias_tables(rel_table)
    whole = pl.BlockSpec(memory_space=pltpu.VMEM)
    return pl.pallas_call(
        functools.partial(_nsa_prompt_kernel, ncp=ncp, n_top=n_top),
        grid=(nb,),
        in_specs=[pl.BlockSpec((Q_BLOCK, NH * DH), lambda i: (i, O_Q // (NH * DH))),
                  pl.BlockSpec((Q_BLOCK, 128), lambda i: (i, O_GATE // 128)),
                  whole, whole, whole, whole, whole, whole, whole, whole, whole, whole, whole],
        out_specs=pl.BlockSpec((Q_BLOCK, NH * DH), lambda i: (i, 0)),
        out_shape=jax.ShapeDtypeStruct((t, NH * DH), F32),
        scratch_shapes=[pltpu.VMEM((HKV, 16 + ncp, rows), F32), pltpu.VMEM((HKV, VROWS, rows), F32),
                        pltpu.VMEM((HKV, 2, rows, SEL_HALF + KVW), BF), pltpu.VMEM((HKV, rows, KVW), BF),
                        pltpu.VMEM((HKV, KT, rows), BF), pltpu.VMEM((HKV, KT, rows), BF),
                        pltpu.VMEM((HKV, 1, rows), F32), pltpu.VMEM((HKV, 1, rows), F32)],
        compiler_params=_cp("arbitrary"),
        name="nsa_prompt",
    )(proj, proj, jnp.concatenate([q_g, q_g]).reshape(1, 2 * DH), kcp, vct, amat, kaug, vselt, kwinp, vwint,
      bc, bs, bw)


def _pick_head(x, lane_h):
    return jnp.where(lane_h == 0, x[0:DH], x[DH:2 * DH])


def _nsa_dec_pre_kernel(qbd_ref, kc_ref, vc_ref, amat_ref, bc_ref, win_ref, wnew_ref, bwa_ref, bwb_ref,
                        pen_ref, oc_ref, ow_ref, *, ts, pos0, n_top):
    r_all = GRP * HKV * ts
    qbd = qbd_ref[0]
    lane_h = (lax.broadcasted_iota(jnp.int32, (1, r_all), 1) // ts) % HKV

    p_c = _masked_softmax_rows(_dot(kc_ref[0], qbd) + bc_ref[...], 0)
    oc_ref[0] = _pick_head(_dot_tn(vc_ref[0], p_c.astype(BF)), lane_h)

    hi, mid, lo = _split3(p_c)
    amat = amat_ref[...]
    sc = _dot(amat, hi) + _dot(amat, mid) + _dot(amat, lo)
    w8 = HKV * ts
    score = sc[:, 0:w8]
    for g in range(1, GRP):
        score = score + sc[:, g * w8:(g + 1) * w8]
    qp = pos0 + lax.broadcasted_iota(jnp.int32, (1, w8), 1) % ts
    sel = _top_blocks(score, qp, n_top)
    pen_ref[0] = (jnp.concatenate([sel] * GRP, axis=1) - 1.0) * PEN

    win = win_ref[0]
    wnew = wnew_ref[0]
    s_a = _dot(win[:, 0:KVW].astype(BF), qbd) + bwa_ref[...]
    s_b = _dot(wnew[:, 0:KVW].astype(BF), qbd) + bwb_ref[...]
    m = jnp.maximum(jnp.max(s_a, axis=0, keepdims=True), jnp.max(s_b, axis=0, keepdims=True))
    e_a = jnp.where(s_a > 0.5 * NEG, jnp.exp(s_a - m), 0.0)
    e_b = jnp.where(s_b > 0.5 * NEG, jnp.exp(s_b - m), 0.0)
    den = jnp.maximum(jnp.sum(e_a, axis=0, keepdims=True) + jnp.sum(e_b, axis=0, keepdims=True), 1e-30)
    o_w = (_dot_tn(win[:, KVW:].astype(BF), e_a.astype(BF)) + _dot_tn(wnew[:, KVW:].astype(BF), e_b.astype(BF)))
    ow_ref[0] = _pick_head(o_w, lane_h) / den


SEL_PAGES = 32


def _nsa_dec_sel_kernel(*refs, npg, page, ts):
    pages = refs[1:1 + npg]
    (q_ref, pen_ref, expt_ref, blast_ref, knew_ref, vnew_ref, bnew_ref, gate_ref, oc_ref, ow_ref,
     o_ref, m_ref, l_ref, acc_ref) = refs[1 + npg:]
    j = pl.program_id(1)
    last = pl.num_programs(1) - 1
    rows = GRP * ts

    @pl.when(j == 0)
    def _():
        m_ref[...] = jnp.full((HKV * rows, 1), NEG, F32)
        l_ref[...] = jnp.zeros((HKV * rows, 1), F32)
        acc_ref[...] = jnp.zeros((HKV * rows, DH), F32)

    def online(s, pvs):
        m = m_ref[...]
        m2 = jnp.maximum(m, jnp.max(s, axis=-1, keepdims=True))
        a = jnp.exp(m - m2)
        p = jnp.exp(s - m2)
        m_ref[...] = m2
        l_ref[...] = a * l_ref[...] + jnp.sum(p, axis=-1, keepdims=True)
        pb = p.astype(BF)
        pv = jnp.concatenate([pvs[h](pb[h * rows:(h + 1) * rows]) for h in range(HKV)], axis=0)
        acc_ref[...] = a * acc_ref[...] + pv

    both = lambda f: jnp.concatenate([f(h) for h in range(HKV)], axis=0)
    vts = [jnp.concatenate([r[0, 1, h] for r in pages], axis=1).astype(BF) for h in range(HKV)]
    s = both(lambda h: _dot(q_ref[0, h], jnp.concatenate([r[0, 0, h] for r in pages], axis=1).astype(BF))
             + _dot(pen_ref[0, h, j], expt_ref[...]))
    s = s + jnp.where(j == last, blast_ref[...], 0.0)
    online(s, [functools.partial(lambda vt, p: _dot_nt(p, vt), vts[h]) for h in range(HKV)])

    @pl.when(j == last)
    def _():
        s_new = both(lambda h: _dot_nt(q_ref[0, h], knew_ref[0, h].astype(BF))) + bnew_ref[...]
        online(s_new, [functools.partial(lambda v, p: _dot(p, v), vnew_ref[0, h].astype(BF)) for h in range(HKV)])
        o_s = acc_ref[...] / jnp.maximum(l_ref[...], 1e-30)
        g = jax.nn.sigmoid(gate_ref[0])
        o_ref[0] = g[:, 0:1] * oc_ref[0] + g[:, 1:2] * o_s + g[:, 2:3] * ow_ref[0]


def _dec_bias_tables(rel_table, pos0, ts, nch, wb, page, npg):
    tab = rel_table.astype(F32)
    far = tab[REL_BUCKETS - 1]
    r = np.arange(GRP * HKV * ts)
    head = ((r // ts) % HKV) * GRP + r // (HKV * ts)
    qpos = pos0 + r % ts
    far_r = far[head][None, :]

    def bias(kpos, dmax, shift):
        dist = qpos[None, :] - np.asarray(kpos)[:, None]
        ok = (dist >= 0) & (dist <= dmax)
        out = jnp.where(ok, 0.0 if shift else far_r, NEG)
        near = np.nonzero((ok & (dist < REL_MAX_DIST)).any(axis=1))[0]
        if near.size:
            lo, hi = int(near.min()), int(near.max()) + 1
            b = tab[_t5_bucket(jnp.asarray(dist[lo:hi], jnp.int32)), head[None, :]] - (far_r if shift else 0.0)
            out = jnp.concatenate([out[:lo], jnp.where(ok[lo:hi], b, NEG), out[hi:]], axis=0)
        return out

    big = 1 << 30
    bc = bias(np.arange(nch) * CMP_STRIDE + CMP_LEN - 1, big, False)
    bwa = bias(pos0 - wb + np.arange(wb), WINDOW, False)
    tnew = np.arange(8)
    newpos = np.where(tnew < ts, pos0 + tnew, pos0 + 2 * WINDOW + SEL_BLOCK)
    bwb = bias(newpos, WINDOW, False)
    bnew = bias(newpos, big, True)
    step = npg * page
    blast = jnp.concatenate([jnp.zeros((step - page, r.shape[0]), F32), bias(pos0 - page + np.arange(page), big, True)],
                            axis=0)
    return bc, bwa, bwb, bnew, blast


def nsa_decode(qn, gate_raw, kc, vc, cachet, page_table, win_state, kv_new, win_new, rel_table):
    bsz, ts, _ = qn.shape
    n_pages = page_table.shape[1]
    page = cachet.shape[-1]
    pos0 = n_pages * page
    nch = kc.shape[1]
    wb = win_state.shape[1]
    r_all = GRP * HKV * ts
    assert pos0 % SEL_BLOCK == 0 and ts <= 8 and ts <= SEL_BLOCK and nch % 8 == 0
    ns = -(-(pos0 + ts) // SEL_BLOCK)
    npg = min(SEL_PAGES, n_pages)
    assert n_pages % npg == 0
    bps = npg * page // SEL_BLOCK
    ns_pad = max(-(-ns // 8) * 8, (n_pages // npg) * bps)
    n_top = min(TOP_N, ns)

    q5 = qn.reshape(bsz, ts, HKV, GRP, DH)
    qbd = jnp.einsum('bqhgd,hk->bkdghq', q5, jnp.eye(HKV, dtype=F32)).reshape(bsz, 2 * DH, r_all).astype(BF)
    gate_t = gate_raw.reshape(bsz, ts, HKV, GRP, 3).transpose(0, 4, 3, 2, 1).reshape(bsz, 3, r_all)
    pad8 = lambda a: jnp.pad(a, ((0, 0), (0, 8 - ts), (0, 0)))
    amat = jnp.asarray(_selection_matrix(nch - 1, ns_pad, nch), BF)
    expand = jnp.asarray(np.repeat(np.eye(bps, dtype=np.float32), SEL_BLOCK, axis=0), BF)
    bc, bwa, bwb, bnew, blast = _dec_bias_tables(rel_table, pos0, ts, nch, wb, page, npg)

    full = lambda a: pl.BlockSpec(a.shape, functools.partial(lambda nd, *_: (0,) * nd, a.ndim))
    per_b = lambda a: pl.BlockSpec((1,) + a.shape[1:], functools.partial(lambda nd, b, *_: (b,) + (0,) * nd, a.ndim - 1))
    wnew = pad8(win_new)
    pre_in = [qbd, kc, vc, amat, bc, win_state, wnew, bwa, bwb]
    pre_specs = [per_b(qbd), per_b(kc), per_b(vc), full(amat), full(bc), per_b(win_state), per_b(wnew),
                 full(bwa), full(bwb)]
    small = jax.ShapeDtypeStruct((bsz, DH, r_all), F32)
    pen, o_c, o_w = pl.pallas_call(
        functools.partial(_nsa_dec_pre_kernel, ts=ts, pos0=pos0, n_top=n_top),
        grid=(bsz,), in_specs=pre_specs,
        out_specs=[pl.BlockSpec((1, ns_pad, r_all), lambda b: (b, 0, 0)),
                   pl.BlockSpec((1, DH, r_all), lambda b: (b, 0, 0)),
                   pl.BlockSpec((1, DH, r_all), lambda b: (b, 0, 0))],
        out_shape=[jax.ShapeDtypeStruct((bsz, ns_pad, r_all), F32), small, small],
        compiler_params=_cp("parallel"), name="nsa_dec_pre",
    )(*pre_in)

    rows = GRP * ts
    nsteps = n_pages // npg
    by_head = lambda a: a.reshape(a.shape[:-1] + (GRP, HKV, ts))
    q_h = q5.transpose(0, 2, 3, 1, 4).reshape(bsz, HKV, rows, DH).astype(BF)
    gate_h = gate_raw.reshape(bsz, ts, HKV, GRP, 3).transpose(0, 2, 3, 1, 4).reshape(bsz, HKV, rows, 3)
    pen_h = by_head(pen[:, :nsteps * bps].reshape(bsz, nsteps, bps, r_all)).transpose(0, 4, 1, 3, 5, 2).reshape(
        bsz, HKV, nsteps, rows, bps).astype(BF)
    oc_h, ow_h = (by_head(a).transpose(0, 3, 2, 4, 1).reshape(bsz, HKV, rows, DH) for a in (o_c, o_w))
    blast_h, bnew_h = (by_head(a).transpose(2, 1, 3, 0).reshape(HKV, rows, a.shape[0]) for a in (blast, bnew))
    new5 = pad8(kv_new).reshape(bsz, 8, 4, HKV, DH)
    knew = new5[:, :, 2].transpose(0, 2, 1, 3)
    vnew = new5[:, :, 3].transpose(0, 2, 1, 3)
    page_specs = [pl.BlockSpec((1, 2, HKV, DH, page),
                               functools.partial(lambda r, b, j, pt: (pt[b, j * npg + r], 1, 0, 0, 0), r))
                  for r in range(npg)]
    heads_rows = lambda a: a.reshape(a.shape[:-3] + (HKV * rows, a.shape[-1]))
    blast_h, bnew_h, gate_h, oc_h, ow_h = map(heads_rows, (blast_h, bnew_h, gate_h, oc_h, ow_h))
    sel_in = [q_h, pen_h, expand.T, blast_h, knew, vnew, bnew_h, gate_h, oc_h, ow_h]
    sel_specs = [per_b(q_h), per_b(pen_h), full(expand.T), full(blast_h), per_b(knew), per_b(vnew), full(bnew_h),
                 per_b(gate_h), per_b(oc_h), per_b(ow_h)]
    o_h = pl.pallas_call(
        functools.partial(_nsa_dec_sel_kernel, npg=npg, page=page, ts=ts),
        grid_spec=pltpu.PrefetchScalarGridSpec(
            num_scalar_prefetch=1, grid=(bsz, nsteps), in_specs=page_specs + sel_specs,
            out_specs=pl.BlockSpec((1, HKV * rows, DH), lambda b, j, pt: (b, 0, 0)),
            scratch_shapes=[pltpu.VMEM((HKV * rows, 1), F32), pltpu.VMEM((HKV * rows, 1), F32),
                            pltpu.VMEM((HKV * rows, DH), F32)]),
        out_shape=jax.ShapeDtypeStruct((bsz, HKV * rows, DH), F32),
        compiler_params=_cp("parallel", "arbitrary"), name="nsa_dec_sel",
    )(page_table, *([cachet] * npg), *sel_in)
    return o_h.reshape(bsz, HKV, GRP, ts, DH).transpose(0, 3, 1, 2, 4).reshape(bsz, ts, NH * DH)


def _even_weights(p, e):
    w_in = jnp.pad(p['att_w_in'][e], ((0, 0), (0, ATT_IN_PAD - ATT_IN_COLS))).astype(BF)
    w_out = p['att_w_out'][e].astype(BF)
    return dict(
        w_in=w_in, w_out_conv=w_out[:C_CONV], w_out_att=w_out[C_CONV:],
        cmp=_compress_weights(p['cmp_pe'][e], p['cmp_w1'][e], p['cmp_b1'][e], p['cmp_w2'][e], p['k_norm_g'][e][0]),
        k_g=p['k_norm_g'][e], q_g=p['q_norm_g'][e], conv_w=p['conv_w'][e], conv_b=p['conv_b'][e],
        ln_g=p['conv_ln_g'][e], ln_b=p['conv_ln_b'][e])


def _even_prompt(x2, norm_g, w, rel_table):
    t = x2.shape[0]
    proj = norm_matmul(x2, norm_g, w['w_in'])
    conv_y, conv_new = conformer_conv(proj[None], jnp.zeros((1, CONV_WIDTH - 1, C_CONV), F32),
                                      w['conv_w'], w['conv_b'], w['ln_g'], w['ln_b'])
    kv_new, win_new = kv_post(proj, w['k_g'])
    kc, vc = compress_dense(proj[None], w['cmp'])
    o = nsa_prompt(proj, kc[0], vc[0], kv_new[:, 2 * KVW:3 * KVW], kv_new[:, 3 * KVW:], win_new[:, :KVW],
                   win_new[:, KVW:], w['q_g'], rel_table)
    x2 = out_proj2(x2, conv_y[0], o, w['w_out_conv'], w['w_out_att'])
    keep = min(WINDOW, t)
    return (x2, kv_new.reshape(1, t, 4, HKV, DH), win_new[t - keep:].reshape(1, keep, 2, HKV, DH), conv_new)


def _even_decode(x3, norm_g, w, rel_table, cachet, page_table, win_state, conv_state):
    bsz, t, d = x3.shape
    x2 = x3.reshape(bsz * t, d)
    proj = norm_matmul(x2, norm_g, w['w_in'])
    proj3 = proj.reshape(bsz, t, ATT_IN_PAD)
    conv_y, conv_new = conformer_conv(proj3, conv_state, w['conv_w'], w['conv_b'], w['ln_g'], w['ln_b'])
    kv_new, win_new = kv_post(proj, w['k_g'])
    qn = q_norm(proj, w['q_g'])
    kc, vc = compress_paged(cachet, page_table, w['cmp'])
    wb = win_state.shape[1]
    o = nsa_decode(qn.reshape(bsz, t, NH * DH), proj3[:, :, O_GATE:ATT_IN_COLS], kc, vc, cachet, page_table,
                   win_state.reshape(bsz, wb, 2 * KVW), kv_new.reshape(bsz, t, 4 * KVW),
                   win_new.reshape(bsz, t, 2 * KVW), rel_table)
    x2 = out_proj2(x2, conv_y.reshape(bsz * t, C_CONV), o.reshape(bsz * t, NH * DH), w['w_out_conv'],
                   w['w_out_att'])
    win_all = jnp.concatenate([win_state, win_new.reshape(bsz, t, 2, HKV, DH)], axis=1)
    keep = min(WINDOW, wb + t)
    return (x2.reshape(bsz, t, d), kv_new.reshape(bsz, t, 4, HKV, DH), win_all[:, wb + t - keep:], conv_new)


def _odd_layer(x3, s0, norm_g, w_in_bf, w_out_bf, lb, hg_norm_g):
    bsz, t, d = x3.shape
    x2 = x3.reshape(bsz * t, d)
    z = norm_matmul(x2, norm_g, w_in_bf, tn=2048).reshape(bsz, t, 4 * d)
    tp = -(-t // HG_CHUNK) * HG_CHUNK
    if tp != t:
        z = jnp.pad(z, ((0, 0), (0, tp - t), (0, 0)))
    o, s_new = hgrn2(z, s0, lb, hg_norm_g, t)
    x2 = out_proj1(x2, o[:, :t].reshape(bsz * t, d), w_out_bf)
    return x2.reshape(bsz, t, d), s_new


def kernel(x_prompt, x_sample, cache_nsa_kv, page_table, state_nsa_win, state_conv, state_hgrn, rel_bias_table,
           norm_mix_g, norm_mlp_g, w_mlp_up, w_mlp_down, att_w_in, att_w_out, q_norm_g, k_norm_g, cmp_pe, cmp_w1,
           cmp_b1, cmp_w2, conv_w, conv_b, conv_ln_g, conv_ln_b, hg_w_in, hg_w_out, hg_lb_logits, hg_norm_g):
    p = dict(att_w_in=att_w_in, att_w_out=att_w_out, q_norm_g=q_norm_g, k_norm_g=k_norm_g, cmp_pe=cmp_pe,
             cmp_w1=cmp_w1, cmp_b1=cmp_b1, cmp_w2=cmp_w2, conv_w=conv_w, conv_b=conv_b, conv_ln_g=conv_ln_g,
             conv_ln_b=conv_ln_b)
    bp, tp_, d = x_prompt.shape
    assert bp == 1
    db, ts, _ = x_sample.shape
    w_up = w_mlp_up.astype(BF)
    w_down = w_mlp_down.astype(BF)
    cum = jnp.cumsum(jax.nn.softmax(hg_lb_logits.astype(F32), axis=0), axis=0)

    xp = x_prompt[0]
    kv_p, win_p, conv_p, hg_p = [], [], [], []
    for layer in range(norm_mix_g.shape[0]):
        if layer % 2 == 0:
            e = layer // 2
            w = _even_weights(p, e)
            xp, kv_new, win_new, conv_new = _even_prompt(xp, norm_mix_g[layer], w, rel_bias_table)
            kv_p.append(kv_new)
            win_p.append(win_new)
            conv_p.append(conv_new)
        else:
            o = layer // 2
            x3, s_new = _odd_layer(xp[None], jnp.zeros((1, HG_HEADS, HG_DK, HG_DV), F32), norm_mix_g[layer],
                                   hg_w_in[o].astype(BF), hg_w_out[o].astype(BF), cum[layer] - cum[0],
                                   hg_norm_g[o])
            xp = x3[0]
            hg_p.append(s_new)
        xp = mlp(xp, norm_mlp_g[layer], w_up[layer], w_down[layer])

    cachet = cache_nsa_kv.transpose(0, 1, 3, 4, 5, 2)
    xs = x_sample
    kv_s, win_s, conv_s, hg_s = [], [], [], []
    for layer in range(norm_mix_g.shape[0]):
        if layer % 2 == 0:
            e = layer // 2
            w = _even_weights(p, e)
            xs, kv_new, win_new, conv_new = _even_decode(xs, norm_mix_g[layer], w, rel_bias_table, cachet[e],
                                                         page_table, state_nsa_win[e], state_conv[e])
            kv_s.append(kv_new)
            win_s.append(win_new)
            conv_s.append(conv_new)
        else:
            o = layer // 2
            xs, s_new = _odd_layer(xs, state_hgrn[o], norm_mix_g[layer], hg_w_in[o].astype(BF),
                                   hg_w_out[o].astype(BF), cum[layer] - cum[0], hg_norm_g[o])
            hg_s.append(s_new)
        xs = mlp(xs.reshape(db * ts, d), norm_mlp_g[layer], w_up[layer], w_down[layer]).reshape(db, ts, d)
    return (xp[None], xs, jnp.stack(kv_p), jnp.stack(kv_s), jnp.stack(win_p), jnp.stack(win_s),
            jnp.stack(conv_p), jnp.stack(conv_s), jnp.stack(hg_p), jnp.stack(hg_s))
```

```python
import functools
import math

import jax
import jax.numpy as jnp
import numpy as np
from jax import lax
from jax.experimental import pallas as pl
from jax.experimental.pallas import tpu as pltpu

D_MODEL = 1024
C_CONV = 512
CONV_WIDTH = 31
DH = 64
HKV = 2
GRP = 4
NH = HKV * GRP
CMP_STRIDE = 16
CMP_LEN = 32
CMP_HIDDEN = 128
SEL_BLOCK = 64
TOP_N = 16
WINDOW = 512
Q_BLOCK = 128
REL_BUCKETS = 32
REL_EXACT = 16
REL_MAX_DIST = 128
HG_DK = 128
HG_HEADS = 8
HG_DV = 128
HG_CHUNK = 64
RMS_EPS = 1e-6
NEG = -1e30
BIG = 1e9
O_Q = 2 * C_CONV
O_KV = O_Q + NH * DH
O_GATE = O_KV + 3 * 2 * HKV * DH
ATT_IN_COLS = O_GATE + 3 * NH
ATT_IN_PAD = 2432
KVW = HKV * DH

VMEM_LIMIT = 56 * 1024 * 1024
BF = jnp.bfloat16
F32 = jnp.float32


def _cp(*sem):
    return pltpu.CompilerParams(dimension_semantics=sem, vmem_limit_bytes=VMEM_LIMIT)


def _dot(a, b):
    return jnp.dot(a, b, preferred_element_type=F32)


def _dot_nt(a, b):
    return lax.dot_general(a, b, (((1,), (1,)), ((), ())), preferred_element_type=F32)


def _dot_tn(a, b):
    return lax.dot_general(a, b, (((0,), (0,)), ((), ())), preferred_element_type=F32)


def _rms(x, g):
    return x * lax.rsqrt(jnp.mean(x * x, axis=-1, keepdims=True) + RMS_EPS) * g


def _norm_matmul_kernel(x_ref, g_ref, w_ref, o_ref):
    hn = _rms(x_ref[...], g_ref[...]).astype(BF)
    o_ref[...] = _dot(hn, w_ref[...])


def norm_matmul(x, g, w_bf, tn=None):
    m, d = x.shape
    n = w_bf.shape[1]
    tm = min(m, 512)
    tn = n if tn is None else tn
    return pl.pallas_call(
        _norm_matmul_kernel,
        grid=(m // tm, n // tn),
        in_specs=[pl.BlockSpec((tm, d), lambda i, j: (i, 0)),
                  pl.BlockSpec((1, d), lambda i, j: (0, 0)),
                  pl.BlockSpec((d, tn), lambda i, j: (0, j))],
        out_specs=pl.BlockSpec((tm, tn), lambda i, j: (i, j)),
        out_shape=jax.ShapeDtypeStruct((m, n), F32),
        compiler_params=_cp("parallel", "arbitrary"),
        name="norm_matmul",
    )(x, g.reshape(1, d), w_bf)


def _out_proj_kernel(r_ref, a1_ref, a2_ref, w1_ref, w2_ref, o_ref):
    o_ref[...] = (r_ref[...] + _dot(a1_ref[...].astype(BF), w1_ref[...])
                  + _dot(a2_ref[...].astype(BF), w2_ref[...]))


def out_proj2(res, a1, a2, w1_bf, w2_bf):
    m, d = res.shape
    k1, k2 = a1.shape[1], a2.shape[1]
    tm = min(m, 512)
    return pl.pallas_call(
        _out_proj_kernel,
        grid=(m // tm,),
        in_specs=[pl.BlockSpec((tm, d), lambda i: (i, 0)),
                  pl.BlockSpec((tm, k1), lambda i: (i, 0)),
                  pl.BlockSpec((tm, k2), lambda i: (i, 0)),
                  pl.BlockSpec((k1, d), lambda i: (0, 0)),
                  pl.BlockSpec((k2, d), lambda i: (0, 0))],
        out_specs=pl.BlockSpec((tm, d), lambda i: (i, 0)),
        out_shape=jax.ShapeDtypeStruct((m, d), F32),
        compiler_params=_cp("parallel"),
        name="out_proj2",
    )(res, a1, a2, w1_bf, w2_bf)


def _out_proj1_kernel(r_ref, a_ref, w_ref, o_ref):
    o_ref[...] = r_ref[...] + _dot(a_ref[...].astype(BF), w_ref[...])


def out_proj1(res, a, w_bf):
    m, d = res.shape
    k = a.shape[1]
    tm = min(m, 512)
    return pl.pallas_call(
        _out_proj1_kernel,
        grid=(m // tm,),
        in_specs=[pl.BlockSpec((tm, d), lambda i: (i, 0)),
                  pl.BlockSpec((tm, k), lambda i: (i, 0)),
                  pl.BlockSpec((k, d), lambda i: (0, 0))],
        out_specs=pl.BlockSpec((tm, d), lambda i: (i, 0)),
        out_shape=jax.ShapeDtypeStruct((m, d), F32),
        compiler_params=_cp("parallel"),
        name="out_proj1",
    )(res, a, w_bf)


def _mlp_kernel(x_ref, g_ref, wu_ref, wd_ref, o_ref, hn_ref, acc_ref):
    j = pl.program_id(1)

    @pl.when(j == 0)
    def _():
        hn_ref[...] = _rms(x_ref[...], g_ref[...]).astype(BF)
        acc_ref[...] = x_ref[...]

    hid = jnp.maximum(_dot(hn_ref[...], wu_ref[...]), 0.0)
    acc_ref[...] += _dot((hid * hid).astype(BF), wd_ref[...])

    @pl.when(j == pl.num_programs(1) - 1)
    def _():
        o_ref[...] = acc_ref[...]


def mlp(x, g, wu_bf, wd_bf):
    m, d = x.shape
    hdim = wu_bf.shape[1]
    tm = min(m, 1024)
    th = 1024
    return pl.pallas_call(
        _mlp_kernel,
        grid=(m // tm, hdim // th),
        in_specs=[pl.BlockSpec((tm, d), lambda i, j: (i, 0)),
                  pl.BlockSpec((1, d), lambda i, j: (0, 0)),
                  pl.BlockSpec((d, th), lambda i, j: (0, j)),
                  pl.BlockSpec((th, d), lambda i, j: (j, 0))],
        out_specs=pl.BlockSpec((tm, d), lambda i, j: (i, 0)),
        out_shape=jax.ShapeDtypeStruct((m, d), F32),
        scratch_shapes=[pltpu.VMEM((tm, d), BF), pltpu.VMEM((tm, d), F32)],
        compiler_params=_cp("parallel", "arbitrary"),
        name="mlp",
    )(x, g.reshape(1, d), wu_bf, wd_bf)


CONV_HALO = 32
CONV_PAD = CONV_HALO - (CONV_WIDTH - 1)


def _conv_kernel(u_ref, st_ref, w_ref, b_ref, lg_ref, lb_ref, y_ref, new_ref, xin_ref, ph_ref, *, tt):
    t = pl.program_id(1)

    @pl.when(t == 0)
    def _():
        xin_ref[0:CONV_HALO, :] = st_ref[0]

    a = u_ref[0, :, 0:C_CONV]
    gt = u_ref[0, :, C_CONV:2 * C_CONV]
    xin_ref[CONV_HALO:CONV_HALO + tt, :] = a * jax.nn.sigmoid(gt)

    rb = min(tt, 128)
    for c in range(C_CONV // 128):
        cs = slice(c * 128, (c + 1) * 128)
        starts = list(range(0, tt, rb))
        accs = [jnp.zeros((rb, 128), F32) + b_ref[:, cs] for _ in starts]
        for ph in range(min(8, CONV_WIDTH)):
            taps = range(ph, CONV_WIDTH, 8)
            n = rb + 8 * (len(taps) - 1)
            if rb % 8 == 0:
                for bi, r0 in enumerate(starts):
                    ph_ref[bi, 0:n, :] = xin_ref[CONV_PAD + ph + r0:CONV_PAD + ph + r0 + n, cs]
            for a, k in enumerate(taps):
                wk = w_ref[k:k + 1, cs]
                for bi, r0 in enumerate(starts):
                    if rb % 8 == 0:
                        accs[bi] = accs[bi] + wk * ph_ref[bi, 8 * a:8 * a + rb, :]
                    else:
                        accs[bi] = accs[bi] + wk * xin_ref[CONV_PAD + k + r0:CONV_PAD + k + r0 + rb, cs]
        for bi, r0 in enumerate(starts):
            y_ref[0, r0:r0 + rb, cs] = accs[bi]
    y = y_ref[0]
    mu = jnp.mean(y, axis=-1, keepdims=True)
    yc = y - mu
    var = jnp.mean(yc * yc, axis=-1, keepdims=True)
    z = yc * lax.rsqrt(var + RMS_EPS) * lg_ref[...] + lb_ref[...]
    y_ref[0] = z * jax.nn.sigmoid(z)

    @pl.when(t == pl.num_programs(1) - 1)
    def _():
        new_ref[0] = xin_ref[tt + CONV_PAD:tt + CONV_HALO, :]

    if tt >= CONV_HALO:
        @pl.when(t < pl.num_programs(1) - 1)
        def _():
            xin_ref[0:CONV_HALO, :] = xin_ref[tt:tt + CONV_HALO, :]


def conformer_conv(proj, state, w, b, ln_g, ln_b):
    bsz, t, _ = proj.shape
    tt = min(t, 256)
    assert t % tt == 0 and (t == tt or tt >= CONV_HALO)
    st = jnp.pad(state, ((0, 0), (CONV_PAD, 0), (0, 0)))
    row = lambda v: v.reshape(1, C_CONV)
    return pl.pallas_call(
        functools.partial(_conv_kernel, tt=tt),
        grid=(bsz, t // tt),
        in_specs=[pl.BlockSpec((1, tt, 2 * C_CONV), lambda i, j: (i, j, 0)),
                  pl.BlockSpec((1, CONV_HALO, C_CONV), lambda i, j: (i, 0, 0)),
                  pl.BlockSpec((CONV_WIDTH, C_CONV), lambda i, j: (0, 0)),
                  pl.BlockSpec((1, C_CONV), lambda i, j: (0, 0)),
                  pl.BlockSpec((1, C_CONV), lambda i, j: (0, 0)),
                  pl.BlockSpec((1, C_CONV), lambda i, j: (0, 0))],
        out_specs=[pl.BlockSpec((1, tt, C_CONV), lambda i, j: (i, j, 0)),
                   pl.BlockSpec((1, CONV_WIDTH - 1, C_CONV), lambda i, j: (i, 0, 0))],
        out_shape=[jax.ShapeDtypeStruct((bsz, t, C_CONV), F32),
                   jax.ShapeDtypeStruct((bsz, CONV_WIDTH - 1, C_CONV), F32)],
        scratch_shapes=[pltpu.VMEM((CONV_HALO + tt, C_CONV), F32),
                        pltpu.VMEM((max(tt // 128, 1), min(tt, 128) + 8 * ((CONV_WIDTH - 1) // 8), 128), F32)],
        compiler_params=_cp("parallel", "arbitrary"),
        name="conformer_conv",
    )(proj, st, w, row(b), row(ln_g), row(ln_b))


HG_LEVELS = (32, 16, 8, 4, 2, 1)


def _hgrn_tables():
    c = HG_CHUNK
    idx = np.arange(c)
    masks = []
    for h in HG_LEVELS:
        blk = idx // (2 * h)
        upper = (idx % (2 * h)) >= h
        masks.append((blk[:, None] == blk[None, :]) & upper[:, None] & (~upper)[None, :])
    masks.append(np.eye(c, dtype=bool))
    return (idx[None, :] <= idx[:, None]).astype(np.float32), np.stack(masks).astype(np.float32)


HG_HEADS_PER_STEP = 8


def _hgrn_kernel(q_ref, fz_ref, v_ref, g_ref, lb_ref, ng_ref, s0_ref, tab_ref, msk_ref,
                 o_ref, sn_ref, st_ref, *, t_valid, nh):
    ci = pl.program_id(2)
    c = HG_CHUNK

    @pl.when(ci == 0)
    def _():
        for hh in range(nh):
            st_ref[hh] = s0_ref[0, hh].T

    rowc = lax.broadcasted_iota(jnp.int32, (c, 1), 0)
    sub8 = lax.broadcasted_iota(jnp.int32, (8, 1), 0)
    live = ci * c + rowc < t_valid
    tab = tab_ref[...]
    for hh in range(nh):
        cs = slice(hh * HG_DK, (hh + 1) * HG_DK)
        lb = lb_ref[:, cs]
        f = lb + (1.0 - lb) * jax.nn.sigmoid(fz_ref[0, :, cs])
        lf = jnp.where(live, jnp.log(f), 0.0)
        k = jnp.where(live, 1.0 - f, 0.0)
        q = q_ref[0, :, cs]
        v = v_ref[0, :, cs]

        hi = lf.astype(BF)
        b2 = _dot(tab, jnp.concatenate([hi, (lf - hi.astype(F32)).astype(BF)], axis=1))
        b = b2[:, 0:HG_DK] + b2[:, HG_DK:]

        def pivot_rows(h):
            if 2 * h >= 8:
                return jnp.concatenate(
                    [jnp.broadcast_to(b[blk * 2 * h + h - 1:blk * 2 * h + h, :], (2 * h, HG_DK))
                     for blk in range(c // (2 * h))], axis=0)
            groups = []
            for g8 in range(c // 8):
                piece = None
                for kb in range(8 // (2 * h)):
                    r = 8 * g8 + kb * 2 * h + h - 1
                    cand = jnp.broadcast_to(b[r:r + 1, :], (8, HG_DK))
                    piece = cand if piece is None else jnp.where(sub8 >= kb * 2 * h, cand, piece)
                groups.append(piece)
            return jnp.concatenate(groups, axis=0)

        attn = jnp.zeros((c, c), F32)
        for li, h in enumerate(HG_LEVELS):
            piv = pivot_rows(h)
            fac = jnp.exp(jnp.where((rowc % (2 * h)) >= h, b - piv, piv - b))
            attn = attn + msk_ref[li] * _dot_nt((q * fac).astype(BF), (k * fac).astype(BF))
        attn = attn + msk_ref[len(HG_LEVELS)] * _dot_nt(q.astype(BF), k.astype(BF))

        st = st_ref[hh]
        qb = (q * jnp.exp(b)).astype(BF)
        o = _dot(attn.astype(BF), v.astype(BF)) + _dot_nt(qb, st.astype(BF))
        ke = (k * jnp.exp(b[c - 1:c] - b)).astype(BF)
        decay = jnp.exp(b[c - 1:c])
        st_ref[hh] = st * decay + _dot(v.T.astype(BF), ke)

        gate = g_ref[0, :, cs]
        o_ref[0, :, cs] = _rms(o, ng_ref[...]) * (gate * jax.nn.sigmoid(gate))

    @pl.when(ci == pl.num_programs(2) - 1)
    def _():
        for hh in range(nh):
            sn_ref[0, hh] = st_ref[hh].T


def hgrn2(z, s0, lb, norm_g, t_valid):
    bsz, tp, _ = z.shape
    c = HG_CHUNK
    nc = tp // c
    tab, msk = _hgrn_tables()
    nh = HG_HEADS_PER_STEP
    ng = HG_HEADS // nh
    w = nh * HG_DK
    blk = lambda off: pl.BlockSpec((1, c, w), lambda b, h, i: (b, i, off + h))
    return pl.pallas_call(
        functools.partial(_hgrn_kernel, t_valid=t_valid, nh=nh),
        grid=(bsz, ng, nc),
        in_specs=[blk(0), blk(ng), blk(2 * ng), blk(3 * ng),
                  pl.BlockSpec((1, w), lambda b, h, i: (0, h)),
                  pl.BlockSpec((1, HG_DV), lambda b, h, i: (0, 0)),
                  pl.BlockSpec((1, nh, HG_DK, HG_DV), lambda b, h, i: (b, h, 0, 0)),
                  pl.BlockSpec(tab.shape, lambda b, h, i: (0, 0)),
                  pl.BlockSpec(msk.shape, lambda b, h, i: (0, 0, 0))],
        out_specs=[pl.BlockSpec((1, c, w), lambda b, h, i: (b, i, h)),
                   pl.BlockSpec((1, nh, HG_DK, HG_DV), lambda b, h, i: (b, h, 0, 0))],
        out_shape=[jax.ShapeDtypeStruct((bsz, tp, D_MODEL), F32),
                   jax.ShapeDtypeStruct((bsz, HG_HEADS, HG_DK, HG_DV), F32)],
        scratch_shapes=[pltpu.VMEM((nh, HG_DV, HG_DK), F32)],
        compiler_params=_cp("parallel", "parallel", "arbitrary"),
        name="hgrn2",
    )(z, z, z, z, lb.reshape(1, D_MODEL), norm_g.reshape(1, HG_DV), s0,
      jnp.asarray(tab, BF), jnp.asarray(msk))


def _pair_norm(seg, g):
    lo = lax.broadcasted_iota(jnp.int32, seg.shape, 1) < DH
    sq = seg * seg
    s0 = jnp.sum(jnp.where(lo, sq, 0.0), axis=-1, keepdims=True)
    s1 = jnp.sum(jnp.where(lo, 0.0, sq), axis=-1, keepdims=True)
    inv = jnp.where(lo, lax.rsqrt(s0 * (1.0 / DH) + RMS_EPS), lax.rsqrt(s1 * (1.0 / DH) + RMS_EPS))
    return seg * inv * g


def _kv_post_kernel(p_ref, kg_ref, kv_ref, win_ref):
    x = p_ref[...]
    kv_ref[:, 0:2 * KVW] = x[:, 0:2 * KVW]
    kv_ref[:, 2 * KVW:3 * KVW] = _pair_norm(x[:, 2 * KVW:3 * KVW], kg_ref[1:2, :])
    kv_ref[:, 3 * KVW:4 * KVW] = x[:, 3 * KVW:4 * KVW]
    win_ref[:, 0:KVW] = _pair_norm(x[:, 4 * KVW:5 * KVW], kg_ref[2:3, :])
    win_ref[:, KVW:2 * KVW] = x[:, 5 * KVW:6 * KVW]


def kv_post(proj, k_g):
    m = proj.shape[0]
    tm = min(m, 1024)
    kvcols = 6 * KVW
    assert O_KV % kvcols == 0
    return pl.pallas_call(
        _kv_post_kernel,
        grid=(m // tm,),
        in_specs=[pl.BlockSpec((tm, kvcols), lambda i: (i, O_KV // kvcols)),
                  pl.BlockSpec((3, KVW), lambda i: (0, 0))],
        out_specs=[pl.BlockSpec((tm, 4 * KVW), lambda i: (i, 0)),
                   pl.BlockSpec((tm, 2 * KVW), lambda i: (i, 0))],
        out_shape=[jax.ShapeDtypeStruct((m, 4 * KVW), F32), jax.ShapeDtypeStruct((m, 2 * KVW), F32)],
        compiler_params=_cp("parallel"),
        name="kv_post",
    )(proj, jnp.concatenate([k_g, k_g], axis=1))


def _q_norm_kernel(q_ref, g_ref, o_ref):
    for c in range(NH // 2):
        cs = slice(c * 2 * DH, (c + 1) * 2 * DH)
        o_ref[:, cs] = _pair_norm(q_ref[:, cs], g_ref[...]) * (DH ** -0.5)


def q_norm(proj, q_g):
    m = proj.shape[0]
    tm = min(m, 1024)
    return pl.pallas_call(
        _q_norm_kernel,
        grid=(m // tm,),
        in_specs=[pl.BlockSpec((tm, NH * DH), lambda i: (i, O_Q // (NH * DH))),
                  pl.BlockSpec((1, 2 * DH), lambda i: (0, 0))],
        out_specs=pl.BlockSpec((tm, NH * DH), lambda i: (i, 0)),
        out_shape=jax.ShapeDtypeStruct((m, NH * DH), F32),
        compiler_params=_cp("parallel"),
        name="q_norm",
    )(proj, jnp.concatenate([q_g, q_g]).reshape(1, 2 * DH))


def _compress_kernel(*refs, n_prefetch, n_src, rows, nch):
    refs = refs[n_prefetch:]
    srcs = (refs[:n_src], refs[n_src:2 * n_src])
    pef_ref, pes_ref, w1f_ref, w1s_ref, b1_ref, w2_ref, kg_ref, kc_ref, vc_ref, hf_ref, hs_ref = refs[2 * n_src:]
    j = pl.program_id(1)
    cpr = rows // CMP_STRIDE
    m = cpr * n_src

    @pl.when(j == 0)
    def _():
        hs_ref[:, nch:nch + 8, :] = jnp.zeros((2, 8, 2 * CMP_HIDDEN), F32)

    for kind in range(2):
        accf = jnp.zeros((m, 2 * CMP_HIDDEN), F32)
        accs = jnp.zeros((m, 2 * CMP_HIDDEN), F32)
        for s in range(CMP_STRIDE):
            xs = jnp.concatenate(
                [r[0, pl.ds(s, cpr, stride=CMP_STRIDE), :] for r in srcs[kind]], axis=0)
            accf = accf + _dot((xs + pef_ref[kind, s:s + 1, :]).astype(BF), w1f_ref[kind, s])
            accs = accs + _dot((xs + pes_ref[kind, s:s + 1, :]).astype(BF), w1s_ref[kind, s])
        row0 = pl.multiple_of(j * m, 8)
        hf_ref[kind, pl.ds(row0, m), :] = accf
        hs_ref[kind, pl.ds(row0, m), :] = accs

    @pl.when(j == pl.num_programs(1) - 1)
    def _():
        for kind in range(2):
            hid = jax.nn.gelu(hf_ref[kind, 0:nch, :] + hs_ref[kind, 1:nch + 1, :] + b1_ref[kind])
            out = _dot(hid.astype(BF), w2_ref[kind])
            if kind == 0:
                kc_ref[0] = _pair_norm(out, kg_ref[...]).astype(BF)
            else:
                vc_ref[0] = out.astype(BF)


def _compress_weights(pe, w1, b1, w2, kg0):
    eye = jnp.eye(HKV, dtype=F32)
    bd = lambda w: jnp.einsum('ab,ksdf->ksadbf', eye, w).reshape(2, CMP_STRIDE, KVW, 2 * CMP_HIDDEN).astype(BF)
    tile2 = lambda a: jnp.concatenate([a, a], axis=-1)
    w2bd = jnp.einsum('ab,kfd->kafbd', eye, w2).reshape(2, 2 * CMP_HIDDEN, KVW).astype(BF)
    return (tile2(pe[:, :CMP_STRIDE]), tile2(pe[:, CMP_STRIDE:]), bd(w1[:, :CMP_STRIDE]), bd(w1[:, CMP_STRIDE:]),
            tile2(b1).reshape(2, 1, 2 * CMP_HIDDEN), w2bd, tile2(kg0).reshape(1, KVW))


def _compress_common(n_prefetch, n_src, rows, nch, bsz, weights):
    wspecs = [pl.BlockSpec(w.shape, functools.partial(lambda nd, *a: (0,) * nd, w.ndim)) for w in weights]
    out_specs = [pl.BlockSpec((1, nch, KVW), lambda b, j, *a: (b, 0, 0))] * 2
    out_shape = [jax.ShapeDtypeStruct((bsz, nch, KVW), BF)] * 2
    scratch = [pltpu.VMEM((2, nch + 8, 2 * CMP_HIDDEN), F32)] * 2
    kern = functools.partial(_compress_kernel, n_prefetch=n_prefetch, n_src=n_src, rows=rows, nch=nch)
    return kern, wspecs, out_specs, out_shape, scratch


def compress_dense(proj3, weights):
    bsz, t, _ = proj3.shape
    rows = min(t, 2048)
    nch = t // CMP_STRIDE
    assert O_KV % KVW == 0
    kern, wspecs, out_specs, out_shape, scratch = _compress_common(0, 1, rows, nch, bsz, weights)
    src = lambda kind: pl.BlockSpec((1, rows, KVW), lambda b, j: (b, j, O_KV // KVW + kind))
    return pl.pallas_call(
        kern, grid=(bsz, t // rows),
        in_specs=[src(0), src(1)] + wspecs,
        out_specs=out_specs, out_shape=out_shape, scratch_shapes=scratch,
        compiler_params=_cp("parallel", "arbitrary"), name="compress_dense",
    )(proj3, proj3, *weights)


CMP_PAGES = 16
CMP_PE_ROWS = 16


def _compress_paged_kernel(*refs, npg, page, nch):
    pages = refs[1:1 + npg]
    pe2_ref, w1p_ref, b1_ref, w2_ref, kg_ref, kc_ref, vc_ref, hf_ref, hs_ref, xs_ref = refs[1 + npg:]
    j = pl.program_id(1)
    m = npg * page // CMP_STRIDE
    hid2 = 2 * CMP_HIDDEN

    @pl.when(j == 0)
    def _():
        hs_ref[:, nch:nch + 8, :] = jnp.zeros((2, 8, hid2), F32)

    for kind in range(2):
        for r in range(npg):
            xt = jnp.concatenate([pages[r][0, kind, h] for h in range(HKV)], axis=0)
            xs_ref[kind, r * page:(r + 1) * page, :] = xt.T
        acc = jnp.zeros((m + CMP_PE_ROWS, 2 * hid2), F32)
        for s2 in range(CMP_STRIDE // 2):
            xa = xs_ref[kind, pl.ds(2 * s2, m, stride=CMP_STRIDE), :]
            xb = xs_ref[kind, pl.ds(2 * s2 + 1, m, stride=CMP_STRIDE), :]
            lhs = jnp.concatenate([jnp.concatenate([xa, xb], axis=1).astype(BF), pe2_ref[kind, s2]], axis=0)
            acc = acc + _dot(lhs, w1p_ref[kind, s2])
        row0 = pl.multiple_of(j * m, 8)
        hf_ref[kind, pl.ds(row0, m), :] = acc[0:m, 0:hid2] + acc[m:m + 1, 0:hid2]
        hs_ref[kind, pl.ds(row0, m), :] = acc[0:m, hid2:] + acc[m + 1:m + 2, hid2:]

    @pl.when(j == pl.num_programs(1) - 1)
    def _():
        for kind in range(2):
            hid = jax.nn.gelu(hf_ref[kind, 0:nch, :] + hs_ref[kind, 1:nch + 1, :] + b1_ref[kind])
            out = _dot(hid.astype(BF), w2_ref[kind])
            if kind == 0:
                kc_ref[0] = _pair_norm(out, kg_ref[...]).astype(BF)
            else:
                vc_ref[0] = out.astype(BF)


def compress_paged(cachet, page_table, weights):
    bsz, n_pages = page_table.shape
    page = cachet.shape[-1]
    npg = min(CMP_PAGES, n_pages)
    nch = n_pages * page // CMP_STRIDE
    src_specs = [pl.BlockSpec((1, 2, HKV, DH, page),
                              functools.partial(lambda r, b, j, pt: (pt[b, j * npg + r], 0, 0, 0, 0), r))
                 for r in range(npg)]
    pef, pes, w1f, w1s, b1, w2, kg = weights
    np2 = CMP_STRIDE // 2
    pair_rows = lambda a: a.reshape(2, np2, 2 * KVW)
    pe2 = jnp.zeros((2, np2, CMP_PE_ROWS, 2 * KVW), F32).at[:, :, 0].set(pair_rows(pef)).at[:, :, 1].set(
        pair_rows(pes)).astype(BF)
    pair_w = lambda w: w.reshape(2, np2, 2 * KVW, 2 * CMP_HIDDEN)
    w1p = jnp.concatenate([pair_w(w1f), pair_w(w1s)], axis=-1)
    weights = (pe2, w1p, b1, w2, kg)
    _, wspecs, out_specs, out_shape, scratch = _compress_common(1, npg, page, nch, bsz, weights)
    return pl.pallas_call(
        functools.partial(_compress_paged_kernel, npg=npg, page=page, nch=nch),
        grid_spec=pltpu.PrefetchScalarGridSpec(
            num_scalar_prefetch=1, grid=(bsz, n_pages // npg), in_specs=src_specs + wspecs,
            out_specs=out_specs, scratch_shapes=scratch + [pltpu.VMEM((2, npg * page, KVW), F32)]),
        out_shape=out_shape, compiler_params=_cp("parallel", "arbitrary"), name="compress_paged",
    )(page_table, *([cachet] * npg), *weights)


NS_PAD = 256
SEL_HALF = 128
LOG2E = math.log2(math.e)
PEN = 30000.0
KT = 256
SEL_SPLIT = 2
VROWS = 80
HALF_TILES = SEL_HALF * SEL_BLOCK // KT
CBAND = 24


def _t5_bucket(dist):
    n = jnp.maximum(dist, 0)
    large = REL_EXACT + (jnp.log(jnp.maximum(n, 1).astype(F32) / REL_EXACT)
                         / math.log(REL_MAX_DIST / REL_EXACT) * (REL_BUCKETS - REL_EXACT)).astype(jnp.int32)
    return jnp.where(n < REL_EXACT, n, jnp.minimum(large, REL_BUCKETS - 1))


def _selection_matrix(nc, ns_pad, nc_pad):
    ratio, span = SEL_BLOCK // CMP_STRIDE, CMP_LEN // CMP_STRIDE
    a = np.zeros((ns_pad, nc_pad), np.float32)
    for j in range(ns_pad):
        for mm in range(ratio):
            for nn in range(span):
                n = ratio * j + mm - nn
                if 0 <= n < nc:
                    a[j, n] += 1.0
    return a


def _split3(x):
    hi = x.astype(BF)
    r1 = x - hi.astype(F32)
    mid = r1.astype(BF)
    lo = (r1 - mid.astype(F32)).astype(BF)
    return hi, mid, lo


def _top_blocks(s, qp, n_rounds):
    j = lax.broadcasted_iota(jnp.int32, s.shape, 0)
    cur = qp // SEL_BLOCK
    forced = (j == 0) | (j == cur) | (j == cur - 1)
    valid = j * SEL_BLOCK <= qp
    s = jnp.where(forced, BIG, s)
    s = jnp.where(valid, s, -BIG)
    for _ in range(n_rounds):
        mx = jnp.max(s, axis=0, keepdims=True)
        jm = jnp.min(jnp.where(s == mx, j, 2 * NS_PAD), axis=0, keepdims=True)
        s = jnp.where(j == jm, -3e38, s)
    return jnp.where(s < -2 * BIG, 1.0, 0.0)


def _masked_softmax_rows(s, axis):
    m = jnp.max(s, axis=axis, keepdims=True)
    e = jnp.exp(s - m)
    den = jnp.maximum(jnp.sum(e, axis=axis, keepdims=True), 1e-30)
    return e * jnp.where(m > 0.5 * NEG, 1.0 / den, 0.0)


def _nsa_prompt_kernel(q_ref, gate_ref, qg_ref, kc_ref, vct_ref, amat_ref, kaug_ref, vselt_ref, kwin_ref,
                       vwint_ref, bc_ref, bs_ref, bw_ref, o_ref, lg_ref, acc_ref, qa_ref, qh_ref, sa_ref, sb_ref,
                       m_ref, mx_ref, *, ncp, n_top):
    i = pl.program_id(0)
    qb = Q_BLOCK
    rows = GRP * qb
    start = i * qb
    qall = q_ref[...]
    lane = lax.broadcasted_iota(jnp.int32, (qb, 2 * DH), 1)
    qpos_row = start + lax.broadcasted_iota(jnp.int32, (1, qb), 1)
    t_last = start // KT
    n_far = jnp.maximum(t_last - 1, 0)

    @pl.when(i == 0)
    def _():
        lg_ref[:, 0:16, :] = jnp.zeros((HKV, 16, rows), F32)

    o_cs = []
    for h in range(HKV):
        parts = []
        for g in range(GRP):
            hd = h * GRP + g
            slab = qall[:, (hd // 2) * 2 * DH:(hd // 2 + 1) * 2 * DH]
            mine = (lane >= DH) if hd % 2 else (lane < DH)
            ss = jnp.sum(jnp.where(mine, slab * slab, 0.0), axis=-1, keepdims=True)
            xn = jnp.where(mine, slab * lax.rsqrt(ss * (1.0 / DH) + RMS_EPS) * qg_ref[...] * (DH ** -0.5), 0.0)
            if hd % 2 != h:
                xn = pltpu.roll(xn, DH, 1)
            parts.append(xn)
        xf = jnp.concatenate(parts, axis=0)
        qh = xf.astype(BF)

        def compressed(n, h=h, qh=qh):
            lg_ref[h, 16:16 + n, :] = _dot_nt(kc_ref[0:n, :], qh)
            band0 = pl.multiple_of(i * 8, 8)
            lg_ref[h, pl.ds(band0, CBAND), :] = lg_ref[h, pl.ds(band0, CBAND), :] + bc_ref[h]
            tok = lax.broadcasted_iota(jnp.int32, (n, 1), 0)
            p_c = _masked_softmax_rows(jnp.where(tok < 8 * i + 8, lg_ref[h, 16:16 + n, :], NEG), 0)
            o_c = _dot(vct_ref[h, :, 0:n], p_c.astype(BF))
            imp = p_c[:, 0:qb]
            for g in range(1, GRP):
                imp = imp + p_c[:, g * qb:(g + 1) * qb]
            sc3 = _dot(amat_ref[:, 0:n], jnp.concatenate(_split3(imp), axis=1))
            return o_c, sc3[:, 0:qb] + sc3[:, qb:2 * qb] + sc3[:, 2 * qb:]

        cch = min(ncp, 2 * Q_BLOCK)
        nbr = ncp // cch
        o_c, score = lax.switch(jnp.minimum((8 * i + 7) // cch, nbr - 1),
                                [functools.partial(compressed, (k + 1) * cch) for k in range(nbr)])
        sel = _top_blocks(score, qpos_row, n_top)
        pen = ((sel.T - 1.0) * PEN).astype(BF)
        qh2 = (xf * LOG2E).astype(BF)
        qa_ref[h, 0] = jnp.concatenate([jnp.concatenate([pen[:, 0:SEL_HALF]] * GRP, axis=0), qh2], axis=1)
        qa_ref[h, 1] = jnp.concatenate([jnp.concatenate([pen[:, SEL_HALF:]] * GRP, axis=0), qh2], axis=1)
        qh_ref[h] = qh
        o_cs.append(o_c)

    hr = rows // SEL_SPLIT
    chains = [(h, slice(sp * hr, (sp + 1) * hr)) for h in range(HKV) for sp in range(SEL_SPLIT)]
    nchain = len(chains)

    def scores(c, t):
        h, rs = chains[c]
        qa = jnp.where(t < HALF_TILES, qa_ref[h, 0, rs, :], qa_ref[h, 1, rs, :])
        return _dot_nt(kaug_ref[t + 1], qa)

    def absorb(c, s, mx, t, m):
        h, rs = chains[c]
        m2 = jnp.maximum(m, mx)
        p = jnp.exp2(s - m2)
        acc_ref[h, :, rs] = jnp.exp2(m - m2) * acc_ref[h, :, rs] + _dot(vselt_ref[h, t + 1], p.astype(BF))
        return m2

    col_max = lambda s: jnp.max(s, axis=0, keepdims=True)

    def pair_body(k, carry):
        ms, mxa = list(carry[:nchain]), list(carry[nchain:])
        mxb = []
        for c, (h, rs) in enumerate(chains):
            s1 = scores(c, 2 * k + 1)
            sb_ref[h, :, rs] = s1
            mxb.append(col_max(s1))
            ms[c] = absorb(c, sa_ref[h, :, rs], mxa[c], 2 * k, ms[c])
        for c, (h, rs) in enumerate(chains):
            s2 = scores(c, 2 * k + 2)
            ms[c] = absorb(c, sb_ref[h, :, rs], mxb[c], 2 * k + 1, ms[c])
            sa_ref[h, :, rs] = s2
            mxa[c] = col_max(s2)
        return tuple(ms) + tuple(mxa)

    acc_ref[...] = jnp.zeros((HKV, VROWS, rows), F32)
    mxa = []
    for c, (h, rs) in enumerate(chains):
        s0 = scores(c, 0)
        sa_ref[h, :, rs] = s0
        mxa.append(col_max(s0))
    carry = tuple(jnp.full((1, hr), NEG, F32) for _ in range(nchain)) + tuple(mxa)
    carry = lax.fori_loop(0, n_far // 2, pair_body, carry)
    for c, (h, rs) in enumerate(chains):
        m_ref[h, :, rs] = carry[c]
        mx_ref[h, :, rs] = carry[nchain + c]

    @pl.when(n_far % 2 == 1)
    def _():
        for c, (h, rs) in enumerate(chains):
            m_ref[h, :, rs] = absorb(c, sa_ref[h, :, rs], mx_ref[h, :, rs], n_far - 1, m_ref[h, :, rs])

    kiota = lax.broadcasted_iota(jnp.int32, (KT, 1), 0)
    ms = [m_ref[h, :, rs] for h, rs in chains]
    for u in range(2):
        t = t_last - 1 + u
        for c, (h, rs) in enumerate(chains):
            s = jnp.where(t * KT + kiota >= 0, scores(c, t) + bs_ref[i % 2, h, u, :, rs], NEG)
            ms[c] = absorb(c, s, col_max(s), t, ms[c])

    gates_t = jax.nn.sigmoid(gate_ref[...]).T
    wkeys = WINDOW + qb
    for h in range(HKV):
        o_s = acc_ref[h, 0:DH, :] / jnp.maximum(acc_ref[h, DH:DH + 1, :], 1e-30)

        s = _dot_nt(kwin_ref[pl.ds(pl.multiple_of(start, qb), wkeys), :], qh_ref[h]) + bw_ref[h]
        kpos = start - WINDOW + lax.broadcasted_iota(jnp.int32, (wkeys, 1), 0)
        s = jnp.where(kpos >= 0, s, NEG)
        e = jnp.exp(s - jnp.max(s, axis=0, keepdims=True))
        vw = jnp.concatenate([vwint_ref[h, i + u] for u in range(wkeys // qb)], axis=1)
        ow = _dot(vw, e.astype(BF))
        o_w = ow[0:DH] / jnp.maximum(ow[DH:DH + 1], 1e-30)

        gate = lambda br: jnp.concatenate(
            [gates_t[(h * GRP + g) * 3 + br:(h * GRP + g) * 3 + br + 1, :] for g in range(GRP)], axis=1)
        o_t = gate(0) * o_cs[h] + gate(1) * o_s + gate(2) * o_w
        for g in range(GRP):
            o_ref[:, (h * GRP + g) * DH:(h * GRP + g + 1) * DH] = o_t[:, g * qb:(g + 1) * qb].T


def _bias_tables(rel_table):
    tab = rel_table.astype(F32)
    far = tab[REL_BUCKETS - 1]
    qb = Q_BLOCK

    def toeplitz(off, nk, dmax, shift):
        d = off - (nk - 1) + jnp.arange(nk + qb - 1, dtype=jnp.int32)
        w = jnp.where(((d >= 0) & (d <= dmax))[None, :], (tab[_t5_bucket(d)] - (far if shift else 0.0)).T, NEG)
        p = w.shape[1]
        hank = jnp.tile(w, (1, nk + 1))[:, :nk * (p + 1)].reshape(NH, nk, p + 1)[:, :, :qb]
        return hank[:, ::-1, :]

    def lanes_gq(b):
        return b.reshape(HKV, GRP, b.shape[1], qb).transpose(0, 2, 1, 3).reshape(HKV, b.shape[1], GRP * qb)

    big = 1 << 30
    bc = lanes_gq(toeplitz(16 * CMP_STRIDE - (CMP_LEN - 1), CBAND * CMP_STRIDE, big, True)[:, ::CMP_STRIDE, :])
    bs = jnp.stack([lanes_gq(toeplitz(off, 2 * KT, big, True)).reshape(HKV, 2, KT, GRP * qb)
                    for off in (KT, KT + qb)]) * LOG2E
    bw = lanes_gq(toeplitz(WINDOW, WINDOW + qb, WINDOW, False))
    return bc, bs, bw


def nsa_prompt(proj, kc, vc, ksel, vsel, kwin, vwin, q_g, rel_table):
    t = proj.shape[0]
    nb = t // Q_BLOCK
    nch = kc.shape[0]
    ncp = -(-nch // 128) * 128
    assert t // SEL_BLOCK <= NS_PAD and t % KT == 0
    n_top = min(TOP_N, t // SEL_BLOCK)
    kcp = jnp.pad(kc, ((0, ncp - nch), (0, 0)))
    vct = jnp.pad(vc, ((0, ncp - nch), (0, 0))).reshape(ncp, HKV, DH).transpose(1, 2, 0)
    amat = jnp.asarray(_selection_matrix(nch - 1, NS_PAD, ncp), BF)

    def values_t(v, pad_rows, tile):
        vt = jnp.pad(v.astype(BF), ((pad_rows, 0), (0, 0))).reshape(-1, tile, HKV, DH).transpose(2, 0, 3, 1)
        return jnp.concatenate([vt, jnp.ones(vt.shape[:2] + (1, tile), BF),
                                jnp.zeros(vt.shape[:2] + (VROWS - DH - 1, tile), BF)], axis=2)

    onehot = ((jnp.arange(t, dtype=jnp.int32)[:, None] // SEL_BLOCK) % SEL_HALF
              == jnp.arange(SEL_HALF, dtype=jnp.int32)[None, :]).astype(BF)
    kaug = jnp.pad(jnp.concatenate([onehot, ksel.astype(BF)], axis=1), ((KT, 0), (0, 0))).reshape(-1, KT, 2 * KVW)
    vselt = values_t(vsel, KT, KT)
    vwint = values_t(vwin, WINDOW, Q_BLOCK)
    rows = GRP * Q_BLOCK
    kwinp = jnp.pad(kwin.astype(BF), ((WINDOW, 0), (0, 0)))
    bc, bs, bw = _bias_tables(rel_table)
    whole = pl.BlockSpec(memory_space=pltpu.VMEM)
    return pl.pallas_call(
        functools.partial(_nsa_prompt_kernel, ncp=ncp, n_top=n_top),
        grid=(nb,),
        in_specs=[pl.BlockSpec((Q_BLOCK, NH * DH), lambda i: (i, O_Q // (NH * DH))),
                  pl.BlockSpec((Q_BLOCK, 128), lambda i: (i, O_GATE // 128)),
                  whole, whole, whole, whole, whole, whole, whole, whole, whole, whole, whole],
        out_specs=pl.BlockSpec((Q_BLOCK, NH * DH), lambda i: (i, 0)),
        out_shape=jax.ShapeDtypeStruct((t, NH * DH), F32),
        scratch_shapes=[pltpu.VMEM((HKV, 16 + ncp, rows), F32), pltpu.VMEM((HKV, VROWS, rows), F32),
                        pltpu.VMEM((HKV, 2, rows, SEL_HALF + KVW), BF), pltpu.VMEM((HKV, rows, KVW), BF),
                        pltpu.VMEM((HKV, KT, rows), F32), pltpu.VMEM((HKV, KT, rows), F32),
                        pltpu.VMEM((HKV, 1, rows), F32), pltpu.VMEM((HKV, 1, rows), F32)],
        compiler_params=_cp("arbitrary"),
        name="nsa_prompt",
    )(proj, proj, jnp.concatenate([q_g, q_g]).reshape(1, 2 * DH), kcp, vct, amat, kaug, vselt, kwinp, vwint,
      bc, bs, bw)


def _pick_head(x, lane_h):
    return jnp.where(lane_h == 0, x[0:DH], x[DH:2 * DH])


def _nsa_dec_pre_kernel(qbd_ref, kc_ref, vc_ref, amat_ref, bc_ref, win_ref, wnew_ref, bwa_ref, bwb_ref,
                        pen_ref, oc_ref, ow_ref, *, ts, pos0, n_top):
    r_all = GRP * HKV * ts
    qbd = qbd_ref[0]
    lane_h = (lax.broadcasted_iota(jnp.int32, (1, r_all), 1) // ts) % HKV

    p_c = _masked_softmax_rows(_dot(kc_ref[0], qbd) + bc_ref[...], 0)
    oc_ref[0] = _pick_head(_dot_tn(vc_ref[0], p_c.astype(BF)), lane_h)

    hi, mid, lo = _split3(p_c)
    amat = amat_ref[...]
    sc = _dot(amat, hi) + _dot(amat, mid) + _dot(amat, lo)
    w8 = HKV * ts
    score = sc[:, 0:w8]
    for g in range(1, GRP):
        score = score + sc[:, g * w8:(g + 1) * w8]
    qp = pos0 + lax.broadcasted_iota(jnp.int32, (1, w8), 1) % ts
    sel = _top_blocks(score, qp, n_top)
    pen_ref[0] = (jnp.concatenate([sel] * GRP, axis=1) - 1.0) * PEN

    win = win_ref[0]
    wnew = wnew_ref[0]
    s_a = _dot(win[:, 0:KVW].astype(BF), qbd) + bwa_ref[...]
    s_b = _dot(wnew[:, 0:KVW].astype(BF), qbd) + bwb_ref[...]
    m = jnp.maximum(jnp.max(s_a, axis=0, keepdims=True), jnp.max(s_b, axis=0, keepdims=True))
    e_a = jnp.where(s_a > 0.5 * NEG, jnp.exp(s_a - m), 0.0)
    e_b = jnp.where(s_b > 0.5 * NEG, jnp.exp(s_b - m), 0.0)
    den = jnp.maximum(jnp.sum(e_a, axis=0, keepdims=True) + jnp.sum(e_b, axis=0, keepdims=True), 1e-30)
    o_w = (_dot_tn(win[:, KVW:].astype(BF), e_a.astype(BF)) + _dot_tn(wnew[:, KVW:].astype(BF), e_b.astype(BF)))
    ow_ref[0] = _pick_head(o_w, lane_h) / den


SEL_PAGES = 32


def _nsa_dec_sel_kernel(*refs, npg, page, ts):
    pages = refs[1:1 + npg]
    (q_ref, pen_ref, expt_ref, blast_ref, knew_ref, vnew_ref, bnew_ref, gate_ref, oc_ref, ow_ref,
     o_ref, m_ref, l_ref, acc_ref) = refs[1 + npg:]
    j = pl.program_id(1)
    last = pl.num_programs(1) - 1
    rows = GRP * ts

    @pl.when(j == 0)
    def _():
        m_ref[...] = jnp.full((HKV * rows, 1), NEG, F32)
        l_ref[...] = jnp.zeros((HKV * rows, 1), F32)
        acc_ref[...] = jnp.zeros((HKV * rows, DH), F32)

    def online(s, pvs):
        m = m_ref[...]
        m2 = jnp.maximum(m, jnp.max(s, axis=-1, keepdims=True))
        a = jnp.exp(m - m2)
        p = jnp.exp(s - m2)
        m_ref[...] = m2
        l_ref[...] = a * l_ref[...] + jnp.sum(p, axis=-1, keepdims=True)
        pb = p.astype(BF)
        pv = jnp.concatenate([pvs[h](pb[h * rows:(h + 1) * rows]) for h in range(HKV)], axis=0)
        acc_ref[...] = a * acc_ref[...] + pv

    both = lambda f: jnp.concatenate([f(h) for h in range(HKV)], axis=0)
    vts = [jnp.concatenate([r[0, 1, h] for r in pages], axis=1).astype(BF) for h in range(HKV)]
    s = both(lambda h: _dot(q_ref[0, h], jnp.concatenate([r[0, 0, h] for r in pages], axis=1).astype(BF))
             + _dot(pen_ref[0, h, j], expt_ref[...]))
    s = s + jnp.where(j == last, blast_ref[...], 0.0)
    online(s, [functools.partial(lambda vt, p: _dot_nt(p, vt), vts[h]) for h in range(HKV)])

    @pl.when(j == last)
    def _():
        s_new = both(lambda h: _dot_nt(q_ref[0, h], knew_ref[0, h].astype(BF))) + bnew_ref[...]
        online(s_new, [functools.partial(lambda v, p: _dot(p, v), vnew_ref[0, h].astype(BF)) for h in range(HKV)])
        o_s = acc_ref[...] / jnp.maximum(l_ref[...], 1e-30)
        g = jax.nn.sigmoid(gate_ref[0])
        o_ref[0] = g[:, 0:1] * oc_ref[0] + g[:, 1:2] * o_s + g[:, 2:3] * ow_ref[0]


def _dec_bias_tables(rel_table, pos0, ts, nch, wb, page, npg):
    tab = rel_table.astype(F32)
    far = tab[REL_BUCKETS - 1]
    r = np.arange(GRP * HKV * ts)
    head = ((r // ts) % HKV) * GRP + r // (HKV * ts)
    qpos = pos0 + r % ts
    far_r = far[head][None, :]

    def bias(kpos, dmax, shift):
        dist = qpos[None, :] - np.asarray(kpos)[:, None]
        ok = (dist >= 0) & (dist <= dmax)
        out = jnp.where(ok, 0.0 if shift else far_r, NEG)
        near = np.nonzero((ok & (dist < REL_MAX_DIST)).any(axis=1))[0]
        if near.size:
            lo, hi = int(near.min()), int(near.max()) + 1
            b = tab[_t5_bucket(jnp.asarray(dist[lo:hi], jnp.int32)), head[None, :]] - (far_r if shift else 0.0)
            out = jnp.concatenate([out[:lo], jnp.where(ok[lo:hi], b, NEG), out[hi:]], axis=0)
        return out

    big = 1 << 30
    bc = bias(np.arange(nch) * CMP_STRIDE + CMP_LEN - 1, big, False)
    bwa = bias(pos0 - wb + np.arange(wb), WINDOW, False)
    tnew = np.arange(8)
    newpos = np.where(tnew < ts, pos0 + tnew, pos0 + 2 * WINDOW + SEL_BLOCK)
    bwb = bias(newpos, WINDOW, False)
    bnew = bias(newpos, big, True)
    step = npg * page
    blast = jnp.concatenate([jnp.zeros((step - page, r.shape[0]), F32), bias(pos0 - page + np.arange(page), big, True)],
                            axis=0)
    return bc, bwa, bwb, bnew, blast


def nsa_decode(qn, gate_raw, kc, vc, cachet, page_table, win_state, kv_new, win_new, rel_table):
    bsz, ts, _ = qn.shape
    n_pages = page_table.shape[1]
    page = cachet.shape[-1]
    pos0 = n_pages * page
    nch = kc.shape[1]
    wb = win_state.shape[1]
    r_all = GRP * HKV * ts
    assert pos0 % SEL_BLOCK == 0 and ts <= 8 and ts <= SEL_BLOCK and nch % 8 == 0
    ns = -(-(pos0 + ts) // SEL_BLOCK)
    npg = min(SEL_PAGES, n_pages)
    assert n_pages % npg == 0
    bps = npg * page // SEL_BLOCK
    ns_pad = max(-(-ns // 8) * 8, (n_pages // npg) * bps)
    n_top = min(TOP_N, ns)

    q5 = qn.reshape(bsz, ts, HKV, GRP, DH)
    qbd = jnp.einsum('bqhgd,hk->bkdghq', q5, jnp.eye(HKV, dtype=F32)).reshape(bsz, 2 * DH, r_all).astype(BF)
    gate_t = gate_raw.reshape(bsz, ts, HKV, GRP, 3).transpose(0, 4, 3, 2, 1).reshape(bsz, 3, r_all)
    pad8 = lambda a: jnp.pad(a, ((0, 0), (0, 8 - ts), (0, 0)))
    amat = jnp.asarray(_selection_matrix(nch - 1, ns_pad, nch), BF)
    expand = jnp.asarray(np.repeat(np.eye(bps, dtype=np.float32), SEL_BLOCK, axis=0), BF)
    bc, bwa, bwb, bnew, blast = _dec_bias_tables(rel_table, pos0, ts, nch, wb, page, npg)

    full = lambda a: pl.BlockSpec(a.shape, functools.partial(lambda nd, *_: (0,) * nd, a.ndim))
    per_b = lambda a: pl.BlockSpec((1,) + a.shape[1:], functools.partial(lambda nd, b, *_: (b,) + (0,) * nd, a.ndim - 1))
    wnew = pad8(win_new)
    pre_in = [qbd, kc, vc, amat, bc, win_state, wnew, bwa, bwb]
    pre_specs = [per_b(qbd), per_b(kc), per_b(vc), full(amat), full(bc), per_b(win_state), per_b(wnew),
                 full(bwa), full(bwb)]
    small = jax.ShapeDtypeStruct((bsz, DH, r_all), F32)
    pen, o_c, o_w = pl.pallas_call(
        functools.partial(_nsa_dec_pre_kernel, ts=ts, pos0=pos0, n_top=n_top),
        grid=(bsz,), in_specs=pre_specs,
        out_specs=[pl.BlockSpec((1, ns_pad, r_all), lambda b: (b, 0, 0)),
                   pl.BlockSpec((1, DH, r_all), lambda b: (b, 0, 0)),
                   pl.BlockSpec((1, DH, r_all), lambda b: (b, 0, 0))],
        out_shape=[jax.ShapeDtypeStruct((bsz, ns_pad, r_all), F32), small, small],
        compiler_params=_cp("parallel"), name="nsa_dec_pre",
    )(*pre_in)

    rows = GRP * ts
    nsteps = n_pages // npg
    by_head = lambda a: a.reshape(a.shape[:-1] + (GRP, HKV, ts))
    q_h = q5.transpose(0, 2, 3, 1, 4).reshape(bsz, HKV, rows, DH).astype(BF)
    gate_h = gate_raw.reshape(bsz, ts, HKV, GRP, 3).transpose(0, 2, 3, 1, 4).reshape(bsz, HKV, rows, 3)
    pen_h = by_head(pen[:, :nsteps * bps].reshape(bsz, nsteps, bps, r_all)).transpose(0, 4, 1, 3, 5, 2).reshape(
        bsz, HKV, nsteps, rows, bps).astype(BF)
    oc_h, ow_h = (by_head(a).transpose(0, 3, 2, 4, 1).reshape(bsz, HKV, rows, DH) for a in (o_c, o_w))
    blast_h, bnew_h = (by_head(a).transpose(2, 1, 3, 0).reshape(HKV, rows, a.shape[0]) for a in (blast, bnew))
    new5 = pad8(kv_new).reshape(bsz, 8, 4, HKV, DH)
    knew = new5[:, :, 2].transpose(0, 2, 1, 3)
    vnew = new5[:, :, 3].transpose(0, 2, 1, 3)
    page_specs = [pl.BlockSpec((1, 2, HKV, DH, page),
                               functools.partial(lambda r, b, j, pt: (pt[b, j * npg + r], 1, 0, 0, 0), r))
                  for r in range(npg)]
    heads_rows = lambda a: a.reshape(a.shape[:-3] + (HKV * rows, a.shape[-1]))
    blast_h, bnew_h, gate_h, oc_h, ow_h = map(heads_rows, (blast_h, bnew_h, gate_h, oc_h, ow_h))
    sel_in = [q_h, pen_h, expand.T, blast_h, knew, vnew, bnew_h, gate_h, oc_h, ow_h]
    sel_specs = [per_b(q_h), per_b(pen_h), full(expand.T), full(blast_h), per_b(knew), per_b(vnew), full(bnew_h),
                 per_b(gate_h), per_b(oc_h), per_b(ow_h)]
    o_h = pl.pallas_call(
        functools.partial(_nsa_dec_sel_kernel, npg=npg, page=page, ts=ts),
        grid_spec=pltpu.PrefetchScalarGridSpec(
            num_scalar_prefetch=1, grid=(bsz, nsteps), in_specs=page_specs + sel_specs,
            out_specs=pl.BlockSpec((1, HKV * rows, DH), lambda b, j, pt: (b, 0, 0)),
            scratch_shapes=[pltpu.VMEM((HKV * rows, 1), F32), pltpu.VMEM((HKV * rows, 1), F32),
                            pltpu.VMEM((HKV * rows, DH), F32)]),
        out_shape=jax.ShapeDtypeStruct((bsz, HKV * rows, DH), F32),
        compiler_params=_cp("parallel", "arbitrary"), name="nsa_dec_sel",
    )(page_table, *([cachet] * npg), *sel_in)
    return o_h.reshape(bsz, HKV, GRP, ts, DH).transpose(0, 3, 1, 2, 4).reshape(bsz, ts, NH * DH)


def _even_weights(p, e):
    w_in = jnp.pad(p['att_w_in'][e], ((0, 0), (0, ATT_IN_PAD - ATT_IN_COLS))).astype(BF)
    w_out = p['att_w_out'][e].astype(BF)
    return dict(
        w_in=w_in, w_out_conv=w_out[:C_CONV], w_out_att=w_out[C_CONV:],
        cmp=_compress_weights(p['cmp_pe'][e], p['cmp_w1'][e], p['cmp_b1'][e], p['cmp_w2'][e], p['k_norm_g'][e][0]),
        k_g=p['k_norm_g'][e], q_g=p['q_norm_g'][e], conv_w=p['conv_w'][e], conv_b=p['conv_b'][e],
        ln_g=p['conv_ln_g'][e], ln_b=p['conv_ln_b'][e])


def _even_prompt(x2, norm_g, w, rel_table):
    t = x2.shape[0]
    proj = norm_matmul(x2, norm_g, w['w_in'])
    conv_y, conv_new = conformer_conv(proj[None], jnp.zeros((1, CONV_WIDTH - 1, C_CONV), F32),
                                      w['conv_w'], w['conv_b'], w['ln_g'], w['ln_b'])
    kv_new, win_new = kv_post(proj, w['k_g'])
    kc, vc = compress_dense(proj[None], w['cmp'])
    o = nsa_prompt(proj, kc[0], vc[0], kv_new[:, 2 * KVW:3 * KVW], kv_new[:, 3 * KVW:], win_new[:, :KVW],
                   win_new[:, KVW:], w['q_g'], rel_table)
    x2 = out_proj2(x2, conv_y[0], o, w['w_out_conv'], w['w_out_att'])
    keep = min(WINDOW, t)
    return (x2, kv_new.reshape(1, t, 4, HKV, DH), win_new[t - keep:].reshape(1, keep, 2, HKV, DH), conv_new)


def _even_decode(x3, norm_g, w, rel_table, cachet, page_table, win_state, conv_state):
    bsz, t, d = x3.shape
    x2 = x3.reshape(bsz * t, d)
    proj = norm_matmul(x2, norm_g, w['w_in'])
    proj3 = proj.reshape(bsz, t, ATT_IN_PAD)
    conv_y, conv_new = conformer_conv(proj3, conv_state, w['conv_w'], w['conv_b'], w['ln_g'], w['ln_b'])
    kv_new, win_new = kv_post(proj, w['k_g'])
    qn = q_norm(proj, w['q_g'])
    kc, vc = compress_paged(cachet, page_table, w['cmp'])
    wb = win_state.shape[1]
    o = nsa_decode(qn.reshape(bsz, t, NH * DH), proj3[:, :, O_GATE:ATT_IN_COLS], kc, vc, cachet, page_table,
                   win_state.reshape(bsz, wb, 2 * KVW), kv_new.reshape(bsz, t, 4 * KVW),
                   win_new.reshape(bsz, t, 2 * KVW), rel_table)
    x2 = out_proj2(x2, conv_y.reshape(bsz * t, C_CONV), o.reshape(bsz * t, NH * DH), w['w_out_conv'],
                   w['w_out_att'])
    win_all = jnp.concatenate([win_state, win_new.reshape(bsz, t, 2, HKV, DH)], axis=1)
    keep = min(WINDOW, wb + t)
    return (x2.reshape(bsz, t, d), kv_new.reshape(bsz, t, 4, HKV, DH), win_all[:, wb + t - keep:], conv_new)


def _odd_layer(x3, s0, norm_g, w_in_bf, w_out_bf, lb, hg_norm_g):
    bsz, t, d = x3.shape
    x2 = x3.reshape(bsz * t, d)
    z = norm_matmul(x2, norm_g, w_in_bf).reshape(bsz, t, 4 * d)
    tp = -(-t // HG_CHUNK) * HG_CHUNK
    if tp != t:
        z = jnp.pad(z, ((0, 0), (0, tp - t), (0, 0)))
    o, s_new = hgrn2(z, s0, lb, hg_norm_g, t)
    x2 = out_proj1(x2, o[:, :t].reshape(bsz * t, d), w_out_bf)
    return x2.reshape(bsz, t, d), s_new


def kernel(x_prompt, x_sample, cache_nsa_kv, page_table, state_nsa_win, state_conv, state_hgrn, rel_bias_table,
           norm_mix_g, norm_mlp_g, w_mlp_up, w_mlp_down, att_w_in, att_w_out, q_norm_g, k_norm_g, cmp_pe, cmp_w1,
           cmp_b1, cmp_w2, conv_w, conv_b, conv_ln_g, conv_ln_b, hg_w_in, hg_w_out, hg_lb_logits, hg_norm_g):
    p = dict(att_w_in=att_w_in, att_w_out=att_w_out, q_norm_g=q_norm_g, k_norm_g=k_norm_g, cmp_pe=cmp_pe,
             cmp_w1=cmp_w1, cmp_b1=cmp_b1, cmp_w2=cmp_w2, conv_w=conv_w, conv_b=conv_b, conv_ln_g=conv_ln_g,
             conv_ln_b=conv_ln_b)
    bp, tp_, d = x_prompt.shape
    assert bp == 1
    db, ts, _ = x_sample.shape
    w_up = w_mlp_up.astype(BF)
    w_down = w_mlp_down.astype(BF)
    cum = jnp.cumsum(jax.nn.softmax(hg_lb_logits.astype(F32), axis=0), axis=0)

    xp = x_prompt[0]
    kv_p, win_p, conv_p, hg_p = [], [], [], []
    for layer in range(norm_mix_g.shape[0]):
        if layer % 2 == 0:
            e = layer // 2
            w = _even_weights(p, e)
            xp, kv_new, win_new, conv_new = _even_prompt(xp, norm_mix_g[layer], w, rel_bias_table)
            kv_p.append(kv_new)
            win_p.append(win_new)
            conv_p.append(conv_new)
        else:
            o = layer // 2
            x3, s_new = _odd_layer(xp[None], jnp.zeros((1, HG_HEADS, HG_DK, HG_DV), F32), norm_mix_g[layer],
                                   hg_w_in[o].astype(BF), hg_w_out[o].astype(BF), cum[layer] - cum[0],
                                   hg_norm_g[o])
            xp = x3[0]
            hg_p.append(s_new)
        xp = mlp(xp, norm_mlp_g[layer], w_up[layer], w_down[layer])

    cachet = cache_nsa_kv.transpose(0, 1, 3, 4, 5, 2)
    xs = x_sample
    kv_s, win_s, conv_s, hg_s = [], [], [], []
    for layer in range(norm_mix_g.shape[0]):
        if layer % 2 == 0:
            e = layer // 2
            w = _even_weights(p, e)
            xs, kv_new, win_new, conv_new = _even_decode(xs, norm_mix_g[layer], w, rel_bias_table, cachet[e],
                                                         page_table, state_nsa_win[e], state_conv[e])
            kv_s.append(kv_new)
            win_s.append(win_new)
            conv_s.append(conv_new)
        else:
            o = layer // 2
            xs, s_new = _odd_layer(xs, state_hgrn[o], norm_mix_g[layer], hg_w_in[o].astype(BF),
                                   hg_w_out[o].astype(BF), cum[layer] - cum[0], hg_norm_g[o])
            hg_s.append(s_new)
        xs = mlp(xs.reshape(db * ts, d), norm_mlp_g[layer], w_up[layer], w_down[layer]).reshape(db, ts, d)
    return (xp[None], xs, jnp.stack(kv_p), jnp.stack(kv_s), jnp.stack(win_p), jnp.stack(win_s),
            jnp.stack(conv_p), jnp.stack(conv_s), jnp.stack(hg_p), jnp.stack(hg_s))
```

```python
import functools
import math

import jax
import jax.numpy as jnp
import numpy as np
from jax import lax
from jax.experimental import pallas as pl
from jax.experimental.pallas import tpu as pltpu

D_MODEL = 1024
C_CONV = 512
CONV_WIDTH = 31
DH = 64
HKV = 2
GRP = 4
NH = HKV * GRP
CMP_STRIDE = 16
CMP_LEN = 32
CMP_HIDDEN = 128
SEL_BLOCK = 64
TOP_N = 16
WINDOW = 512
Q_BLOCK = 128
REL_BUCKETS = 32
REL_EXACT = 16
REL_MAX_DIST = 128
HG_DK = 128
HG_HEADS = 8
HG_DV = 128
HG_CHUNK = 64
RMS_EPS = 1e-6
NEG = -1e30
BIG = 1e9
O_Q = 2 * C_CONV
O_KV = O_Q + NH * DH
O_GATE = O_KV + 3 * 2 * HKV * DH
ATT_IN_COLS = O_GATE + 3 * NH
ATT_IN_PAD = 2432
KVW = HKV * DH

VMEM_LIMIT = 56 * 1024 * 1024
BF = jnp.bfloat16
F32 = jnp.float32


def _cp(*sem):
    return pltpu.CompilerParams(dimension_semantics=sem, vmem_limit_bytes=VMEM_LIMIT)


def _dot(a, b):
    return jnp.dot(a, b, preferred_element_type=F32)


def _dot_nt(a, b):
    return lax.dot_general(a, b, (((1,), (1,)), ((), ())), preferred_element_type=F32)


def _dot_tn(a, b):
    return lax.dot_general(a, b, (((0,), (0,)), ((), ())), preferred_element_type=F32)


def _rms(x, g):
    return x * lax.rsqrt(jnp.mean(x * x, axis=-1, keepdims=True) + RMS_EPS) * g


def _norm_matmul_kernel(x_ref, g_ref, w_ref, o_ref):
    hn = _rms(x_ref[...], g_ref[...]).astype(BF)
    o_ref[...] = _dot(hn, w_ref[...])


def norm_matmul(x, g, w_bf, tn=None):
    m, d = x.shape
    n = w_bf.shape[1]
    tm = min(m, 512)
    tn = n if tn is None else tn
    return pl.pallas_call(
        _norm_matmul_kernel,
        grid=(m // tm, n // tn),
        in_specs=[pl.BlockSpec((tm, d), lambda i, j: (i, 0)),
                  pl.BlockSpec((1, d), lambda i, j: (0, 0)),
                  pl.BlockSpec((d, tn), lambda i, j: (0, j))],
        out_specs=pl.BlockSpec((tm, tn), lambda i, j: (i, j)),
        out_shape=jax.ShapeDtypeStruct((m, n), F32),
        compiler_params=_cp("parallel", "arbitrary"),
        name="norm_matmul",
    )(x, g.reshape(1, d), w_bf)


def _out_proj_kernel(r_ref, a1_ref, a2_ref, w1_ref, w2_ref, o_ref):
    o_ref[...] = (r_ref[...] + _dot(a1_ref[...].astype(BF), w1_ref[...])
                  + _dot(a2_ref[...].astype(BF), w2_ref[...]))


def out_proj2(res, a1, a2, w1_bf, w2_bf):
    m, d = res.shape
    k1, k2 = a1.shape[1], a2.shape[1]
    tm = min(m, 512)
    return pl.pallas_call(
        _out_proj_kernel,
        grid=(m // tm,),
        in_specs=[pl.BlockSpec((tm, d), lambda i: (i, 0)),
                  pl.BlockSpec((tm, k1), lambda i: (i, 0)),
                  pl.BlockSpec((tm, k2), lambda i: (i, 0)),
                  pl.BlockSpec((k1, d), lambda i: (0, 0)),
                  pl.BlockSpec((k2, d), lambda i: (0, 0))],
        out_specs=pl.BlockSpec((tm, d), lambda i: (i, 0)),
        out_shape=jax.ShapeDtypeStruct((m, d), F32),
        compiler_params=_cp("parallel"),
        name="out_proj2",
    )(res, a1, a2, w1_bf, w2_bf)


def _out_proj1_kernel(r_ref, a_ref, w_ref, o_ref):
    o_ref[...] = r_ref[...] + _dot(a_ref[...].astype(BF), w_ref[...])


def out_proj1(res, a, w_bf):
    m, d = res.shape
    k = a.shape[1]
    tm = min(m, 512)
    return pl.pallas_call(
        _out_proj1_kernel,
        grid=(m // tm,),
        in_specs=[pl.BlockSpec((tm, d), lambda i: (i, 0)),
                  pl.BlockSpec((tm, k), lambda i: (i, 0)),
                  pl.BlockSpec((k, d), lambda i: (0, 0))],
        out_specs=pl.BlockSpec((tm, d), lambda i: (i, 0)),
        out_shape=jax.ShapeDtypeStruct((m, d), F32),
        compiler_params=_cp("parallel"),
        name="out_proj1",
    )(res, a, w_bf)


def _mlp_kernel(x_ref, g_ref, wu_ref, wd_ref, o_ref, hn_ref, acc_ref):
    j = pl.program_id(1)

    @pl.when(j == 0)
    def _():
        hn_ref[...] = _rms(x_ref[...], g_ref[...]).astype(BF)
        acc_ref[...] = x_ref[...]

    hid = jnp.maximum(_dot(hn_ref[...], wu_ref[...]), 0.0)
    acc_ref[...] += _dot((hid * hid).astype(BF), wd_ref[...])

    @pl.when(j == pl.num_programs(1) - 1)
    def _():
        o_ref[...] = acc_ref[...]


def mlp(x, g, wu_bf, wd_bf):
    m, d = x.shape
    hdim = wu_bf.shape[1]
    tm = min(m, 1024)
    th = 1024
    return pl.pallas_call(
        _mlp_kernel,
        grid=(m // tm, hdim // th),
        in_specs=[pl.BlockSpec((tm, d), lambda i, j: (i, 0)),
                  pl.BlockSpec((1, d), lambda i, j: (0, 0)),
                  pl.BlockSpec((d, th), lambda i, j: (0, j)),
                  pl.BlockSpec((th, d), lambda i, j: (j, 0))],
        out_specs=pl.BlockSpec((tm, d), lambda i, j: (i, 0)),
        out_shape=jax.ShapeDtypeStruct((m, d), F32),
        scratch_shapes=[pltpu.VMEM((tm, d), BF), pltpu.VMEM((tm, d), F32)],
        compiler_params=_cp("parallel", "arbitrary"),
        name="mlp",
    )(x, g.reshape(1, d), wu_bf, wd_bf)


CONV_HALO = 32
CONV_PAD = CONV_HALO - (CONV_WIDTH - 1)


def _conv_kernel(u_ref, st_ref, w_ref, b_ref, lg_ref, lb_ref, y_ref, new_ref, xin_ref, ph_ref, *, tt):
    t = pl.program_id(1)

    @pl.when(t == 0)
    def _():
        xin_ref[0:CONV_HALO, :] = st_ref[0]

    a = u_ref[0, :, 0:C_CONV]
    gt = u_ref[0, :, C_CONV:2 * C_CONV]
    xin_ref[CONV_HALO:CONV_HALO + tt, :] = a * jax.nn.sigmoid(gt)

    rb = min(tt, 128)
    for c in range(C_CONV // 128):
        cs = slice(c * 128, (c + 1) * 128)
        starts = list(range(0, tt, rb))
        accs = [jnp.zeros((rb, 128), F32) + b_ref[:, cs] for _ in starts]
        for ph in range(min(8, CONV_WIDTH)):
            taps = range(ph, CONV_WIDTH, 8)
            n = rb + 8 * (len(taps) - 1)
            if rb % 8 == 0:
                for bi, r0 in enumerate(starts):
                    ph_ref[bi, 0:n, :] = xin_ref[CONV_PAD + ph + r0:CONV_PAD + ph + r0 + n, cs]
            for a, k in enumerate(taps):
                wk = w_ref[k:k + 1, cs]
                for bi, r0 in enumerate(starts):
                    if rb % 8 == 0:
                        accs[bi] = accs[bi] + wk * ph_ref[bi, 8 * a:8 * a + rb, :]
                    else:
                        accs[bi] = accs[bi] + wk * xin_ref[CONV_PAD + k + r0:CONV_PAD + k + r0 + rb, cs]
        for bi, r0 in enumerate(starts):
            y_ref[0, r0:r0 + rb, cs] = accs[bi]
    y = y_ref[0]
    mu = jnp.mean(y, axis=-1, keepdims=True)
    yc = y - mu
    var = jnp.mean(yc * yc, axis=-1, keepdims=True)
    z = yc * lax.rsqrt(var + RMS_EPS) * lg_ref[...] + lb_ref[...]
    y_ref[0] = z * jax.nn.sigmoid(z)

    @pl.when(t == pl.num_programs(1) - 1)
    def _():
        new_ref[0] = xin_ref[tt + CONV_PAD:tt + CONV_HALO, :]

    if tt >= CONV_HALO:
        @pl.when(t < pl.num_programs(1) - 1)
        def _():
            xin_ref[0:CONV_HALO, :] = xin_ref[tt:tt + CONV_HALO, :]


def conformer_conv(proj, state, w, b, ln_g, ln_b):
    bsz, t, _ = proj.shape
    tt = min(t, 256)
    assert t % tt == 0 and (t == tt or tt >= CONV_HALO)
    st = jnp.pad(state, ((0, 0), (CONV_PAD, 0), (0, 0)))
    row = lambda v: v.reshape(1, C_CONV)
    return pl.pallas_call(
        functools.partial(_conv_kernel, tt=tt),
        grid=(bsz, t // tt),
        in_specs=[pl.BlockSpec((1, tt, 2 * C_CONV), lambda i, j: (i, j, 0)),
                  pl.BlockSpec((1, CONV_HALO, C_CONV), lambda i, j: (i, 0, 0)),
                  pl.BlockSpec((CONV_WIDTH, C_CONV), lambda i, j: (0, 0)),
                  pl.BlockSpec((1, C_CONV), lambda i, j: (0, 0)),
                  pl.BlockSpec((1, C_CONV), lambda i, j: (0, 0)),
                  pl.BlockSpec((1, C_CONV), lambda i, j: (0, 0))],
        out_specs=[pl.BlockSpec((1, tt, C_CONV), lambda i, j: (i, j, 0)),
                   pl.BlockSpec((1, CONV_WIDTH - 1, C_CONV), lambda i, j: (i, 0, 0))],
        out_shape=[jax.ShapeDtypeStruct((bsz, t, C_CONV), F32),
                   jax.ShapeDtypeStruct((bsz, CONV_WIDTH - 1, C_CONV), F32)],
        scratch_shapes=[pltpu.VMEM((CONV_HALO + tt, C_CONV), F32),
                        pltpu.VMEM((max(tt // 128, 1), min(tt, 128) + 8 * ((CONV_WIDTH - 1) // 8), 128), F32)],
        compiler_params=_cp("parallel", "arbitrary"),
        name="conformer_conv",
    )(proj, st, w, row(b), row(ln_g), row(ln_b))


HG_LEVELS = (32, 16, 8, 4, 2, 1)


def _hgrn_tables():
    c = HG_CHUNK
    idx = np.arange(c)
    masks = []
    for h in HG_LEVELS:
        blk = idx // (2 * h)
        upper = (idx % (2 * h)) >= h
        masks.append((blk[:, None] == blk[None, :]) & upper[:, None] & (~upper)[None, :])
    masks.append(np.eye(c, dtype=bool))
    return (idx[None, :] <= idx[:, None]).astype(np.float32), np.stack(masks).astype(np.float32)


HG_HEADS_PER_STEP = 8


def _hgrn_kernel(q_ref, fz_ref, v_ref, g_ref, lb_ref, ng_ref, s0_ref, tab_ref, msk_ref,
                 o_ref, sn_ref, st_ref, *, t_valid, nh):
    ci = pl.program_id(2)
    c = HG_CHUNK

    @pl.when(ci == 0)
    def _():
        for hh in range(nh):
            st_ref[hh] = s0_ref[0, hh].T

    rowc = lax.broadcasted_iota(jnp.int32, (c, 1), 0)
    sub8 = lax.broadcasted_iota(jnp.int32, (8, 1), 0)
    live = ci * c + rowc < t_valid
    tab = tab_ref[...]
    for hh in range(nh):
        cs = slice(hh * HG_DK, (hh + 1) * HG_DK)
        lb = lb_ref[:, cs]
        f = lb + (1.0 - lb) * jax.nn.sigmoid(fz_ref[0, :, cs])
        lf = jnp.where(live, jnp.log(f), 0.0)
        k = jnp.where(live, 1.0 - f, 0.0)
        q = q_ref[0, :, cs]
        v = v_ref[0, :, cs]

        hi = lf.astype(BF)
        b2 = _dot(tab, jnp.concatenate([hi, (lf - hi.astype(F32)).astype(BF)], axis=1))
        b = b2[:, 0:HG_DK] + b2[:, HG_DK:]

        def pivot_rows(h):
            if 2 * h >= 8:
                return jnp.concatenate(
                    [jnp.broadcast_to(b[blk * 2 * h + h - 1:blk * 2 * h + h, :], (2 * h, HG_DK))
                     for blk in range(c // (2 * h))], axis=0)
            groups = []
            for g8 in range(c // 8):
                piece = None
                for kb in range(8 // (2 * h)):
                    r = 8 * g8 + kb * 2 * h + h - 1
                    cand = jnp.broadcast_to(b[r:r + 1, :], (8, HG_DK))
                    piece = cand if piece is None else jnp.where(sub8 >= kb * 2 * h, cand, piece)
                groups.append(piece)
            return jnp.concatenate(groups, axis=0)

        attn = jnp.zeros((c, c), F32)
        for li, h in enumerate(HG_LEVELS):
            piv = pivot_rows(h)
            fac = jnp.exp(jnp.where((rowc % (2 * h)) >= h, b - piv, piv - b))
            attn = attn + msk_ref[li] * _dot_nt((q * fac).astype(BF), (k * fac).astype(BF))
        attn = attn + msk_ref[len(HG_LEVELS)] * _dot_nt(q.astype(BF), k.astype(BF))

        st = st_ref[hh]
        qb = (q * jnp.exp(b)).astype(BF)
        o = _dot(attn.astype(BF), v.astype(BF)) + _dot_nt(qb, st.astype(BF))
        ke = (k * jnp.exp(b[c - 1:c] - b)).astype(BF)
        decay = jnp.exp(b[c - 1:c])
        st_ref[hh] = st * decay + _dot(v.T.astype(BF), ke)

        gate = g_ref[0, :, cs]
        o_ref[0, :, cs] = _rms(o, ng_ref[...]) * (gate * jax.nn.sigmoid(gate))

    @pl.when(ci == pl.num_programs(2) - 1)
    def _():
        for hh in range(nh):
            sn_ref[0, hh] = st_ref[hh].T


def hgrn2(z, s0, lb, norm_g, t_valid):
    bsz, tp, _ = z.shape
    c = HG_CHUNK
    nc = tp // c
    tab, msk = _hgrn_tables()
    nh = HG_HEADS_PER_STEP
    ng = HG_HEADS // nh
    w = nh * HG_DK
    blk = lambda off: pl.BlockSpec((1, c, w), lambda b, h, i: (b, i, off + h))
    return pl.pallas_call(
        functools.partial(_hgrn_kernel, t_valid=t_valid, nh=nh),
        grid=(bsz, ng, nc),
        in_specs=[blk(0), blk(ng), blk(2 * ng), blk(3 * ng),
                  pl.BlockSpec((1, w), lambda b, h, i: (0, h)),
                  pl.BlockSpec((1, HG_DV), lambda b, h, i: (0, 0)),
                  pl.BlockSpec((1, nh, HG_DK, HG_DV), lambda b, h, i: (b, h, 0, 0)),
                  pl.BlockSpec(tab.shape, lambda b, h, i: (0, 0)),
                  pl.BlockSpec(msk.shape, lambda b, h, i: (0, 0, 0))],
        out_specs=[pl.BlockSpec((1, c, w), lambda b, h, i: (b, i, h)),
                   pl.BlockSpec((1, nh, HG_DK, HG_DV), lambda b, h, i: (b, h, 0, 0))],
        out_shape=[jax.ShapeDtypeStruct((bsz, tp, D_MODEL), F32),
                   jax.ShapeDtypeStruct((bsz, HG_HEADS, HG_DK, HG_DV), F32)],
        scratch_shapes=[pltpu.VMEM((nh, HG_DV, HG_DK), F32)],
        compiler_params=_cp("parallel", "parallel", "arbitrary"),
        name="hgrn2",
    )(z, z, z, z, lb.reshape(1, D_MODEL), norm_g.reshape(1, HG_DV), s0,
      jnp.asarray(tab, BF), jnp.asarray(msk))


def _pair_norm(seg, g):
    lo = lax.broadcasted_iota(jnp.int32, seg.shape, 1) < DH
    sq = seg * seg
    s0 = jnp.sum(jnp.where(lo, sq, 0.0), axis=-1, keepdims=True)
    s1 = jnp.sum(jnp.where(lo, 0.0, sq), axis=-1, keepdims=True)
    inv = jnp.where(lo, lax.rsqrt(s0 * (1.0 / DH) + RMS_EPS), lax.rsqrt(s1 * (1.0 / DH) + RMS_EPS))
    return seg * inv * g


def _kv_post_kernel(p_ref, kg_ref, kv_ref, win_ref):
    x = p_ref[...]
    kv_ref[:, 0:2 * KVW] = x[:, 0:2 * KVW]
    kv_ref[:, 2 * KVW:3 * KVW] = _pair_norm(x[:, 2 * KVW:3 * KVW], kg_ref[1:2, :])
    kv_ref[:, 3 * KVW:4 * KVW] = x[:, 3 * KVW:4 * KVW]
    win_ref[:, 0:KVW] = _pair_norm(x[:, 4 * KVW:5 * KVW], kg_ref[2:3, :])
    win_ref[:, KVW:2 * KVW] = x[:, 5 * KVW:6 * KVW]


def kv_post(proj, k_g):
    m = proj.shape[0]
    tm = min(m, 1024)
    kvcols = 6 * KVW
    assert O_KV % kvcols == 0
    return pl.pallas_call(
        _kv_post_kernel,
        grid=(m // tm,),
        in_specs=[pl.BlockSpec((tm, kvcols), lambda i: (i, O_KV // kvcols)),
                  pl.BlockSpec((3, KVW), lambda i: (0, 0))],
        out_specs=[pl.BlockSpec((tm, 4 * KVW), lambda i: (i, 0)),
                   pl.BlockSpec((tm, 2 * KVW), lambda i: (i, 0))],
        out_shape=[jax.ShapeDtypeStruct((m, 4 * KVW), F32), jax.ShapeDtypeStruct((m, 2 * KVW), F32)],
        compiler_params=_cp("parallel"),
        name="kv_post",
    )(proj, jnp.concatenate([k_g, k_g], axis=1))


def _q_norm_kernel(q_ref, g_ref, o_ref):
    for c in range(NH // 2):
        cs = slice(c * 2 * DH, (c + 1) * 2 * DH)
        o_ref[:, cs] = _pair_norm(q_ref[:, cs], g_ref[...]) * (DH ** -0.5)


def q_norm(proj, q_g):
    m = proj.shape[0]
    tm = min(m, 1024)
    return pl.pallas_call(
        _q_norm_kernel,
        grid=(m // tm,),
        in_specs=[pl.BlockSpec((tm, NH * DH), lambda i: (i, O_Q // (NH * DH))),
                  pl.BlockSpec((1, 2 * DH), lambda i: (0, 0))],
        out_specs=pl.BlockSpec((tm, NH * DH), lambda i: (i, 0)),
        out_shape=jax.ShapeDtypeStruct((m, NH * DH), F32),
        compiler_params=_cp("parallel"),
        name="q_norm",
    )(proj, jnp.concatenate([q_g, q_g]).reshape(1, 2 * DH))


def _compress_kernel(*refs, n_prefetch, n_src, rows, nch):
    refs = refs[n_prefetch:]
    srcs = (refs[:n_src], refs[n_src:2 * n_src])
    pef_ref, pes_ref, w1f_ref, w1s_ref, b1_ref, w2_ref, kg_ref, kc_ref, vc_ref, hf_ref, hs_ref = refs[2 * n_src:]
    j = pl.program_id(1)
    cpr = rows // CMP_STRIDE
    m = cpr * n_src

    @pl.when(j == 0)
    def _():
        hs_ref[:, nch:nch + 8, :] = jnp.zeros((2, 8, 2 * CMP_HIDDEN), F32)

    for kind in range(2):
        accf = jnp.zeros((m, 2 * CMP_HIDDEN), F32)
        accs = jnp.zeros((m, 2 * CMP_HIDDEN), F32)
        for s in range(CMP_STRIDE):
            xs = jnp.concatenate(
                [r[0, pl.ds(s, cpr, stride=CMP_STRIDE), :] for r in srcs[kind]], axis=0)
            accf = accf + _dot((xs + pef_ref[kind, s:s + 1, :]).astype(BF), w1f_ref[kind, s])
            accs = accs + _dot((xs + pes_ref[kind, s:s + 1, :]).astype(BF), w1s_ref[kind, s])
        row0 = pl.multiple_of(j * m, 8)
        hf_ref[kind, pl.ds(row0, m), :] = accf
        hs_ref[kind, pl.ds(row0, m), :] = accs

    @pl.when(j == pl.num_programs(1) - 1)
    def _():
        for kind in range(2):
            hid = jax.nn.gelu(hf_ref[kind, 0:nch, :] + hs_ref[kind, 1:nch + 1, :] + b1_ref[kind])
            out = _dot(hid.astype(BF), w2_ref[kind])
            if kind == 0:
                kc_ref[0] = _pair_norm(out, kg_ref[...]).astype(BF)
            else:
                vc_ref[0] = out.astype(BF)


def _compress_weights(pe, w1, b1, w2, kg0):
    eye = jnp.eye(HKV, dtype=F32)
    bd = lambda w: jnp.einsum('ab,ksdf->ksadbf', eye, w).reshape(2, CMP_STRIDE, KVW, 2 * CMP_HIDDEN).astype(BF)
    tile2 = lambda a: jnp.concatenate([a, a], axis=-1)
    w2bd = jnp.einsum('ab,kfd->kafbd', eye, w2).reshape(2, 2 * CMP_HIDDEN, KVW).astype(BF)
    return (tile2(pe[:, :CMP_STRIDE]), tile2(pe[:, CMP_STRIDE:]), bd(w1[:, :CMP_STRIDE]), bd(w1[:, CMP_STRIDE:]),
            tile2(b1).reshape(2, 1, 2 * CMP_HIDDEN), w2bd, tile2(kg0).reshape(1, KVW))


def _compress_common(n_prefetch, n_src, rows, nch, bsz, weights):
    wspecs = [pl.BlockSpec(w.shape, functools.partial(lambda nd, *a: (0,) * nd, w.ndim)) for w in weights]
    out_specs = [pl.BlockSpec((1, nch, KVW), lambda b, j, *a: (b, 0, 0))] * 2
    out_shape = [jax.ShapeDtypeStruct((bsz, nch, KVW), BF)] * 2
    scratch = [pltpu.VMEM((2, nch + 8, 2 * CMP_HIDDEN), F32)] * 2
    kern = functools.partial(_compress_kernel, n_prefetch=n_prefetch, n_src=n_src, rows=rows, nch=nch)
    return kern, wspecs, out_specs, out_shape, scratch


def compress_dense(proj3, weights):
    bsz, t, _ = proj3.shape
    rows = min(t, 2048)
    nch = t // CMP_STRIDE
    assert O_KV % KVW == 0
    kern, wspecs, out_specs, out_shape, scratch = _compress_common(0, 1, rows, nch, bsz, weights)
    src = lambda kind: pl.BlockSpec((1, rows, KVW), lambda b, j: (b, j, O_KV // KVW + kind))
    return pl.pallas_call(
        kern, grid=(bsz, t // rows),
        in_specs=[src(0), src(1)] + wspecs,
        out_specs=out_specs, out_shape=out_shape, scratch_shapes=scratch,
        compiler_params=_cp("parallel", "arbitrary"), name="compress_dense",
    )(proj3, proj3, *weights)


CMP_PAGES = 16
CMP_PE_ROWS = 16


def _compress_paged_kernel(*refs, npg, page, nch):
    pages = refs[1:1 + npg]
    pe2_ref, w1p_ref, b1_ref, w2_ref, kg_ref, kc_ref, vc_ref, hf_ref, hs_ref, xs_ref = refs[1 + npg:]
    j = pl.program_id(1)
    m = npg * page // CMP_STRIDE
    hid2 = 2 * CMP_HIDDEN

    @pl.when(j == 0)
    def _():
        hs_ref[:, nch:nch + 8, :] = jnp.zeros((2, 8, hid2), F32)

    for kind in range(2):
        for r in range(npg):
            xt = jnp.concatenate([pages[r][0, kind, h] for h in range(HKV)], axis=0)
            xs_ref[kind, r * page:(r + 1) * page, :] = xt.T
        acc = jnp.zeros((m + CMP_PE_ROWS, 2 * hid2), F32)
        for s2 in range(CMP_STRIDE // 2):
            xa = xs_ref[kind, pl.ds(2 * s2, m, stride=CMP_STRIDE), :]
            xb = xs_ref[kind, pl.ds(2 * s2 + 1, m, stride=CMP_STRIDE), :]
            lhs = jnp.concatenate([jnp.concatenate([xa, xb], axis=1).astype(BF), pe2_ref[kind, s2]], axis=0)
            acc = acc + _dot(lhs, w1p_ref[kind, s2])
        row0 = pl.multiple_of(j * m, 8)
        hf_ref[kind, pl.ds(row0, m), :] = acc[0:m, 0:hid2] + acc[m:m + 1, 0:hid2]
        hs_ref[kind, pl.ds(row0, m), :] = acc[0:m, hid2:] + acc[m + 1:m + 2, hid2:]

    @pl.when(j == pl.num_programs(1) - 1)
    def _():
        for kind in range(2):
            hid = jax.nn.gelu(hf_ref[kind, 0:nch, :] + hs_ref[kind, 1:nch + 1, :] + b1_ref[kind])
            out = _dot(hid.astype(BF), w2_ref[kind])
            if kind == 0:
                kc_ref[0] = _pair_norm(out, kg_ref[...]).astype(BF)
            else:
                vc_ref[0] = out.astype(BF)


def compress_paged(cachet, page_table, weights):
    bsz, n_pages = page_table.shape
    page = cachet.shape[-1]
    npg = min(CMP_PAGES, n_pages)
    nch = n_pages * page // CMP_STRIDE
    src_specs = [pl.BlockSpec((1, 2, HKV, DH, page),
                              functools.partial(lambda r, b, j, pt: (pt[b, j * npg + r], 0, 0, 0, 0), r))
                 for r in range(npg)]
    pef, pes, w1f, w1s, b1, w2, kg = weights
    np2 = CMP_STRIDE // 2
    pair_rows = lambda a: a.reshape(2, np2, 2 * KVW)
    pe2 = jnp.zeros((2, np2, CMP_PE_ROWS, 2 * KVW), F32).at[:, :, 0].set(pair_rows(pef)).at[:, :, 1].set(
        pair_rows(pes)).astype(BF)
    pair_w = lambda w: w.reshape(2, np2, 2 * KVW, 2 * CMP_HIDDEN)
    w1p = jnp.concatenate([pair_w(w1f), pair_w(w1s)], axis=-1)
    weights = (pe2, w1p, b1, w2, kg)
    _, wspecs, out_specs, out_shape, scratch = _compress_common(1, npg, page, nch, bsz, weights)
    return pl.pallas_call(
        functools.partial(_compress_paged_kernel, npg=npg, page=page, nch=nch),
        grid_spec=pltpu.PrefetchScalarGridSpec(
            num_scalar_prefetch=1, grid=(bsz, n_pages // npg), in_specs=src_specs + wspecs,
            out_specs=out_specs, scratch_shapes=scratch + [pltpu.VMEM((2, npg * page, KVW), F32)]),
        out_shape=out_shape, compiler_params=_cp("parallel", "arbitrary"), name="compress_paged",
    )(page_table, *([cachet] * npg), *weights)


NS_PAD = 256
SEL_HALF = 128
LOG2E = math.log2(math.e)
PEN = 30000.0
KT = 256
SEL_SPLIT = 2
VROWS = 80
HALF_TILES = SEL_HALF * SEL_BLOCK // KT
CBAND = 24


def _t5_bucket(dist):
    n = jnp.maximum(dist, 0)
    large = REL_EXACT + (jnp.log(jnp.maximum(n, 1).astype(F32) / REL_EXACT)
                         / math.log(REL_MAX_DIST / REL_EXACT) * (REL_BUCKETS - REL_EXACT)).astype(jnp.int32)
    return jnp.where(n < REL_EXACT, n, jnp.minimum(large, REL_BUCKETS - 1))


def _selection_matrix(nc, ns_pad, nc_pad):
    ratio, span = SEL_BLOCK // CMP_STRIDE, CMP_LEN // CMP_STRIDE
    a = np.zeros((ns_pad, nc_pad), np.float32)
    for j in range(ns_pad):
        for mm in range(ratio):
            for nn in range(span):
                n = ratio * j + mm - nn
                if 0 <= n < nc:
                    a[j, n] += 1.0
    return a


def _split3(x):
    hi = x.astype(BF)
    r1 = x - hi.astype(F32)
    mid = r1.astype(BF)
    lo = (r1 - mid.astype(F32)).astype(BF)
    return hi, mid, lo


def _top_blocks(s, qp, n_rounds):
    j = lax.broadcasted_iota(jnp.int32, s.shape, 0)
    cur = qp // SEL_BLOCK
    forced = (j == 0) | (j == cur) | (j == cur - 1)
    valid = j * SEL_BLOCK <= qp
    s = jnp.where(forced, BIG, s)
    s = jnp.where(valid, s, -BIG)
    for _ in range(n_rounds):
        mx = jnp.max(s, axis=0, keepdims=True)
        jm = jnp.min(jnp.where(s == mx, j, 2 * NS_PAD), axis=0, keepdims=True)
        s = jnp.where(j == jm, -3e38, s)
    return jnp.where(s < -2 * BIG, 1.0, 0.0)


def _masked_softmax_rows(s, axis):
    m = jnp.max(s, axis=axis, keepdims=True)
    e = jnp.exp(s - m)
    den = jnp.maximum(jnp.sum(e, axis=axis, keepdims=True), 1e-30)
    return e * jnp.where(m > 0.5 * NEG, 1.0 / den, 0.0)


def _nsa_prompt_kernel(q_ref, gate_ref, qg_ref, kc_ref, vct_ref, amat_ref, kaug_ref, vselt_ref, kwin_ref,
                       vwint_ref, bc_ref, bs_ref, bw_ref, o_ref, lg_ref, acc_ref, qa_ref, qh_ref, sa_ref, sb_ref,
                       m_ref, mx_ref, *, ncp, n_top):
    i = pl.program_id(0)
    qb = Q_BLOCK
    rows = GRP * qb
    start = i * qb
    qall = q_ref[...]
    lane = lax.broadcasted_iota(jnp.int32, (qb, 2 * DH), 1)
    qpos_row = start + lax.broadcasted_iota(jnp.int32, (1, qb), 1)
    t_last = start // KT
    n_far = jnp.maximum(t_last - 1, 0)

    @pl.when(i == 0)
    def _():
        lg_ref[:, 0:16, :] = jnp.zeros((HKV, 16, rows), F32)

    o_cs = []
    for h in range(HKV):
        parts = []
        for g in range(GRP):
            hd = h * GRP + g
            slab = qall[:, (hd // 2) * 2 * DH:(hd // 2 + 1) * 2 * DH]
            mine = (lane >= DH) if hd % 2 else (lane < DH)
            ss = jnp.sum(jnp.where(mine, slab * slab, 0.0), axis=-1, keepdims=True)
            xn = jnp.where(mine, slab * lax.rsqrt(ss * (1.0 / DH) + RMS_EPS) * qg_ref[...] * (DH ** -0.5), 0.0)
            if hd % 2 != h:
                xn = pltpu.roll(xn, DH, 1)
            parts.append(xn)
        xf = jnp.concatenate(parts, axis=0)
        qh = xf.astype(BF)

        def compressed(n, h=h, qh=qh):
            lg_ref[h, 16:16 + n, :] = _dot_nt(kc_ref[0:n, :], qh)
            band0 = pl.multiple_of(i * 8, 8)
            lg_ref[h, pl.ds(band0, CBAND), :] = lg_ref[h, pl.ds(band0, CBAND), :] + bc_ref[h]
            tok = lax.broadcasted_iota(jnp.int32, (n, 1), 0)
            p_c = _masked_softmax_rows(jnp.where(tok < 8 * i + 8, lg_ref[h, 16:16 + n, :], NEG), 0)
            o_c = _dot(vct_ref[h, :, 0:n], p_c.astype(BF))
            imp = p_c[:, 0:qb]
            for g in range(1, GRP):
                imp = imp + p_c[:, g * qb:(g + 1) * qb]
            sc3 = _dot(amat_ref[:, 0:n], jnp.concatenate(_split3(imp), axis=1))
            return o_c, sc3[:, 0:qb] + sc3[:, qb:2 * qb] + sc3[:, 2 * qb:]

        cch = min(ncp, 2 * Q_BLOCK)
        nbr = ncp // cch
        o_c, score = lax.switch(jnp.minimum((8 * i + 7) // cch, nbr - 1),
                                [functools.partial(compressed, (k + 1) * cch) for k in range(nbr)])
        sel = _top_blocks(score, qpos_row, n_top)
        pen = ((sel.T - 1.0) * PEN).astype(BF)
        qh2 = (xf * LOG2E).astype(BF)
        qa_ref[h, 0] = jnp.concatenate([jnp.concatenate([pen[:, 0:SEL_HALF]] * GRP, axis=0), qh2], axis=1)
        qa_ref[h, 1] = jnp.concatenate([jnp.concatenate([pen[:, SEL_HALF:]] * GRP, axis=0), qh2], axis=1)
        qh_ref[h] = qh
        o_cs.append(o_c)

    hr = rows // SEL_SPLIT
    chains = [(h, slice(sp * hr, (sp + 1) * hr)) for h in range(HKV) for sp in range(SEL_SPLIT)]
    nchain = len(chains)

    def scores(c, t):
        h, rs = chains[c]
        qa = jnp.where(t < HALF_TILES, qa_ref[h, 0, rs, :], qa_ref[h, 1, rs, :])
        return _dot_nt(kaug_ref[t + 1], qa)

    def absorb(c, s, mx, t, m):
        h, rs = chains[c]
        m2 = jnp.maximum(m, mx)
        p = jnp.exp2(s - m2)
        acc_ref[h, :, rs] = jnp.exp2(m - m2) * acc_ref[h, :, rs] + _dot(vselt_ref[h, t + 1], p.astype(BF))
        return m2

    col_max = lambda s: jnp.max(s, axis=0, keepdims=True)

    def pair_body(k, carry):
        ms, mxa = list(carry[:nchain]), list(carry[nchain:])
        mxb = []
        for c, (h, rs) in enumerate(chains):
            s1 = scores(c, 2 * k + 1)
            sb_ref[h, :, rs] = s1
            mxb.append(col_max(s1))
            ms[c] = absorb(c, sa_ref[h, :, rs], mxa[c], 2 * k, ms[c])
        for c, (h, rs) in enumerate(chains):
            s2 = scores(c, 2 * k + 2)
            ms[c] = absorb(c, sb_ref[h, :, rs], mxb[c], 2 * k + 1, ms[c])
            sa_ref[h, :, rs] = s2
            mxa[c] = col_max(s2)
        return tuple(ms) + tuple(mxa)

    acc_ref[...] = jnp.zeros((HKV, VROWS, rows), F32)
    mxa = []
    for c, (h, rs) in enumerate(chains):
        s0 = scores(c, 0)
        sa_ref[h, :, rs] = s0
        mxa.append(col_max(s0))
    carry = tuple(jnp.full((1, hr), NEG, F32) for _ in range(nchain)) + tuple(mxa)
    quad_body = lambda k, cr: pair_body(2 * k + 1, pair_body(2 * k, cr))
    oct_body = lambda k, cr: quad_body(2 * k + 1, quad_body(2 * k, cr))
    carry = lax.fori_loop(0, n_far // 8, oct_body, carry)
    carry = lax.fori_loop(2 * (n_far // 8), n_far // 4, quad_body, carry)
    carry = lax.fori_loop(2 * (n_far // 4), n_far // 2, pair_body, carry)
    for c, (h, rs) in enumerate(chains):
        m_ref[h, :, rs] = carry[c]
        mx_ref[h, :, rs] = carry[nchain + c]

    @pl.when(n_far % 2 == 1)
    def _():
        for c, (h, rs) in enumerate(chains):
            m_ref[h, :, rs] = absorb(c, sa_ref[h, :, rs], mx_ref[h, :, rs], n_far - 1, m_ref[h, :, rs])

    kiota = lax.broadcasted_iota(jnp.int32, (KT, 1), 0)
    ms = [m_ref[h, :, rs] for h, rs in chains]
    for u in range(2):
        t = t_last - 1 + u
        for c, (h, rs) in enumerate(chains):
            s = jnp.where(t * KT + kiota >= 0, scores(c, t) + bs_ref[i % 2, h, u, :, rs], NEG)
            ms[c] = absorb(c, s, col_max(s), t, ms[c])

    gates_t = jax.nn.sigmoid(gate_ref[...]).T
    wkeys = WINDOW + qb
    for h in range(HKV):
        o_s = acc_ref[h, 0:DH, :] / jnp.maximum(acc_ref[h, DH:DH + 1, :], 1e-30)

        s = _dot_nt(kwin_ref[pl.ds(pl.multiple_of(start, qb), wkeys), :], qh_ref[h]) + bw_ref[h]
        kpos = start - WINDOW + lax.broadcasted_iota(jnp.int32, (wkeys, 1), 0)
        s = jnp.where(kpos >= 0, s, NEG)
        e = jnp.exp(s - jnp.max(s, axis=0, keepdims=True))
        vw = jnp.concatenate([vwint_ref[h, i + u] for u in range(wkeys // qb)], axis=1)
        ow = _dot(vw, e.astype(BF))
        o_w = ow[0:DH] / jnp.maximum(ow[DH:DH + 1], 1e-30)

        gate = lambda br: jnp.concatenate(
            [gates_t[(h * GRP + g) * 3 + br:(h * GRP + g) * 3 + br + 1, :] for g in range(GRP)], axis=1)
        o_t = gate(0) * o_cs[h] + gate(1) * o_s + gate(2) * o_w
        for g in range(GRP):
            o_ref[:, (h * GRP + g) * DH:(h * GRP + g + 1) * DH] = o_t[:, g * qb:(g + 1) * qb].T


def _bias_tables(rel_table):
    tab = rel_table.astype(F32)
    far = tab[REL_BUCKETS - 1]
    qb = Q_BLOCK

    def toeplitz(off, nk, dmax, shift):
        d = off + qb - 1 - jnp.arange(nk + qb - 1, dtype=jnp.int32)
        w = jnp.where(((d >= 0) & (d <= dmax))[None, :], (tab[_t5_bucket(d)] - (far if shift else 0.0)).T, NEG)
        return jnp.stack([w[:, qb - 1 - c:qb - 1 - c + nk] for c in range(qb)], axis=1)

    def lanes_gq(b):
        return b.reshape(HKV, GRP, qb, b.shape[2]).transpose(0, 3, 1, 2).reshape(HKV, b.shape[2], GRP * qb)

    big = 1 << 30
    bc = lanes_gq(toeplitz(16 * CMP_STRIDE - (CMP_LEN - 1), CBAND * CMP_STRIDE, big, True)[:, :, ::CMP_STRIDE])
    bs = jnp.stack([lanes_gq(toeplitz(off, 2 * KT, big, True)).reshape(HKV, 2, KT, GRP * qb)
                    for off in (KT, KT + qb)]) * LOG2E
    bw = lanes_gq(toeplitz(WINDOW, WINDOW + qb, WINDOW, False))
    return bc, bs, bw


def nsa_prompt(proj, kc, vc, ksel, vsel, kwin, vwin, q_g, rel_table):
    t = proj.shape[0]
    nb = t // Q_BLOCK
    nch = kc.shape[0]
    ncp = -(-nch // 128) * 128
    assert t // SEL_BLOCK <= NS_PAD and t % KT == 0
    n_top = min(TOP_N, t // SEL_BLOCK)
    kcp = jnp.pad(kc, ((0, ncp - nch), (0, 0)))
    vct = jnp.pad(vc, ((0, ncp - nch), (0, 0))).reshape(ncp, HKV, DH).transpose(1, 2, 0)
    amat = jnp.asarray(_selection_matrix(nch - 1, NS_PAD, ncp), BF)

    def values_t(v, pad_rows, tile):
        vt = jnp.pad(v.astype(BF), ((pad_rows, 0), (0, 0))).reshape(-1, tile, HKV, DH).transpose(2, 0, 3, 1)
        return jnp.concatenate([vt, jnp.ones(vt.shape[:2] + (1, tile), BF),
                                jnp.zeros(vt.shape[:2] + (VROWS - DH - 1, tile), BF)], axis=2)

    onehot = ((jnp.arange(t, dtype=jnp.int32)[:, None] // SEL_BLOCK) % SEL_HALF
              == jnp.arange(SEL_HALF, dtype=jnp.int32)[None, :]).astype(BF)
    kaug = jnp.pad(jnp.concatenate([onehot, ksel.astype(BF)], axis=1), ((KT, 0), (0, 0))).reshape(-1, KT, 2 * KVW)
    vselt = values_t(vsel, KT, KT)
    vwint = values_t(vwin, WINDOW, Q_BLOCK)
    rows = GRP * Q_BLOCK
    kwinp = jnp.pad(kwin.astype(BF), ((WINDOW, 0), (0, 0)))
    bc, bs, bw = _bias_tables(rel_table)
    whole = pl.BlockSpec(memory_space=pltpu.VMEM)
    return pl.pallas_call(
        functools.partial(_nsa_prompt_kernel, ncp=ncp, n_top=n_top),
        grid=(nb,),
        in_specs=[pl.BlockSpec((Q_BLOCK, NH * DH), lambda i: (i, O_Q // (NH * DH))),
                  pl.BlockSpec((Q_BLOCK, 128), lambda i: (i, O_GATE // 128)),
                  whole, whole, whole, whole, whole, whole, whole, whole, whole, whole, whole],
        out_specs=pl.BlockSpec((Q_BLOCK, NH * DH), lambda i: (i, 0)),
        out_shape=jax.ShapeDtypeStruct((t, NH * DH), F32),
        scratch_shapes=[pltpu.VMEM((HKV, 16 + ncp, rows), F32), pltpu.VMEM((HKV, VROWS, rows), F32),
                        pltpu.VMEM((HKV, 2, rows, SEL_HALF + KVW), BF), pltpu.VMEM((HKV, rows, KVW), BF),
                        pltpu.VMEM((HKV, KT, rows), F32), pltpu.VMEM((HKV, KT, rows), F32),
                        pltpu.VMEM((HKV, 1, rows), F32), pltpu.VMEM((HKV, 1, rows), F32)],
        compiler_params=_cp("arbitrary"),
        name="nsa_prompt",
    )(proj, proj, jnp.concatenate([q_g, q_g]).reshape(1, 2 * DH), kcp, vct, amat, kaug, vselt, kwinp, vwint,
      bc, bs, bw)


def _pick_head(x, lane_h):
    return jnp.where(lane_h == 0, x[0:DH], x[DH:2 * DH])


def _nsa_dec_pre_kernel(qbd_ref, kc_ref, vc_ref, amat_ref, bc_ref, win_ref, wnew_ref, bwa_ref, bwb_ref,
                        pen_ref, oc_ref, ow_ref, *, ts, pos0, n_top):
    r_all = GRP * HKV * ts
    qbd = qbd_ref[0]
    lane_h = (lax.broadcasted_iota(jnp.int32, (1, r_all), 1) // ts) % HKV

    p_c = _masked_softmax_rows(_dot(kc_ref[0], qbd) + bc_ref[...], 0)
    oc_ref[0] = _pick_head(_dot_tn(vc_ref[0], p_c.astype(BF)), lane_h)

    hi, mid, lo = _split3(p_c)
    amat = amat_ref[...]
    sc = _dot(amat, hi) + _dot(amat, mid) + _dot(amat, lo)
    w8 = HKV * ts
    score = sc[:, 0:w8]
    for g in range(1, GRP):
        score = score + sc[:, g * w8:(g + 1) * w8]
    qp = pos0 + lax.broadcasted_iota(jnp.int32, (1, w8), 1) % ts
    sel = _top_blocks(score, qp, n_top)
    pen_ref[0] = (jnp.concatenate([sel] * GRP, axis=1) - 1.0) * PEN

    win = win_ref[0]
    wnew = wnew_ref[0]
    s_a = _dot(win[:, 0:KVW].astype(BF), qbd) + bwa_ref[...]
    s_b = _dot(wnew[:, 0:KVW].astype(BF), qbd) + bwb_ref[...]
    m = jnp.maximum(jnp.max(s_a, axis=0, keepdims=True), jnp.max(s_b, axis=0, keepdims=True))
    e_a = jnp.where(s_a > 0.5 * NEG, jnp.exp(s_a - m), 0.0)
    e_b = jnp.where(s_b > 0.5 * NEG, jnp.exp(s_b - m), 0.0)
    den = jnp.maximum(jnp.sum(e_a, axis=0, keepdims=True) + jnp.sum(e_b, axis=0, keepdims=True), 1e-30)
    o_w = (_dot_tn(win[:, KVW:].astype(BF), e_a.astype(BF)) + _dot_tn(wnew[:, KVW:].astype(BF), e_b.astype(BF)))
    ow_ref[0] = _pick_head(o_w, lane_h) / den


SEL_PAGES = 32


def _nsa_dec_sel_kernel(*refs, npg, page, ts):
    pages = refs[1:1 + npg]
    (q_ref, pen_ref, expt_ref, blast_ref, knew_ref, vnew_ref, bnew_ref, gate_ref, oc_ref, ow_ref,
     o_ref, m_ref, l_ref, acc_ref) = refs[1 + npg:]
    j = pl.program_id(1)
    last = pl.num_programs(1) - 1
    rows = GRP * ts

    @pl.when(j == 0)
    def _():
        m_ref[...] = jnp.full((HKV * rows, 1), NEG, F32)
        l_ref[...] = jnp.zeros((HKV * rows, 1), F32)
        acc_ref[...] = jnp.zeros((HKV * rows, DH), F32)

    def online(s, pvs):
        m = m_ref[...]
        m2 = jnp.maximum(m, jnp.max(s, axis=-1, keepdims=True))
        a = jnp.exp(m - m2)
        p = jnp.exp(s - m2)
        m_ref[...] = m2
        l_ref[...] = a * l_ref[...] + jnp.sum(p, axis=-1, keepdims=True)
        pb = p.astype(BF)
        pv = jnp.concatenate([pvs[h](pb[h * rows:(h + 1) * rows]) for h in range(HKV)], axis=0)
        acc_ref[...] = a * acc_ref[...] + pv

    both = lambda f: jnp.concatenate([f(h) for h in range(HKV)], axis=0)
    vts = [jnp.concatenate([r[0, 1, h] for r in pages], axis=1).astype(BF) for h in range(HKV)]
    s = both(lambda h: _dot(q_ref[0, h], jnp.concatenate([r[0, 0, h] for r in pages], axis=1).astype(BF))
             + _dot(pen_ref[0, h, j], expt_ref[...]))
    s = s + jnp.where(j == last, blast_ref[...], 0.0)
    online(s, [functools.partial(lambda vt, p: _dot_nt(p, vt), vts[h]) for h in range(HKV)])

    @pl.when(j == last)
    def _():
        s_new = both(lambda h: _dot_nt(q_ref[0, h], knew_ref[0, h].astype(BF))) + bnew_ref[...]
        online(s_new, [functools.partial(lambda v, p: _dot(p, v), vnew_ref[0, h].astype(BF)) for h in range(HKV)])
        o_s = acc_ref[...] / jnp.maximum(l_ref[...], 1e-30)
        g = jax.nn.sigmoid(gate_ref[0])
        o_ref[0] = g[:, 0:1] * oc_ref[0] + g[:, 1:2] * o_s + g[:, 2:3] * ow_ref[0]


def _dec_bias_tables(rel_table, pos0, ts, nch, wb, page, npg):
    tab = rel_table.astype(F32)
    far = tab[REL_BUCKETS - 1]
    r = np.arange(GRP * HKV * ts)
    head = ((r // ts) % HKV) * GRP + r // (HKV * ts)
    qpos = pos0 + r % ts
    far_r = far[head][None, :]

    def bias(kpos, dmax, shift):
        dist = qpos[None, :] - np.asarray(kpos)[:, None]
        ok = (dist >= 0) & (dist <= dmax)
        out = jnp.where(ok, 0.0 if shift else far_r, NEG)
        near = np.nonzero((ok & (dist < REL_MAX_DIST)).any(axis=1))[0]
        if near.size:
            lo, hi = int(near.min()), int(near.max()) + 1
            b = tab[_t5_bucket(jnp.asarray(dist[lo:hi], jnp.int32)), head[None, :]] - (far_r if shift else 0.0)
            out = jnp.concatenate([out[:lo], jnp.where(ok[lo:hi], b, NEG), out[hi:]], axis=0)
        return out

    big = 1 << 30
    bc = bias(np.arange(nch) * CMP_STRIDE + CMP_LEN - 1, big, False)
    bwa = bias(pos0 - wb + np.arange(wb), WINDOW, False)
    tnew = np.arange(8)
    newpos = np.where(tnew < ts, pos0 + tnew, pos0 + 2 * WINDOW + SEL_BLOCK)
    bwb = bias(newpos, WINDOW, False)
    bnew = bias(newpos, big, True)
    step = npg * page
    blast = jnp.concatenate([jnp.zeros((step - page, r.shape[0]), F32), bias(pos0 - page + np.arange(page), big, True)],
                            axis=0)
    return bc, bwa, bwb, bnew, blast


def nsa_decode(qn, gate_raw, kc, vc, cachet, page_table, win_state, kv_new, win_new, rel_table):
    bsz, ts, _ = qn.shape
    n_pages = page_table.shape[1]
    page = cachet.shape[-1]
    pos0 = n_pages * page
    nch = kc.shape[1]
    wb = win_state.shape[1]
    r_all = GRP * HKV * ts
    assert pos0 % SEL_BLOCK == 0 and ts <= 8 and ts <= SEL_BLOCK and nch % 8 == 0
    ns = -(-(pos0 + ts) // SEL_BLOCK)
    npg = min(SEL_PAGES, n_pages)
    assert n_pages % npg == 0
    bps = npg * page // SEL_BLOCK
    ns_pad = max(-(-ns // 8) * 8, (n_pages // npg) * bps)
    n_top = min(TOP_N, ns)

    q5 = qn.reshape(bsz, ts, HKV, GRP, DH)
    qbd = jnp.einsum('bqhgd,hk->bkdghq', q5, jnp.eye(HKV, dtype=F32)).reshape(bsz, 2 * DH, r_all).astype(BF)
    gate_t = gate_raw.reshape(bsz, ts, HKV, GRP, 3).transpose(0, 4, 3, 2, 1).reshape(bsz, 3, r_all)
    pad8 = lambda a: jnp.pad(a, ((0, 0), (0, 8 - ts), (0, 0)))
    amat = jnp.asarray(_selection_matrix(nch - 1, ns_pad, nch), BF)
    expand = jnp.asarray(np.repeat(np.eye(bps, dtype=np.float32), SEL_BLOCK, axis=0), BF)
    bc, bwa, bwb, bnew, blast = _dec_bias_tables(rel_table, pos0, ts, nch, wb, page, npg)

    full = lambda a: pl.BlockSpec(a.shape, functools.partial(lambda nd, *_: (0,) * nd, a.ndim))
    per_b = lambda a: pl.BlockSpec((1,) + a.shape[1:], functools.partial(lambda nd, b, *_: (b,) + (0,) * nd, a.ndim - 1))
    wnew = pad8(win_new)
    pre_in = [qbd, kc, vc, amat, bc, win_state, wnew, bwa, bwb]
    pre_specs = [per_b(qbd), per_b(kc), per_b(vc), full(amat), full(bc), per_b(win_state), per_b(wnew),
                 full(bwa), full(bwb)]
    small = jax.ShapeDtypeStruct((bsz, DH, r_all), F32)
    pen, o_c, o_w = pl.pallas_call(
        functools.partial(_nsa_dec_pre_kernel, ts=ts, pos0=pos0, n_top=n_top),
        grid=(bsz,), in_specs=pre_specs,
        out_specs=[pl.BlockSpec((1, ns_pad, r_all), lambda b: (b, 0, 0)),
                   pl.BlockSpec((1, DH, r_all), lambda b: (b, 0, 0)),
                   pl.BlockSpec((1, DH, r_all), lambda b: (b, 0, 0))],
        out_shape=[jax.ShapeDtypeStruct((bsz, ns_pad, r_all), F32), small, small],
        compiler_params=_cp("parallel"), name="nsa_dec_pre",
    )(*pre_in)

    rows = GRP * ts
    nsteps = n_pages // npg
    by_head = lambda a: a.reshape(a.shape[:-1] + (GRP, HKV, ts))
    q_h = q5.transpose(0, 2, 3, 1, 4).reshape(bsz, HKV, rows, DH).astype(BF)
    gate_h = gate_raw.reshape(bsz, ts, HKV, GRP, 3).transpose(0, 2, 3, 1, 4).reshape(bsz, HKV, rows, 3)
    pen_h = by_head(pen[:, :nsteps * bps].reshape(bsz, nsteps, bps, r_all)).transpose(0, 4, 1, 3, 5, 2).reshape(
        bsz, HKV, nsteps, rows, bps).astype(BF)
    oc_h, ow_h = (by_head(a).transpose(0, 3, 2, 4, 1).reshape(bsz, HKV, rows, DH) for a in (o_c, o_w))
    blast_h, bnew_h = (by_head(a).transpose(2, 1, 3, 0).reshape(HKV, rows, a.shape[0]) for a in (blast, bnew))
    new5 = pad8(kv_new).reshape(bsz, 8, 4, HKV, DH)
    knew = new5[:, :, 2].transpose(0, 2, 1, 3)
    vnew = new5[:, :, 3].transpose(0, 2, 1, 3)
    page_specs = [pl.BlockSpec((1, 2, HKV, DH, page),
                               functools.partial(lambda r, b, j, pt: (pt[b, j * npg + r], 1, 0, 0, 0), r))
                  for r in range(npg)]
    heads_rows = lambda a: a.reshape(a.shape[:-3] + (HKV * rows, a.shape[-1]))
    blast_h, bnew_h, gate_h, oc_h, ow_h = map(heads_rows, (blast_h, bnew_h, gate_h, oc_h, ow_h))
    sel_in = [q_h, pen_h, expand.T, blast_h, knew, vnew, bnew_h, gate_h, oc_h, ow_h]
    sel_specs = [per_b(q_h), per_b(pen_h), full(expand.T), full(blast_h), per_b(knew), per_b(vnew), full(bnew_h),
                 per_b(gate_h), per_b(oc_h), per_b(ow_h)]
    o_h = pl.pallas_call(
        functools.partial(_nsa_dec_sel_kernel, npg=npg, page=page, ts=ts),
        grid_spec=pltpu.PrefetchScalarGridSpec(
            num_scalar_prefetch=1, grid=(bsz, nsteps), in_specs=page_specs + sel_specs,
            out_specs=pl.BlockSpec((1, HKV * rows, DH), lambda b, j, pt: (b, 0, 0)),
            scratch_shapes=[pltpu.VMEM((HKV * rows, 1), F32), pltpu.VMEM((HKV * rows, 1), F32),
                            pltpu.VMEM((HKV * rows, DH), F32)]),
        out_shape=jax.ShapeDtypeStruct((bsz, HKV * rows, DH), F32),
        compiler_params=_cp("parallel", "arbitrary"), name="nsa_dec_sel",
    )(page_table, *([cachet] * npg), *sel_in)
    return o_h.reshape(bsz, HKV, GRP, ts, DH).transpose(0, 3, 1, 2, 4).reshape(bsz, ts, NH * DH)


def _even_weights(p, e):
    w_in = jnp.pad(p['att_w_in'][e], ((0, 0), (0, ATT_IN_PAD - ATT_IN_COLS))).astype(BF)
    w_out = p['att_w_out'][e].astype(BF)
    return dict(
        w_in=w_in, w_out_conv=w_out[:C_CONV], w_out_att=w_out[C_CONV:],
        cmp=_compress_weights(p['cmp_pe'][e], p['cmp_w1'][e], p['cmp_b1'][e], p['cmp_w2'][e], p['k_norm_g'][e][0]),
        k_g=p['k_norm_g'][e], q_g=p['q_norm_g'][e], conv_w=p['conv_w'][e], conv_b=p['conv_b'][e],
        ln_g=p['conv_ln_g'][e], ln_b=p['conv_ln_b'][e])


def _even_prompt(x2, norm_g, w, rel_table):
    t = x2.shape[0]
    proj = norm_matmul(x2, norm_g, w['w_in'])
    conv_y, conv_new = conformer_conv(proj[None], jnp.zeros((1, CONV_WIDTH - 1, C_CONV), F32),
                                      w['conv_w'], w['conv_b'], w['ln_g'], w['ln_b'])
    kv_new, win_new = kv_post(proj, w['k_g'])
    kc, vc = compress_dense(proj[None], w['cmp'])
    o = nsa_prompt(proj, kc[0], vc[0], kv_new[:, 2 * KVW:3 * KVW], kv_new[:, 3 * KVW:], win_new[:, :KVW],
                   win_new[:, KVW:], w['q_g'], rel_table)
    x2 = out_proj2(x2, conv_y[0], o, w['w_out_conv'], w['w_out_att'])
    keep = min(WINDOW, t)
    return (x2, kv_new.reshape(1, t, 4, HKV, DH), win_new[t - keep:].reshape(1, keep, 2, HKV, DH), conv_new)


def _even_decode(x3, norm_g, w, rel_table, cachet, page_table, win_state, conv_state):
    bsz, t, d = x3.shape
    x2 = x3.reshape(bsz * t, d)
    proj = norm_matmul(x2, norm_g, w['w_in'])
    proj3 = proj.reshape(bsz, t, ATT_IN_PAD)
    conv_y, conv_new = conformer_conv(proj3, conv_state, w['conv_w'], w['conv_b'], w['ln_g'], w['ln_b'])
    kv_new, win_new = kv_post(proj, w['k_g'])
    qn = q_norm(proj, w['q_g'])
    kc, vc = compress_paged(cachet, page_table, w['cmp'])
    wb = win_state.shape[1]
    o = nsa_decode(qn.reshape(bsz, t, NH * DH), proj3[:, :, O_GATE:ATT_IN_COLS], kc, vc, cachet, page_table,
                   win_state.reshape(bsz, wb, 2 * KVW), kv_new.reshape(bsz, t, 4 * KVW),
                   win_new.reshape(bsz, t, 2 * KVW), rel_table)
    x2 = out_proj2(x2, conv_y.reshape(bsz * t, C_CONV), o.reshape(bsz * t, NH * DH), w['w_out_conv'],
                   w['w_out_att'])
    win_all = jnp.concatenate([win_state, win_new.reshape(bsz, t, 2, HKV, DH)], axis=1)
    keep = min(WINDOW, wb + t)
    return (x2.reshape(bsz, t, d), kv_new.reshape(bsz, t, 4, HKV, DH), win_all[:, wb + t - keep:], conv_new)


def _odd_layer(x3, s0, norm_g, w_in_bf, w_out_bf, lb, hg_norm_g):
    bsz, t, d = x3.shape
    x2 = x3.reshape(bsz * t, d)
    z = norm_matmul(x2, norm_g, w_in_bf).reshape(bsz, t, 4 * d)
    tp = -(-t // HG_CHUNK) * HG_CHUNK
    if tp != t:
        z = jnp.pad(z, ((0, 0), (0, tp - t), (0, 0)))
    o, s_new = hgrn2(z, s0, lb, hg_norm_g, t)
    x2 = out_proj1(x2, o[:, :t].reshape(bsz * t, d), w_out_bf)
    return x2.reshape(bsz, t, d), s_new


def kernel(x_prompt, x_sample, cache_nsa_kv, page_table, state_nsa_win, state_conv, state_hgrn, rel_bias_table,
           norm_mix_g, norm_mlp_g, w_mlp_up, w_mlp_down, att_w_in, att_w_out, q_norm_g, k_norm_g, cmp_pe, cmp_w1,
           cmp_b1, cmp_w2, conv_w, conv_b, conv_ln_g, conv_ln_b, hg_w_in, hg_w_out, hg_lb_logits, hg_norm_g):
    p = dict(att_w_in=att_w_in, att_w_out=att_w_out, q_norm_g=q_norm_g, k_norm_g=k_norm_g, cmp_pe=cmp_pe,
             cmp_w1=cmp_w1, cmp_b1=cmp_b1, cmp_w2=cmp_w2, conv_w=conv_w, conv_b=conv_b, conv_ln_g=conv_ln_g,
             conv_ln_b=conv_ln_b)
    bp, tp_, d = x_prompt.shape
    assert bp == 1
    db, ts, _ = x_sample.shape
    w_up = w_mlp_up.astype(BF)
    w_down = w_mlp_down.astype(BF)
    cum = jnp.cumsum(jax.nn.softmax(hg_lb_logits.astype(F32), axis=0), axis=0)

    xp = x_prompt[0]
    kv_p, win_p, conv_p, hg_p = [], [], [], []
    for layer in range(norm_mix_g.shape[0]):
        if layer % 2 == 0:
            e = layer // 2
            w = _even_weights(p, e)
            xp, kv_new, win_new, conv_new = _even_prompt(xp, norm_mix_g[layer], w, rel_bias_table)
            kv_p.append(kv_new)
            win_p.append(win_new)
            conv_p.append(conv_new)
        else:
            o = layer // 2
            x3, s_new = _odd_layer(xp[None], jnp.zeros((1, HG_HEADS, HG_DK, HG_DV), F32), norm_mix_g[layer],
                                   hg_w_in[o].astype(BF), hg_w_out[o].astype(BF), cum[layer] - cum[0],
                                   hg_norm_g[o])
            xp = x3[0]
            hg_p.append(s_new)
        xp = mlp(xp, norm_mlp_g[layer], w_up[layer], w_down[layer])

    cachet = cache_nsa_kv.transpose(0, 1, 3, 4, 5, 2)
    xs = x_sample
    kv_s, win_s, conv_s, hg_s = [], [], [], []
    for layer in range(norm_mix_g.shape[0]):
        if layer % 2 == 0:
            e = layer // 2
            w = _even_weights(p, e)
            xs, kv_new, win_new, conv_new = _even_decode(xs, norm_mix_g[layer], w, rel_bias_table, cachet[e],
                                                         page_table, state_nsa_win[e], state_conv[e])
            kv_s.append(kv_new)
            win_s.append(win_new)
            conv_s.append(conv_new)
        else:
            o = layer // 2
            xs, s_new = _odd_layer(xs, state_hgrn[o], norm_mix_g[layer], hg_w_in[o].astype(BF),
                                   hg_w_out[o].astype(BF), cum[layer] - cum[0], hg_norm_g[o])
            hg_s.append(s_new)
        xs = mlp(xs.reshape(db * ts, d), norm_mlp_g[layer], w_up[layer], w_down[layer]).reshape(db, ts, d)
    return (xp[None], xs, jnp.stack(kv_p), jnp.stack(kv_s), jnp.stack(win_p), jnp.stack(win_s),
            jnp.stack(conv_p), jnp.stack(conv_s), jnp.stack(hg_p), jnp.stack(hg_s))
```

```python
import functools
import math

import jax
import jax.numpy as jnp
import numpy as np
from jax import lax
from jax.experimental import pallas as pl
from jax.experimental.pallas import tpu as pltpu

D_MODEL = 1024
C_CONV = 512
CONV_WIDTH = 31
DH = 64
HKV = 2
GRP = 4
NH = HKV * GRP
CMP_STRIDE = 16
CMP_LEN = 32
CMP_HIDDEN = 128
SEL_BLOCK = 64
TOP_N = 16
WINDOW = 512
Q_BLOCK = 128
REL_BUCKETS = 32
REL_EXACT = 16
REL_MAX_DIST = 128
HG_DK = 128
HG_HEADS = 8
HG_DV = 128
HG_CHUNK = 64
RMS_EPS = 1e-6
NEG = -1e30
BIG = 1e9
O_Q = 2 * C_CONV
O_KV = O_Q + NH * DH
O_GATE = O_KV + 3 * 2 * HKV * DH
ATT_IN_COLS = O_GATE + 3 * NH
ATT_IN_PAD = 2432
KVW = HKV * DH

VMEM_LIMIT = 56 * 1024 * 1024
BF = jnp.bfloat16
F32 = jnp.float32


def _cp(*sem):
    return pltpu.CompilerParams(dimension_semantics=sem, vmem_limit_bytes=VMEM_LIMIT)


def _dot(a, b):
    return jnp.dot(a, b, preferred_element_type=F32)


def _dot_nt(a, b):
    return lax.dot_general(a, b, (((1,), (1,)), ((), ())), preferred_element_type=F32)


def _dot_tn(a, b):
    return lax.dot_general(a, b, (((0,), (0,)), ((), ())), preferred_element_type=F32)


def _rms(x, g):
    return x * lax.rsqrt(jnp.mean(x * x, axis=-1, keepdims=True) + RMS_EPS) * g


def _norm_matmul_kernel(x_ref, g_ref, w_ref, o_ref):
    hn = _rms(x_ref[...], g_ref[...]).astype(BF)
    o_ref[...] = _dot(hn, w_ref[...])


def norm_matmul(x, g, w_bf, tn=None):
    m, d = x.shape
    n = w_bf.shape[1]
    tm = min(m, 512)
    tn = n if tn is None else tn
    return pl.pallas_call(
        _norm_matmul_kernel,
        grid=(m // tm, n // tn),
        in_specs=[pl.BlockSpec((tm, d), lambda i, j: (i, 0)),
                  pl.BlockSpec((1, d), lambda i, j: (0, 0)),
                  pl.BlockSpec((d, tn), lambda i, j: (0, j))],
        out_specs=pl.BlockSpec((tm, tn), lambda i, j: (i, j)),
        out_shape=jax.ShapeDtypeStruct((m, n), F32),
        compiler_params=_cp("parallel", "arbitrary"),
        name="norm_matmul",
    )(x, g.reshape(1, d), w_bf)


def _out_proj_kernel(r_ref, a1_ref, a2_ref, w1_ref, w2_ref, o_ref):
    o_ref[...] = (r_ref[...] + _dot(a1_ref[...].astype(BF), w1_ref[...])
                  + _dot(a2_ref[...].astype(BF), w2_ref[...]))


def out_proj2(res, a1, a2, w1_bf, w2_bf):
    m, d = res.shape
    k1, k2 = a1.shape[1], a2.shape[1]
    tm = min(m, 512)
    return pl.pallas_call(
        _out_proj_kernel,
        grid=(m // tm,),
        in_specs=[pl.BlockSpec((tm, d), lambda i: (i, 0)),
                  pl.BlockSpec((tm, k1), lambda i: (i, 0)),
                  pl.BlockSpec((tm, k2), lambda i: (i, 0)),
                  pl.BlockSpec((k1, d), lambda i: (0, 0)),
                  pl.BlockSpec((k2, d), lambda i: (0, 0))],
        out_specs=pl.BlockSpec((tm, d), lambda i: (i, 0)),
        out_shape=jax.ShapeDtypeStruct((m, d), F32),
        compiler_params=_cp("parallel"),
        name="out_proj2",
    )(res, a1, a2, w1_bf, w2_bf)


def _out_proj1_kernel(r_ref, a_ref, w_ref, o_ref):
    o_ref[...] = r_ref[...] + _dot(a_ref[...].astype(BF), w_ref[...])


def out_proj1(res, a, w_bf):
    m, d = res.shape
    k = a.shape[1]
    tm = min(m, 512)
    return pl.pallas_call(
        _out_proj1_kernel,
        grid=(m // tm,),
        in_specs=[pl.BlockSpec((tm, d), lambda i: (i, 0)),
                  pl.BlockSpec((tm, k), lambda i: (i, 0)),
                  pl.BlockSpec((k, d), lambda i: (0, 0))],
        out_specs=pl.BlockSpec((tm, d), lambda i: (i, 0)),
        out_shape=jax.ShapeDtypeStruct((m, d), F32),
        compiler_params=_cp("parallel"),
        name="out_proj1",
    )(res, a, w_bf)


def _mlp_kernel(x_ref, g_ref, wu_ref, wd_ref, o_ref, hn_ref, acc_ref):
    j = pl.program_id(1)

    @pl.when(j == 0)
    def _():
        hn_ref[...] = _rms(x_ref[...], g_ref[...]).astype(BF)
        acc_ref[...] = x_ref[...]

    hid = jnp.maximum(_dot(hn_ref[...], wu_ref[...]), 0.0)
    acc_ref[...] += _dot((hid * hid).astype(BF), wd_ref[...])

    @pl.when(j == pl.num_programs(1) - 1)
    def _():
        o_ref[...] = acc_ref[...]


def mlp(x, g, wu_bf, wd_bf):
    m, d = x.shape
    hdim = wu_bf.shape[1]
    tm = min(m, 1024)
    th = 1024
    return pl.pallas_call(
        _mlp_kernel,
        grid=(m // tm, hdim // th),
        in_specs=[pl.BlockSpec((tm, d), lambda i, j: (i, 0)),
                  pl.BlockSpec((1, d), lambda i, j: (0, 0)),
                  pl.BlockSpec((d, th), lambda i, j: (0, j)),
                  pl.BlockSpec((th, d), lambda i, j: (j, 0))],
        out_specs=pl.BlockSpec((tm, d), lambda i, j: (i, 0)),
        out_shape=jax.ShapeDtypeStruct((m, d), F32),
        scratch_shapes=[pltpu.VMEM((tm, d), BF), pltpu.VMEM((tm, d), F32)],
        compiler_params=_cp("parallel", "arbitrary"),
        name="mlp",
    )(x, g.reshape(1, d), wu_bf, wd_bf)


CONV_HALO = 32
CONV_PAD = CONV_HALO - (CONV_WIDTH - 1)


def _conv_kernel(u_ref, st_ref, w_ref, b_ref, lg_ref, lb_ref, y_ref, new_ref, xin_ref, ph_ref, *, tt):
    t = pl.program_id(1)

    @pl.when(t == 0)
    def _():
        xin_ref[0:CONV_HALO, :] = st_ref[0]

    a = u_ref[0, :, 0:C_CONV]
    gt = u_ref[0, :, C_CONV:2 * C_CONV]
    xin_ref[CONV_HALO:CONV_HALO + tt, :] = a * jax.nn.sigmoid(gt)

    rb = min(tt, 128)
    for c in range(C_CONV // 128):
        cs = slice(c * 128, (c + 1) * 128)
        starts = list(range(0, tt, rb))
        accs = [jnp.zeros((rb, 128), F32) + b_ref[:, cs] for _ in starts]
        for ph in range(min(8, CONV_WIDTH)):
            taps = range(ph, CONV_WIDTH, 8)
            n = rb + 8 * (len(taps) - 1)
            if rb % 8 == 0:
                for bi, r0 in enumerate(starts):
                    ph_ref[bi, 0:n, :] = xin_ref[CONV_PAD + ph + r0:CONV_PAD + ph + r0 + n, cs]
            for a, k in enumerate(taps):
                wk = w_ref[k:k + 1, cs]
                for bi, r0 in enumerate(starts):
                    if rb % 8 == 0:
                        accs[bi] = accs[bi] + wk * ph_ref[bi, 8 * a:8 * a + rb, :]
                    else:
                        accs[bi] = accs[bi] + wk * xin_ref[CONV_PAD + k + r0:CONV_PAD + k + r0 + rb, cs]
        for bi, r0 in enumerate(starts):
            y_ref[0, r0:r0 + rb, cs] = accs[bi]
    y = y_ref[0]
    mu = jnp.mean(y, axis=-1, keepdims=True)
    yc = y - mu
    var = jnp.mean(yc * yc, axis=-1, keepdims=True)
    z = yc * lax.rsqrt(var + RMS_EPS) * lg_ref[...] + lb_ref[...]
    y_ref[0] = z * jax.nn.sigmoid(z)

    @pl.when(t == pl.num_programs(1) - 1)
    def _():
        new_ref[0] = xin_ref[tt + CONV_PAD:tt + CONV_HALO, :]

    if tt >= CONV_HALO:
        @pl.when(t < pl.num_programs(1) - 1)
        def _():
            xin_ref[0:CONV_HALO, :] = xin_ref[tt:tt + CONV_HALO, :]


def conformer_conv(proj, state, w, b, ln_g, ln_b):
    bsz, t, _ = proj.shape
    tt = min(t, 256)
    assert t % tt == 0 and (t == tt or tt >= CONV_HALO)
    st = jnp.pad(state, ((0, 0), (CONV_PAD, 0), (0, 0)))
    row = lambda v: v.reshape(1, C_CONV)
    return pl.pallas_call(
        functools.partial(_conv_kernel, tt=tt),
        grid=(bsz, t // tt),
        in_specs=[pl.BlockSpec((1, tt, 2 * C_CONV), lambda i, j: (i, j, 0)),
                  pl.BlockSpec((1, CONV_HALO, C_CONV), lambda i, j: (i, 0, 0)),
                  pl.BlockSpec((CONV_WIDTH, C_CONV), lambda i, j: (0, 0)),
                  pl.BlockSpec((1, C_CONV), lambda i, j: (0, 0)),
                  pl.BlockSpec((1, C_CONV), lambda i, j: (0, 0)),
                  pl.BlockSpec((1, C_CONV), lambda i, j: (0, 0))],
        out_specs=[pl.BlockSpec((1, tt, C_CONV), lambda i, j: (i, j, 0)),
                   pl.BlockSpec((1, CONV_WIDTH - 1, C_CONV), lambda i, j: (i, 0, 0))],
        out_shape=[jax.ShapeDtypeStruct((bsz, t, C_CONV), F32),
                   jax.ShapeDtypeStruct((bsz, CONV_WIDTH - 1, C_CONV), F32)],
        scratch_shapes=[pltpu.VMEM((CONV_HALO + tt, C_CONV), F32),
                        pltpu.VMEM((max(tt // 128, 1), min(tt, 128) + 8 * ((CONV_WIDTH - 1) // 8), 128), F32)],
        compiler_params=_cp("parallel", "arbitrary"),
        name="conformer_conv",
    )(proj, st, w, row(b), row(ln_g), row(ln_b))


HG_LEVELS = (32, 16, 8, 4, 2, 1)


def _hgrn_tables():
    c = HG_CHUNK
    idx = np.arange(c)
    masks = []
    for h in HG_LEVELS:
        blk = idx // (2 * h)
        upper = (idx % (2 * h)) >= h
        masks.append((blk[:, None] == blk[None, :]) & upper[:, None] & (~upper)[None, :])
    masks.append(np.eye(c, dtype=bool))
    return (idx[None, :] <= idx[:, None]).astype(np.float32), np.stack(masks).astype(np.float32)


HG_HEADS_PER_STEP = 8


def _hgrn_kernel(q_ref, fz_ref, v_ref, g_ref, lb_ref, ng_ref, s0_ref, tab_ref, msk_ref,
                 o_ref, sn_ref, st_ref, *, t_valid, nh):
    ci = pl.program_id(2)
    c = HG_CHUNK

    @pl.when(ci == 0)
    def _():
        for hh in range(nh):
            st_ref[hh] = s0_ref[0, hh].T

    rowc = lax.broadcasted_iota(jnp.int32, (c, 1), 0)
    sub8 = lax.broadcasted_iota(jnp.int32, (8, 1), 0)
    live = ci * c + rowc < t_valid
    tab = tab_ref[...]
    for hh in range(nh):
        cs = slice(hh * HG_DK, (hh + 1) * HG_DK)
        lb = lb_ref[:, cs]
        f = lb + (1.0 - lb) * jax.nn.sigmoid(fz_ref[0, :, cs])
        lf = jnp.where(live, jnp.log(f), 0.0)
        k = jnp.where(live, 1.0 - f, 0.0)
        q = q_ref[0, :, cs]
        v = v_ref[0, :, cs]

        hi = lf.astype(BF)
        b2 = _dot(tab, jnp.concatenate([hi, (lf - hi.astype(F32)).astype(BF)], axis=1))
        b = b2[:, 0:HG_DK] + b2[:, HG_DK:]

        def pivot_rows(h):
            if 2 * h >= 8:
                return jnp.concatenate(
                    [jnp.broadcast_to(b[blk * 2 * h + h - 1:blk * 2 * h + h, :], (2 * h, HG_DK))
                     for blk in range(c // (2 * h))], axis=0)
            groups = []
            for g8 in range(c // 8):
                piece = None
                for kb in range(8 // (2 * h)):
                    r = 8 * g8 + kb * 2 * h + h - 1
                    cand = jnp.broadcast_to(b[r:r + 1, :], (8, HG_DK))
                    piece = cand if piece is None else jnp.where(sub8 >= kb * 2 * h, cand, piece)
                groups.append(piece)
            return jnp.concatenate(groups, axis=0)

        attn = jnp.zeros((c, c), F32)
        for li, h in enumerate(HG_LEVELS):
            piv = pivot_rows(h)
            fac = jnp.exp(jnp.where((rowc % (2 * h)) >= h, b - piv, piv - b))
            attn = attn + msk_ref[li] * _dot_nt((q * fac).astype(BF), (k * fac).astype(BF))
        attn = attn + msk_ref[len(HG_LEVELS)] * _dot_nt(q.astype(BF), k.astype(BF))

        st = st_ref[hh]
        qb = (q * jnp.exp(b)).astype(BF)
        o = _dot(attn.astype(BF), v.astype(BF)) + _dot_nt(qb, st.astype(BF))
        ke = (k * jnp.exp(b[c - 1:c] - b)).astype(BF)
        decay = jnp.exp(b[c - 1:c])
        st_ref[hh] = st * decay + _dot(v.T.astype(BF), ke)

        gate = g_ref[0, :, cs]
        o_ref[0, :, cs] = _rms(o, ng_ref[...]) * (gate * jax.nn.sigmoid(gate))

    @pl.when(ci == pl.num_programs(2) - 1)
    def _():
        for hh in range(nh):
            sn_ref[0, hh] = st_ref[hh].T


def hgrn2(z, s0, lb, norm_g, t_valid):
    bsz, tp, _ = z.shape
    c = HG_CHUNK
    nc = tp // c
    tab, msk = _hgrn_tables()
    nh = HG_HEADS_PER_STEP
    ng = HG_HEADS // nh
    w = nh * HG_DK
    blk = lambda off: pl.BlockSpec((1, c, w), lambda b, h, i: (b, i, off + h))
    return pl.pallas_call(
        functools.partial(_hgrn_kernel, t_valid=t_valid, nh=nh),
        grid=(bsz, ng, nc),
        in_specs=[blk(0), blk(ng), blk(2 * ng), blk(3 * ng),
                  pl.BlockSpec((1, w), lambda b, h, i: (0, h)),
                  pl.BlockSpec((1, HG_DV), lambda b, h, i: (0, 0)),
                  pl.BlockSpec((1, nh, HG_DK, HG_DV), lambda b, h, i: (b, h, 0, 0)),
                  pl.BlockSpec(tab.shape, lambda b, h, i: (0, 0)),
                  pl.BlockSpec(msk.shape, lambda b, h, i: (0, 0, 0))],
        out_specs=[pl.BlockSpec((1, c, w), lambda b, h, i: (b, i, h)),
                   pl.BlockSpec((1, nh, HG_DK, HG_DV), lambda b, h, i: (b, h, 0, 0))],
        out_shape=[jax.ShapeDtypeStruct((bsz, tp, D_MODEL), F32),
                   jax.ShapeDtypeStruct((bsz, HG_HEADS, HG_DK, HG_DV), F32)],
        scratch_shapes=[pltpu.VMEM((nh, HG_DV, HG_DK), F32)],
        compiler_params=_cp("parallel", "parallel", "arbitrary"),
        name="hgrn2",
    )(z, z, z, z, lb.reshape(1, D_MODEL), norm_g.reshape(1, HG_DV), s0,
      jnp.asarray(tab, BF), jnp.asarray(msk))


def _pair_norm(seg, g):
    lo = lax.broadcasted_iota(jnp.int32, seg.shape, 1) < DH
    sq = seg * seg
    s0 = jnp.sum(jnp.where(lo, sq, 0.0), axis=-1, keepdims=True)
    s1 = jnp.sum(jnp.where(lo, 0.0, sq), axis=-1, keepdims=True)
    inv = jnp.where(lo, lax.rsqrt(s0 * (1.0 / DH) + RMS_EPS), lax.rsqrt(s1 * (1.0 / DH) + RMS_EPS))
    return seg * inv * g


def _kv_post_kernel(p_ref, kg_ref, kv_ref, win_ref):
    x = p_ref[...]
    kv_ref[:, 0:2 * KVW] = x[:, 0:2 * KVW]
    kv_ref[:, 2 * KVW:3 * KVW] = _pair_norm(x[:, 2 * KVW:3 * KVW], kg_ref[1:2, :])
    kv_ref[:, 3 * KVW:4 * KVW] = x[:, 3 * KVW:4 * KVW]
    win_ref[:, 0:KVW] = _pair_norm(x[:, 4 * KVW:5 * KVW], kg_ref[2:3, :])
    win_ref[:, KVW:2 * KVW] = x[:, 5 * KVW:6 * KVW]


def kv_post(proj, k_g):
    m = proj.shape[0]
    tm = min(m, 1024)
    kvcols = 6 * KVW
    assert O_KV % kvcols == 0
    return pl.pallas_call(
        _kv_post_kernel,
        grid=(m // tm,),
        in_specs=[pl.BlockSpec((tm, kvcols), lambda i: (i, O_KV // kvcols)),
                  pl.BlockSpec((3, KVW), lambda i: (0, 0))],
        out_specs=[pl.BlockSpec((tm, 4 * KVW), lambda i: (i, 0)),
                   pl.BlockSpec((tm, 2 * KVW), lambda i: (i, 0))],
        out_shape=[jax.ShapeDtypeStruct((m, 4 * KVW), F32), jax.ShapeDtypeStruct((m, 2 * KVW), F32)],
        compiler_params=_cp("parallel"),
        name="kv_post",
    )(proj, jnp.concatenate([k_g, k_g], axis=1))


def _q_norm_kernel(q_ref, g_ref, o_ref):
    for c in range(NH // 2):
        cs = slice(c * 2 * DH, (c + 1) * 2 * DH)
        o_ref[:, cs] = _pair_norm(q_ref[:, cs], g_ref[...]) * (DH ** -0.5)


def q_norm(proj, q_g):
    m = proj.shape[0]
    tm = min(m, 1024)
    return pl.pallas_call(
        _q_norm_kernel,
        grid=(m // tm,),
        in_specs=[pl.BlockSpec((tm, NH * DH), lambda i: (i, O_Q // (NH * DH))),
                  pl.BlockSpec((1, 2 * DH), lambda i: (0, 0))],
        out_specs=pl.BlockSpec((tm, NH * DH), lambda i: (i, 0)),
        out_shape=jax.ShapeDtypeStruct((m, NH * DH), F32),
        compiler_params=_cp("parallel"),
        name="q_norm",
    )(proj, jnp.concatenate([q_g, q_g]).reshape(1, 2 * DH))


def _compress_kernel(*refs, n_prefetch, n_src, rows, nch):
    refs = refs[n_prefetch:]
    srcs = (refs[:n_src], refs[n_src:2 * n_src])
    pef_ref, pes_ref, w1f_ref, w1s_ref, b1_ref, w2_ref, kg_ref, kc_ref, vc_ref, hf_ref, hs_ref = refs[2 * n_src:]
    j = pl.program_id(1)
    cpr = rows // CMP_STRIDE
    m = cpr * n_src

    @pl.when(j == 0)
    def _():
        hs_ref[:, nch:nch + 8, :] = jnp.zeros((2, 8, 2 * CMP_HIDDEN), F32)

    for kind in range(2):
        accf = jnp.zeros((m, 2 * CMP_HIDDEN), F32)
        accs = jnp.zeros((m, 2 * CMP_HIDDEN), F32)
        for s in range(CMP_STRIDE):
            xs = jnp.concatenate(
                [r[0, pl.ds(s, cpr, stride=CMP_STRIDE), :] for r in srcs[kind]], axis=0)
            accf = accf + _dot((xs + pef_ref[kind, s:s + 1, :]).astype(BF), w1f_ref[kind, s])
            accs = accs + _dot((xs + pes_ref[kind, s:s + 1, :]).astype(BF), w1s_ref[kind, s])
        row0 = pl.multiple_of(j * m, 8)
        hf_ref[kind, pl.ds(row0, m), :] = accf
        hs_ref[kind, pl.ds(row0, m), :] = accs

    @pl.when(j == pl.num_programs(1) - 1)
    def _():
        for kind in range(2):
            hid = jax.nn.gelu(hf_ref[kind, 0:nch, :] + hs_ref[kind, 1:nch + 1, :] + b1_ref[kind])
            out = _dot(hid.astype(BF), w2_ref[kind])
            if kind == 0:
                kc_ref[0] = _pair_norm(out, kg_ref[...]).astype(BF)
            else:
                vc_ref[0] = out.astype(BF)


def _compress_weights(pe, w1, b1, w2, kg0):
    eye = jnp.eye(HKV, dtype=F32)
    bd = lambda w: jnp.einsum('ab,ksdf->ksadbf', eye, w).reshape(2, CMP_STRIDE, KVW, 2 * CMP_HIDDEN).astype(BF)
    tile2 = lambda a: jnp.concatenate([a, a], axis=-1)
    w2bd = jnp.einsum('ab,kfd->kafbd', eye, w2).reshape(2, 2 * CMP_HIDDEN, KVW).astype(BF)
    return (tile2(pe[:, :CMP_STRIDE]), tile2(pe[:, CMP_STRIDE:]), bd(w1[:, :CMP_STRIDE]), bd(w1[:, CMP_STRIDE:]),
            tile2(b1).reshape(2, 1, 2 * CMP_HIDDEN), w2bd, tile2(kg0).reshape(1, KVW))


def _compress_common(n_prefetch, n_src, rows, nch, bsz, weights):
    wspecs = [pl.BlockSpec(w.shape, functools.partial(lambda nd, *a: (0,) * nd, w.ndim)) for w in weights]
    out_specs = [pl.BlockSpec((1, nch, KVW), lambda b, j, *a: (b, 0, 0))] * 2
    out_shape = [jax.ShapeDtypeStruct((bsz, nch, KVW), BF)] * 2
    scratch = [pltpu.VMEM((2, nch + 8, 2 * CMP_HIDDEN), F32)] * 2
    kern = functools.partial(_compress_kernel, n_prefetch=n_prefetch, n_src=n_src, rows=rows, nch=nch)
    return kern, wspecs, out_specs, out_shape, scratch


def compress_dense(proj3, weights):
    bsz, t, _ = proj3.shape
    rows = min(t, 2048)
    nch = t // CMP_STRIDE
    assert O_KV % KVW == 0
    kern, wspecs, out_specs, out_shape, scratch = _compress_common(0, 1, rows, nch, bsz, weights)
    src = lambda kind: pl.BlockSpec((1, rows, KVW), lambda b, j: (b, j, O_KV // KVW + kind))
    return pl.pallas_call(
        kern, grid=(bsz, t // rows),
        in_specs=[src(0), src(1)] + wspecs,
        out_specs=out_specs, out_shape=out_shape, scratch_shapes=scratch,
        compiler_params=_cp("parallel", "arbitrary"), name="compress_dense",
    )(proj3, proj3, *weights)


CMP_PAGES = 16
CMP_PE_ROWS = 16


def _compress_paged_kernel(*refs, npg, page, nch):
    pages = refs[1:1 + npg]
    pe2_ref, w1p_ref, b1_ref, w2_ref, kg_ref, kc_ref, vc_ref, hf_ref, hs_ref, xs_ref = refs[1 + npg:]
    j = pl.program_id(1)
    m = npg * page // CMP_STRIDE
    hid2 = 2 * CMP_HIDDEN

    @pl.when(j == 0)
    def _():
        hs_ref[:, nch:nch + 8, :] = jnp.zeros((2, 8, hid2), F32)

    for kind in range(2):
        for r in range(npg):
            xt = jnp.concatenate([pages[r][0, kind, h] for h in range(HKV)], axis=0)
            xs_ref[kind, r * page:(r + 1) * page, :] = xt.T
        acc = jnp.zeros((m + CMP_PE_ROWS, 2 * hid2), F32)
        for s2 in range(CMP_STRIDE // 2):
            xa = xs_ref[kind, pl.ds(2 * s2, m, stride=CMP_STRIDE), :]
            xb = xs_ref[kind, pl.ds(2 * s2 + 1, m, stride=CMP_STRIDE), :]
            lhs = jnp.concatenate([jnp.concatenate([xa, xb], axis=1).astype(BF), pe2_ref[kind, s2]], axis=0)
            acc = acc + _dot(lhs, w1p_ref[kind, s2])
        row0 = pl.multiple_of(j * m, 8)
        hf_ref[kind, pl.ds(row0, m), :] = acc[0:m, 0:hid2] + acc[m:m + 1, 0:hid2]
        hs_ref[kind, pl.ds(row0, m), :] = acc[0:m, hid2:] + acc[m + 1:m + 2, hid2:]

    @pl.when(j == pl.num_programs(1) - 1)
    def _():
        for kind in range(2):
            hid = jax.nn.gelu(hf_ref[kind, 0:nch, :] + hs_ref[kind, 1:nch + 1, :] + b1_ref[kind])
            out = _dot(hid.astype(BF), w2_ref[kind])
            if kind == 0:
                kc_ref[0] = _pair_norm(out, kg_ref[...]).astype(BF)
            else:
                vc_ref[0] = out.astype(BF)


def compress_paged(cachet, page_table, weights):
    bsz, n_pages = page_table.shape
    page = cachet.shape[-1]
    npg = min(CMP_PAGES, n_pages)
    nch = n_pages * page // CMP_STRIDE
    src_specs = [pl.BlockSpec((1, 2, HKV, DH, page),
                              functools.partial(lambda r, b, j, pt: (pt[b, j * npg + r], 0, 0, 0, 0), r))
                 for r in range(npg)]
    pef, pes, w1f, w1s, b1, w2, kg = weights
    np2 = CMP_STRIDE // 2
    pair_rows = lambda a: a.reshape(2, np2, 2 * KVW)
    pe2 = jnp.zeros((2, np2, CMP_PE_ROWS, 2 * KVW), F32).at[:, :, 0].set(pair_rows(pef)).at[:, :, 1].set(
        pair_rows(pes)).astype(BF)
    pair_w = lambda w: w.reshape(2, np2, 2 * KVW, 2 * CMP_HIDDEN)
    w1p = jnp.concatenate([pair_w(w1f), pair_w(w1s)], axis=-1)
    weights = (pe2, w1p, b1, w2, kg)
    _, wspecs, out_specs, out_shape, scratch = _compress_common(1, npg, page, nch, bsz, weights)
    return pl.pallas_call(
        functools.partial(_compress_paged_kernel, npg=npg, page=page, nch=nch),
        grid_spec=pltpu.PrefetchScalarGridSpec(
            num_scalar_prefetch=1, grid=(bsz, n_pages // npg), in_specs=src_specs + wspecs,
            out_specs=out_specs, scratch_shapes=scratch + [pltpu.VMEM((2, npg * page, KVW), F32)]),
        out_shape=out_shape, compiler_params=_cp("parallel", "arbitrary"), name="compress_paged",
    )(page_table, *([cachet] * npg), *weights)


NS_PAD = 256
SEL_HALF = 128
LOG2E = math.log2(math.e)
PEN = 30000.0
KT = 256
SEL_SPLIT = 2
VROWS = 80
HALF_TILES = SEL_HALF * SEL_BLOCK // KT
CBAND = 24


def _t5_bucket(dist):
    n = jnp.maximum(dist, 0)
    large = REL_EXACT + (jnp.log(jnp.maximum(n, 1).astype(F32) / REL_EXACT)
                         / math.log(REL_MAX_DIST / REL_EXACT) * (REL_BUCKETS - REL_EXACT)).astype(jnp.int32)
    return jnp.where(n < REL_EXACT, n, jnp.minimum(large, REL_BUCKETS - 1))


def _selection_matrix(nc, ns_pad, nc_pad):
    ratio, span = SEL_BLOCK // CMP_STRIDE, CMP_LEN // CMP_STRIDE
    a = np.zeros((ns_pad, nc_pad), np.float32)
    for j in range(ns_pad):
        for mm in range(ratio):
            for nn in range(span):
                n = ratio * j + mm - nn
                if 0 <= n < nc:
                    a[j, n] += 1.0
    return a


def _split3(x):
    hi = x.astype(BF)
    r1 = x - hi.astype(F32)
    mid = r1.astype(BF)
    lo = (r1 - mid.astype(F32)).astype(BF)
    return hi, mid, lo


def _top_blocks(s, qp, n_rounds):
    j = lax.broadcasted_iota(jnp.int32, s.shape, 0)
    cur = qp // SEL_BLOCK
    forced = (j == 0) | (j == cur) | (j == cur - 1)
    valid = j * SEL_BLOCK <= qp
    s = jnp.where(forced, BIG, s)
    s = jnp.where(valid, s, -BIG)
    for _ in range(n_rounds):
        mx = jnp.max(s, axis=0, keepdims=True)
        jm = jnp.min(jnp.where(s == mx, j, 2 * NS_PAD), axis=0, keepdims=True)
        s = jnp.where(j == jm, -3e38, s)
    return jnp.where(s < -2 * BIG, 1.0, 0.0)


def _masked_softmax_rows(s, axis):
    m = jnp.max(s, axis=axis, keepdims=True)
    e = jnp.exp(s - m)
    den = jnp.maximum(jnp.sum(e, axis=axis, keepdims=True), 1e-30)
    return e * jnp.where(m > 0.5 * NEG, 1.0 / den, 0.0)


def _nsa_prompt_kernel(q_ref, gate_ref, qg_ref, kc_ref, vct_ref, amat_ref, kaug_ref, vselt_ref, kwin_ref,
                       vwint_ref, bc_ref, bs_ref, bw_ref, o_ref, lg_ref, acc_ref, qa_ref, qh_ref, sa_ref, sb_ref,
                       m_ref, mx_ref, *, ncp, n_top):
    i = pl.program_id(0)
    qb = Q_BLOCK
    rows = GRP * qb
    start = i * qb
    qall = q_ref[...]
    lane = lax.broadcasted_iota(jnp.int32, (qb, 2 * DH), 1)
    qpos_row = start + lax.broadcasted_iota(jnp.int32, (1, qb), 1)
    t_last = start // KT
    n_far = jnp.maximum(t_last - 1, 0)

    @pl.when(i == 0)
    def _():
        lg_ref[:, 0:16, :] = jnp.zeros((HKV, 16, rows), F32)

    xfs = []
    for h in range(HKV):
        parts = []
        for g in range(GRP):
            hd = h * GRP + g
            slab = qall[:, (hd // 2) * 2 * DH:(hd // 2 + 1) * 2 * DH]
            mine = (lane >= DH) if hd % 2 else (lane < DH)
            ss = jnp.sum(jnp.where(mine, slab * slab, 0.0), axis=-1, keepdims=True)
            xn = jnp.where(mine, slab * lax.rsqrt(ss * (1.0 / DH) + RMS_EPS) * qg_ref[...] * (DH ** -0.5), 0.0)
            if hd % 2 != h:
                xn = pltpu.roll(xn, DH, 1)
            parts.append(xn)
        xf = jnp.concatenate(parts, axis=0)
        xfs.append(xf)
        qh_ref[h] = xf.astype(BF)

    def compressed(n):
        outs = []
        for h in range(HKV):
            lg_ref[h, 16:16 + n, :] = _dot_nt(kc_ref[0:n, :], qh_ref[h])
            band0 = pl.multiple_of(i * 8, 8)
            lg_ref[h, pl.ds(band0, CBAND), :] = lg_ref[h, pl.ds(band0, CBAND), :] + bc_ref[h]
            tok = lax.broadcasted_iota(jnp.int32, (n, 1), 0)
            p_c = _masked_softmax_rows(jnp.where(tok < 8 * i + 8, lg_ref[h, 16:16 + n, :], NEG), 0)
            o_c = _dot(vct_ref[h, :, 0:n], p_c.astype(BF))
            imp = p_c[:, 0:qb]
            for g in range(1, GRP):
                imp = imp + p_c[:, g * qb:(g + 1) * qb]
            sc3 = _dot(amat_ref[:, 0:n], jnp.concatenate(_split3(imp), axis=1))
            outs += [o_c, sc3[:, 0:qb] + sc3[:, qb:2 * qb] + sc3[:, 2 * qb:]]
        return tuple(outs)

    cch = min(ncp, 2 * Q_BLOCK)
    nbr = ncp // cch
    outs = lax.switch(jnp.minimum((8 * i + 7) // cch, nbr - 1),
                      [functools.partial(compressed, (k + 1) * cch) for k in range(nbr)])
    o_cs = [outs[2 * h] for h in range(HKV)]
    sel = _top_blocks(jnp.concatenate([outs[2 * h + 1] for h in range(HKV)], axis=1),
                      jnp.concatenate([qpos_row] * HKV, axis=1), n_top)
    for h in range(HKV):
        pen = ((sel[:, h * qb:(h + 1) * qb].T - 1.0) * PEN).astype(BF)
        qh2 = (xfs[h] * LOG2E).astype(BF)
        qa_ref[h, 0] = jnp.concatenate([jnp.concatenate([pen[:, 0:SEL_HALF]] * GRP, axis=0), qh2], axis=1)
        qa_ref[h, 1] = jnp.concatenate([jnp.concatenate([pen[:, SEL_HALF:]] * GRP, axis=0), qh2], axis=1)

    hr = rows // SEL_SPLIT
    chains = [(h, slice(sp * hr, (sp + 1) * hr)) for h in range(HKV) for sp in range(SEL_SPLIT)]
    nchain = len(chains)

    def scores(c, t):
        h, rs = chains[c]
        qa = jnp.where(t < HALF_TILES, qa_ref[h, 0, rs, :], qa_ref[h, 1, rs, :])
        return _dot_nt(kaug_ref[t + 1], qa)

    def absorb(c, s, mx, t, m):
        h, rs = chains[c]
        m2 = jnp.maximum(m, mx)
        p = jnp.exp2(s - m2)
        acc_ref[h, :, rs] = jnp.exp2(m - m2) * acc_ref[h, :, rs] + _dot(vselt_ref[h, t + 1], p.astype(BF))
        return m2

    col_max = lambda s: jnp.max(s, axis=0, keepdims=True)

    def pair_body(k, carry):
        ms, mxa = list(carry[:nchain]), list(carry[nchain:])
        mxb = []
        for c, (h, rs) in enumerate(chains):
            s1 = scores(c, 2 * k + 1)
            sb_ref[h, :, rs] = s1
            mxb.append(col_max(s1))
            ms[c] = absorb(c, sa_ref[h, :, rs], mxa[c], 2 * k, ms[c])
        for c, (h, rs) in enumerate(chains):
            s2 = scores(c, 2 * k + 2)
            ms[c] = absorb(c, sb_ref[h, :, rs], mxb[c], 2 * k + 1, ms[c])
            sa_ref[h, :, rs] = s2
            mxa[c] = col_max(s2)
        return tuple(ms) + tuple(mxa)

    acc_ref[...] = jnp.zeros((HKV, VROWS, rows), F32)
    mxa = []
    for c, (h, rs) in enumerate(chains):
        s0 = scores(c, 0)
        sa_ref[h, :, rs] = s0
        mxa.append(col_max(s0))
    carry = tuple(jnp.full((1, hr), NEG, F32) for _ in range(nchain)) + tuple(mxa)
    quad_body = lambda k, cr: pair_body(2 * k + 1, pair_body(2 * k, cr))
    oct_body = lambda k, cr: quad_body(2 * k + 1, quad_body(2 * k, cr))
    carry = lax.fori_loop(0, n_far // 8, oct_body, carry)
    carry = lax.fori_loop(2 * (n_far // 8), n_far // 4, quad_body, carry)
    carry = lax.fori_loop(2 * (n_far // 4), n_far // 2, pair_body, carry)
    for c, (h, rs) in enumerate(chains):
        m_ref[h, :, rs] = carry[c]
        mx_ref[h, :, rs] = carry[nchain + c]

    @pl.when(n_far % 2 == 1)
    def _():
        for c, (h, rs) in enumerate(chains):
            m_ref[h, :, rs] = absorb(c, sa_ref[h, :, rs], mx_ref[h, :, rs], n_far - 1, m_ref[h, :, rs])

    kiota = lax.broadcasted_iota(jnp.int32, (KT, 1), 0)
    ms = [m_ref[h, :, rs] for h, rs in chains]
    for u in range(2):
        t = t_last - 1 + u
        for c, (h, rs) in enumerate(chains):
            s = jnp.where(t * KT + kiota >= 0, scores(c, t) + bs_ref[i % 2, h, u, :, rs], NEG)
            ms[c] = absorb(c, s, col_max(s), t, ms[c])

    gates_t = jax.nn.sigmoid(gate_ref[...]).T
    wkeys = WINDOW + qb
    for h in range(HKV):
        o_s = acc_ref[h, 0:DH, :] / jnp.maximum(acc_ref[h, DH:DH + 1, :], 1e-30)

        s = _dot_nt(kwin_ref[pl.ds(pl.multiple_of(start, qb), wkeys), :], qh_ref[h]) + bw_ref[h]
        kpos = start - WINDOW + lax.broadcasted_iota(jnp.int32, (wkeys, 1), 0)
        s = jnp.where(kpos >= 0, s, NEG)
        e = jnp.exp(s - jnp.max(s, axis=0, keepdims=True))
        vw = jnp.concatenate([vwint_ref[h, i + u] for u in range(wkeys // qb)], axis=1)
        ow = _dot(vw, e.astype(BF))
        o_w = ow[0:DH] / jnp.maximum(ow[DH:DH + 1], 1e-30)

        gate = lambda br: jnp.concatenate(
            [gates_t[(h * GRP + g) * 3 + br:(h * GRP + g) * 3 + br + 1, :] for g in range(GRP)], axis=1)
        o_t = gate(0) * o_cs[h] + gate(1) * o_s + gate(2) * o_w
        for g in range(GRP):
            o_ref[:, (h * GRP + g) * DH:(h * GRP + g + 1) * DH] = o_t[:, g * qb:(g + 1) * qb].T


def _bias_tables(rel_table):
    tab = rel_table.astype(F32)
    far = tab[REL_BUCKETS - 1]
    qb = Q_BLOCK

    def toeplitz(off, nk, dmax, shift):
        d = off - (nk - 1) + jnp.arange(nk + qb - 1, dtype=jnp.int32)
        w = jnp.where(((d >= 0) & (d <= dmax))[None, :], (tab[_t5_bucket(d)] - (far if shift else 0.0)).T, NEG)
        p = w.shape[1]
        hank = jnp.tile(w, (1, nk + 1))[:, :nk * (p + 1)].reshape(NH, nk, p + 1)[:, :, :qb]
        return hank[:, ::-1, :]

    def lanes_gq(b):
        return b.reshape(HKV, GRP, b.shape[1], qb).transpose(0, 2, 1, 3).reshape(HKV, b.shape[1], GRP * qb)

    big = 1 << 30
    bc = lanes_gq(toeplitz(16 * CMP_STRIDE - (CMP_LEN - 1), CBAND * CMP_STRIDE, big, True)[:, ::CMP_STRIDE, :])
    bs = jnp.stack([lanes_gq(toeplitz(off, 2 * KT, big, True)).reshape(HKV, 2, KT, GRP * qb)
                    for off in (KT, KT + qb)]) * LOG2E
    bw = lanes_gq(toeplitz(WINDOW, WINDOW + qb, WINDOW, False))
    return bc, bs, bw


def nsa_prompt(proj, kc, vc, ksel, vsel, kwin, vwin, q_g, rel_table):
    t = proj.shape[0]
    nb = t // Q_BLOCK
    nch = kc.shape[0]
    ncp = -(-nch // 128) * 128
    assert t // SEL_BLOCK <= NS_PAD and t % KT == 0
    n_top = min(TOP_N, t // SEL_BLOCK)
    kcp = jnp.pad(kc, ((0, ncp - nch), (0, 0)))
    vct = jnp.pad(vc, ((0, ncp - nch), (0, 0))).reshape(ncp, HKV, DH).transpose(1, 2, 0)
    amat = jnp.asarray(_selection_matrix(nch - 1, NS_PAD, ncp), BF)

    def values_t(v, pad_rows, tile):
        vt = jnp.pad(v.astype(BF), ((pad_rows, 0), (0, 0))).reshape(-1, tile, HKV, DH).transpose(2, 0, 3, 1)
        return jnp.concatenate([vt, jnp.ones(vt.shape[:2] + (1, tile), BF),
                                jnp.zeros(vt.shape[:2] + (VROWS - DH - 1, tile), BF)], axis=2)

    onehot = ((jnp.arange(t, dtype=jnp.int32)[:, None] // SEL_BLOCK) % SEL_HALF
              == jnp.arange(SEL_HALF, dtype=jnp.int32)[None, :]).astype(BF)
    kaug = jnp.pad(jnp.concatenate([onehot, ksel.astype(BF)], axis=1), ((KT, 0), (0, 0))).reshape(-1, KT, 2 * KVW)
    vselt = values_t(vsel, KT, KT)
    vwint = values_t(vwin, WINDOW, Q_BLOCK)
    rows = GRP * Q_BLOCK
    kwinp = jnp.pad(kwin.astype(BF), ((WINDOW, 0), (0, 0)))
    bc, bs, bw = _bias_tables(rel_table)
    whole = pl.BlockSpec(memory_space=pltpu.VMEM)
    return pl.pallas_call(
        functools.partial(_nsa_prompt_kernel, ncp=ncp, n_top=n_top),
        grid=(nb,),
        in_specs=[pl.BlockSpec((Q_BLOCK, NH * DH), lambda i: (i, O_Q // (NH * DH))),
                  pl.BlockSpec((Q_BLOCK, 128), lambda i: (i, O_GATE // 128)),
                  whole, whole, whole, whole, whole, whole, whole, whole, whole, whole, whole],
        out_specs=pl.BlockSpec((Q_BLOCK, NH * DH), lambda i: (i, 0)),
        out_shape=jax.ShapeDtypeStruct((t, NH * DH), F32),
        scratch_shapes=[pltpu.VMEM((HKV, 16 + ncp, rows), F32), pltpu.VMEM((HKV, VROWS, rows), F32),
                        pltpu.VMEM((HKV, 2, rows, SEL_HALF + KVW), BF), pltpu.VMEM((HKV, rows, KVW), BF),
                        pltpu.VMEM((HKV, KT, rows), F32), pltpu.VMEM((HKV, KT, rows), F32),
                        pltpu.VMEM((HKV, 1, rows), F32), pltpu.VMEM((HKV, 1, rows), F32)],
        compiler_params=_cp("arbitrary"),
        name="nsa_prompt",
    )(proj, proj, jnp.concatenate([q_g, q_g]).reshape(1, 2 * DH), kcp, vct, amat, kaug, vselt, kwinp, vwint,
      bc, bs, bw)


def _pick_head(x, lane_h):
    return jnp.where(lane_h == 0, x[0:DH], x[DH:2 * DH])


def _nsa_dec_pre_kernel(qbd_ref, kc_ref, vc_ref, amat_ref, bc_ref, win_ref, wnew_ref, bwa_ref, bwb_ref,
                        pen_ref, oc_ref, ow_ref, *, ts, pos0, n_top):
    r_all = GRP * HKV * ts
    qbd = qbd_ref[0]
    lane_h = (lax.broadcasted_iota(jnp.int32, (1, r_all), 1) // ts) % HKV

    p_c = _masked_softmax_rows(_dot(kc_ref[0], qbd) + bc_ref[...], 0)
    oc_ref[0] = _pick_head(_dot_tn(vc_ref[0], p_c.astype(BF)), lane_h)

    hi, mid, lo = _split3(p_c)
    amat = amat_ref[...]
    sc = _dot(amat, hi) + _dot(amat, mid) + _dot(amat, lo)
    w8 = HKV * ts
    score = sc[:, 0:w8]
    for g in range(1, GRP):
        score = score + sc[:, g * w8:(g + 1) * w8]
    qp = pos0 + lax.broadcasted_iota(jnp.int32, (1, w8), 1) % ts
    sel = _top_blocks(score, qp, n_top)
    pen_ref[0] = (jnp.concatenate([sel] * GRP, axis=1) - 1.0) * PEN

    win = win_ref[0]
    wnew = wnew_ref[0]
    s_a = _dot(win[:, 0:KVW].astype(BF), qbd) + bwa_ref[...]
    s_b = _dot(wnew[:, 0:KVW].astype(BF), qbd) + bwb_ref[...]
    m = jnp.maximum(jnp.max(s_a, axis=0, keepdims=True), jnp.max(s_b, axis=0, keepdims=True))
    e_a = jnp.where(s_a > 0.5 * NEG, jnp.exp(s_a - m), 0.0)
    e_b = jnp.where(s_b > 0.5 * NEG, jnp.exp(s_b - m), 0.0)
    den = jnp.maximum(jnp.sum(e_a, axis=0, keepdims=True) + jnp.sum(e_b, axis=0, keepdims=True), 1e-30)
    o_w = (_dot_tn(win[:, KVW:].astype(BF), e_a.astype(BF)) + _dot_tn(wnew[:, KVW:].astype(BF), e_b.astype(BF)))
    ow_ref[0] = _pick_head(o_w, lane_h) / den


SEL_PAGES = 32


def _nsa_dec_sel_kernel(*refs, npg, page, ts):
    pages = refs[1:1 + npg]
    (q_ref, pen_ref, expt_ref, blast_ref, knew_ref, vnew_ref, bnew_ref, gate_ref, oc_ref, ow_ref,
     o_ref, m_ref, l_ref, acc_ref) = refs[1 + npg:]
    j = pl.program_id(1)
    last = pl.num_programs(1) - 1
    rows = GRP * ts

    @pl.when(j == 0)
    def _():
        m_ref[...] = jnp.full((HKV * rows, 1), NEG, F32)
        l_ref[...] = jnp.zeros((HKV * rows, 1), F32)
        acc_ref[...] = jnp.zeros((HKV * rows, DH), F32)

    def online(s, pvs):
        m = m_ref[...]
        m2 = jnp.maximum(m, jnp.max(s, axis=-1, keepdims=True))
        a = jnp.exp(m - m2)
        p = jnp.exp(s - m2)
        m_ref[...] = m2
        l_ref[...] = a * l_ref[...] + jnp.sum(p, axis=-1, keepdims=True)
        pb = p.astype(BF)
        pv = jnp.concatenate([pvs[h](pb[h * rows:(h + 1) * rows]) for h in range(HKV)], axis=0)
        acc_ref[...] = a * acc_ref[...] + pv

    both = lambda f: jnp.concatenate([f(h) for h in range(HKV)], axis=0)
    vts = [jnp.concatenate([r[0, 1, h] for r in pages], axis=1).astype(BF) for h in range(HKV)]
    s = both(lambda h: _dot(q_ref[0, h], jnp.concatenate([r[0, 0, h] for r in pages], axis=1).astype(BF))
             + _dot(pen_ref[0, h, j], expt_ref[...]))
    s = s + jnp.where(j == last, blast_ref[...], 0.0)
    online(s, [functools.partial(lambda vt, p: _dot_nt(p, vt), vts[h]) for h in range(HKV)])

    @pl.when(j == last)
    def _():
        s_new = both(lambda h: _dot_nt(q_ref[0, h], knew_ref[0, h].astype(BF))) + bnew_ref[...]
        online(s_new, [functools.partial(lambda v, p: _dot(p, v), vnew_ref[0, h].astype(BF)) for h in range(HKV)])
        o_s = acc_ref[...] / jnp.maximum(l_ref[...], 1e-30)
        g = jax.nn.sigmoid(gate_ref[0])
        o_ref[0] = g[:, 0:1] * oc_ref[0] + g[:, 1:2] * o_s + g[:, 2:3] * ow_ref[0]


def _dec_bias_tables(rel_table, pos0, ts, nch, wb, page, npg):
    tab = rel_table.astype(F32)
    far = tab[REL_BUCKETS - 1]
    r = np.arange(GRP * HKV * ts)
    head = ((r // ts) % HKV) * GRP + r // (HKV * ts)
    qpos = pos0 + r % ts
    far_r = far[head][None, :]

    def bias(kpos, dmax, shift):
        dist = qpos[None, :] - np.asarray(kpos)[:, None]
        ok = (dist >= 0) & (dist <= dmax)
        out = jnp.where(ok, 0.0 if shift else far_r, NEG)
        near = np.nonzero((ok & (dist < REL_MAX_DIST)).any(axis=1))[0]
        if near.size:
            lo, hi = int(near.min()), int(near.max()) + 1
            b = tab[_t5_bucket(jnp.asarray(dist[lo:hi], jnp.int32)), head[None, :]] - (far_r if shift else 0.0)
            out = jnp.concatenate([out[:lo], jnp.where(ok[lo:hi], b, NEG), out[hi:]], axis=0)
        return out

    big = 1 << 30
    bc = bias(np.arange(nch) * CMP_STRIDE + CMP_LEN - 1, big, False)
    bwa = bias(pos0 - wb + np.arange(wb), WINDOW, False)
    tnew = np.arange(8)
    newpos = np.where(tnew < ts, pos0 + tnew, pos0 + 2 * WINDOW + SEL_BLOCK)
    bwb = bias(newpos, WINDOW, False)
    bnew = bias(newpos, big, True)
    step = npg * page
    blast = jnp.concatenate([jnp.zeros((step - page, r.shape[0]), F32), bias(pos0 - page + np.arange(page), big, True)],
                            axis=0)
    return bc, bwa, bwb, bnew, blast


def nsa_decode(qn, gate_raw, kc, vc, cachet, page_table, win_state, kv_new, win_new, rel_table):
    bsz, ts, _ = qn.shape
    n_pages = page_table.shape[1]
    page = cachet.shape[-1]
    pos0 = n_pages * page
    nch = kc.shape[1]
    wb = win_state.shape[1]
    r_all = GRP * HKV * ts
    assert pos0 % SEL_BLOCK == 0 and ts <= 8 and ts <= SEL_BLOCK and nch % 8 == 0
    ns = -(-(pos0 + ts) // SEL_BLOCK)
    npg = min(SEL_PAGES, n_pages)
    assert n_pages % npg == 0
    bps = npg * page // SEL_BLOCK
    ns_pad = max(-(-ns // 8) * 8, (n_pages // npg) * bps)
    n_top = min(TOP_N, ns)

    q5 = qn.reshape(bsz, ts, HKV, GRP, DH)
    qbd = jnp.einsum('bqhgd,hk->bkdghq', q5, jnp.eye(HKV, dtype=F32)).reshape(bsz, 2 * DH, r_all).astype(BF)
    gate_t = gate_raw.reshape(bsz, ts, HKV, GRP, 3).transpose(0, 4, 3, 2, 1).reshape(bsz, 3, r_all)
    pad8 = lambda a: jnp.pad(a, ((0, 0), (0, 8 - ts), (0, 0)))
    amat = jnp.asarray(_selection_matrix(nch - 1, ns_pad, nch), BF)
    expand = jnp.asarray(np.repeat(np.eye(bps, dtype=np.float32), SEL_BLOCK, axis=0), BF)
    bc, bwa, bwb, bnew, blast = _dec_bias_tables(rel_table, pos0, ts, nch, wb, page, npg)

    full = lambda a: pl.BlockSpec(a.shape, functools.partial(lambda nd, *_: (0,) * nd, a.ndim))
    per_b = lambda a: pl.BlockSpec((1,) + a.shape[1:], functools.partial(lambda nd, b, *_: (b,) + (0,) * nd, a.ndim - 1))
    wnew = pad8(win_new)
    pre_in = [qbd, kc, vc, amat, bc, win_state, wnew, bwa, bwb]
    pre_specs = [per_b(qbd), per_b(kc), per_b(vc), full(amat), full(bc), per_b(win_state), per_b(wnew),
                 full(bwa), full(bwb)]
    small = jax.ShapeDtypeStruct((bsz, DH, r_all), F32)
    pen, o_c, o_w = pl.pallas_call(
        functools.partial(_nsa_dec_pre_kernel, ts=ts, pos0=pos0, n_top=n_top),
        grid=(bsz,), in_specs=pre_specs,
        out_specs=[pl.BlockSpec((1, ns_pad, r_all), lambda b: (b, 0, 0)),
                   pl.BlockSpec((1, DH, r_all), lambda b: (b, 0, 0)),
                   pl.BlockSpec((1, DH, r_all), lambda b: (b, 0, 0))],
        out_shape=[jax.ShapeDtypeStruct((bsz, ns_pad, r_all), F32), small, small],
        compiler_params=_cp("parallel"), name="nsa_dec_pre",
    )(*pre_in)

    rows = GRP * ts
    nsteps = n_pages // npg
    by_head = lambda a: a.reshape(a.shape[:-1] + (GRP, HKV, ts))
    q_h = q5.transpose(0, 2, 3, 1, 4).reshape(bsz, HKV, rows, DH).astype(BF)
    gate_h = gate_raw.reshape(bsz, ts, HKV, GRP, 3).transpose(0, 2, 3, 1, 4).reshape(bsz, HKV, rows, 3)
    pen_h = by_head(pen[:, :nsteps * bps].reshape(bsz, nsteps, bps, r_all)).transpose(0, 4, 1, 3, 5, 2).reshape(
        bsz, HKV, nsteps, rows, bps).astype(BF)
    oc_h, ow_h = (by_head(a).transpose(0, 3, 2, 4, 1).reshape(bsz, HKV, rows, DH) for a in (o_c, o_w))
    blast_h, bnew_h = (by_head(a).transpose(2, 1, 3, 0).reshape(HKV, rows, a.shape[0]) for a in (blast, bnew))
    new5 = pad8(kv_new).reshape(bsz, 8, 4, HKV, DH)
    knew = new5[:, :, 2].transpose(0, 2, 1, 3)
    vnew = new5[:, :, 3].transpose(0, 2, 1, 3)
    page_specs = [pl.BlockSpec((1, 2, HKV, DH, page),
                               functools.partial(lambda r, b, j, pt: (pt[b, j * npg + r], 1, 0, 0, 0), r))
                  for r in range(npg)]
    heads_rows = lambda a: a.reshape(a.shape[:-3] + (HKV * rows, a.shape[-1]))
    blast_h, bnew_h, gate_h, oc_h, ow_h = map(heads_rows, (blast_h, bnew_h, gate_h, oc_h, ow_h))
    sel_in = [q_h, pen_h, expand.T, blast_h, knew, vnew, bnew_h, gate_h, oc_h, ow_h]
    sel_specs = [per_b(q_h), per_b(pen_h), full(expand.T), full(blast_h), per_b(knew), per_b(vnew), full(bnew_h),
                 per_b(gate_h), per_b(oc_h), per_b(ow_h)]
    o_h = pl.pallas_call(
        functools.partial(_nsa_dec_sel_kernel, npg=npg, page=page, ts=ts),
        grid_spec=pltpu.PrefetchScalarGridSpec(
            num_scalar_prefetch=1, grid=(bsz, nsteps), in_specs=page_specs + sel_specs,
            out_specs=pl.BlockSpec((1, HKV * rows, DH), lambda b, j, pt: (b, 0, 0)),
            scratch_shapes=[pltpu.VMEM((HKV * rows, 1), F32), pltpu.VMEM((HKV * rows, 1), F32),
                            pltpu.VMEM((HKV * rows, DH), F32)]),
        out_shape=jax.ShapeDtypeStruct((bsz, HKV * rows, DH), F32),
        compiler_params=_cp("parallel", "arbitrary"), name="nsa_dec_sel",
    )(page_table, *([cachet] * npg), *sel_in)
    return o_h.reshape(bsz, HKV, GRP, ts, DH).transpose(0, 3, 1, 2, 4).reshape(bsz, ts, NH * DH)


def _even_weights(p, e):
    w_in = jnp.pad(p['att_w_in'][e], ((0, 0), (0, ATT_IN_PAD - ATT_IN_COLS))).astype(BF)
    w_out = p['att_w_out'][e].astype(BF)
    return dict(
        w_in=w_in, w_out_conv=w_out[:C_CONV], w_out_att=w_out[C_CONV:],
        cmp=_compress_weights(p['cmp_pe'][e], p['cmp_w1'][e], p['cmp_b1'][e], p['cmp_w2'][e], p['k_norm_g'][e][0]),
        k_g=p['k_norm_g'][e], q_g=p['q_norm_g'][e], conv_w=p['conv_w'][e], conv_b=p['conv_b'][e],
        ln_g=p['conv_ln_g'][e], ln_b=p['conv_ln_b'][e])


def _even_prompt(x2, norm_g, w, rel_table):
    t = x2.shape[0]
    proj = norm_matmul(x2, norm_g, w['w_in'])
    conv_y, conv_new = conformer_conv(proj[None], jnp.zeros((1, CONV_WIDTH - 1, C_CONV), F32),
                                      w['conv_w'], w['conv_b'], w['ln_g'], w['ln_b'])
    kv_new, win_new = kv_post(proj, w['k_g'])
    kc, vc = compress_dense(proj[None], w['cmp'])
    o = nsa_prompt(proj, kc[0], vc[0], kv_new[:, 2 * KVW:3 * KVW], kv_new[:, 3 * KVW:], win_new[:, :KVW],
                   win_new[:, KVW:], w['q_g'], rel_table)
    x2 = out_proj2(x2, conv_y[0], o, w['w_out_conv'], w['w_out_att'])
    keep = min(WINDOW, t)
    return (x2, kv_new.reshape(1, t, 4, HKV, DH), win_new[t - keep:].reshape(1, keep, 2, HKV, DH), conv_new)


def _even_decode(x3, norm_g, w, rel_table, cachet, page_table, win_state, conv_state):
    bsz, t, d = x3.shape
    x2 = x3.reshape(bsz * t, d)
    proj = norm_matmul(x2, norm_g, w['w_in'])
    proj3 = proj.reshape(bsz, t, ATT_IN_PAD)
    conv_y, conv_new = conformer_conv(proj3, conv_state, w['conv_w'], w['conv_b'], w['ln_g'], w['ln_b'])
    kv_new, win_new = kv_post(proj, w['k_g'])
    qn = q_norm(proj, w['q_g'])
    kc, vc = compress_paged(cachet, page_table, w['cmp'])
    wb = win_state.shape[1]
    o = nsa_decode(qn.reshape(bsz, t, NH * DH), proj3[:, :, O_GATE:ATT_IN_COLS], kc, vc, cachet, page_table,
                   win_state.reshape(bsz, wb, 2 * KVW), kv_new.reshape(bsz, t, 4 * KVW),
                   win_new.reshape(bsz, t, 2 * KVW), rel_table)
    x2 = out_proj2(x2, conv_y.reshape(bsz * t, C_CONV), o.reshape(bsz * t, NH * DH), w['w_out_conv'],
                   w['w_out_att'])
    win_all = jnp.concatenate([win_state, win_new.reshape(bsz, t, 2, HKV, DH)], axis=1)
    keep = min(WINDOW, wb + t)
    return (x2.reshape(bsz, t, d), kv_new.reshape(bsz, t, 4, HKV, DH), win_all[:, wb + t - keep:], conv_new)


def _odd_layer(x3, s0, norm_g, w_in_bf, w_out_bf, lb, hg_norm_g):
    bsz, t, d = x3.shape
    x2 = x3.reshape(bsz * t, d)
    z = norm_matmul(x2, norm_g, w_in_bf).reshape(bsz, t, 4 * d)
    tp = -(-t // HG_CHUNK) * HG_CHUNK
    if tp != t:
        z = jnp.pad(z, ((0, 0), (0, tp - t), (0, 0)))
    o, s_new = hgrn2(z, s0, lb, hg_norm_g, t)
    x2 = out_proj1(x2, o[:, :t].reshape(bsz * t, d), w_out_bf)
    return x2.reshape(bsz, t, d), s_new


def kernel(x_prompt, x_sample, cache_nsa_kv, page_table, state_nsa_win, state_conv, state_hgrn, rel_bias_table,
           norm_mix_g, norm_mlp_g, w_mlp_up, w_mlp_down, att_w_in, att_w_out, q_norm_g, k_norm_g, cmp_pe, cmp_w1,
           cmp_b1, cmp_w2, conv_w, conv_b, conv_ln_g, conv_ln_b, hg_w_in, hg_w_out, hg_lb_logits, hg_norm_g):
    p = dict(att_w_in=att_w_in, att_w_out=att_w_out, q_norm_g=q_norm_g, k_norm_g=k_norm_g, cmp_pe=cmp_pe,
             cmp_w1=cmp_w1, cmp_b1=cmp_b1, cmp_w2=cmp_w2, conv_w=conv_w, conv_b=conv_b, conv_ln_g=conv_ln_g,
             conv_ln_b=conv_ln_b)
    bp, tp_, d = x_prompt.shape
    assert bp == 1
    db, ts, _ = x_sample.shape
    w_up = w_mlp_up.astype(BF)
    w_down = w_mlp_down.astype(BF)
    cum = jnp.cumsum(jax.nn.softmax(hg_lb_logits.astype(F32), axis=0), axis=0)

    xp = x_prompt[0]
    kv_p, win_p, conv_p, hg_p = [], [], [], []
    for layer in range(norm_mix_g.shape[0]):
        if layer % 2 == 0:
            e = layer // 2
            w = _even_weights(p, e)
            xp, kv_new, win_new, conv_new = _even_prompt(xp, norm_mix_g[layer], w, rel_bias_table)
            kv_p.append(kv_new)
            win_p.append(win_new)
            conv_p.append(conv_new)
        else:
            o = layer // 2
            x3, s_new = _odd_layer(xp[None], jnp.zeros((1, HG_HEADS, HG_DK, HG_DV), F32), norm_mix_g[layer],
                                   hg_w_in[o].astype(BF), hg_w_out[o].astype(BF), cum[layer] - cum[0],
                                   hg_norm_g[o])
            xp = x3[0]
            hg_p.append(s_new)
        xp = mlp(xp, norm_mlp_g[layer], w_up[layer], w_down[layer])

    cachet = cache_nsa_kv.transpose(0, 1, 3, 4, 5, 2)
    xs = x_sample
    kv_s, win_s, conv_s, hg_s = [], [], [], []
    for layer in range(norm_mix_g.shape[0]):
        if layer % 2 == 0:
            e = layer // 2
            w = _even_weights(p, e)
            xs, kv_new, win_new, conv_new = _even_decode(xs, norm_mix_g[layer], w, rel_bias_table, cachet[e],
                                                         page_table, state_nsa_win[e], state_conv[e])
            kv_s.append(kv_new)
            win_s.append(win_new)
            conv_s.append(conv_new)
        else:
            o = layer // 2
            xs, s_new = _odd_layer(xs, state_hgrn[o], norm_mix_g[layer], hg_w_in[o].astype(BF),
                                   hg_w_out[o].astype(BF), cum[layer] - cum[0], hg_norm_g[o])
            hg_s.append(s_new)
        xs = mlp(xs.reshape(db * ts, d), norm_mlp_g[layer], w_up[layer], w_down[layer]).reshape(db, ts, d)
    return (xp[None], xs, jnp.stack(kv_p), jnp.stack(kv_s), jnp.stack(win_p), jnp.stack(win_s),
            jnp.stack(conv_p), jnp.stack(conv_s), jnp.stack(hg_p), jnp.stack(hg_s))
```

```python
import functools
import math

import jax
import jax.numpy as jnp
import numpy as np
from jax import lax
from jax.experimental import pallas as pl
from jax.experimental.pallas import tpu as pltpu

D_MODEL = 1024
C_CONV = 512
CONV_WIDTH = 31
DH = 64
HKV = 2
GRP = 4
NH = HKV * GRP
CMP_STRIDE = 16
CMP_LEN = 32
CMP_HIDDEN = 128
SEL_BLOCK = 64
TOP_N = 16
WINDOW = 512
Q_BLOCK = 128
REL_BUCKETS = 32
REL_EXACT = 16
REL_MAX_DIST = 128
HG_DK = 128
HG_HEADS = 8
HG_DV = 128
HG_CHUNK = 64
RMS_EPS = 1e-6
NEG = -1e30
BIG = 1e9
O_Q = 2 * C_CONV
O_KV = O_Q + NH * DH
O_GATE = O_KV + 3 * 2 * HKV * DH
ATT_IN_COLS = O_GATE + 3 * NH
ATT_IN_PAD = 2432
KVW = HKV * DH

VMEM_LIMIT = 56 * 1024 * 1024
BF = jnp.bfloat16
F32 = jnp.float32


def _cp(*sem):
    return pltpu.CompilerParams(dimension_semantics=sem, vmem_limit_bytes=VMEM_LIMIT)


def _dot(a, b):
    return jnp.dot(a, b, preferred_element_type=F32)


def _dot_nt(a, b):
    return lax.dot_general(a, b, (((1,), (1,)), ((), ())), preferred_element_type=F32)


def _dot_tn(a, b):
    return lax.dot_general(a, b, (((0,), (0,)), ((), ())), preferred_element_type=F32)


def _rms(x, g):
    return x * lax.rsqrt(jnp.mean(x * x, axis=-1, keepdims=True) + RMS_EPS) * g


def _norm_matmul_kernel(x_ref, g_ref, w_ref, o_ref):
    hn = _rms(x_ref[...], g_ref[...]).astype(BF)
    o_ref[...] = _dot(hn, w_ref[...])


def norm_matmul(x, g, w_bf, tn=None):
    m, d = x.shape
    n = w_bf.shape[1]
    tm = min(m, 512)
    tn = n if tn is None else tn
    return pl.pallas_call(
        _norm_matmul_kernel,
        grid=(m // tm, n // tn),
        in_specs=[pl.BlockSpec((tm, d), lambda i, j: (i, 0)),
                  pl.BlockSpec((1, d), lambda i, j: (0, 0)),
                  pl.BlockSpec((d, tn), lambda i, j: (0, j))],
        out_specs=pl.BlockSpec((tm, tn), lambda i, j: (i, j)),
        out_shape=jax.ShapeDtypeStruct((m, n), F32),
        compiler_params=_cp("parallel", "arbitrary"),
        name="norm_matmul",
    )(x, g.reshape(1, d), w_bf)


def _out_proj_kernel(r_ref, a1_ref, a2_ref, w1_ref, w2_ref, o_ref):
    o_ref[...] = (r_ref[...] + _dot(a1_ref[...].astype(BF), w1_ref[...])
                  + _dot(a2_ref[...].astype(BF), w2_ref[...]))


def out_proj2(res, a1, a2, w1_bf, w2_bf):
    m, d = res.shape
    k1, k2 = a1.shape[1], a2.shape[1]
    tm = min(m, 512)
    return pl.pallas_call(
        _out_proj_kernel,
        grid=(m // tm,),
        in_specs=[pl.BlockSpec((tm, d), lambda i: (i, 0)),
                  pl.BlockSpec((tm, k1), lambda i: (i, 0)),
                  pl.BlockSpec((tm, k2), lambda i: (i, 0)),
                  pl.BlockSpec((k1, d), lambda i: (0, 0)),
                  pl.BlockSpec((k2, d), lambda i: (0, 0))],
        out_specs=pl.BlockSpec((tm, d), lambda i: (i, 0)),
        out_shape=jax.ShapeDtypeStruct((m, d), F32),
        compiler_params=_cp("parallel"),
        name="out_proj2",
    )(res, a1, a2, w1_bf, w2_bf)


def _out_proj1_kernel(r_ref, a_ref, w_ref, o_ref):
    o_ref[...] = r_ref[...] + _dot(a_ref[...].astype(BF), w_ref[...])


def out_proj1(res, a, w_bf):
    m, d = res.shape
    k = a.shape[1]
    tm = min(m, 512)
    return pl.pallas_call(
        _out_proj1_kernel,
        grid=(m // tm,),
        in_specs=[pl.BlockSpec((tm, d), lambda i: (i, 0)),
                  pl.BlockSpec((tm, k), lambda i: (i, 0)),
                  pl.BlockSpec((k, d), lambda i: (0, 0))],
        out_specs=pl.BlockSpec((tm, d), lambda i: (i, 0)),
        out_shape=jax.ShapeDtypeStruct((m, d), F32),
        compiler_params=_cp("parallel"),
        name="out_proj1",
    )(res, a, w_bf)


def _mlp_kernel(x_ref, g_ref, wu_ref, wd_ref, o_ref, hn_ref, acc_ref):
    j = pl.program_id(1)

    @pl.when(j == 0)
    def _():
        hn_ref[...] = _rms(x_ref[...], g_ref[...]).astype(BF)
        acc_ref[...] = x_ref[...]

    hid = jnp.maximum(_dot(hn_ref[...], wu_ref[...]), 0.0)
    acc_ref[...] += _dot((hid * hid).astype(BF), wd_ref[...])

    @pl.when(j == pl.num_programs(1) - 1)
    def _():
        o_ref[...] = acc_ref[...]


def mlp(x, g, wu_bf, wd_bf):
    m, d = x.shape
    hdim = wu_bf.shape[1]
    tm = min(m, 1024)
    th = 1024
    return pl.pallas_call(
        _mlp_kernel,
        grid=(m // tm, hdim // th),
        in_specs=[pl.BlockSpec((tm, d), lambda i, j: (i, 0)),
                  pl.BlockSpec((1, d), lambda i, j: (0, 0)),
                  pl.BlockSpec((d, th), lambda i, j: (0, j)),
                  pl.BlockSpec((th, d), lambda i, j: (j, 0))],
        out_specs=pl.BlockSpec((tm, d), lambda i, j: (i, 0)),
        out_shape=jax.ShapeDtypeStruct((m, d), F32),
        scratch_shapes=[pltpu.VMEM((tm, d), BF), pltpu.VMEM((tm, d), F32)],
        compiler_params=_cp("parallel", "arbitrary"),
        name="mlp",
    )(x, g.reshape(1, d), wu_bf, wd_bf)


def _proj_mlp_kernel(*refs, n_in):
    r_ref = refs[0]
    a_refs = refs[1:1 + n_in]
    w_refs = refs[1 + n_in:1 + 2 * n_in]
    g_ref, wu_ref, wd_ref, o_ref, hn_ref, acc_ref = refs[1 + 2 * n_in:]
    j = pl.program_id(1)

    @pl.when(j == 0)
    def _():
        x = r_ref[...]
        for a_ref, w_ref in zip(a_refs, w_refs):
            x = x + _dot(a_ref[...].astype(BF), w_ref[...])
        hn_ref[...] = _rms(x, g_ref[...]).astype(BF)
        acc_ref[...] = x

    hid = jnp.maximum(_dot(hn_ref[...], wu_ref[...]), 0.0)
    acc_ref[...] += _dot((hid * hid).astype(BF), wd_ref[...])

    @pl.when(j == pl.num_programs(1) - 1)
    def _():
        o_ref[...] = acc_ref[...]


def proj_mlp(res, acts, ws_bf, g, wu_bf, wd_bf):
    m, d = res.shape
    hdim = wu_bf.shape[1]
    tm = min(m, 1024)
    th = 1024
    n_in = len(acts)
    return pl.pallas_call(
        functools.partial(_proj_mlp_kernel, n_in=n_in),
        grid=(m // tm, hdim // th),
        in_specs=([pl.BlockSpec((tm, d), lambda i, j: (i, 0))]
                  + [pl.BlockSpec((tm, a.shape[1]), lambda i, j: (i, 0)) for a in acts]
                  + [pl.BlockSpec(w.shape, lambda i, j: (0, 0)) for w in ws_bf]
                  + [pl.BlockSpec((1, d), lambda i, j: (0, 0)),
                     pl.BlockSpec((d, th), lambda i, j: (0, j)),
                     pl.BlockSpec((th, d), lambda i, j: (j, 0))]),
        out_specs=pl.BlockSpec((tm, d), lambda i, j: (i, 0)),
        out_shape=jax.ShapeDtypeStruct((m, d), F32),
        scratch_shapes=[pltpu.VMEM((tm, d), BF), pltpu.VMEM((tm, d), F32)],
        compiler_params=_cp("parallel", "arbitrary"),
        name="proj_mlp",
    )(res, *acts, *ws_bf, g.reshape(1, d), wu_bf, wd_bf)


CONV_HALO = 32
CONV_PAD = CONV_HALO - (CONV_WIDTH - 1)


def _conv_kernel(u_ref, st_ref, w_ref, b_ref, lg_ref, lb_ref, y_ref, new_ref, xin_ref, ph_ref, *, tt):
    t = pl.program_id(1)

    @pl.when(t == 0)
    def _():
        xin_ref[0:CONV_HALO, :] = st_ref[0]

    a = u_ref[0, :, 0:C_CONV]
    gt = u_ref[0, :, C_CONV:2 * C_CONV]
    xin_ref[CONV_HALO:CONV_HALO + tt, :] = a * jax.nn.sigmoid(gt)

    rb = min(tt, 128)
    for c in range(C_CONV // 128):
        cs = slice(c * 128, (c + 1) * 128)
        starts = list(range(0, tt, rb))
        accs = [jnp.zeros((rb, 128), F32) + b_ref[:, cs] for _ in starts]
        for ph in range(min(8, CONV_WIDTH)):
            taps = range(ph, CONV_WIDTH, 8)
            n = rb + 8 * (len(taps) - 1)
            if rb % 8 == 0:
                for bi, r0 in enumerate(starts):
                    ph_ref[bi, 0:n, :] = xin_ref[CONV_PAD + ph + r0:CONV_PAD + ph + r0 + n, cs]
            for a, k in enumerate(taps):
                wk = w_ref[k:k + 1, cs]
                for bi, r0 in enumerate(starts):
                    if rb % 8 == 0:
                        accs[bi] = accs[bi] + wk * ph_ref[bi, 8 * a:8 * a + rb, :]
                    else:
                        accs[bi] = accs[bi] + wk * xin_ref[CONV_PAD + k + r0:CONV_PAD + k + r0 + rb, cs]
        for bi, r0 in enumerate(starts):
            y_ref[0, r0:r0 + rb, cs] = accs[bi]
    y = y_ref[0]
    mu = jnp.mean(y, axis=-1, keepdims=True)
    yc = y - mu
    var = jnp.mean(yc * yc, axis=-1, keepdims=True)
    z = yc * lax.rsqrt(var + RMS_EPS) * lg_ref[...] + lb_ref[...]
    y_ref[0] = z * jax.nn.sigmoid(z)

    @pl.when(t == pl.num_programs(1) - 1)
    def _():
        new_ref[0] = xin_ref[tt + CONV_PAD:tt + CONV_HALO, :]

    if tt >= CONV_HALO:
        @pl.when(t < pl.num_programs(1) - 1)
        def _():
            xin_ref[0:CONV_HALO, :] = xin_ref[tt:tt + CONV_HALO, :]


def conformer_conv(proj, state, w, b, ln_g, ln_b):
    bsz, t, _ = proj.shape
    tt = min(t, 256)
    assert t % tt == 0 and (t == tt or tt >= CONV_HALO)
    st = jnp.pad(state, ((0, 0), (CONV_PAD, 0), (0, 0)))
    row = lambda v: v.reshape(1, C_CONV)
    return pl.pallas_call(
        functools.partial(_conv_kernel, tt=tt),
        grid=(bsz, t // tt),
        in_specs=[pl.BlockSpec((1, tt, 2 * C_CONV), lambda i, j: (i, j, 0)),
                  pl.BlockSpec((1, CONV_HALO, C_CONV), lambda i, j: (i, 0, 0)),
                  pl.BlockSpec((CONV_WIDTH, C_CONV), lambda i, j: (0, 0)),
                  pl.BlockSpec((1, C_CONV), lambda i, j: (0, 0)),
                  pl.BlockSpec((1, C_CONV), lambda i, j: (0, 0)),
                  pl.BlockSpec((1, C_CONV), lambda i, j: (0, 0))],
        out_specs=[pl.BlockSpec((1, tt, C_CONV), lambda i, j: (i, j, 0)),
                   pl.BlockSpec((1, CONV_WIDTH - 1, C_CONV), lambda i, j: (i, 0, 0))],
        out_shape=[jax.ShapeDtypeStruct((bsz, t, C_CONV), F32),
                   jax.ShapeDtypeStruct((bsz, CONV_WIDTH - 1, C_CONV), F32)],
        scratch_shapes=[pltpu.VMEM((CONV_HALO + tt, C_CONV), F32),
                        pltpu.VMEM((max(tt // 128, 1), min(tt, 128) + 8 * ((CONV_WIDTH - 1) // 8), 128), F32)],
        compiler_params=_cp("parallel", "arbitrary"),
        name="conformer_conv",
    )(proj, st, w, row(b), row(ln_g), row(ln_b))


HG_LEVELS = (32, 16, 8, 4, 2, 1)


def _hgrn_tables():
    c = HG_CHUNK
    idx = np.arange(c)
    masks = []
    for h in HG_LEVELS:
        blk = idx // (2 * h)
        upper = (idx % (2 * h)) >= h
        masks.append((blk[:, None] == blk[None, :]) & upper[:, None] & (~upper)[None, :])
    masks.append(np.eye(c, dtype=bool))
    return (idx[None, :] <= idx[:, None]).astype(np.float32), np.stack(masks).astype(np.float32)


HG_HEADS_PER_STEP = 8


def _hgrn_kernel(q_ref, fz_ref, v_ref, g_ref, lb_ref, ng_ref, s0_ref, tab_ref, msk_ref,
                 o_ref, sn_ref, st_ref, *, t_valid, nh, ncs):
    ci = pl.program_id(2)
    c = HG_CHUNK

    @pl.when(ci == 0)
    def _():
        for hh in range(nh):
            st_ref[hh] = s0_ref[0, hh].T

    rowc = lax.broadcasted_iota(jnp.int32, (c, 1), 0)
    sub8 = lax.broadcasted_iota(jnp.int32, (8, 1), 0)
    tab = tab_ref[...]
    for sub, hh in [(s_, h_) for s_ in range(ncs) for h_ in range(nh)]:
        rs = slice(sub * c, (sub + 1) * c)
        live = (ci * ncs + sub) * c + rowc < t_valid
        cs = slice(hh * HG_DK, (hh + 1) * HG_DK)
        lb = lb_ref[:, cs]
        f = lb + (1.0 - lb) * jax.nn.sigmoid(fz_ref[0, rs, cs])
        lf = jnp.where(live, jnp.log(f), 0.0)
        k = jnp.where(live, 1.0 - f, 0.0)
        q = q_ref[0, rs, cs]
        v = v_ref[0, rs, cs]

        hi = lf.astype(BF)
        b2 = _dot(tab, jnp.concatenate([hi, (lf - hi.astype(F32)).astype(BF)], axis=1))
        b = b2[:, 0:HG_DK] + b2[:, HG_DK:]

        def pivot_rows(h):
            if 2 * h >= 8:
                return jnp.concatenate(
                    [jnp.broadcast_to(b[blk * 2 * h + h - 1:blk * 2 * h + h, :], (2 * h, HG_DK))
                     for blk in range(c // (2 * h))], axis=0)
            groups = []
            for g8 in range(c // 8):
                piece = None
                for kb in range(8 // (2 * h)):
                    r = 8 * g8 + kb * 2 * h + h - 1
                    cand = jnp.broadcast_to(b[r:r + 1, :], (8, HG_DK))
                    piece = cand if piece is None else jnp.where(sub8 >= kb * 2 * h, cand, piece)
                groups.append(piece)
            return jnp.concatenate(groups, axis=0)

        attn = jnp.zeros((c, c), F32)
        for li, h in enumerate(HG_LEVELS):
            piv = pivot_rows(h)
            fac = jnp.exp(jnp.where((rowc % (2 * h)) >= h, b - piv, piv - b))
            attn = attn + msk_ref[li] * _dot_nt((q * fac).astype(BF), (k * fac).astype(BF))
        attn = attn + msk_ref[len(HG_LEVELS)] * _dot_nt(q.astype(BF), k.astype(BF))

        st = st_ref[hh]
        qb = (q * jnp.exp(b)).astype(BF)
        o = _dot(attn.astype(BF), v.astype(BF)) + _dot_nt(qb, st.astype(BF))
        ke = (k * jnp.exp(b[c - 1:c] - b)).astype(BF)
        decay = jnp.exp(b[c - 1:c])
        st_ref[hh] = st * decay + _dot(v.T.astype(BF), ke)

        gate = g_ref[0, rs, cs]
        o_ref[0, rs, cs] = _rms(o, ng_ref[...]) * (gate * jax.nn.sigmoid(gate))

    @pl.when(ci == pl.num_programs(2) - 1)
    def _():
        for hh in range(nh):
            sn_ref[0, hh] = st_ref[hh].T


def hgrn2(z, s0, lb, norm_g, t_valid):
    bsz, tp, _ = z.shape
    ncs = 2 if (tp // HG_CHUNK) % 2 == 0 else 1
    c = HG_CHUNK * ncs
    nc = tp // c
    tab, msk = _hgrn_tables()
    nh = HG_HEADS_PER_STEP
    ng = HG_HEADS // nh
    w = nh * HG_DK
    blk = lambda off: pl.BlockSpec((1, c, w), lambda b, h, i: (b, i, off + h))
    return pl.pallas_call(
        functools.partial(_hgrn_kernel, t_valid=t_valid, nh=nh, ncs=ncs),
        grid=(bsz, ng, nc),
        in_specs=[blk(0), blk(ng), blk(2 * ng), blk(3 * ng),
                  pl.BlockSpec((1, w), lambda b, h, i: (0, h)),
                  pl.BlockSpec((1, HG_DV), lambda b, h, i: (0, 0)),
                  pl.BlockSpec((1, nh, HG_DK, HG_DV), lambda b, h, i: (b, h, 0, 0)),
                  pl.BlockSpec(tab.shape, lambda b, h, i: (0, 0)),
                  pl.BlockSpec(msk.shape, lambda b, h, i: (0, 0, 0))],
        out_specs=[pl.BlockSpec((1, c, w), lambda b, h, i: (b, i, h)),
                   pl.BlockSpec((1, nh, HG_DK, HG_DV), lambda b, h, i: (b, h, 0, 0))],
        out_shape=[jax.ShapeDtypeStruct((bsz, tp, D_MODEL), F32),
                   jax.ShapeDtypeStruct((bsz, HG_HEADS, HG_DK, HG_DV), F32)],
        scratch_shapes=[pltpu.VMEM((nh, HG_DV, HG_DK), F32)],
        compiler_params=_cp("parallel", "parallel", "arbitrary"),
        name="hgrn2",
    )(z, z, z, z, lb.reshape(1, D_MODEL), norm_g.reshape(1, HG_DV), s0,
      jnp.asarray(tab, BF), jnp.asarray(msk))


def _pair_norm(seg, g):
    lo = lax.broadcasted_iota(jnp.int32, seg.shape, 1) < DH
    sq = seg * seg
    s0 = jnp.sum(jnp.where(lo, sq, 0.0), axis=-1, keepdims=True)
    s1 = jnp.sum(jnp.where(lo, 0.0, sq), axis=-1, keepdims=True)
    inv = jnp.where(lo, lax.rsqrt(s0 * (1.0 / DH) + RMS_EPS), lax.rsqrt(s1 * (1.0 / DH) + RMS_EPS))
    return seg * inv * g


def _kv_post_kernel(p_ref, kg_ref, kv_ref, win_ref):
    x = p_ref[...]
    kv_ref[:, 0:2 * KVW] = x[:, 0:2 * KVW]
    kv_ref[:, 2 * KVW:3 * KVW] = _pair_norm(x[:, 2 * KVW:3 * KVW], kg_ref[1:2, :])
    kv_ref[:, 3 * KVW:4 * KVW] = x[:, 3 * KVW:4 * KVW]
    win_ref[:, 0:KVW] = _pair_norm(x[:, 4 * KVW:5 * KVW], kg_ref[2:3, :])
    win_ref[:, KVW:2 * KVW] = x[:, 5 * KVW:6 * KVW]


def kv_post(proj, k_g):
    m = proj.shape[0]
    tm = min(m, 1024)
    kvcols = 6 * KVW
    assert O_KV % kvcols == 0
    return pl.pallas_call(
        _kv_post_kernel,
        grid=(m // tm,),
        in_specs=[pl.BlockSpec((tm, kvcols), lambda i: (i, O_KV // kvcols)),
                  pl.BlockSpec((3, KVW), lambda i: (0, 0))],
        out_specs=[pl.BlockSpec((tm, 4 * KVW), lambda i: (i, 0)),
                   pl.BlockSpec((tm, 2 * KVW), lambda i: (i, 0))],
        out_shape=[jax.ShapeDtypeStruct((m, 4 * KVW), F32), jax.ShapeDtypeStruct((m, 2 * KVW), F32)],
        compiler_params=_cp("parallel"),
        name="kv_post",
    )(proj, jnp.concatenate([k_g, k_g], axis=1))


def _q_norm_kernel(q_ref, g_ref, o_ref):
    for c in range(NH // 2):
        cs = slice(c * 2 * DH, (c + 1) * 2 * DH)
        o_ref[:, cs] = _pair_norm(q_ref[:, cs], g_ref[...]) * (DH ** -0.5)


def q_norm(proj, q_g):
    m = proj.shape[0]
    tm = min(m, 1024)
    return pl.pallas_call(
        _q_norm_kernel,
        grid=(m // tm,),
        in_specs=[pl.BlockSpec((tm, NH * DH), lambda i: (i, O_Q // (NH * DH))),
                  pl.BlockSpec((1, 2 * DH), lambda i: (0, 0))],
        out_specs=pl.BlockSpec((tm, NH * DH), lambda i: (i, 0)),
        out_shape=jax.ShapeDtypeStruct((m, NH * DH), F32),
        compiler_params=_cp("parallel"),
        name="q_norm",
    )(proj, jnp.concatenate([q_g, q_g]).reshape(1, 2 * DH))


def _compress_kernel(*refs, n_prefetch, n_src, rows, nch):
    refs = refs[n_prefetch:]
    srcs = (refs[:n_src], refs[n_src:2 * n_src])
    pef_ref, pes_ref, w1f_ref, w1s_ref, b1_ref, w2_ref, kg_ref, kc_ref, vc_ref, hf_ref, hs_ref = refs[2 * n_src:]
    j = pl.program_id(1)
    cpr = rows // CMP_STRIDE
    m = cpr * n_src

    @pl.when(j == 0)
    def _():
        hs_ref[:, nch:nch + 8, :] = jnp.zeros((2, 8, 2 * CMP_HIDDEN), F32)

    for kind in range(2):
        accf = jnp.zeros((m, 2 * CMP_HIDDEN), F32)
        accs = jnp.zeros((m, 2 * CMP_HIDDEN), F32)
        for s in range(CMP_STRIDE):
            xs = jnp.concatenate(
                [r[0, pl.ds(s, cpr, stride=CMP_STRIDE), :] for r in srcs[kind]], axis=0)
            accf = accf + _dot((xs + pef_ref[kind, s:s + 1, :]).astype(BF), w1f_ref[kind, s])
            accs = accs + _dot((xs + pes_ref[kind, s:s + 1, :]).astype(BF), w1s_ref[kind, s])
        row0 = pl.multiple_of(j * m, 8)
        hf_ref[kind, pl.ds(row0, m), :] = accf
        hs_ref[kind, pl.ds(row0, m), :] = accs

    @pl.when(j == pl.num_programs(1) - 1)
    def _():
        for kind in range(2):
            hid = jax.nn.gelu(hf_ref[kind, 0:nch, :] + hs_ref[kind, 1:nch + 1, :] + b1_ref[kind])
            out = _dot(hid.astype(BF), w2_ref[kind])
            if kind == 0:
                kc_ref[0] = _pair_norm(out, kg_ref[...]).astype(BF)
            else:
                vc_ref[0] = out.astype(BF)


def _compress_weights(pe, w1, b1, w2, kg0):
    eye = jnp.eye(HKV, dtype=F32)
    bd = lambda w: jnp.einsum('ab,ksdf->ksadbf', eye, w).reshape(2, CMP_STRIDE, KVW, 2 * CMP_HIDDEN).astype(BF)
    tile2 = lambda a: jnp.concatenate([a, a], axis=-1)
    w2bd = jnp.einsum('ab,kfd->kafbd', eye, w2).reshape(2, 2 * CMP_HIDDEN, KVW).astype(BF)
    return (tile2(pe[:, :CMP_STRIDE]), tile2(pe[:, CMP_STRIDE:]), bd(w1[:, :CMP_STRIDE]), bd(w1[:, CMP_STRIDE:]),
            tile2(b1).reshape(2, 1, 2 * CMP_HIDDEN), w2bd, tile2(kg0).reshape(1, KVW))


def _compress_common(n_prefetch, n_src, rows, nch, bsz, weights):
    wspecs = [pl.BlockSpec(w.shape, functools.partial(lambda nd, *a: (0,) * nd, w.ndim)) for w in weights]
    out_specs = [pl.BlockSpec((1, nch, KVW), lambda b, j, *a: (b, 0, 0))] * 2
    out_shape = [jax.ShapeDtypeStruct((bsz, nch, KVW), BF)] * 2
    scratch = [pltpu.VMEM((2, nch + 8, 2 * CMP_HIDDEN), F32)] * 2
    kern = functools.partial(_compress_kernel, n_prefetch=n_prefetch, n_src=n_src, rows=rows, nch=nch)
    return kern, wspecs, out_specs, out_shape, scratch


def compress_dense(proj3, weights):
    bsz, t, _ = proj3.shape
    rows = min(t, 2048)
    nch = t // CMP_STRIDE
    assert O_KV % KVW == 0
    kern, wspecs, out_specs, out_shape, scratch = _compress_common(0, 1, rows, nch, bsz, weights)
    src = lambda kind: pl.BlockSpec((1, rows, KVW), lambda b, j: (b, j, O_KV // KVW + kind))
    return pl.pallas_call(
        kern, grid=(bsz, t // rows),
        in_specs=[src(0), src(1)] + wspecs,
        out_specs=out_specs, out_shape=out_shape, scratch_shapes=scratch,
        compiler_params=_cp("parallel", "arbitrary"), name="compress_dense",
    )(proj3, proj3, *weights)


CMP_PAGES = 16
CMP_PE_ROWS = 16


def _compress_paged_kernel(*refs, npg, page, nch):
    pages = refs[1:1 + npg]
    pe2_ref, w1p_ref, b1_ref, w2_ref, kg_ref, kc_ref, vc_ref, hf_ref, hs_ref, xs_ref = refs[1 + npg:]
    j = pl.program_id(1)
    m = npg * page // CMP_STRIDE
    hid2 = 2 * CMP_HIDDEN

    @pl.when(j == 0)
    def _():
        hs_ref[:, nch:nch + 8, :] = jnp.zeros((2, 8, hid2), F32)

    for kind in range(2):
        for r in range(npg):
            xt = jnp.concatenate([pages[r][0, kind, h] for h in range(HKV)], axis=0)
            xs_ref[kind, r * page:(r + 1) * page, :] = xt.T
        acc = jnp.zeros((m + CMP_PE_ROWS, 2 * hid2), F32)
        for s2 in range(CMP_STRIDE // 2):
            xa = xs_ref[kind, pl.ds(2 * s2, m, stride=CMP_STRIDE), :]
            xb = xs_ref[kind, pl.ds(2 * s2 + 1, m, stride=CMP_STRIDE), :]
            lhs = jnp.concatenate([jnp.concatenate([xa, xb], axis=1).astype(BF), pe2_ref[kind, s2]], axis=0)
            acc = acc + _dot(lhs, w1p_ref[kind, s2])
        row0 = pl.multiple_of(j * m, 8)
        hf_ref[kind, pl.ds(row0, m), :] = acc[0:m, 0:hid2] + acc[m:m + 1, 0:hid2]
        hs_ref[kind, pl.ds(row0, m), :] = acc[0:m, hid2:] + acc[m + 1:m + 2, hid2:]

    @pl.when(j == pl.num_programs(1) - 1)
    def _():
        for kind in range(2):
            hid = jax.nn.gelu(hf_ref[kind, 0:nch, :] + hs_ref[kind, 1:nch + 1, :] + b1_ref[kind])
            out = _dot(hid.astype(BF), w2_ref[kind])
            if kind == 0:
                kc_ref[0] = _pair_norm(out, kg_ref[...]).astype(BF)
            else:
                vc_ref[0] = out.astype(BF)


def compress_paged(cachet, page_table, weights):
    bsz, n_pages = page_table.shape
    page = cachet.shape[-1]
    npg = min(CMP_PAGES, n_pages)
    nch = n_pages * page // CMP_STRIDE
    src_specs = [pl.BlockSpec((1, 2, HKV, DH, page),
                              functools.partial(lambda r, b, j, pt: (pt[b, j * npg + r], 0, 0, 0, 0), r))
                 for r in range(npg)]
    pef, pes, w1f, w1s, b1, w2, kg = weights
    np2 = CMP_STRIDE // 2
    pair_rows = lambda a: a.reshape(2, np2, 2 * KVW)
    pe2 = jnp.zeros((2, np2, CMP_PE_ROWS, 2 * KVW), F32).at[:, :, 0].set(pair_rows(pef)).at[:, :, 1].set(
        pair_rows(pes)).astype(BF)
    pair_w = lambda w: w.reshape(2, np2, 2 * KVW, 2 * CMP_HIDDEN)
    w1p = jnp.concatenate([pair_w(w1f), pair_w(w1s)], axis=-1)
    weights = (pe2, w1p, b1, w2, kg)
    _, wspecs, out_specs, out_shape, scratch = _compress_common(1, npg, page, nch, bsz, weights)
    return pl.pallas_call(
        functools.partial(_compress_paged_kernel, npg=npg, page=page, nch=nch),
        grid_spec=pltpu.PrefetchScalarGridSpec(
            num_scalar_prefetch=1, grid=(bsz, n_pages // npg), in_specs=src_specs + wspecs,
            out_specs=out_specs, scratch_shapes=scratch + [pltpu.VMEM((2, npg * page, KVW), F32)]),
        out_shape=out_shape, compiler_params=_cp("parallel", "arbitrary"), name="compress_paged",
    )(page_table, *([cachet] * npg), *weights)


NS_PAD = 256
SEL_HALF = 128
LOG2E = math.log2(math.e)
PEN = 30000.0
KT = 256
SEL_SPLIT = 2
VROWS = 80
HALF_TILES = SEL_HALF * SEL_BLOCK // KT
CBAND = 24


def _t5_bucket(dist):
    n = jnp.maximum(dist, 0)
    large = REL_EXACT + (jnp.log(jnp.maximum(n, 1).astype(F32) / REL_EXACT)
                         / math.log(REL_MAX_DIST / REL_EXACT) * (REL_BUCKETS - REL_EXACT)).astype(jnp.int32)
    return jnp.where(n < REL_EXACT, n, jnp.minimum(large, REL_BUCKETS - 1))


def _selection_matrix(nc, ns_pad, nc_pad):
    ratio, span = SEL_BLOCK // CMP_STRIDE, CMP_LEN // CMP_STRIDE
    a = np.zeros((ns_pad, nc_pad), np.float32)
    for j in range(ns_pad):
        for mm in range(ratio):
            for nn in range(span):
                n = ratio * j + mm - nn
                if 0 <= n < nc:
                    a[j, n] += 1.0
    return a


def _split3(x):
    hi = x.astype(BF)
    r1 = x - hi.astype(F32)
    mid = r1.astype(BF)
    lo = (r1 - mid.astype(F32)).astype(BF)
    return hi, mid, lo


def _top_blocks(s, qp, n_rounds):
    j = lax.broadcasted_iota(jnp.int32, s.shape, 0)
    cur = qp // SEL_BLOCK
    forced = (j == 0) | (j == cur) | (j == cur - 1)
    valid = j * SEL_BLOCK <= qp
    s = jnp.where(forced, BIG, s)
    s = jnp.where(valid, s, -BIG)
    for _ in range(n_rounds):
        mx = jnp.max(s, axis=0, keepdims=True)
        jm = jnp.min(jnp.where(s == mx, j, 2 * NS_PAD), axis=0, keepdims=True)
        s = jnp.where(j == jm, -3e38, s)
    return jnp.where(s < -2 * BIG, 1.0, 0.0)


def _masked_softmax_rows(s, axis):
    m = jnp.max(s, axis=axis, keepdims=True)
    e = jnp.exp(s - m)
    den = jnp.maximum(jnp.sum(e, axis=axis, keepdims=True), 1e-30)
    return e * jnp.where(m > 0.5 * NEG, 1.0 / den, 0.0)


def _nsa_prompt_kernel(q_ref, gate_ref, qg_ref, kc_ref, vct_ref, amat_ref, kaug_ref, vselt_ref, kwin_ref,
                       vwint_ref, bc_ref, bs_ref, bw_ref, o_ref, lg_ref, acc_ref, qa_ref, qh_ref, sa_ref, sb_ref,
                       m_ref, mx_ref, *, ncp, n_top):
    i = pl.program_id(0)
    qb = Q_BLOCK
    rows = GRP * qb
    start = i * qb
    qall = q_ref[...]
    lane = lax.broadcasted_iota(jnp.int32, (qb, 2 * DH), 1)
    qpos_row = start + lax.broadcasted_iota(jnp.int32, (1, qb), 1)
    t_last = start // KT
    n_far = jnp.maximum(t_last - 1, 0)

    @pl.when(i == 0)
    def _():
        lg_ref[:, 0:16, :] = jnp.zeros((HKV, 16, rows), F32)

    xfs = []
    for h in range(HKV):
        parts = []
        for g in range(GRP):
            hd = h * GRP + g
            slab = qall[:, (hd // 2) * 2 * DH:(hd // 2 + 1) * 2 * DH]
            mine = (lane >= DH) if hd % 2 else (lane < DH)
            ss = jnp.sum(jnp.where(mine, slab * slab, 0.0), axis=-1, keepdims=True)
            xn = jnp.where(mine, slab * lax.rsqrt(ss * (1.0 / DH) + RMS_EPS) * qg_ref[...] * (DH ** -0.5), 0.0)
            if hd % 2 != h:
                xn = pltpu.roll(xn, DH, 1)
            parts.append(xn)
        xf = jnp.concatenate(parts, axis=0)
        xfs.append(xf)
        qh_ref[h] = xf.astype(BF)

    def compressed(n):
        outs = []
        for h in range(HKV):
            lg_ref[h, 16:16 + n, :] = _dot_nt(kc_ref[0:n, :], qh_ref[h])
            band0 = pl.multiple_of(i * 8, 8)
            lg_ref[h, pl.ds(band0, CBAND), :] = lg_ref[h, pl.ds(band0, CBAND), :] + bc_ref[h]
            tok = lax.broadcasted_iota(jnp.int32, (n, 1), 0)
            p_c = _masked_softmax_rows(jnp.where(tok < 8 * i + 8, lg_ref[h, 16:16 + n, :], NEG), 0)
            o_c = _dot(vct_ref[h, :, 0:n], p_c.astype(BF))
            imp = p_c[:, 0:qb]
            for g in range(1, GRP):
                imp = imp + p_c[:, g * qb:(g + 1) * qb]
            sc3 = _dot(amat_ref[:, 0:n], jnp.concatenate(_split3(imp), axis=1))
            outs += [o_c, sc3[:, 0:qb] + sc3[:, qb:2 * qb] + sc3[:, 2 * qb:]]
        return tuple(outs)

    cch = min(ncp, 2 * Q_BLOCK)
    nbr = ncp // cch
    outs = lax.switch(jnp.minimum((8 * i + 7) // cch, nbr - 1),
                      [functools.partial(compressed, (k + 1) * cch) for k in range(nbr)])
    o_cs = [outs[2 * h] for h in range(HKV)]
    sel = _top_blocks(jnp.concatenate([outs[2 * h + 1] for h in range(HKV)], axis=1),
                      jnp.concatenate([qpos_row] * HKV, axis=1), n_top)
    for h in range(HKV):
        pen = ((sel[:, h * qb:(h + 1) * qb].T - 1.0) * PEN).astype(BF)
        qh2 = (xfs[h] * LOG2E).astype(BF)
        qa_ref[h, 0] = jnp.concatenate([jnp.concatenate([pen[:, 0:SEL_HALF]] * GRP, axis=0), qh2], axis=1)
        qa_ref[h, 1] = jnp.concatenate([jnp.concatenate([pen[:, SEL_HALF:]] * GRP, axis=0), qh2], axis=1)

    hr = rows // SEL_SPLIT
    chains = [(h, slice(sp * hr, (sp + 1) * hr)) for h in range(HKV) for sp in range(SEL_SPLIT)]
    nchain = len(chains)

    def scores(c, t):
        h, rs = chains[c]
        qa = jnp.where(t < HALF_TILES, qa_ref[h, 0, rs, :], qa_ref[h, 1, rs, :])
        return _dot_nt(kaug_ref[t + 1], qa)

    def absorb(c, s, mx, t, m):
        h, rs = chains[c]
        m2 = jnp.maximum(m, mx)
        p = jnp.exp2(s - m2)
        acc_ref[h, :, rs] = jnp.exp2(m - m2) * acc_ref[h, :, rs] + _dot(vselt_ref[h, t + 1], p.astype(BF))
        return m2

    col_max = lambda s: jnp.max(s, axis=0, keepdims=True)

    def pair_body(k, carry):
        ms, mxa = list(carry[:nchain]), list(carry[nchain:])
        mxb = []
        for c, (h, rs) in enumerate(chains):
            s1 = scores(c, 2 * k + 1)
            sb_ref[h, :, rs] = s1
            mxb.append(col_max(s1))
            ms[c] = absorb(c, sa_ref[h, :, rs], mxa[c], 2 * k, ms[c])
        for c, (h, rs) in enumerate(chains):
            s2 = scores(c, 2 * k + 2)
            ms[c] = absorb(c, sb_ref[h, :, rs], mxb[c], 2 * k + 1, ms[c])
            sa_ref[h, :, rs] = s2
            mxa[c] = col_max(s2)
        return tuple(ms) + tuple(mxa)

    acc_ref[...] = jnp.zeros((HKV, VROWS, rows), F32)
    mxa = []
    for c, (h, rs) in enumerate(chains):
        s0 = scores(c, 0)
        sa_ref[h, :, rs] = s0
        mxa.append(col_max(s0))
    carry = tuple(jnp.full((1, hr), NEG, F32) for _ in range(nchain)) + tuple(mxa)
    quad_body = lambda k, cr: pair_body(2 * k + 1, pair_body(2 * k, cr))
    oct_body = lambda k, cr: quad_body(2 * k + 1, quad_body(2 * k, cr))
    carry = lax.fori_loop(0, n_far // 8, oct_body, carry)
    carry = lax.fori_loop(2 * (n_far // 8), n_far // 4, quad_body, carry)
    carry = lax.fori_loop(2 * (n_far // 4), n_far // 2, pair_body, carry)
    for c, (h, rs) in enumerate(chains):
        m_ref[h, :, rs] = carry[c]
        mx_ref[h, :, rs] = carry[nchain + c]

    @pl.when(n_far % 2 == 1)
    def _():
        for c, (h, rs) in enumerate(chains):
            m_ref[h, :, rs] = absorb(c, sa_ref[h, :, rs], mx_ref[h, :, rs], n_far - 1, m_ref[h, :, rs])

    kiota = lax.broadcasted_iota(jnp.int32, (KT, 1), 0)
    ms = [m_ref[h, :, rs] for h, rs in chains]
    for u in range(2):
        t = t_last - 1 + u
        for c, (h, rs) in enumerate(chains):
            s = jnp.where(t * KT + kiota >= 0, scores(c, t) + bs_ref[i % 2, h, u, :, rs], NEG)
            ms[c] = absorb(c, s, col_max(s), t, ms[c])

    gates_t = jax.nn.sigmoid(gate_ref[...]).T
    wkeys = WINDOW + qb
    for h in range(HKV):
        o_s = acc_ref[h, 0:DH, :] / jnp.maximum(acc_ref[h, DH:DH + 1, :], 1e-30)

        s = _dot_nt(kwin_ref[pl.ds(pl.multiple_of(start, qb), wkeys), :], qh_ref[h]) + bw_ref[h]
        kpos = start - WINDOW + lax.broadcasted_iota(jnp.int32, (wkeys, 1), 0)
        s = jnp.where(kpos >= 0, s, NEG)
        e = jnp.exp(s - jnp.max(s, axis=0, keepdims=True))
        vw = jnp.concatenate([vwint_ref[h, i + u] for u in range(wkeys // qb)], axis=1)
        ow = _dot(vw, e.astype(BF))
        o_w = ow[0:DH] / jnp.maximum(ow[DH:DH + 1], 1e-30)

        gate = lambda br: jnp.concatenate(
            [gates_t[(h * GRP + g) * 3 + br:(h * GRP + g) * 3 + br + 1, :] for g in range(GRP)], axis=1)
        o_t = gate(0) * o_cs[h] + gate(1) * o_s + gate(2) * o_w
        for g in range(GRP):
            o_ref[:, (h * GRP + g) * DH:(h * GRP + g + 1) * DH] = o_t[:, g * qb:(g + 1) * qb].T


def _bias_tables(rel_table):
    tab = rel_table.astype(F32)
    far = tab[REL_BUCKETS - 1]
    qb = Q_BLOCK

    def toeplitz(off, nk, dmax, shift):
        d = off - (nk - 1) + jnp.arange(nk + qb - 1, dtype=jnp.int32)
        w = jnp.where(((d >= 0) & (d <= dmax))[None, :], (tab[_t5_bucket(d)] - (far if shift else 0.0)).T, NEG)
        p = w.shape[1]
        hank = jnp.tile(w, (1, nk + 1))[:, :nk * (p + 1)].reshape(NH, nk, p + 1)[:, :, :qb]
        return hank[:, ::-1, :]

    def lanes_gq(b):
        return b.reshape(HKV, GRP, b.shape[1], qb).transpose(0, 2, 1, 3).reshape(HKV, b.shape[1], GRP * qb)

    big = 1 << 30
    bc = lanes_gq(toeplitz(16 * CMP_STRIDE - (CMP_LEN - 1), CBAND * CMP_STRIDE, big, True)[:, ::CMP_STRIDE, :])
    bs = jnp.stack([lanes_gq(toeplitz(off, 2 * KT, big, True)).reshape(HKV, 2, KT, GRP * qb)
                    for off in (KT, KT + qb)]) * LOG2E
    bw = lanes_gq(toeplitz(WINDOW, WINDOW + qb, WINDOW, False))
    return bc, bs, bw


def nsa_prompt(proj, kc, vc, ksel, vsel, kwin, vwin, q_g, rel_table):
    t = proj.shape[0]
    nb = t // Q_BLOCK
    nch = kc.shape[0]
    ncp = -(-nch // 128) * 128
    assert t // SEL_BLOCK <= NS_PAD and t % KT == 0
    n_top = min(TOP_N, t // SEL_BLOCK)
    kcp = jnp.pad(kc, ((0, ncp - nch), (0, 0)))
    vct = jnp.pad(vc, ((0, ncp - nch), (0, 0))).reshape(ncp, HKV, DH).transpose(1, 2, 0)
    amat = jnp.asarray(_selection_matrix(nch - 1, NS_PAD, ncp), BF)

    def values_t(v, pad_rows, tile):
        vt = jnp.pad(v.astype(BF), ((pad_rows, 0), (0, 0))).reshape(-1, tile, HKV, DH).transpose(2, 0, 3, 1)
        return jnp.concatenate([vt, jnp.ones(vt.shape[:2] + (1, tile), BF),
                                jnp.zeros(vt.shape[:2] + (VROWS - DH - 1, tile), BF)], axis=2)

    onehot = ((jnp.arange(t, dtype=jnp.int32)[:, None] // SEL_BLOCK) % SEL_HALF
              == jnp.arange(SEL_HALF, dtype=jnp.int32)[None, :]).astype(BF)
    kaug = jnp.pad(jnp.concatenate([onehot, ksel.astype(BF)], axis=1), ((KT, 0), (0, 0))).reshape(-1, KT, 2 * KVW)
    vselt = values_t(vsel, KT, KT)
    vwint = values_t(vwin, WINDOW, Q_BLOCK)
    rows = GRP * Q_BLOCK
    kwinp = jnp.pad(kwin.astype(BF), ((WINDOW, 0), (0, 0)))
    bc, bs, bw = _bias_tables(rel_table)
    whole = pl.BlockSpec(memory_space=pltpu.VMEM)
    return pl.pallas_call(
        functools.partial(_nsa_prompt_kernel, ncp=ncp, n_top=n_top),
        grid=(nb,),
        in_specs=[pl.BlockSpec((Q_BLOCK, NH * DH), lambda i: (i, O_Q // (NH * DH))),
                  pl.BlockSpec((Q_BLOCK, 128), lambda i: (i, O_GATE // 128)),
                  whole, whole, whole, whole, whole, whole, whole, whole, whole, whole, whole],
        out_specs=pl.BlockSpec((Q_BLOCK, NH * DH), lambda i: (i, 0)),
        out_shape=jax.ShapeDtypeStruct((t, NH * DH), F32),
        scratch_shapes=[pltpu.VMEM((HKV, 16 + ncp, rows), F32), pltpu.VMEM((HKV, VROWS, rows), F32),
                        pltpu.VMEM((HKV, 2, rows, SEL_HALF + KVW), BF), pltpu.VMEM((HKV, rows, KVW), BF),
                        pltpu.VMEM((HKV, KT, rows), F32), pltpu.VMEM((HKV, KT, rows), F32),
                        pltpu.VMEM((HKV, 1, rows), F32), pltpu.VMEM((HKV, 1, rows), F32)],
        compiler_params=_cp("arbitrary"),
        name="nsa_prompt",
    )(proj, proj, jnp.concatenate([q_g, q_g]).reshape(1, 2 * DH), kcp, vct, amat, kaug, vselt, kwinp, vwint,
      bc, bs, bw)


def _pick_head(x, lane_h):
    return jnp.where(lane_h == 0, x[0:DH], x[DH:2 * DH])


def _nsa_dec_pre_kernel(qbd_ref, kc_ref, vc_ref, amat_ref, bc_ref, win_ref, wnew_ref, bwa_ref, bwb_ref,
                        pen_ref, oc_ref, ow_ref, *, ts, pos0, n_top):
    r_all = GRP * HKV * ts
    qbd = qbd_ref[0]
    lane_h = (lax.broadcasted_iota(jnp.int32, (1, r_all), 1) // ts) % HKV

    p_c = _masked_softmax_rows(_dot(kc_ref[0], qbd) + bc_ref[...], 0)
    oc_ref[0] = _pick_head(_dot_tn(vc_ref[0], p_c.astype(BF)), lane_h)

    hi, mid, lo = _split3(p_c)
    amat = amat_ref[...]
    sc = _dot(amat, hi) + _dot(amat, mid) + _dot(amat, lo)
    w8 = HKV * ts
    score = sc[:, 0:w8]
    for g in range(1, GRP):
        score = score + sc[:, g * w8:(g + 1) * w8]
    qp = pos0 + lax.broadcasted_iota(jnp.int32, (1, w8), 1) % ts
    sel = _top_blocks(score, qp, n_top)
    pen_ref[0] = (jnp.concatenate([sel] * GRP, axis=1) - 1.0) * PEN

    win = win_ref[0]
    wnew = wnew_ref[0]
    s_a = _dot(win[:, 0:KVW].astype(BF), qbd) + bwa_ref[...]
    s_b = _dot(wnew[:, 0:KVW].astype(BF), qbd) + bwb_ref[...]
    m = jnp.maximum(jnp.max(s_a, axis=0, keepdims=True), jnp.max(s_b, axis=0, keepdims=True))
    e_a = jnp.where(s_a > 0.5 * NEG, jnp.exp(s_a - m), 0.0)
    e_b = jnp.where(s_b > 0.5 * NEG, jnp.exp(s_b - m), 0.0)
    den = jnp.maximum(jnp.sum(e_a, axis=0, keepdims=True) + jnp.sum(e_b, axis=0, keepdims=True), 1e-30)
    o_w = (_dot_tn(win[:, KVW:].astype(BF), e_a.astype(BF)) + _dot_tn(wnew[:, KVW:].astype(BF), e_b.astype(BF)))
    ow_ref[0] = _pick_head(o_w, lane_h) / den


SEL_PAGES = 32


def _nsa_dec_sel_kernel(*refs, npg, page, ts):
    pages = refs[1:1 + npg]
    (q_ref, pen_ref, expt_ref, blast_ref, knew_ref, vnew_ref, bnew_ref, gate_ref, oc_ref, ow_ref,
     o_ref, m_ref, l_ref, acc_ref) = refs[1 + npg:]
    j = pl.program_id(1)
    last = pl.num_programs(1) - 1
    rows = GRP * ts

    @pl.when(j == 0)
    def _():
        m_ref[...] = jnp.full((HKV * rows, 1), NEG, F32)
        l_ref[...] = jnp.zeros((HKV * rows, 1), F32)
        acc_ref[...] = jnp.zeros((HKV * rows, DH), F32)

    def online(s, pvs):
        m = m_ref[...]
        m2 = jnp.maximum(m, jnp.max(s, axis=-1, keepdims=True))
        a = jnp.exp(m - m2)
        p = jnp.exp(s - m2)
        m_ref[...] = m2
        l_ref[...] = a * l_ref[...] + jnp.sum(p, axis=-1, keepdims=True)
        pb = p.astype(BF)
        pv = jnp.concatenate([pvs[h](pb[h * rows:(h + 1) * rows]) for h in range(HKV)], axis=0)
        acc_ref[...] = a * acc_ref[...] + pv

    both = lambda f: jnp.concatenate([f(h) for h in range(HKV)], axis=0)
    vts = [jnp.concatenate([r[0, 1, h] for r in pages], axis=1).astype(BF) for h in range(HKV)]
    s = both(lambda h: _dot(q_ref[0, h], jnp.concatenate([r[0, 0, h] for r in pages], axis=1).astype(BF))
             + _dot(pen_ref[0, h, j], expt_ref[...]))
    s = s + jnp.where(j == last, blast_ref[...], 0.0)
    online(s, [functools.partial(lambda vt, p: _dot_nt(p, vt), vts[h]) for h in range(HKV)])

    @pl.when(j == last)
    def _():
        s_new = both(lambda h: _dot_nt(q_ref[0, h], knew_ref[0, h].astype(BF))) + bnew_ref[...]
        online(s_new, [functools.partial(lambda v, p: _dot(p, v), vnew_ref[0, h].astype(BF)) for h in range(HKV)])
        o_s = acc_ref[...] / jnp.maximum(l_ref[...], 1e-30)
        g = jax.nn.sigmoid(gate_ref[0])
        o_ref[0] = g[:, 0:1] * oc_ref[0] + g[:, 1:2] * o_s + g[:, 2:3] * ow_ref[0]


def _dec_bias_tables(rel_table, pos0, ts, nch, wb, page, npg):
    tab = rel_table.astype(F32)
    far = tab[REL_BUCKETS - 1]
    r = np.arange(GRP * HKV * ts)
    head = ((r // ts) % HKV) * GRP + r // (HKV * ts)
    qpos = pos0 + r % ts
    far_r = far[head][None, :]

    def bias(kpos, dmax, shift):
        dist = qpos[None, :] - np.asarray(kpos)[:, None]
        ok = (dist >= 0) & (dist <= dmax)
        out = jnp.where(ok, 0.0 if shift else far_r, NEG)
        near = np.nonzero((ok & (dist < REL_MAX_DIST)).any(axis=1))[0]
        if near.size:
            lo, hi = int(near.min()), int(near.max()) + 1
            b = tab[_t5_bucket(jnp.asarray(dist[lo:hi], jnp.int32)), head[None, :]] - (far_r if shift else 0.0)
            out = jnp.concatenate([out[:lo], jnp.where(ok[lo:hi], b, NEG), out[hi:]], axis=0)
        return out

    big = 1 << 30
    bc = bias(np.arange(nch) * CMP_STRIDE + CMP_LEN - 1, big, False)
    bwa = bias(pos0 - wb + np.arange(wb), WINDOW, False)
    tnew = np.arange(8)
    newpos = np.where(tnew < ts, pos0 + tnew, pos0 + 2 * WINDOW + SEL_BLOCK)
    bwb = bias(newpos, WINDOW, False)
    bnew = bias(newpos, big, True)
    step = npg * page
    blast = jnp.concatenate([jnp.zeros((step - page, r.shape[0]), F32), bias(pos0 - page + np.arange(page), big, True)],
                            axis=0)
    return bc, bwa, bwb, bnew, blast


def nsa_decode(qn, gate_raw, kc, vc, cachet, page_table, win_state, kv_new, win_new, rel_table):
    bsz, ts, _ = qn.shape
    n_pages = page_table.shape[1]
    page = cachet.shape[-1]
    pos0 = n_pages * page
    nch = kc.shape[1]
    wb = win_state.shape[1]
    r_all = GRP * HKV * ts
    assert pos0 % SEL_BLOCK == 0 and ts <= 8 and ts <= SEL_BLOCK and nch % 8 == 0
    ns = -(-(pos0 + ts) // SEL_BLOCK)
    npg = min(SEL_PAGES, n_pages)
    assert n_pages % npg == 0
    bps = npg * page // SEL_BLOCK
    ns_pad = max(-(-ns // 8) * 8, (n_pages // npg) * bps)
    n_top = min(TOP_N, ns)

    q5 = qn.reshape(bsz, ts, HKV, GRP, DH)
    qbd = jnp.einsum('bqhgd,hk->bkdghq', q5, jnp.eye(HKV, dtype=F32)).reshape(bsz, 2 * DH, r_all).astype(BF)
    gate_t = gate_raw.reshape(bsz, ts, HKV, GRP, 3).transpose(0, 4, 3, 2, 1).reshape(bsz, 3, r_all)
    pad8 = lambda a: jnp.pad(a, ((0, 0), (0, 8 - ts), (0, 0)))
    amat = jnp.asarray(_selection_matrix(nch - 1, ns_pad, nch), BF)
    expand = jnp.asarray(np.repeat(np.eye(bps, dtype=np.float32), SEL_BLOCK, axis=0), BF)
    bc, bwa, bwb, bnew, blast = _dec_bias_tables(rel_table, pos0, ts, nch, wb, page, npg)

    full = lambda a: pl.BlockSpec(a.shape, functools.partial(lambda nd, *_: (0,) * nd, a.ndim))
    per_b = lambda a: pl.BlockSpec((1,) + a.shape[1:], functools.partial(lambda nd, b, *_: (b,) + (0,) * nd, a.ndim - 1))
    wnew = pad8(win_new)
    pre_in = [qbd, kc, vc, amat, bc, win_state, wnew, bwa, bwb]
    pre_specs = [per_b(qbd), per_b(kc), per_b(vc), full(amat), full(bc), per_b(win_state), per_b(wnew),
                 full(bwa), full(bwb)]
    small = jax.ShapeDtypeStruct((bsz, DH, r_all), F32)
    pen, o_c, o_w = pl.pallas_call(
        functools.partial(_nsa_dec_pre_kernel, ts=ts, pos0=pos0, n_top=n_top),
        grid=(bsz,), in_specs=pre_specs,
        out_specs=[pl.BlockSpec((1, ns_pad, r_all), lambda b: (b, 0, 0)),
                   pl.BlockSpec((1, DH, r_all), lambda b: (b, 0, 0)),
                   pl.BlockSpec((1, DH, r_all), lambda b: (b, 0, 0))],
        out_shape=[jax.ShapeDtypeStruct((bsz, ns_pad, r_all), F32), small, small],
        compiler_params=_cp("parallel"), name="nsa_dec_pre",
    )(*pre_in)

    rows = GRP * ts
    nsteps = n_pages // npg
    by_head = lambda a: a.reshape(a.shape[:-1] + (GRP, HKV, ts))
    q_h = q5.transpose(0, 2, 3, 1, 4).reshape(bsz, HKV, rows, DH).astype(BF)
    gate_h = gate_raw.reshape(bsz, ts, HKV, GRP, 3).transpose(0, 2, 3, 1, 4).reshape(bsz, HKV, rows, 3)
    pen_h = by_head(pen[:, :nsteps * bps].reshape(bsz, nsteps, bps, r_all)).transpose(0, 4, 1, 3, 5, 2).reshape(
        bsz, HKV, nsteps, rows, bps).astype(BF)
    oc_h, ow_h = (by_head(a).transpose(0, 3, 2, 4, 1).reshape(bsz, HKV, rows, DH) for a in (o_c, o_w))
    blast_h, bnew_h = (by_head(a).transpose(2, 1, 3, 0).reshape(HKV, rows, a.shape[0]) for a in (blast, bnew))
    new5 = pad8(kv_new).reshape(bsz, 8, 4, HKV, DH)
    knew = new5[:, :, 2].transpose(0, 2, 1, 3)
    vnew = new5[:, :, 3].transpose(0, 2, 1, 3)
    page_specs = [pl.BlockSpec((1, 2, HKV, DH, page),
                               functools.partial(lambda r, b, j, pt: (pt[b, j * npg + r], 1, 0, 0, 0), r))
                  for r in range(npg)]
    heads_rows = lambda a: a.reshape(a.shape[:-3] + (HKV * rows, a.shape[-1]))
    blast_h, bnew_h, gate_h, oc_h, ow_h = map(heads_rows, (blast_h, bnew_h, gate_h, oc_h, ow_h))
    sel_in = [q_h, pen_h, expand.T, blast_h, knew, vnew, bnew_h, gate_h, oc_h, ow_h]
    sel_specs = [per_b(q_h), per_b(pen_h), full(expand.T), full(blast_h), per_b(knew), per_b(vnew), full(bnew_h),
                 per_b(gate_h), per_b(oc_h), per_b(ow_h)]
    o_h = pl.pallas_call(
        functools.partial(_nsa_dec_sel_kernel, npg=npg, page=page, ts=ts),
        grid_spec=pltpu.PrefetchScalarGridSpec(
            num_scalar_prefetch=1, grid=(bsz, nsteps), in_specs=page_specs + sel_specs,
            out_specs=pl.BlockSpec((1, HKV * rows, DH), lambda b, j, pt: (b, 0, 0)),
            scratch_shapes=[pltpu.VMEM((HKV * rows, 1), F32), pltpu.VMEM((HKV * rows, 1), F32),
                            pltpu.VMEM((HKV * rows, DH), F32)]),
        out_shape=jax.ShapeDtypeStruct((bsz, HKV * rows, DH), F32),
        compiler_params=_cp("parallel", "arbitrary"), name="nsa_dec_sel",
    )(page_table, *([cachet] * npg), *sel_in)
    return o_h.reshape(bsz, HKV, GRP, ts, DH).transpose(0, 3, 1, 2, 4).reshape(bsz, ts, NH * DH)


def _even_weights(p, e):
    w_in = jnp.pad(p['att_w_in'][e], ((0, 0), (0, ATT_IN_PAD - ATT_IN_COLS))).astype(BF)
    w_out = p['att_w_out'][e].astype(BF)
    return dict(
        w_in=w_in, w_out_conv=w_out[:C_CONV], w_out_att=w_out[C_CONV:],
        cmp=_compress_weights(p['cmp_pe'][e], p['cmp_w1'][e], p['cmp_b1'][e], p['cmp_w2'][e], p['k_norm_g'][e][0]),
        k_g=p['k_norm_g'][e], q_g=p['q_norm_g'][e], conv_w=p['conv_w'][e], conv_b=p['conv_b'][e],
        ln_g=p['conv_ln_g'][e], ln_b=p['conv_ln_b'][e])


def _even_prompt(x2, norm_g, w, rel_table, mlp_w):
    t = x2.shape[0]
    proj = norm_matmul(x2, norm_g, w['w_in'])
    conv_y, conv_new = conformer_conv(proj[None], jnp.zeros((1, CONV_WIDTH - 1, C_CONV), F32),
                                      w['conv_w'], w['conv_b'], w['ln_g'], w['ln_b'])
    kv_new, win_new = kv_post(proj, w['k_g'])
    kc, vc = compress_dense(proj[None], w['cmp'])
    o = nsa_prompt(proj, kc[0], vc[0], kv_new[:, 2 * KVW:3 * KVW], kv_new[:, 3 * KVW:], win_new[:, :KVW],
                   win_new[:, KVW:], w['q_g'], rel_table)
    x2 = proj_mlp(x2, [conv_y[0], o], [w['w_out_conv'], w['w_out_att']], *mlp_w)
    keep = min(WINDOW, t)
    return (x2, kv_new.reshape(1, t, 4, HKV, DH), win_new[t - keep:].reshape(1, keep, 2, HKV, DH), conv_new)


def _even_decode(x3, norm_g, w, rel_table, cachet, page_table, win_state, conv_state, mlp_w):
    bsz, t, d = x3.shape
    x2 = x3.reshape(bsz * t, d)
    proj = norm_matmul(x2, norm_g, w['w_in'])
    proj3 = proj.reshape(bsz, t, ATT_IN_PAD)
    conv_y, conv_new = conformer_conv(proj3, conv_state, w['conv_w'], w['conv_b'], w['ln_g'], w['ln_b'])
    kv_new, win_new = kv_post(proj, w['k_g'])
    qn = q_norm(proj, w['q_g'])
    kc, vc = compress_paged(cachet, page_table, w['cmp'])
    wb = win_state.shape[1]
    o = nsa_decode(qn.reshape(bsz, t, NH * DH), proj3[:, :, O_GATE:ATT_IN_COLS], kc, vc, cachet, page_table,
                   win_state.reshape(bsz, wb, 2 * KVW), kv_new.reshape(bsz, t, 4 * KVW),
                   win_new.reshape(bsz, t, 2 * KVW), rel_table)
    x2 = proj_mlp(x2, [conv_y.reshape(bsz * t, C_CONV), o.reshape(bsz * t, NH * DH)],
                  [w['w_out_conv'], w['w_out_att']], *mlp_w)
    win_all = jnp.concatenate([win_state, win_new.reshape(bsz, t, 2, HKV, DH)], axis=1)
    keep = min(WINDOW, wb + t)
    return (x2.reshape(bsz, t, d), kv_new.reshape(bsz, t, 4, HKV, DH), win_all[:, wb + t - keep:], conv_new)


def _odd_layer(x3, s0, norm_g, w_in_bf, w_out_bf, lb, hg_norm_g, mlp_w):
    bsz, t, d = x3.shape
    x2 = x3.reshape(bsz * t, d)
    z = norm_matmul(x2, norm_g, w_in_bf).reshape(bsz, t, 4 * d)
    tp = -(-t // HG_CHUNK) * HG_CHUNK
    if tp != t:
        z = jnp.pad(z, ((0, 0), (0, tp - t), (0, 0)))
    o, s_new = hgrn2(z, s0, lb, hg_norm_g, t)
    x2 = proj_mlp(x2, [o[:, :t].reshape(bsz * t, d)], [w_out_bf], *mlp_w)
    return x2.reshape(bsz, t, d), s_new


def kernel(x_prompt, x_sample, cache_nsa_kv, page_table, state_nsa_win, state_conv, state_hgrn, rel_bias_table,
           norm_mix_g, norm_mlp_g, w_mlp_up, w_mlp_down, att_w_in, att_w_out, q_norm_g, k_norm_g, cmp_pe, cmp_w1,
           cmp_b1, cmp_w2, conv_w, conv_b, conv_ln_g, conv_ln_b, hg_w_in, hg_w_out, hg_lb_logits, hg_norm_g):
    p = dict(att_w_in=att_w_in, att_w_out=att_w_out, q_norm_g=q_norm_g, k_norm_g=k_norm_g, cmp_pe=cmp_pe,
             cmp_w1=cmp_w1, cmp_b1=cmp_b1, cmp_w2=cmp_w2, conv_w=conv_w, conv_b=conv_b, conv_ln_g=conv_ln_g,
             conv_ln_b=conv_ln_b)
    bp, tp_, d = x_prompt.shape
    assert bp == 1
    db, ts, _ = x_sample.shape
    w_up = w_mlp_up.astype(BF)
    w_down = w_mlp_down.astype(BF)
    cum = jnp.cumsum(jax.nn.softmax(hg_lb_logits.astype(F32), axis=0), axis=0)
    mlp_ws = [(norm_mlp_g[layer], w_up[layer], w_down[layer]) for layer in range(norm_mix_g.shape[0])]

    xp = x_prompt[0]
    kv_p, win_p, conv_p, hg_p = [], [], [], []
    for layer in range(norm_mix_g.shape[0]):
        if layer % 2 == 0:
            e = layer // 2
            w = _even_weights(p, e)
            xp, kv_new, win_new, conv_new = _even_prompt(xp, norm_mix_g[layer], w, rel_bias_table, mlp_ws[layer])
            kv_p.append(kv_new)
            win_p.append(win_new)
            conv_p.append(conv_new)
        else:
            o = layer // 2
            x3, s_new = _odd_layer(xp[None], jnp.zeros((1, HG_HEADS, HG_DK, HG_DV), F32), norm_mix_g[layer],
                                   hg_w_in[o].astype(BF), hg_w_out[o].astype(BF), cum[layer] - cum[0],
                                   hg_norm_g[o], mlp_ws[layer])
            xp = x3[0]
            hg_p.append(s_new)

    cachet = cache_nsa_kv.transpose(0, 1, 3, 4, 5, 2)
    xs = x_sample
    kv_s, win_s, conv_s, hg_s = [], [], [], []
    for layer in range(norm_mix_g.shape[0]):
        if layer % 2 == 0:
            e = layer // 2
            w = _even_weights(p, e)
            xs, kv_new, win_new, conv_new = _even_decode(xs, norm_mix_g[layer], w, rel_bias_table, cachet[e],
                                                         page_table, state_nsa_win[e], state_conv[e], mlp_ws[layer])
            kv_s.append(kv_new)
            win_s.append(win_new)
            conv_s.append(conv_new)
        else:
            o = layer // 2
            xs, s_new = _odd_layer(xs, state_hgrn[o], norm_mix_g[layer], hg_w_in[o].astype(BF),
                                   hg_w_out[o].astype(BF), cum[layer] - cum[0], hg_norm_g[o], mlp_ws[layer])
            hg_s.append(s_new)
    return (xp[None], xs, jnp.stack(kv_p), jnp.stack(kv_s), jnp.stack(win_p), jnp.stack(win_s),
            jnp.stack(conv_p), jnp.stack(conv_s), jnp.stack(hg_p), jnp.stack(hg_s))
```

```python
import functools
import math

import jax
import jax.numpy as jnp
import numpy as np
from jax import lax
from jax.experimental import pallas as pl
from jax.experimental.pallas import tpu as pltpu

D_MODEL = 1024
C_CONV = 512
CONV_WIDTH = 31
DH = 64
HKV = 2
GRP = 4
NH = HKV * GRP
CMP_STRIDE = 16
CMP_LEN = 32
CMP_HIDDEN = 128
SEL_BLOCK = 64
TOP_N = 16
WINDOW = 512
Q_BLOCK = 128
REL_BUCKETS = 32
REL_EXACT = 16
REL_MAX_DIST = 128
HG_DK = 128
HG_HEADS = 8
HG_DV = 128
HG_CHUNK = 64
RMS_EPS = 1e-6
NEG = -1e30
BIG = 1e9
O_Q = 2 * C_CONV
O_KV = O_Q + NH * DH
O_GATE = O_KV + 3 * 2 * HKV * DH
ATT_IN_COLS = O_GATE + 3 * NH
ATT_IN_PAD = 2432
KVW = HKV * DH

VMEM_LIMIT = 56 * 1024 * 1024
BF = jnp.bfloat16
F32 = jnp.float32


def _cp(*sem):
    return pltpu.CompilerParams(dimension_semantics=sem, vmem_limit_bytes=VMEM_LIMIT)


def _dot(a, b):
    return jnp.dot(a, b, preferred_element_type=F32)


def _dot_nt(a, b):
    return lax.dot_general(a, b, (((1,), (1,)), ((), ())), preferred_element_type=F32)


def _dot_tn(a, b):
    return lax.dot_general(a, b, (((0,), (0,)), ((), ())), preferred_element_type=F32)


def _rms(x, g):
    return x * lax.rsqrt(jnp.mean(x * x, axis=-1, keepdims=True) + RMS_EPS) * g


def _norm_matmul_kernel(x_ref, g_ref, w_ref, o_ref):
    hn = _rms(x_ref[...], g_ref[...]).astype(BF)
    o_ref[...] = _dot(hn, w_ref[...])


def norm_matmul(x, g, w_bf, tn=None):
    m, d = x.shape
    n = w_bf.shape[1]
    tm = min(m, 512)
    tn = n if tn is None else tn
    return pl.pallas_call(
        _norm_matmul_kernel,
        grid=(m // tm, n // tn),
        in_specs=[pl.BlockSpec((tm, d), lambda i, j: (i, 0)),
                  pl.BlockSpec((1, d), lambda i, j: (0, 0)),
                  pl.BlockSpec((d, tn), lambda i, j: (0, j))],
        out_specs=pl.BlockSpec((tm, tn), lambda i, j: (i, j)),
        out_shape=jax.ShapeDtypeStruct((m, n), F32),
        compiler_params=_cp("parallel", "arbitrary"),
        name="norm_matmul",
    )(x, g.reshape(1, d), w_bf)


def _out_proj_kernel(r_ref, a1_ref, a2_ref, w1_ref, w2_ref, o_ref):
    o_ref[...] = (r_ref[...] + _dot(a1_ref[...].astype(BF), w1_ref[...])
                  + _dot(a2_ref[...].astype(BF), w2_ref[...]))


def out_proj2(res, a1, a2, w1_bf, w2_bf):
    m, d = res.shape
    k1, k2 = a1.shape[1], a2.shape[1]
    tm = min(m, 512)
    return pl.pallas_call(
        _out_proj_kernel,
        grid=(m // tm,),
        in_specs=[pl.BlockSpec((tm, d), lambda i: (i, 0)),
                  pl.BlockSpec((tm, k1), lambda i: (i, 0)),
                  pl.BlockSpec((tm, k2), lambda i: (i, 0)),
                  pl.BlockSpec((k1, d), lambda i: (0, 0)),
                  pl.BlockSpec((k2, d), lambda i: (0, 0))],
        out_specs=pl.BlockSpec((tm, d), lambda i: (i, 0)),
        out_shape=jax.ShapeDtypeStruct((m, d), F32),
        compiler_params=_cp("parallel"),
        name="out_proj2",
    )(res, a1, a2, w1_bf, w2_bf)


def _out_proj1_kernel(r_ref, a_ref, w_ref, o_ref):
    o_ref[...] = r_ref[...] + _dot(a_ref[...].astype(BF), w_ref[...])


def out_proj1(res, a, w_bf):
    m, d = res.shape
    k = a.shape[1]
    tm = min(m, 512)
    return pl.pallas_call(
        _out_proj1_kernel,
        grid=(m // tm,),
        in_specs=[pl.BlockSpec((tm, d), lambda i: (i, 0)),
                  pl.BlockSpec((tm, k), lambda i: (i, 0)),
                  pl.BlockSpec((k, d), lambda i: (0, 0))],
        out_specs=pl.BlockSpec((tm, d), lambda i: (i, 0)),
        out_shape=jax.ShapeDtypeStruct((m, d), F32),
        compiler_params=_cp("parallel"),
        name="out_proj1",
    )(res, a, w_bf)


def _mlp_kernel(x_ref, g_ref, wu_ref, wd_ref, o_ref, hn_ref, acc_ref):
    j = pl.program_id(1)

    @pl.when(j == 0)
    def _():
        hn_ref[...] = _rms(x_ref[...], g_ref[...]).astype(BF)
        acc_ref[...] = x_ref[...]

    hid = jnp.maximum(_dot(hn_ref[...], wu_ref[...]), 0.0)
    acc_ref[...] += _dot((hid * hid).astype(BF), wd_ref[...])

    @pl.when(j == pl.num_programs(1) - 1)
    def _():
        o_ref[...] = acc_ref[...]


def mlp(x, g, wu_bf, wd_bf):
    m, d = x.shape
    hdim = wu_bf.shape[1]
    tm = min(m, 1024)
    th = 1024
    return pl.pallas_call(
        _mlp_kernel,
        grid=(m // tm, hdim // th),
        in_specs=[pl.BlockSpec((tm, d), lambda i, j: (i, 0)),
                  pl.BlockSpec((1, d), lambda i, j: (0, 0)),
                  pl.BlockSpec((d, th), lambda i, j: (0, j)),
                  pl.BlockSpec((th, d), lambda i, j: (j, 0))],
        out_specs=pl.BlockSpec((tm, d), lambda i, j: (i, 0)),
        out_shape=jax.ShapeDtypeStruct((m, d), F32),
        scratch_shapes=[pltpu.VMEM((tm, d), BF), pltpu.VMEM((tm, d), F32)],
        compiler_params=_cp("parallel", "arbitrary"),
        name="mlp",
    )(x, g.reshape(1, d), wu_bf, wd_bf)


def _proj_mlp_kernel(*refs, n_in):
    r_ref = refs[0]
    a_refs = refs[1:1 + n_in]
    w_refs = refs[1 + n_in:1 + 2 * n_in]
    g_ref, wu_ref, wd_ref, o_ref, hn_ref, acc_ref = refs[1 + 2 * n_in:]
    j = pl.program_id(1)

    @pl.when(j == 0)
    def _():
        x = r_ref[...]
        for a_ref, w_ref in zip(a_refs, w_refs):
            x = x + _dot(a_ref[...].astype(BF), w_ref[...])
        hn_ref[...] = _rms(x, g_ref[...]).astype(BF)
        acc_ref[...] = x

    hid = jnp.maximum(_dot(hn_ref[...], wu_ref[...]), 0.0)
    acc_ref[...] += _dot((hid * hid).astype(BF), wd_ref[...])

    @pl.when(j == pl.num_programs(1) - 1)
    def _():
        o_ref[...] = acc_ref[...]


def proj_mlp(res, acts, ws_bf, g, wu_bf, wd_bf):
    m, d = res.shape
    hdim = wu_bf.shape[1]
    tm = min(m, 1024)
    th = 1024
    n_in = len(acts)
    return pl.pallas_call(
        functools.partial(_proj_mlp_kernel, n_in=n_in),
        grid=(m // tm, hdim // th),
        in_specs=([pl.BlockSpec((tm, d), lambda i, j: (i, 0))]
                  + [pl.BlockSpec((tm, a.shape[1]), lambda i, j: (i, 0)) for a in acts]
                  + [pl.BlockSpec(w.shape, lambda i, j: (0, 0)) for w in ws_bf]
                  + [pl.BlockSpec((1, d), lambda i, j: (0, 0)),
                     pl.BlockSpec((d, th), lambda i, j: (0, j)),
                     pl.BlockSpec((th, d), lambda i, j: (j, 0))]),
        out_specs=pl.BlockSpec((tm, d), lambda i, j: (i, 0)),
        out_shape=jax.ShapeDtypeStruct((m, d), F32),
        scratch_shapes=[pltpu.VMEM((tm, d), BF), pltpu.VMEM((tm, d), F32)],
        compiler_params=_cp("parallel", "arbitrary"),
        name="proj_mlp",
    )(res, *acts, *ws_bf, g.reshape(1, d), wu_bf, wd_bf)


CONV_HALO = 32
CONV_PAD = CONV_HALO - (CONV_WIDTH - 1)


def _conv_kernel(u_ref, st_ref, w_ref, b_ref, lg_ref, lb_ref, y_ref, new_ref, xin_ref, ph_ref, *, tt):
    t = pl.program_id(1)

    @pl.when(t == 0)
    def _():
        xin_ref[0:CONV_HALO, :] = st_ref[0]

    a = u_ref[0, :, 0:C_CONV]
    gt = u_ref[0, :, C_CONV:2 * C_CONV]
    xin_ref[CONV_HALO:CONV_HALO + tt, :] = a * jax.nn.sigmoid(gt)

    rb = min(tt, 128)
    for c in range(C_CONV // 128):
        cs = slice(c * 128, (c + 1) * 128)
        starts = list(range(0, tt, rb))
        accs = [jnp.zeros((rb, 128), F32) + b_ref[:, cs] for _ in starts]
        for ph in range(min(8, CONV_WIDTH)):
            taps = range(ph, CONV_WIDTH, 8)
            n = rb + 8 * (len(taps) - 1)
            if rb % 8 == 0:
                for bi, r0 in enumerate(starts):
                    ph_ref[bi, 0:n, :] = xin_ref[CONV_PAD + ph + r0:CONV_PAD + ph + r0 + n, cs]
            for a, k in enumerate(taps):
                wk = w_ref[k:k + 1, cs]
                for bi, r0 in enumerate(starts):
                    if rb % 8 == 0:
                        accs[bi] = accs[bi] + wk * ph_ref[bi, 8 * a:8 * a + rb, :]
                    else:
                        accs[bi] = accs[bi] + wk * xin_ref[CONV_PAD + k + r0:CONV_PAD + k + r0 + rb, cs]
        for bi, r0 in enumerate(starts):
            y_ref[0, r0:r0 + rb, cs] = accs[bi]
    y = y_ref[0]
    mu = jnp.mean(y, axis=-1, keepdims=True)
    yc = y - mu
    var = jnp.mean(yc * yc, axis=-1, keepdims=True)
    z = yc * lax.rsqrt(var + RMS_EPS) * lg_ref[...] + lb_ref[...]
    y_ref[0] = z * jax.nn.sigmoid(z)

    @pl.when(t == pl.num_programs(1) - 1)
    def _():
        new_ref[0] = xin_ref[tt + CONV_PAD:tt + CONV_HALO, :]

    if tt >= CONV_HALO:
        @pl.when(t < pl.num_programs(1) - 1)
        def _():
            xin_ref[0:CONV_HALO, :] = xin_ref[tt:tt + CONV_HALO, :]


def conformer_conv(proj, state, w, b, ln_g, ln_b):
    bsz, t, _ = proj.shape
    tt = min(t, 256)
    assert t % tt == 0 and (t == tt or tt >= CONV_HALO)
    st = jnp.pad(state, ((0, 0), (CONV_PAD, 0), (0, 0)))
    row = lambda v: v.reshape(1, C_CONV)
    return pl.pallas_call(
        functools.partial(_conv_kernel, tt=tt),
        grid=(bsz, t // tt),
        in_specs=[pl.BlockSpec((1, tt, 2 * C_CONV), lambda i, j: (i, j, 0)),
                  pl.BlockSpec((1, CONV_HALO, C_CONV), lambda i, j: (i, 0, 0)),
                  pl.BlockSpec((CONV_WIDTH, C_CONV), lambda i, j: (0, 0)),
                  pl.BlockSpec((1, C_CONV), lambda i, j: (0, 0)),
                  pl.BlockSpec((1, C_CONV), lambda i, j: (0, 0)),
                  pl.BlockSpec((1, C_CONV), lambda i, j: (0, 0))],
        out_specs=[pl.BlockSpec((1, tt, C_CONV), lambda i, j: (i, j, 0)),
                   pl.BlockSpec((1, CONV_WIDTH - 1, C_CONV), lambda i, j: (i, 0, 0))],
        out_shape=[jax.ShapeDtypeStruct((bsz, t, C_CONV), F32),
                   jax.ShapeDtypeStruct((bsz, CONV_WIDTH - 1, C_CONV), F32)],
        scratch_shapes=[pltpu.VMEM((CONV_HALO + tt, C_CONV), F32),
                        pltpu.VMEM((max(tt // 128, 1), min(tt, 128) + 8 * ((CONV_WIDTH - 1) // 8), 128), F32)],
        compiler_params=_cp("parallel", "arbitrary"),
        name="conformer_conv",
    )(proj, st, w, row(b), row(ln_g), row(ln_b))


HG_LEVELS = (32, 16, 8, 4, 2, 1)


def _hgrn_tables():
    c = HG_CHUNK
    idx = np.arange(c)
    masks = []
    for h in HG_LEVELS:
        blk = idx // (2 * h)
        upper = (idx % (2 * h)) >= h
        masks.append((blk[:, None] == blk[None, :]) & upper[:, None] & (~upper)[None, :])
    masks.append(np.eye(c, dtype=bool))
    return (idx[None, :] <= idx[:, None]).astype(np.float32), np.stack(masks).astype(np.float32)


HG_HEADS_PER_STEP = 8


def _hgrn_kernel(q_ref, fz_ref, v_ref, g_ref, lb_ref, ng_ref, s0_ref, tab_ref, msk_ref,
                 o_ref, sn_ref, st_ref, *, t_valid, nh, ncs):
    ci = pl.program_id(2)
    c = HG_CHUNK

    @pl.when(ci == 0)
    def _():
        for hh in range(nh):
            st_ref[hh] = s0_ref[0, hh].T

    rowc = lax.broadcasted_iota(jnp.int32, (c, 1), 0)
    sub8 = lax.broadcasted_iota(jnp.int32, (8, 1), 0)
    tab = tab_ref[...]
    for sub, hh in [(s_, h_) for s_ in range(ncs) for h_ in range(nh)]:
        rs = slice(sub * c, (sub + 1) * c)
        live = (ci * ncs + sub) * c + rowc < t_valid
        cs = slice(hh * HG_DK, (hh + 1) * HG_DK)
        lb = lb_ref[:, cs]
        f = lb + (1.0 - lb) * jax.nn.sigmoid(fz_ref[0, rs, cs])
        lf = jnp.where(live, jnp.log(f), 0.0)
        k = jnp.where(live, 1.0 - f, 0.0)
        q = q_ref[0, rs, cs]
        v = v_ref[0, rs, cs]

        hi = lf.astype(BF)
        b2 = _dot(tab, jnp.concatenate([hi, (lf - hi.astype(F32)).astype(BF)], axis=1))
        b = b2[:, 0:HG_DK] + b2[:, HG_DK:]

        def pivot_rows(h):
            if 2 * h >= 8:
                return jnp.concatenate(
                    [jnp.broadcast_to(b[blk * 2 * h + h - 1:blk * 2 * h + h, :], (2 * h, HG_DK))
                     for blk in range(c // (2 * h))], axis=0)
            groups = []
            for g8 in range(c // 8):
                piece = None
                for kb in range(8 // (2 * h)):
                    r = 8 * g8 + kb * 2 * h + h - 1
                    cand = jnp.broadcast_to(b[r:r + 1, :], (8, HG_DK))
                    piece = cand if piece is None else jnp.where(sub8 >= kb * 2 * h, cand, piece)
                groups.append(piece)
            return jnp.concatenate(groups, axis=0)

        attn = jnp.zeros((c, c), F32)
        for li, h in enumerate(HG_LEVELS):
            piv = pivot_rows(h)
            fac = jnp.exp(jnp.where((rowc % (2 * h)) >= h, b - piv, piv - b))
            attn = attn + msk_ref[li] * _dot_nt((q * fac).astype(BF), (k * fac).astype(BF))
        attn = attn + msk_ref[len(HG_LEVELS)] * _dot_nt(q.astype(BF), k.astype(BF))

        st = st_ref[hh]
        qb = (q * jnp.exp(b)).astype(BF)
        o = _dot(attn.astype(BF), v.astype(BF)) + _dot_nt(qb, st.astype(BF))
        ke = (k * jnp.exp(b[c - 1:c] - b)).astype(BF)
        decay = jnp.exp(b[c - 1:c])
        st_ref[hh] = st * decay + _dot(v.T.astype(BF), ke)

        gate = g_ref[0, rs, cs]
        o_ref[0, rs, cs] = _rms(o, ng_ref[...]) * (gate * jax.nn.sigmoid(gate))

    @pl.when(ci == pl.num_programs(2) - 1)
    def _():
        for hh in range(nh):
            sn_ref[0, hh] = st_ref[hh].T


def hgrn2(z, s0, lb, norm_g, t_valid):
    bsz, tp, _ = z.shape
    ncs = 2 if (tp // HG_CHUNK) % 2 == 0 else 1
    c = HG_CHUNK * ncs
    nc = tp // c
    tab, msk = _hgrn_tables()
    nh = HG_HEADS_PER_STEP
    ng = HG_HEADS // nh
    w = nh * HG_DK
    blk = lambda off: pl.BlockSpec((1, c, w), lambda b, h, i: (b, i, off + h))
    return pl.pallas_call(
        functools.partial(_hgrn_kernel, t_valid=t_valid, nh=nh, ncs=ncs),
        grid=(bsz, ng, nc),
        in_specs=[blk(0), blk(ng), blk(2 * ng), blk(3 * ng),
                  pl.BlockSpec((1, w), lambda b, h, i: (0, h)),
                  pl.BlockSpec((1, HG_DV), lambda b, h, i: (0, 0)),
                  pl.BlockSpec((1, nh, HG_DK, HG_DV), lambda b, h, i: (b, h, 0, 0)),
                  pl.BlockSpec(tab.shape, lambda b, h, i: (0, 0)),
                  pl.BlockSpec(msk.shape, lambda b, h, i: (0, 0, 0))],
        out_specs=[pl.BlockSpec((1, c, w), lambda b, h, i: (b, i, h)),
                   pl.BlockSpec((1, nh, HG_DK, HG_DV), lambda b, h, i: (b, h, 0, 0))],
        out_shape=[jax.ShapeDtypeStruct((bsz, tp, D_MODEL), F32),
                   jax.ShapeDtypeStruct((bsz, HG_HEADS, HG_DK, HG_DV), F32)],
        scratch_shapes=[pltpu.VMEM((nh, HG_DV, HG_DK), F32)],
        compiler_params=_cp("parallel", "parallel", "arbitrary"),
        name="hgrn2",
    )(z, z, z, z, lb.reshape(1, D_MODEL), norm_g.reshape(1, HG_DV), s0,
      jnp.asarray(tab, BF), jnp.asarray(msk))


def _pair_norm(seg, g):
    lo = lax.broadcasted_iota(jnp.int32, seg.shape, 1) < DH
    sq = seg * seg
    s0 = jnp.sum(jnp.where(lo, sq, 0.0), axis=-1, keepdims=True)
    s1 = jnp.sum(jnp.where(lo, 0.0, sq), axis=-1, keepdims=True)
    inv = jnp.where(lo, lax.rsqrt(s0 * (1.0 / DH) + RMS_EPS), lax.rsqrt(s1 * (1.0 / DH) + RMS_EPS))
    return seg * inv * g


def _kv_post_kernel(p_ref, kg_ref, kv_ref, win_ref):
    x = p_ref[...]
    kv_ref[:, 0:2 * KVW] = x[:, 0:2 * KVW]
    kv_ref[:, 2 * KVW:3 * KVW] = _pair_norm(x[:, 2 * KVW:3 * KVW], kg_ref[1:2, :])
    kv_ref[:, 3 * KVW:4 * KVW] = x[:, 3 * KVW:4 * KVW]
    win_ref[:, 0:KVW] = _pair_norm(x[:, 4 * KVW:5 * KVW], kg_ref[2:3, :])
    win_ref[:, KVW:2 * KVW] = x[:, 5 * KVW:6 * KVW]


def kv_post(proj, k_g):
    m = proj.shape[0]
    tm = min(m, 1024)
    kvcols = 6 * KVW
    assert O_KV % kvcols == 0
    return pl.pallas_call(
        _kv_post_kernel,
        grid=(m // tm,),
        in_specs=[pl.BlockSpec((tm, kvcols), lambda i: (i, O_KV // kvcols)),
                  pl.BlockSpec((3, KVW), lambda i: (0, 0))],
        out_specs=[pl.BlockSpec((tm, 4 * KVW), lambda i: (i, 0)),
                   pl.BlockSpec((tm, 2 * KVW), lambda i: (i, 0))],
        out_shape=[jax.ShapeDtypeStruct((m, 4 * KVW), F32), jax.ShapeDtypeStruct((m, 2 * KVW), F32)],
        compiler_params=_cp("parallel"),
        name="kv_post",
    )(proj, jnp.concatenate([k_g, k_g], axis=1))


def _q_norm_kernel(q_ref, g_ref, o_ref):
    for c in range(NH // 2):
        cs = slice(c * 2 * DH, (c + 1) * 2 * DH)
        o_ref[:, cs] = _pair_norm(q_ref[:, cs], g_ref[...]) * (DH ** -0.5)


def q_norm(proj, q_g):
    m = proj.shape[0]
    tm = min(m, 1024)
    return pl.pallas_call(
        _q_norm_kernel,
        grid=(m // tm,),
        in_specs=[pl.BlockSpec((tm, NH * DH), lambda i: (i, O_Q // (NH * DH))),
                  pl.BlockSpec((1, 2 * DH), lambda i: (0, 0))],
        out_specs=pl.BlockSpec((tm, NH * DH), lambda i: (i, 0)),
        out_shape=jax.ShapeDtypeStruct((m, NH * DH), F32),
        compiler_params=_cp("parallel"),
        name="q_norm",
    )(proj, jnp.concatenate([q_g, q_g]).reshape(1, 2 * DH))


def _compress_kernel(*refs, n_prefetch, n_src, rows, nch):
    refs = refs[n_prefetch:]
    srcs = (refs[:n_src], refs[n_src:2 * n_src])
    pef_ref, pes_ref, w1f_ref, w1s_ref, b1_ref, w2_ref, kg_ref, kc_ref, vc_ref, hf_ref, hs_ref = refs[2 * n_src:]
    j = pl.program_id(1)
    cpr = rows // CMP_STRIDE
    m = cpr * n_src

    @pl.when(j == 0)
    def _():
        hs_ref[:, nch:nch + 8, :] = jnp.zeros((2, 8, 2 * CMP_HIDDEN), F32)

    for kind in range(2):
        accf = jnp.zeros((m, 2 * CMP_HIDDEN), F32)
        accs = jnp.zeros((m, 2 * CMP_HIDDEN), F32)
        for s in range(CMP_STRIDE):
            xs = jnp.concatenate(
                [r[0, pl.ds(s, cpr, stride=CMP_STRIDE), :] for r in srcs[kind]], axis=0)
            accf = accf + _dot((xs + pef_ref[kind, s:s + 1, :]).astype(BF), w1f_ref[kind, s])
            accs = accs + _dot((xs + pes_ref[kind, s:s + 1, :]).astype(BF), w1s_ref[kind, s])
        row0 = pl.multiple_of(j * m, 8)
        hf_ref[kind, pl.ds(row0, m), :] = accf
        hs_ref[kind, pl.ds(row0, m), :] = accs

    @pl.when(j == pl.num_programs(1) - 1)
    def _():
        for kind in range(2):
            hid = jax.nn.gelu(hf_ref[kind, 0:nch, :] + hs_ref[kind, 1:nch + 1, :] + b1_ref[kind])
            out = _dot(hid.astype(BF), w2_ref[kind])
            if kind == 0:
                kc_ref[0] = _pair_norm(out, kg_ref[...]).astype(BF)
            else:
                vc_ref[0] = out.astype(BF)


def _compress_weights(pe, w1, b1, w2, kg0):
    eye = jnp.eye(HKV, dtype=F32)
    bd = lambda w: jnp.einsum('ab,ksdf->ksadbf', eye, w).reshape(2, CMP_STRIDE, KVW, 2 * CMP_HIDDEN).astype(BF)
    tile2 = lambda a: jnp.concatenate([a, a], axis=-1)
    w2bd = jnp.einsum('ab,kfd->kafbd', eye, w2).reshape(2, 2 * CMP_HIDDEN, KVW).astype(BF)
    return (tile2(pe[:, :CMP_STRIDE]), tile2(pe[:, CMP_STRIDE:]), bd(w1[:, :CMP_STRIDE]), bd(w1[:, CMP_STRIDE:]),
            tile2(b1).reshape(2, 1, 2 * CMP_HIDDEN), w2bd, tile2(kg0).reshape(1, KVW))


def _compress_common(n_prefetch, n_src, rows, nch, bsz, weights):
    wspecs = [pl.BlockSpec(w.shape, functools.partial(lambda nd, *a: (0,) * nd, w.ndim)) for w in weights]
    out_specs = [pl.BlockSpec((1, nch, KVW), lambda b, j, *a: (b, 0, 0))] * 2
    out_shape = [jax.ShapeDtypeStruct((bsz, nch, KVW), BF)] * 2
    scratch = [pltpu.VMEM((2, nch + 8, 2 * CMP_HIDDEN), F32)] * 2
    kern = functools.partial(_compress_kernel, n_prefetch=n_prefetch, n_src=n_src, rows=rows, nch=nch)
    return kern, wspecs, out_specs, out_shape, scratch


def compress_dense(proj3, weights):
    bsz, t, _ = proj3.shape
    rows = min(t, 2048)
    nch = t // CMP_STRIDE
    assert O_KV % KVW == 0
    kern, wspecs, out_specs, out_shape, scratch = _compress_common(0, 1, rows, nch, bsz, weights)
    src = lambda kind: pl.BlockSpec((1, rows, KVW), lambda b, j: (b, j, O_KV // KVW + kind))
    return pl.pallas_call(
        kern, grid=(bsz, t // rows),
        in_specs=[src(0), src(1)] + wspecs,
        out_specs=out_specs, out_shape=out_shape, scratch_shapes=scratch,
        compiler_params=_cp("parallel", "arbitrary"), name="compress_dense",
    )(proj3, proj3, *weights)


CMP_PAGES = 32
CMP_PE_ROWS = 16


def _compress_paged_kernel(*refs, npg, page, nch):
    pages = refs[1:1 + npg]
    pe2_ref, w1p_ref, b1_ref, w2_ref, kg_ref, kc_ref, vc_ref, hf_ref, hs_ref, xs_ref = refs[1 + npg:]
    j = pl.program_id(1)
    m = npg * page // CMP_STRIDE
    hid2 = 2 * CMP_HIDDEN

    @pl.when(j == 0)
    def _():
        hs_ref[:, nch:nch + 8, :] = jnp.zeros((2, 8, hid2), F32)

    for kind in range(2):
        for r in range(npg):
            xt = jnp.concatenate([pages[r][0, kind, h] for h in range(HKV)], axis=0)
            xs_ref[kind, r * page:(r + 1) * page, :] = xt.T
        acc = jnp.zeros((m + CMP_PE_ROWS, 2 * hid2), F32)
        for s2 in range(CMP_STRIDE // 2):
            xa = xs_ref[kind, pl.ds(2 * s2, m, stride=CMP_STRIDE), :]
            xb = xs_ref[kind, pl.ds(2 * s2 + 1, m, stride=CMP_STRIDE), :]
            lhs = jnp.concatenate([jnp.concatenate([xa, xb], axis=1).astype(BF), pe2_ref[kind, s2]], axis=0)
            acc = acc + _dot(lhs, w1p_ref[kind, s2])
        row0 = pl.multiple_of(j * m, 8)
        hf_ref[kind, pl.ds(row0, m), :] = acc[0:m, 0:hid2] + acc[m:m + 1, 0:hid2]
        hs_ref[kind, pl.ds(row0, m), :] = acc[0:m, hid2:] + acc[m + 1:m + 2, hid2:]

    @pl.when(j == pl.num_programs(1) - 1)
    def _():
        for kind in range(2):
            hid = jax.nn.gelu(hf_ref[kind, 0:nch, :] + hs_ref[kind, 1:nch + 1, :] + b1_ref[kind])
            out = _dot(hid.astype(BF), w2_ref[kind])
            if kind == 0:
                kc_ref[0] = _pair_norm(out, kg_ref[...]).astype(BF)
            else:
                vc_ref[0] = out.astype(BF)


def compress_paged(cachet, page_table, weights):
    bsz, n_pages = page_table.shape
    page = cachet.shape[-1]
    npg = min(CMP_PAGES, n_pages)
    nch = n_pages * page // CMP_STRIDE
    src_specs = [pl.BlockSpec((1, 2, HKV, DH, page),
                              functools.partial(lambda r, b, j, pt: (pt[b, j * npg + r], 0, 0, 0, 0), r))
                 for r in range(npg)]
    pef, pes, w1f, w1s, b1, w2, kg = weights
    np2 = CMP_STRIDE // 2
    pair_rows = lambda a: a.reshape(2, np2, 2 * KVW)
    pe2 = jnp.zeros((2, np2, CMP_PE_ROWS, 2 * KVW), F32).at[:, :, 0].set(pair_rows(pef)).at[:, :, 1].set(
        pair_rows(pes)).astype(BF)
    pair_w = lambda w: w.reshape(2, np2, 2 * KVW, 2 * CMP_HIDDEN)
    w1p = jnp.concatenate([pair_w(w1f), pair_w(w1s)], axis=-1)
    weights = (pe2, w1p, b1, w2, kg)
    _, wspecs, out_specs, out_shape, scratch = _compress_common(1, npg, page, nch, bsz, weights)
    return pl.pallas_call(
        functools.partial(_compress_paged_kernel, npg=npg, page=page, nch=nch),
        grid_spec=pltpu.PrefetchScalarGridSpec(
            num_scalar_prefetch=1, grid=(bsz, n_pages // npg), in_specs=src_specs + wspecs,
            out_specs=out_specs, scratch_shapes=scratch + [pltpu.VMEM((2, npg * page, KVW), F32)]),
        out_shape=out_shape, compiler_params=_cp("parallel", "arbitrary"), name="compress_paged",
    )(page_table, *([cachet] * npg), *weights)


NS_PAD = 256
SEL_HALF = 128
LOG2E = math.log2(math.e)
PEN = 30000.0
KT = 256
SEL_SPLIT = 2
VROWS = 80
HALF_TILES = SEL_HALF * SEL_BLOCK // KT
CBAND = 24


def _t5_bucket(dist):
    n = jnp.maximum(dist, 0)
    large = REL_EXACT + (jnp.log(jnp.maximum(n, 1).astype(F32) / REL_EXACT)
                         / math.log(REL_MAX_DIST / REL_EXACT) * (REL_BUCKETS - REL_EXACT)).astype(jnp.int32)
    return jnp.where(n < REL_EXACT, n, jnp.minimum(large, REL_BUCKETS - 1))


def _selection_matrix(nc, ns_pad, nc_pad):
    ratio, span = SEL_BLOCK // CMP_STRIDE, CMP_LEN // CMP_STRIDE
    a = np.zeros((ns_pad, nc_pad), np.float32)
    for j in range(ns_pad):
        for mm in range(ratio):
            for nn in range(span):
                n = ratio * j + mm - nn
                if 0 <= n < nc:
                    a[j, n] += 1.0
    return a


def _split3(x):
    hi = x.astype(BF)
    r1 = x - hi.astype(F32)
    mid = r1.astype(BF)
    lo = (r1 - mid.astype(F32)).astype(BF)
    return hi, mid, lo


def _top_blocks(s, qp, n_rounds):
    j = lax.broadcasted_iota(jnp.int32, s.shape, 0)
    cur = qp // SEL_BLOCK
    forced = (j == 0) | (j == cur) | (j == cur - 1)
    valid = j * SEL_BLOCK <= qp
    s = jnp.where(forced, BIG, s)
    s = jnp.where(valid, s, -BIG)
    for _ in range(n_rounds):
        mx = jnp.max(s, axis=0, keepdims=True)
        jm = jnp.min(jnp.where(s == mx, j, 2 * NS_PAD), axis=0, keepdims=True)
        s = jnp.where(j == jm, -3e38, s)
    return jnp.where(s < -2 * BIG, 1.0, 0.0)


def _masked_softmax_rows(s, axis):
    m = jnp.max(s, axis=axis, keepdims=True)
    e = jnp.exp(s - m)
    den = jnp.maximum(jnp.sum(e, axis=axis, keepdims=True), 1e-30)
    return e * jnp.where(m > 0.5 * NEG, 1.0 / den, 0.0)


def _nsa_prompt_kernel(q_ref, gate_ref, qg_ref, kc_ref, vct_ref, amat_ref, kaug_ref, vselt_ref, kwin_ref,
                       vwint_ref, bc_ref, bs_ref, bw_ref, o_ref, lg_ref, acc_ref, qa_ref, qh_ref, sa_ref, sb_ref,
                       m_ref, mx_ref, *, ncp, n_top):
    i = pl.program_id(0)
    qb = Q_BLOCK
    rows = GRP * qb
    start = i * qb
    qall = q_ref[...]
    lane = lax.broadcasted_iota(jnp.int32, (qb, 2 * DH), 1)
    qpos_row = start + lax.broadcasted_iota(jnp.int32, (1, qb), 1)
    t_last = start // KT
    n_far = jnp.maximum(t_last - 1, 0)

    @pl.when(i == 0)
    def _():
        lg_ref[:, 0:16, :] = jnp.zeros((HKV, 16, rows), F32)

    xfs = []
    for h in range(HKV):
        parts = []
        for g in range(GRP):
            hd = h * GRP + g
            slab = qall[:, (hd // 2) * 2 * DH:(hd // 2 + 1) * 2 * DH]
            mine = (lane >= DH) if hd % 2 else (lane < DH)
            ss = jnp.sum(jnp.where(mine, slab * slab, 0.0), axis=-1, keepdims=True)
            xn = jnp.where(mine, slab * lax.rsqrt(ss * (1.0 / DH) + RMS_EPS) * qg_ref[...] * (DH ** -0.5), 0.0)
            if hd % 2 != h:
                xn = pltpu.roll(xn, DH, 1)
            parts.append(xn)
        xf = jnp.concatenate(parts, axis=0)
        xfs.append(xf)
        qh_ref[h] = xf.astype(BF)

    def compressed(n):
        outs = []
        for h in range(HKV):
            lg_ref[h, 16:16 + n, :] = _dot_nt(kc_ref[0:n, :], qh_ref[h])
            band0 = pl.multiple_of(i * 8, 8)
            lg_ref[h, pl.ds(band0, CBAND), :] = lg_ref[h, pl.ds(band0, CBAND), :] + bc_ref[h]
            tok = lax.broadcasted_iota(jnp.int32, (n, 1), 0)
            p_c = _masked_softmax_rows(jnp.where(tok < 8 * i + 8, lg_ref[h, 16:16 + n, :], NEG), 0)
            o_c = _dot(vct_ref[h, :, 0:n], p_c.astype(BF))
            imp = p_c[:, 0:qb]
            for g in range(1, GRP):
                imp = imp + p_c[:, g * qb:(g + 1) * qb]
            sc3 = _dot(amat_ref[:, 0:n], jnp.concatenate(_split3(imp), axis=1))
            outs += [o_c, sc3[:, 0:qb] + sc3[:, qb:2 * qb] + sc3[:, 2 * qb:]]
        return tuple(outs)

    cch = min(ncp, 2 * Q_BLOCK)
    nbr = ncp // cch
    outs = lax.switch(jnp.minimum((8 * i + 7) // cch, nbr - 1),
                      [functools.partial(compressed, (k + 1) * cch) for k in range(nbr)])
    o_cs = [outs[2 * h] for h in range(HKV)]
    sel = _top_blocks(jnp.concatenate([outs[2 * h + 1] for h in range(HKV)], axis=1),
                      jnp.concatenate([qpos_row] * HKV, axis=1), n_top)
    for h in range(HKV):
        pen = ((sel[:, h * qb:(h + 1) * qb].T - 1.0) * PEN).astype(BF)
        qh2 = (xfs[h] * LOG2E).astype(BF)
        qa_ref[h, 0] = jnp.concatenate([jnp.concatenate([pen[:, 0:SEL_HALF]] * GRP, axis=0), qh2], axis=1)
        qa_ref[h, 1] = jnp.concatenate([jnp.concatenate([pen[:, SEL_HALF:]] * GRP, axis=0), qh2], axis=1)

    hr = rows // SEL_SPLIT
    chains = [(h, slice(sp * hr, (sp + 1) * hr)) for h in range(HKV) for sp in range(SEL_SPLIT)]
    nchain = len(chains)

    def scores(c, t):
        h, rs = chains[c]
        qa = jnp.where(t < HALF_TILES, qa_ref[h, 0, rs, :], qa_ref[h, 1, rs, :])
        return _dot_nt(kaug_ref[t + 1], qa)

    def absorb(c, s, mx, t, m):
        h, rs = chains[c]
        m2 = jnp.maximum(m, mx)
        p = jnp.exp2(s - m2)
        acc_ref[h, :, rs] = jnp.exp2(m - m2) * acc_ref[h, :, rs] + _dot(vselt_ref[h, t + 1], p.astype(BF))
        return m2

    col_max = lambda s: jnp.max(s, axis=0, keepdims=True)

    def pair_body(k, carry):
        ms, mxa = list(carry[:nchain]), list(carry[nchain:])
        mxb = []
        for c, (h, rs) in enumerate(chains):
            s1 = scores(c, 2 * k + 1)
            sb_ref[h, :, rs] = s1
            mxb.append(col_max(s1))
            ms[c] = absorb(c, sa_ref[h, :, rs], mxa[c], 2 * k, ms[c])
        for c, (h, rs) in enumerate(chains):
            s2 = scores(c, 2 * k + 2)
            ms[c] = absorb(c, sb_ref[h, :, rs], mxb[c], 2 * k + 1, ms[c])
            sa_ref[h, :, rs] = s2
            mxa[c] = col_max(s2)
        return tuple(ms) + tuple(mxa)

    acc_ref[...] = jnp.zeros((HKV, VROWS, rows), F32)
    mxa = []
    for c, (h, rs) in enumerate(chains):
        s0 = scores(c, 0)
        sa_ref[h, :, rs] = s0
        mxa.append(col_max(s0))
    carry = tuple(jnp.full((1, hr), NEG, F32) for _ in range(nchain)) + tuple(mxa)
    quad_body = lambda k, cr: pair_body(2 * k + 1, pair_body(2 * k, cr))
    oct_body = lambda k, cr: quad_body(2 * k + 1, quad_body(2 * k, cr))
    hex_body = lambda k, cr: oct_body(2 * k + 1, oct_body(2 * k, cr))
    carry = lax.fori_loop(0, n_far // 16, hex_body, carry)
    carry = lax.fori_loop(2 * (n_far // 16), n_far // 8, oct_body, carry)
    carry = lax.fori_loop(2 * (n_far // 8), n_far // 4, quad_body, carry)
    carry = lax.fori_loop(2 * (n_far // 4), n_far // 2, pair_body, carry)
    for c, (h, rs) in enumerate(chains):
        m_ref[h, :, rs] = carry[c]
        mx_ref[h, :, rs] = carry[nchain + c]

    @pl.when(n_far % 2 == 1)
    def _():
        for c, (h, rs) in enumerate(chains):
            m_ref[h, :, rs] = absorb(c, sa_ref[h, :, rs], mx_ref[h, :, rs], n_far - 1, m_ref[h, :, rs])

    kiota = lax.broadcasted_iota(jnp.int32, (KT, 1), 0)
    ms = [m_ref[h, :, rs] for h, rs in chains]
    for u in range(2):
        t = t_last - 1 + u
        for c, (h, rs) in enumerate(chains):
            s = jnp.where(t * KT + kiota >= 0, scores(c, t) + bs_ref[i % 2, h, u, :, rs], NEG)
            ms[c] = absorb(c, s, col_max(s), t, ms[c])

    gates_t = jax.nn.sigmoid(gate_ref[...]).T
    wkeys = WINDOW + qb
    for h in range(HKV):
        o_s = acc_ref[h, 0:DH, :] / jnp.maximum(acc_ref[h, DH:DH + 1, :], 1e-30)

        s = _dot_nt(kwin_ref[pl.ds(pl.multiple_of(start, qb), wkeys), :], qh_ref[h]) + bw_ref[h]
        kpos = start - WINDOW + lax.broadcasted_iota(jnp.int32, (wkeys, 1), 0)
        s = jnp.where(kpos >= 0, s, NEG)
        e = jnp.exp(s - jnp.max(s, axis=0, keepdims=True))
        vw = jnp.concatenate([vwint_ref[h, i + u] for u in range(wkeys // qb)], axis=1)
        ow = _dot(vw, e.astype(BF))
        o_w = ow[0:DH] / jnp.maximum(ow[DH:DH + 1], 1e-30)

        gate = lambda br: jnp.concatenate(
            [gates_t[(h * GRP + g) * 3 + br:(h * GRP + g) * 3 + br + 1, :] for g in range(GRP)], axis=1)
        o_t = gate(0) * o_cs[h] + gate(1) * o_s + gate(2) * o_w
        for g in range(GRP):
            o_ref[:, (h * GRP + g) * DH:(h * GRP + g + 1) * DH] = o_t[:, g * qb:(g + 1) * qb].T


def _bias_tables(rel_table):
    tab = rel_table.astype(F32)
    far = tab[REL_BUCKETS - 1]
    qb = Q_BLOCK

    def toeplitz(off, nk, dmax, shift):
        d = off - (nk - 1) + jnp.arange(nk + qb - 1, dtype=jnp.int32)
        w = jnp.where(((d >= 0) & (d <= dmax))[None, :], (tab[_t5_bucket(d)] - (far if shift else 0.0)).T, NEG)
        p = w.shape[1]
        hank = jnp.tile(w, (1, nk + 1))[:, :nk * (p + 1)].reshape(NH, nk, p + 1)[:, :, :qb]
        return hank[:, ::-1, :]

    def lanes_gq(b):
        return b.reshape(HKV, GRP, b.shape[1], qb).transpose(0, 2, 1, 3).reshape(HKV, b.shape[1], GRP * qb)

    big = 1 << 30
    bc = lanes_gq(toeplitz(16 * CMP_STRIDE - (CMP_LEN - 1), CBAND * CMP_STRIDE, big, True)[:, ::CMP_STRIDE, :])
    bs = jnp.stack([lanes_gq(toeplitz(off, 2 * KT, big, True)).reshape(HKV, 2, KT, GRP * qb)
                    for off in (KT, KT + qb)]) * LOG2E
    bw = lanes_gq(toeplitz(WINDOW, WINDOW + qb, WINDOW, False))
    return bc, bs, bw


def nsa_prompt(proj, kc, vc, ksel, vsel, kwin, vwin, q_g, rel_table):
    t = proj.shape[0]
    nb = t // Q_BLOCK
    nch = kc.shape[0]
    ncp = -(-nch // 128) * 128
    assert t // SEL_BLOCK <= NS_PAD and t % KT == 0
    n_top = min(TOP_N, t // SEL_BLOCK)
    kcp = jnp.pad(kc, ((0, ncp - nch), (0, 0)))
    vct = jnp.pad(vc, ((0, ncp - nch), (0, 0))).reshape(ncp, HKV, DH).transpose(1, 2, 0)
    amat = jnp.asarray(_selection_matrix(nch - 1, NS_PAD, ncp), BF)

    def values_t(v, pad_rows, tile):
        vt = jnp.pad(v.astype(BF), ((pad_rows, 0), (0, 0))).reshape(-1, tile, HKV, DH).transpose(2, 0, 3, 1)
        return jnp.concatenate([vt, jnp.ones(vt.shape[:2] + (1, tile), BF),
                                jnp.zeros(vt.shape[:2] + (VROWS - DH - 1, tile), BF)], axis=2)

    onehot = ((jnp.arange(t, dtype=jnp.int32)[:, None] // SEL_BLOCK) % SEL_HALF
              == jnp.arange(SEL_HALF, dtype=jnp.int32)[None, :]).astype(BF)
    kaug = jnp.pad(jnp.concatenate([onehot, ksel.astype(BF)], axis=1), ((KT, 0), (0, 0))).reshape(-1, KT, 2 * KVW)
    vselt = values_t(vsel, KT, KT)
    vwint = values_t(vwin, WINDOW, Q_BLOCK)
    rows = GRP * Q_BLOCK
    kwinp = jnp.pad(kwin.astype(BF), ((WINDOW, 0), (0, 0)))
    bc, bs, bw = _bias_tables(rel_table)
    whole = pl.BlockSpec(memory_space=pltpu.VMEM)
    return pl.pallas_call(
        functools.partial(_nsa_prompt_kernel, ncp=ncp, n_top=n_top),
        grid=(nb,),
        in_specs=[pl.BlockSpec((Q_BLOCK, NH * DH), lambda i: (i, O_Q // (NH * DH))),
                  pl.BlockSpec((Q_BLOCK, 128), lambda i: (i, O_GATE // 128)),
                  whole, whole, whole, whole, whole, whole, whole, whole, whole, whole, whole],
        out_specs=pl.BlockSpec((Q_BLOCK, NH * DH), lambda i: (i, 0)),
        out_shape=jax.ShapeDtypeStruct((t, NH * DH), F32),
        scratch_shapes=[pltpu.VMEM((HKV, 16 + ncp, rows), F32), pltpu.VMEM((HKV, VROWS, rows), F32),
                        pltpu.VMEM((HKV, 2, rows, SEL_HALF + KVW), BF), pltpu.VMEM((HKV, rows, KVW), BF),
                        pltpu.VMEM((HKV, KT, rows), F32), pltpu.VMEM((HKV, KT, rows), F32),
                        pltpu.VMEM((HKV, 1, rows), F32), pltpu.VMEM((HKV, 1, rows), F32)],
        compiler_params=_cp("arbitrary"),
        name="nsa_prompt",
    )(proj, proj, jnp.concatenate([q_g, q_g]).reshape(1, 2 * DH), kcp, vct, amat, kaug, vselt, kwinp, vwint,
      bc, bs, bw)


def _pick_head(x, lane_h):
    return jnp.where(lane_h == 0, x[0:DH], x[DH:2 * DH])


def _nsa_dec_pre_kernel(qbd_ref, kc_ref, vc_ref, amat_ref, bc_ref, win_ref, wnew_ref, bwa_ref, bwb_ref,
                        pen_ref, oc_ref, ow_ref, *, ts, pos0, n_top):
    r_all = GRP * HKV * ts
    qbd = qbd_ref[0]
    lane_h = (lax.broadcasted_iota(jnp.int32, (1, r_all), 1) // ts) % HKV

    p_c = _masked_softmax_rows(_dot(kc_ref[0], qbd) + bc_ref[...], 0)
    oc_ref[0] = _pick_head(_dot_tn(vc_ref[0], p_c.astype(BF)), lane_h)

    hi, mid, lo = _split3(p_c)
    amat = amat_ref[...]
    sc = _dot(amat, hi) + _dot(amat, mid) + _dot(amat, lo)
    w8 = HKV * ts
    score = sc[:, 0:w8]
    for g in range(1, GRP):
        score = score + sc[:, g * w8:(g + 1) * w8]
    qp = pos0 + lax.broadcasted_iota(jnp.int32, (1, w8), 1) % ts
    sel = _top_blocks(score, qp, n_top)
    pen_ref[0] = (jnp.concatenate([sel] * GRP, axis=1) - 1.0) * PEN

    win = win_ref[0]
    wnew = wnew_ref[0]
    s_a = _dot(win[:, 0:KVW].astype(BF), qbd) + bwa_ref[...]
    s_b = _dot(wnew[:, 0:KVW].astype(BF), qbd) + bwb_ref[...]
    m = jnp.maximum(jnp.max(s_a, axis=0, keepdims=True), jnp.max(s_b, axis=0, keepdims=True))
    e_a = jnp.where(s_a > 0.5 * NEG, jnp.exp(s_a - m), 0.0)
    e_b = jnp.where(s_b > 0.5 * NEG, jnp.exp(s_b - m), 0.0)
    den = jnp.maximum(jnp.sum(e_a, axis=0, keepdims=True) + jnp.sum(e_b, axis=0, keepdims=True), 1e-30)
    o_w = (_dot_tn(win[:, KVW:].astype(BF), e_a.astype(BF)) + _dot_tn(wnew[:, KVW:].astype(BF), e_b.astype(BF)))
    ow_ref[0] = _pick_head(o_w, lane_h) / den


SEL_PAGES = 64


def _nsa_dec_sel_kernel(*refs, npg, page, ts):
    pages = refs[1:1 + npg]
    (q_ref, pen_ref, expt_ref, blast_ref, knew_ref, vnew_ref, bnew_ref, gate_ref, oc_ref, ow_ref,
     o_ref, m_ref, l_ref, acc_ref) = refs[1 + npg:]
    j = pl.program_id(1)
    last = pl.num_programs(1) - 1
    rows = GRP * ts

    @pl.when(j == 0)
    def _():
        m_ref[...] = jnp.full((HKV * rows, 1), NEG, F32)
        l_ref[...] = jnp.zeros((HKV * rows, 1), F32)
        acc_ref[...] = jnp.zeros((HKV * rows, DH), F32)

    def online(s, pvs):
        m = m_ref[...]
        m2 = jnp.maximum(m, jnp.max(s, axis=-1, keepdims=True))
        a = jnp.exp(m - m2)
        p = jnp.exp(s - m2)
        m_ref[...] = m2
        l_ref[...] = a * l_ref[...] + jnp.sum(p, axis=-1, keepdims=True)
        pb = p.astype(BF)
        pv = jnp.concatenate([pvs[h](pb[h * rows:(h + 1) * rows]) for h in range(HKV)], axis=0)
        acc_ref[...] = a * acc_ref[...] + pv

    both = lambda f: jnp.concatenate([f(h) for h in range(HKV)], axis=0)
    vts = [jnp.concatenate([r[0, 1, h] for r in pages], axis=1).astype(BF) for h in range(HKV)]
    s = both(lambda h: _dot(q_ref[0, h], jnp.concatenate([r[0, 0, h] for r in pages], axis=1).astype(BF))
             + _dot(pen_ref[0, h, j], expt_ref[...]))
    s = s + jnp.where(j == last, blast_ref[...], 0.0)
    online(s, [functools.partial(lambda vt, p: _dot_nt(p, vt), vts[h]) for h in range(HKV)])

    @pl.when(j == last)
    def _():
        s_new = both(lambda h: _dot_nt(q_ref[0, h], knew_ref[0, h].astype(BF))) + bnew_ref[...]
        online(s_new, [functools.partial(lambda v, p: _dot(p, v), vnew_ref[0, h].astype(BF)) for h in range(HKV)])
        o_s = acc_ref[...] / jnp.maximum(l_ref[...], 1e-30)
        g = jax.nn.sigmoid(gate_ref[0])
        o_ref[0] = g[:, 0:1] * oc_ref[0] + g[:, 1:2] * o_s + g[:, 2:3] * ow_ref[0]


def _dec_bias_tables(rel_table, pos0, ts, nch, wb, page, npg):
    tab = rel_table.astype(F32)
    far = tab[REL_BUCKETS - 1]
    r = np.arange(GRP * HKV * ts)
    head = ((r // ts) % HKV) * GRP + r // (HKV * ts)
    qpos = pos0 + r % ts
    far_r = far[head][None, :]

    def bias(kpos, dmax, shift):
        dist = qpos[None, :] - np.asarray(kpos)[:, None]
        ok = (dist >= 0) & (dist <= dmax)
        out = jnp.where(ok, 0.0 if shift else far_r, NEG)
        near = np.nonzero((ok & (dist < REL_MAX_DIST)).any(axis=1))[0]
        if near.size:
            lo, hi = int(near.min()), int(near.max()) + 1
            b = tab[_t5_bucket(jnp.asarray(dist[lo:hi], jnp.int32)), head[None, :]] - (far_r if shift else 0.0)
            out = jnp.concatenate([out[:lo], jnp.where(ok[lo:hi], b, NEG), out[hi:]], axis=0)
        return out

    big = 1 << 30
    bc = bias(np.arange(nch) * CMP_STRIDE + CMP_LEN - 1, big, False)
    bwa = bias(pos0 - wb + np.arange(wb), WINDOW, False)
    tnew = np.arange(8)
    newpos = np.where(tnew < ts, pos0 + tnew, pos0 + 2 * WINDOW + SEL_BLOCK)
    bwb = bias(newpos, WINDOW, False)
    bnew = bias(newpos, big, True)
    step = npg * page
    blast = jnp.concatenate([jnp.zeros((step - page, r.shape[0]), F32), bias(pos0 - page + np.arange(page), big, True)],
                            axis=0)
    return bc, bwa, bwb, bnew, blast


def nsa_decode(qn, gate_raw, kc, vc, cachet, page_table, win_state, kv_new, win_new, rel_table):
    bsz, ts, _ = qn.shape
    n_pages = page_table.shape[1]
    page = cachet.shape[-1]
    pos0 = n_pages * page
    nch = kc.shape[1]
    wb = win_state.shape[1]
    r_all = GRP * HKV * ts
    assert pos0 % SEL_BLOCK == 0 and ts <= 8 and ts <= SEL_BLOCK and nch % 8 == 0
    ns = -(-(pos0 + ts) // SEL_BLOCK)
    npg = min(SEL_PAGES, n_pages)
    assert n_pages % npg == 0
    bps = npg * page // SEL_BLOCK
    ns_pad = max(-(-ns // 8) * 8, (n_pages // npg) * bps)
    n_top = min(TOP_N, ns)

    q5 = qn.reshape(bsz, ts, HKV, GRP, DH)
    qbd = jnp.einsum('bqhgd,hk->bkdghq', q5, jnp.eye(HKV, dtype=F32)).reshape(bsz, 2 * DH, r_all).astype(BF)
    gate_t = gate_raw.reshape(bsz, ts, HKV, GRP, 3).transpose(0, 4, 3, 2, 1).reshape(bsz, 3, r_all)
    pad8 = lambda a: jnp.pad(a, ((0, 0), (0, 8 - ts), (0, 0)))
    amat = jnp.asarray(_selection_matrix(nch - 1, ns_pad, nch), BF)
    expand = jnp.asarray(np.repeat(np.eye(bps, dtype=np.float32), SEL_BLOCK, axis=0), BF)
    bc, bwa, bwb, bnew, blast = _dec_bias_tables(rel_table, pos0, ts, nch, wb, page, npg)

    full = lambda a: pl.BlockSpec(a.shape, functools.partial(lambda nd, *_: (0,) * nd, a.ndim))
    per_b = lambda a: pl.BlockSpec((1,) + a.shape[1:], functools.partial(lambda nd, b, *_: (b,) + (0,) * nd, a.ndim - 1))
    wnew = pad8(win_new)
    pre_in = [qbd, kc, vc, amat, bc, win_state, wnew, bwa, bwb]
    pre_specs = [per_b(qbd), per_b(kc), per_b(vc), full(amat), full(bc), per_b(win_state), per_b(wnew),
                 full(bwa), full(bwb)]
    small = jax.ShapeDtypeStruct((bsz, DH, r_all), F32)
    pen, o_c, o_w = pl.pallas_call(
        functools.partial(_nsa_dec_pre_kernel, ts=ts, pos0=pos0, n_top=n_top),
        grid=(bsz,), in_specs=pre_specs,
        out_specs=[pl.BlockSpec((1, ns_pad, r_all), lambda b: (b, 0, 0)),
                   pl.BlockSpec((1, DH, r_all), lambda b: (b, 0, 0)),
                   pl.BlockSpec((1, DH, r_all), lambda b: (b, 0, 0))],
        out_shape=[jax.ShapeDtypeStruct((bsz, ns_pad, r_all), F32), small, small],
        compiler_params=_cp("parallel"), name="nsa_dec_pre",
    )(*pre_in)

    rows = GRP * ts
    nsteps = n_pages // npg
    by_head = lambda a: a.reshape(a.shape[:-1] + (GRP, HKV, ts))
    q_h = q5.transpose(0, 2, 3, 1, 4).reshape(bsz, HKV, rows, DH).astype(BF)
    gate_h = gate_raw.reshape(bsz, ts, HKV, GRP, 3).transpose(0, 2, 3, 1, 4).reshape(bsz, HKV, rows, 3)
    pen_h = by_head(pen[:, :nsteps * bps].reshape(bsz, nsteps, bps, r_all)).transpose(0, 4, 1, 3, 5, 2).reshape(
        bsz, HKV, nsteps, rows, bps).astype(BF)
    oc_h, ow_h = (by_head(a).transpose(0, 3, 2, 4, 1).reshape(bsz, HKV, rows, DH) for a in (o_c, o_w))
    blast_h, bnew_h = (by_head(a).transpose(2, 1, 3, 0).reshape(HKV, rows, a.shape[0]) for a in (blast, bnew))
    new5 = pad8(kv_new).reshape(bsz, 8, 4, HKV, DH)
    knew = new5[:, :, 2].transpose(0, 2, 1, 3)
    vnew = new5[:, :, 3].transpose(0, 2, 1, 3)
    page_specs = [pl.BlockSpec((1, 2, HKV, DH, page),
                               functools.partial(lambda r, b, j, pt: (pt[b, j * npg + r], 1, 0, 0, 0), r))
                  for r in range(npg)]
    heads_rows = lambda a: a.reshape(a.shape[:-3] + (HKV * rows, a.shape[-1]))
    blast_h, bnew_h, gate_h, oc_h, ow_h = map(heads_rows, (blast_h, bnew_h, gate_h, oc_h, ow_h))
    sel_in = [q_h, pen_h, expand.T, blast_h, knew, vnew, bnew_h, gate_h, oc_h, ow_h]
    sel_specs = [per_b(q_h), per_b(pen_h), full(expand.T), full(blast_h), per_b(knew), per_b(vnew), full(bnew_h),
                 per_b(gate_h), per_b(oc_h), per_b(ow_h)]
    o_h = pl.pallas_call(
        functools.partial(_nsa_dec_sel_kernel, npg=npg, page=page, ts=ts),
        grid_spec=pltpu.PrefetchScalarGridSpec(
            num_scalar_prefetch=1, grid=(bsz, nsteps), in_specs=page_specs + sel_specs,
            out_specs=pl.BlockSpec((1, HKV * rows, DH), lambda b, j, pt: (b, 0, 0)),
            scratch_shapes=[pltpu.VMEM((HKV * rows, 1), F32), pltpu.VMEM((HKV * rows, 1), F32),
                            pltpu.VMEM((HKV * rows, DH), F32)]),
        out_shape=jax.ShapeDtypeStruct((bsz, HKV * rows, DH), F32),
        compiler_params=_cp("parallel", "arbitrary"), name="nsa_dec_sel",
    )(page_table, *([cachet] * npg), *sel_in)
    return o_h.reshape(bsz, HKV, GRP, ts, DH).transpose(0, 3, 1, 2, 4).reshape(bsz, ts, NH * DH)


def _even_weights(p, e):
    w_in = jnp.pad(p['att_w_in'][e], ((0, 0), (0, ATT_IN_PAD - ATT_IN_COLS))).astype(BF)
    w_out = p['att_w_out'][e].astype(BF)
    return dict(
        w_in=w_in, w_out_conv=w_out[:C_CONV], w_out_att=w_out[C_CONV:],
        cmp=_compress_weights(p['cmp_pe'][e], p['cmp_w1'][e], p['cmp_b1'][e], p['cmp_w2'][e], p['k_norm_g'][e][0]),
        k_g=p['k_norm_g'][e], q_g=p['q_norm_g'][e], conv_w=p['conv_w'][e], conv_b=p['conv_b'][e],
        ln_g=p['conv_ln_g'][e], ln_b=p['conv_ln_b'][e])


def _even_prompt(x2, norm_g, w, rel_table, mlp_w):
    t = x2.shape[0]
    proj = norm_matmul(x2, norm_g, w['w_in'])
    conv_y, conv_new = conformer_conv(proj[None], jnp.zeros((1, CONV_WIDTH - 1, C_CONV), F32),
                                      w['conv_w'], w['conv_b'], w['ln_g'], w['ln_b'])
    kv_new, win_new = kv_post(proj, w['k_g'])
    kc, vc = compress_dense(proj[None], w['cmp'])
    o = nsa_prompt(proj, kc[0], vc[0], kv_new[:, 2 * KVW:3 * KVW], kv_new[:, 3 * KVW:], win_new[:, :KVW],
                   win_new[:, KVW:], w['q_g'], rel_table)
    x2 = proj_mlp(x2, [conv_y[0], o], [w['w_out_conv'], w['w_out_att']], *mlp_w)
    keep = min(WINDOW, t)
    return (x2, kv_new.reshape(1, t, 4, HKV, DH), win_new[t - keep:].reshape(1, keep, 2, HKV, DH), conv_new)


def _even_decode(x3, norm_g, w, rel_table, cachet, page_table, win_state, conv_state, mlp_w):
    bsz, t, d = x3.shape
    x2 = x3.reshape(bsz * t, d)
    proj = norm_matmul(x2, norm_g, w['w_in'])
    proj3 = proj.reshape(bsz, t, ATT_IN_PAD)
    conv_y, conv_new = conformer_conv(proj3, conv_state, w['conv_w'], w['conv_b'], w['ln_g'], w['ln_b'])
    kv_new, win_new = kv_post(proj, w['k_g'])
    qn = q_norm(proj, w['q_g'])
    kc, vc = compress_paged(cachet, page_table, w['cmp'])
    wb = win_state.shape[1]
    o = nsa_decode(qn.reshape(bsz, t, NH * DH), proj3[:, :, O_GATE:ATT_IN_COLS], kc, vc, cachet, page_table,
                   win_state.reshape(bsz, wb, 2 * KVW), kv_new.reshape(bsz, t, 4 * KVW),
                   win_new.reshape(bsz, t, 2 * KVW), rel_table)
    x2 = proj_mlp(x2, [conv_y.reshape(bsz * t, C_CONV), o.reshape(bsz * t, NH * DH)],
                  [w['w_out_conv'], w['w_out_att']], *mlp_w)
    win_all = jnp.concatenate([win_state, win_new.reshape(bsz, t, 2, HKV, DH)], axis=1)
    keep = min(WINDOW, wb + t)
    return (x2.reshape(bsz, t, d), kv_new.reshape(bsz, t, 4, HKV, DH), win_all[:, wb + t - keep:], conv_new)


def _odd_layer(x3, s0, norm_g, w_in_bf, w_out_bf, lb, hg_norm_g, mlp_w):
    bsz, t, d = x3.shape
    x2 = x3.reshape(bsz * t, d)
    z = norm_matmul(x2, norm_g, w_in_bf).reshape(bsz, t, 4 * d)
    tp = -(-t // HG_CHUNK) * HG_CHUNK
    if tp != t:
        z = jnp.pad(z, ((0, 0), (0, tp - t), (0, 0)))
    o, s_new = hgrn2(z, s0, lb, hg_norm_g, t)
    x2 = proj_mlp(x2, [o[:, :t].reshape(bsz * t, d)], [w_out_bf], *mlp_w)
    return x2.reshape(bsz, t, d), s_new


def kernel(x_prompt, x_sample, cache_nsa_kv, page_table, state_nsa_win, state_conv, state_hgrn, rel_bias_table,
           norm_mix_g, norm_mlp_g, w_mlp_up, w_mlp_down, att_w_in, att_w_out, q_norm_g, k_norm_g, cmp_pe, cmp_w1,
           cmp_b1, cmp_w2, conv_w, conv_b, conv_ln_g, conv_ln_b, hg_w_in, hg_w_out, hg_lb_logits, hg_norm_g):
    p = dict(att_w_in=att_w_in, att_w_out=att_w_out, q_norm_g=q_norm_g, k_norm_g=k_norm_g, cmp_pe=cmp_pe,
             cmp_w1=cmp_w1, cmp_b1=cmp_b1, cmp_w2=cmp_w2, conv_w=conv_w, conv_b=conv_b, conv_ln_g=conv_ln_g,
             conv_ln_b=conv_ln_b)
    bp, tp_, d = x_prompt.shape
    assert bp == 1
    db, ts, _ = x_sample.shape
    w_up = w_mlp_up.astype(BF)
    w_down = w_mlp_down.astype(BF)
    cum = jnp.cumsum(jax.nn.softmax(hg_lb_logits.astype(F32), axis=0), axis=0)
    mlp_ws = [(norm_mlp_g[layer], w_up[layer], w_down[layer]) for layer in range(norm_mix_g.shape[0])]

    xp = x_prompt[0]
    kv_p, win_p, conv_p, hg_p = [], [], [], []
    for layer in range(norm_mix_g.shape[0]):
        if layer % 2 == 0:
            e = layer // 2
            w = _even_weights(p, e)
            xp, kv_new, win_new, conv_new = _even_prompt(xp, norm_mix_g[layer], w, rel_bias_table, mlp_ws[layer])
            kv_p.append(kv_new)
            win_p.append(win_new)
            conv_p.append(conv_new)
        else:
            o = layer // 2
            x3, s_new = _odd_layer(xp[None], jnp.zeros((1, HG_HEADS, HG_DK, HG_DV), F32), norm_mix_g[layer],
                                   hg_w_in[o].astype(BF), hg_w_out[o].astype(BF), cum[layer] - cum[0],
                                   hg_norm_g[o], mlp_ws[layer])
            xp = x3[0]
            hg_p.append(s_new)

    cachet = cache_nsa_kv.transpose(0, 1, 3, 4, 5, 2)
    xs = x_sample
    kv_s, win_s, conv_s, hg_s = [], [], [], []
    for layer in range(norm_mix_g.shape[0]):
        if layer % 2 == 0:
            e = layer // 2
            w = _even_weights(p, e)
            xs, kv_new, win_new, conv_new = _even_decode(xs, norm_mix_g[layer], w, rel_bias_table, cachet[e],
                                                         page_table, state_nsa_win[e], state_conv[e], mlp_ws[layer])
            kv_s.append(kv_new)
            win_s.append(win_new)
            conv_s.append(conv_new)
        else:
            o = layer // 2
            xs, s_new = _odd_layer(xs, state_hgrn[o], norm_mix_g[layer], hg_w_in[o].astype(BF),
                                   hg_w_out[o].astype(BF), cum[layer] - cum[0], hg_norm_g[o], mlp_ws[layer])
            hg_s.append(s_new)
    return (xp[None], xs, jnp.stack(kv_p), jnp.stack(kv_s), jnp.stack(win_p), jnp.stack(win_s),
            jnp.stack(conv_p), jnp.stack(conv_s), jnp.stack(hg_p), jnp.stack(hg_s))
```
